```python
import math
import jax
import jax.numpy as jnp
from jax import lax
import numpy as np

D_MODEL = 1024
BATCH = 8
SEQ = 8192
DEPTH = 4

HEAD_DIM = 64
N_HEADS = D_MODEL // HEAD_DIM
MEM_HEADS = 4
NA_HEADS = (N_HEADS - MEM_HEADS) // 2
DIL_HEADS = N_HEADS - MEM_HEADS - NA_HEADS
NA_WIDTH = NA_HEADS * HEAD_DIM
DIL_WIDTH = DIL_HEADS * HEAD_DIM
MEM_WIDTH = MEM_HEADS * HEAD_DIM
IN_PROJ_DIM = 3 * NA_WIDTH + 3 * DIL_WIDTH + MEM_WIDTH
GRID_W = 64
NA_WIN_ROWS = 8
NA_WIN_COLS = 16
DIL_BRANCHES = ((128, 1), (512, 4), (2048, 16))
DIL_Q_BLOCK = 64
ROPE_THETA = 500000.0
ROT_DIM = HEAD_DIM // 4
N_MEM = 256
FFN_DIM = 3584
N_EXPERTS = 8
TOP_K = 2
MOE_BLOCK = 512
RMS_EPS = 1e-6
ATTN_SCALE = HEAD_DIM ** -0.5

kernel_name = 'hybrid_na_dilated_mem_moe_encoder'


def rms_norm(x, g):
    xf = x.astype(jnp.float32)
    y = xf * lax.rsqrt(jnp.mean(xf * xf, axis=-1, keepdims=True) + RMS_EPS)
    return (y * g.astype(jnp.float32)).astype(x.dtype)


def partial_rotary(x):
    t = x.shape[1]
    half = ROT_DIM // 2
    inv_freq = ROPE_THETA ** (-jnp.arange(0, ROT_DIM, 2, dtype=jnp.float32) / ROT_DIM)
    ang = jnp.arange(t, dtype=jnp.float32)[:, None] * inv_freq[None, :]
    cos = jnp.cos(ang)[None, :, None, :]
    sin = jnp.sin(ang)[None, :, None, :]
    xr = x[..., :ROT_DIM].astype(jnp.float32)
    x1, x2 = xr[..., :half], xr[..., half:]
    rot = jnp.concatenate([x1 * cos - x2 * sin, x2 * cos + x1 * sin], axis=-1).astype(x.dtype)
    return jnp.concatenate([rot, x[..., ROT_DIM:]], axis=-1)


def neighbourhood_attention(q, k, v, rpb):
    bsz, t, _ = q.shape
    rows = t // GRID_W
    win_r = min(NA_WIN_ROWS, rows)

    def grid(a):
        return a.reshape(bsz, rows, GRID_W, NA_HEADS, HEAD_DIM).transpose(0, 3, 1, 2, 4)

    qg, kg, vg = grid(q * ATTN_SCALE), grid(k), grid(v)
    row_start = jnp.clip(jnp.arange(rows) - win_r // 2, 0, rows - win_r)
    col_start = jnp.clip(jnp.arange(GRID_W) - NA_WIN_COLS // 2, 0, GRID_W - NA_WIN_COLS)
    col_idx = col_start[:, None] + jnp.arange(NA_WIN_COLS)[None, :]
    col_off = col_idx - jnp.arange(GRID_W)[:, None] + (NA_WIN_COLS - 1)

    def row_block(r):
        r0 = row_start[r]
        k_rows = lax.dynamic_slice_in_dim(kg, r0, win_r, axis=2)
        v_rows = lax.dynamic_slice_in_dim(vg, r0, win_r, axis=2)
        k_nb = k_rows[:, :, :, col_idx]
        v_nb = v_rows[:, :, :, col_idx]
        q_r = lax.dynamic_index_in_dim(qg, r, axis=2, keepdims=False)
        s = jnp.einsum('bhcd,bhrckd->bhcrk', q_r, k_nb, preferred_element_type=jnp.float32)
        row_off = r0 + jnp.arange(win_r) - r + (NA_WIN_ROWS - 1)
        bias = rpb[:, row_off[None, :, None], col_off[:, None, :]]
        s = s + bias[None].astype(jnp.float32)
        p = jax.nn.softmax(s.reshape(bsz, NA_HEADS, GRID_W, win_r * NA_WIN_COLS), axis=-1)
        p = p.reshape(s.shape).astype(v.dtype)
        return jnp.einsum('bhcrk,bhrckd->bhcd', p, v_nb)

    o = lax.map(row_block, jnp.arange(rows))
    return o.transpose(1, 0, 3, 2, 4).reshape(bsz, t, NA_WIDTH)


def dilated_branch(q, k, v, dil, n_side):
    bsz, nh, t, hd = q.shape
    seg = t // dil
    qb = math.gcd(DIL_Q_BLOCK, seg)
    nb = seg // qb
    width = qb + 2 * n_side

    def by_residue(a):
        return a.reshape(bsz, nh, seg, dil, hd).transpose(0, 1, 3, 2, 4)

    pad = ((0, 0), (0, 0), (0, 0), (n_side, n_side), (0, 0))
    qs = by_residue(q).reshape(bsz, nh, dil, nb, qb, hd)
    ks = jnp.pad(by_residue(k), pad)
    vs = jnp.pad(by_residue(v), pad)
    win = jnp.arange(nb)[:, None] * qb + jnp.arange(width)[None, :]
    kw = ks[:, :, :, win]
    vw = vs[:, :, :, win]
    s = jnp.einsum('bhrnqd,bhrnkd->bhrnqk', qs, kw, preferred_element_type=jnp.float32)
    key_m = win - n_side
    query_m = jnp.arange(nb)[:, None] * qb + jnp.arange(qb)[None, :]
    dist = key_m[:, None, :] - query_m[:, :, None]
    valid = (jnp.abs(dist) <= n_side) & (key_m[:, None, :] >= 0) & (key_m[:, None, :] < seg)
    s = jnp.where(valid, s, -jnp.inf)
    m = jnp.max(s, axis=-1, keepdims=True)
    p = jnp.exp(s - m)
    den = jnp.sum(p, axis=-1, keepdims=True)
    o = jnp.einsum('bhrnqk,bhrnkd->bhrnqd', (p / den).astype(v.dtype), vw)
    lse = (m + jnp.log(den))[..., 0]
    o = o.reshape(bsz, nh, dil, seg, hd).transpose(0, 1, 3, 2, 4).reshape(bsz, nh, t, hd)
    lse = lse.reshape(bsz, nh, dil, seg).transpose(0, 1, 3, 2).reshape(bsz, nh, t)
    return o, lse


def dilated_attention(q, k, v):
    bsz, t, _ = q.shape
    qh = partial_rotary((q * ATTN_SCALE).reshape(bsz, t, DIL_HEADS, HEAD_DIM)).transpose(0, 2, 1, 3)
    kh = partial_rotary(k.reshape(bsz, t, DIL_HEADS, HEAD_DIM)).transpose(0, 2, 1, 3)
    vh = v.reshape(bsz, t, DIL_HEADS, HEAD_DIM).transpose(0, 2, 1, 3)
    outs, lses = [], []
    for window, dil in DIL_BRANCHES:
        o, lse = dilated_branch(qh, kh, vh, dil, window // (2 * dil))
        outs.append(o)
        lses.append(lse)
    w = jax.nn.softmax(jnp.stack(lses), axis=0)
    o = jnp.einsum('gbht,gbhtd->bthd', w, jnp.stack(outs).astype(jnp.float32))
    return o.reshape(bsz, t, DIL_WIDTH).astype(q.dtype)


def memory_attention(q, mem, g_mem, w_mem_kv):
    bsz, t, _ = q.shape
    kv = jnp.dot(rms_norm(mem, g_mem), w_mem_kv)
    k, v = jnp.split(kv, 2, axis=-1)
    qh = (q * ATTN_SCALE).reshape(bsz, t, MEM_HEADS, HEAD_DIM)
    kh = k.reshape(bsz, -1, MEM_HEADS, HEAD_DIM)
    vh = v.reshape(bsz, -1, MEM_HEADS, HEAD_DIM)
    s = jnp.einsum('bthd,bmhd->bhtm', qh, kh, preferred_element_type=jnp.float32)
    p = jax.nn.softmax(s, axis=-1).astype(v.dtype)
    o = jnp.einsum('bhtm,bmhd->bthd', p, vh)
    return o.reshape(bsz, t, MEM_WIDTH)


def swiglu(h, w_gate, w_up, w_down):
    return jnp.dot(jax.nn.silu(jnp.dot(h, w_gate)) * jnp.dot(h, w_up), w_down)


def moe_swiglu(h, w_router, w_gate, w_up, w_down):
    n, d = h.shape
    n_assign = n * TOP_K
    logits = jnp.dot(h, w_router).astype(jnp.float32)
    top_logit, top_idx = lax.top_k(logits, TOP_K)
    gates = jax.nn.softmax(top_logit, axis=-1)
    flat_e = top_idx.reshape(-1)
    flat_tok = jnp.repeat(jnp.arange(n), TOP_K)
    order = jnp.argsort(flat_e)
    e_sorted = flat_e[order]
    tok_sorted = flat_tok[order]
    gate_sorted = gates.reshape(-1)[order]
    counts = jnp.bincount(flat_e, length=N_EXPERTS)
    padded = (counts + MOE_BLOCK - 1) // MOE_BLOCK * MOE_BLOCK
    start = jnp.cumsum(counts) - counts
    pend = jnp.cumsum(padded)
    pstart = pend - padded
    dest = pstart[e_sorted] + jnp.arange(n_assign) - start[e_sorted]
    n_blocks = -(-n_assign // MOE_BLOCK) + N_EXPERTS
    buf = jnp.zeros((n_blocks * MOE_BLOCK, d), h.dtype).at[dest].set(h[tok_sorted])
    block_first = jnp.arange(n_blocks) * MOE_BLOCK
    block_expert = jnp.minimum(jnp.sum(block_first[:, None] >= pend[None, :], axis=1), N_EXPERTS - 1)

    def expert_block(args):
        xb, e = args
        return swiglu(xb, w_gate[e], w_up[e], w_down[e])

    yb = lax.map(expert_block, (buf.reshape(n_blocks, MOE_BLOCK, d), block_expert))
    y_sorted = yb.reshape(-1, d)[dest]
    contrib = gate_sorted[:, None].astype(h.dtype) * y_sorted
    return jax.ops.segment_sum(contrib, tok_sorted, num_segments=n)


def setup_inputs(seed: int = 0) -> dict:
    key = jax.random.key(seed)
    ks = jax.random.split(key, 18)
    f32 = jnp.float32
    n_dense = (DEPTH + 1) // 2
    n_moe = DEPTH // 2
    out_scale = (2 * DEPTH) ** -0.5

    def normal(k, shape, scale):
        return jax.random.normal(k, shape, f32) * scale

    def gain(k, shape):
        return 1.0 + 0.05 * jax.random.normal(k, shape, f32)

    return {
        'x': normal(ks[0], (BATCH, SEQ, D_MODEL), 1.0),
        'mem': normal(ks[1], (BATCH, N_MEM, D_MODEL), 1.0),
        'g_mix_norm': gain(ks[2], (DEPTH, D_MODEL)),
        'w_in': normal(ks[3], (DEPTH, D_MODEL, IN_PROJ_DIM), D_MODEL ** -0.5),
        'rpb': normal(ks[4], (DEPTH, NA_HEADS, 2 * NA_WIN_ROWS - 1, 2 * NA_WIN_COLS - 1), 0.1),
        'g_mem_norm': gain(ks[5], (DEPTH, D_MODEL)),
        'w_mem_kv': normal(ks[6], (DEPTH, D_MODEL, 2 * MEM_WIDTH), D_MODEL ** -0.5),
        'g_mix_out': gain(ks[7], (DEPTH, D_MODEL)),
        'w_out': normal(ks[8], (DEPTH, D_MODEL, D_MODEL), D_MODEL ** -0.5 * out_scale),
        'g_ffn_norm': gain(ks[9], (DEPTH, D_MODEL)),
        'w_dense_gate': normal(ks[10], (n_dense, D_MODEL, FFN_DIM), D_MODEL ** -0.5),
        'w_dense_up': normal(ks[11], (n_dense, D_MODEL, FFN_DIM), D_MODEL ** -0.5),
        'w_dense_down': normal(ks[12], (n_dense, FFN_DIM, D_MODEL), FFN_DIM ** -0.5 * out_scale),
        'w_router': normal(ks[13], (n_moe, D_MODEL, N_EXPERTS), D_MODEL ** -0.5),
        'w_moe_gate': normal(ks[14], (n_moe, N_EXPERTS, D_MODEL, FFN_DIM), D_MODEL ** -0.5),
        'w_moe_up': normal(ks[15], (n_moe, N_EXPERTS, D_MODEL, FFN_DIM), D_MODEL ** -0.5),
        'w_moe_down': normal(ks[16], (n_moe, N_EXPERTS, FFN_DIM, D_MODEL), FFN_DIM ** -0.5 * out_scale),
        'g_final': gain(ks[17], (D_MODEL,)),
    }


def reference(x, mem, g_mix_norm, w_in, rpb, g_mem_norm, w_mem_kv, g_mix_out, w_out,
              g_ffn_norm, w_dense_gate, w_dense_up, w_dense_down, w_router,
              w_moe_gate, w_moe_up, w_moe_down, g_final):
    bsz, t, d = x.shape
    widths = [NA_WIDTH] * 3 + [DIL_WIDTH] * 3 + [MEM_WIDTH]
    split_at = [sum(widths[:i + 1]) for i in range(len(widths) - 1)]
    for layer in range(DEPTH):
        h = rms_norm(x, g_mix_norm[layer])
        proj = jnp.dot(h, w_in[layer])
        qa, ka, va, qb, kb, vb, qm = jnp.split(proj, split_at, axis=-1)
        out_a = neighbourhood_attention(qa, ka, va, rpb[layer])
        out_b = dilated_attention(qb, kb, vb)
        out_m = memory_attention(qm, mem, g_mem_norm[layer], w_mem_kv[layer])
        gm = g_mix_out[layer]
        y = jnp.concatenate([
            rms_norm(out_a, gm[:NA_WIDTH]),
            rms_norm(out_b, gm[NA_WIDTH:NA_WIDTH + DIL_WIDTH]),
            rms_norm(out_m, gm[NA_WIDTH + DIL_WIDTH:]),
        ], axis=-1)
        x = x + jnp.dot(y, w_out[layer])
        h = rms_norm(x, g_ffn_norm[layer]).reshape(bsz * t, d)
        if layer % 2 == 0:
            i = layer // 2
            f = swiglu(h, w_dense_gate[i], w_dense_up[i], w_dense_down[i])
        else:
            i = layer // 2
            f = moe_swiglu(h, w_router[i], w_moe_gate[i], w_moe_up[i], w_moe_down[i])
        x = x + f.reshape(bsz, t, d)
    return rms_norm(x, g_final)
```

```python
import functools
import math

import jax
import jax.numpy as jnp
from jax import lax
from jax.experimental import pallas as pl
from jax.experimental.pallas import tpu as pltpu

HEAD_DIM = 64
NA_HEADS = 6
DIL_HEADS = 6
MEM_HEADS = 4
NA_WIDTH = NA_HEADS * HEAD_DIM
DIL_WIDTH = DIL_HEADS * HEAD_DIM
MEM_WIDTH = MEM_HEADS * HEAD_DIM
GRID_W = 64
NA_WIN_ROWS = 8
NA_WIN_COLS = 16
DIL_BRANCHES = ((128, 1), (512, 4), (2048, 16))
ROPE_THETA = 500000.0
ROT_DIM = HEAD_DIM // 4
TOP_K = 2
RMS_EPS = 1e-6
ATTN_SCALE = HEAD_DIM ** -0.5

LANES = 128
HEADS_PER_LANE_GROUP = LANES // HEAD_DIM
NEG = -1e30
VMEM_LIMIT = 56 * 1024 * 1024

F32 = jnp.float32
BF16 = jnp.bfloat16


def _cparams(sem):
    return pltpu.CompilerParams(dimension_semantics=sem, vmem_limit_bytes=VMEM_LIMIT)


def _rms(x, g):
    ms = jnp.mean(x * x, axis=-1, keepdims=True)
    return x * lax.rsqrt(ms + RMS_EPS) * g


def _pair_heads_attention(q, k, v, bias_fn, n_heads):
    m_rows = q.shape[0]
    lane = lax.broadcasted_iota(jnp.int32, (m_rows, LANES), 1)
    outs, lses = [], []
    for p in range(n_heads // HEADS_PER_LANE_GROUP):
        sl = slice(p * LANES, (p + 1) * LANES)
        qp, kp, vp = q[:, sl], k[:, sl], v[:, sl]
        o_pair = jnp.zeros((m_rows, LANES), F32)
        for hh in range(HEADS_PER_LANE_GROUP):
            sel = (lane >= hh * HEAD_DIM) & (lane < (hh + 1) * HEAD_DIM)
            qm = jnp.where(sel, qp, jnp.zeros_like(qp))
            s = lax.dot_general(qm, kp, (((1,), (1,)), ((), ())), preferred_element_type=F32)
            s = s + bias_fn(p * HEADS_PER_LANE_GROUP + hh)
            mx = jnp.max(s, axis=-1, keepdims=True)
            e = jnp.exp(s - mx)
            den = jnp.sum(e, axis=-1, keepdims=True)
            o = jnp.dot(e.astype(BF16), vp, preferred_element_type=F32) / den
            o_pair = jnp.where(sel, o, o_pair)
            lses.append(mx + jnp.log(den))
        outs.append(o_pair)
    return jnp.concatenate(outs, axis=-1), lses


def _inproj_kernel(x_ref, g_ref, w_ref, cos_ref, sa_ref, sb_ref,
                   qa_ref, ka_ref, va_ref, qb_ref, kb_ref, vb_ref, qm_ref):
    h = _rms(x_ref[...], g_ref[...]).astype(BF16)

    def proj(c0, width):
        return jnp.dot(h, w_ref[:, c0:c0 + width], preferred_element_type=F32)

    def rope(a):
        cos, sa, sb = cos_ref[...], sa_ref[...], sb_ref[...]
        half = ROT_DIM // 2
        parts = []
        for gidx in range(a.shape[1] // LANES):
            xg = a[:, gidx * LANES:(gidx + 1) * LANES]
            parts.append(xg * cos + pltpu.roll(xg, LANES - half, 1) * sa + pltpu.roll(xg, half, 1) * sb)
        return jnp.concatenate(parts, axis=-1)

    c = 0
    qa_ref[...] = (proj(c, NA_WIDTH) * ATTN_SCALE).astype(BF16); c += NA_WIDTH
    ka_ref[...] = proj(c, NA_WIDTH).astype(BF16); c += NA_WIDTH
    va_ref[...] = proj(c, NA_WIDTH).astype(BF16); c += NA_WIDTH
    qb_ref[...] = rope(proj(c, DIL_WIDTH) * ATTN_SCALE).astype(BF16); c += DIL_WIDTH
    kb_ref[...] = rope(proj(c, DIL_WIDTH)).astype(BF16); c += DIL_WIDTH
    vb_ref[...] = proj(c, DIL_WIDTH).astype(BF16); c += DIL_WIDTH
    qm_ref[...] = (proj(c, MEM_WIDTH) * ATTN_SCALE).astype(BF16)


def _rope_tables(t):
    half = ROT_DIM // 2
    inv_freq = ROPE_THETA ** (-jnp.arange(0, ROT_DIM, 2, dtype=F32) / ROT_DIM)
    ang = jnp.arange(t, dtype=F32)[:, None] * inv_freq[None, :]
    cos, sin = jnp.cos(ang), jnp.sin(ang)
    ones = jnp.ones((t, HEAD_DIM - ROT_DIM), F32)
    zeros = jnp.zeros((t, HEAD_DIM - ROT_DIM), F32)
    zh = jnp.zeros((t, half), F32)
    cos_h = jnp.concatenate([cos, cos, ones], axis=1)
    sa_h = jnp.concatenate([-sin, zh, zeros], axis=1)
    sb_h = jnp.concatenate([zh, sin, zeros], axis=1)
    tile = lambda a: jnp.tile(a, (1, HEADS_PER_LANE_GROUP))
    return tile(cos_h), tile(sa_h), tile(sb_h)


def _inproj(x2, g, w, tables, t, tm):
    n, d = x2.shape
    tpb = t // tm
    row = lambda width: pl.BlockSpec((tm, width), lambda i: (i, 0))
    tab = pl.BlockSpec((tm, LANES), lambda i: (i % tpb, 0))
    widths = [NA_WIDTH] * 3 + [DIL_WIDTH] * 3 + [MEM_WIDTH]
    return pl.pallas_call(
        _inproj_kernel,
        grid=(n // tm,),
        in_specs=[row(d), pl.BlockSpec((1, d), lambda i: (0, 0)),
                  pl.BlockSpec(w.shape, lambda i: (0, 0)), tab, tab, tab],
        out_specs=[row(wd) for wd in widths],
        out_shape=[jax.ShapeDtypeStruct((n, wd), BF16) for wd in widths],
        compiler_params=_cparams(("parallel",)),
        name="inproj",
    )(x2, g, w, *tables)


def _na_bias_table(rpb):
    c = jnp.arange(GRID_W)
    c0 = jnp.clip(c - NA_WIN_COLS // 2, 0, GRID_W - NA_WIN_COLS)
    kc = jnp.arange(GRID_W)
    valid = (kc[None, :] >= c0[:, None]) & (kc[None, :] < c0[:, None] + NA_WIN_COLS)
    coff = jnp.clip(kc[None, :] - c[:, None] + (NA_WIN_COLS - 1), 0, 2 * NA_WIN_COLS - 2)
    d = jnp.arange(NA_WIN_ROWS)
    wr = jnp.arange(NA_WIN_ROWS)
    roff = d[:, None] + wr[None, :]
    tab = rpb[:, roff[:, None, :, None], coff[None, :, None, :]]
    tab = jnp.where(valid[None, None, :, None, :], tab.astype(F32), NEG)
    return tab.reshape(NA_HEADS * NA_WIN_ROWS, GRID_W, NA_WIN_ROWS * GRID_W)


def _na_kernel(q_ref, kp_ref, kc_ref, kn_ref, vp_ref, vc_ref, vn_ref, bias_ref, o_ref,
               kwin, vwin, *, rows):
    j = pl.program_id(1)
    blk = NA_WIN_ROWS * GRID_W
    for idx, (kr, vr) in enumerate(((kp_ref, vp_ref), (kc_ref, vc_ref), (kn_ref, vn_ref))):
        kwin[idx * blk:(idx + 1) * blk, :] = kr[...]
        vwin[idx * blk:(idx + 1) * blk, :] = vr[...]

    def row_body(i, carry):
        r = j * NA_WIN_ROWS + i
        r0 = jnp.clip(r - NA_WIN_ROWS // 2, 0, rows - NA_WIN_ROWS)
        dlt = r0 - r + (NA_WIN_ROWS - 1)
        start = pl.multiple_of((r0 - (j - 1) * NA_WIN_ROWS) * GRID_W, GRID_W)
        q = q_ref[pl.ds(pl.multiple_of(i * GRID_W, GRID_W), GRID_W), :]
        k = kwin[pl.ds(start, blk), :]
        v = vwin[pl.ds(start, blk), :]
        o, _ = _pair_heads_attention(q, k, v, lambda h: bias_ref[h * NA_WIN_ROWS + dlt], NA_HEADS)
        o_ref[pl.ds(pl.multiple_of(i * GRID_W, GRID_W), GRID_W), :] = o.astype(BF16)
        return carry

    lax.fori_loop(0, NA_WIN_ROWS, row_body, 0)


def _na_attention(q, k, v, bias):
    b, t, w = q.shape
    rows = t // GRID_W
    assert rows % NA_WIN_ROWS == 0 and rows >= NA_WIN_ROWS
    nj = rows // NA_WIN_ROWS
    blk = NA_WIN_ROWS * GRID_W
    cur = pl.BlockSpec((None, blk, w), lambda bi, j: (bi, j, 0))
    prev = pl.BlockSpec((None, blk, w), lambda bi, j: (bi, jnp.maximum(j - 1, 0), 0))
    nxt = pl.BlockSpec((None, blk, w), lambda bi, j: (bi, jnp.minimum(j + 1, nj - 1), 0))
    return pl.pallas_call(
        functools.partial(_na_kernel, rows=rows),
        grid=(b, nj),
        in_specs=[cur, prev, cur, nxt, prev, cur, nxt,
                  pl.BlockSpec(bias.shape, lambda bi, j: (0, 0, 0))],
        out_specs=cur,
        out_shape=jax.ShapeDtypeStruct((b, t, w), BF16),
        scratch_shapes=[pltpu.VMEM((3 * blk, w), BF16), pltpu.VMEM((3 * blk, w), BF16)],
        compiler_params=_cparams(("parallel", "parallel")),
        name="na_attn",
    )(q, k, k, k, v, v, v, bias)


def _dil_kernel(q_ref, kp_ref, kc_ref, kn_ref, vp_ref, vc_ref, vn_ref, o_ref, lse_ref,
                kwin, vwin, *, seg, tq, halo):
    i = pl.program_id(2)
    width = tq + 2 * halo
    kwin[0:halo, :] = kp_ref[...]
    kwin[halo:halo + tq, :] = kc_ref[...]
    kwin[halo + tq:width, :] = kn_ref[...]
    vwin[0:halo, :] = vp_ref[...]
    vwin[halo:halo + tq, :] = vc_ref[...]
    vwin[halo + tq:width, :] = vn_ref[...]

    a = lax.broadcasted_iota(jnp.int32, (tq, width), 0)
    c = lax.broadcasted_iota(jnp.int32, (tq, width), 1)
    kpos = i * tq - halo + c
    valid = (c >= a) & (c <= a + 2 * halo) & (kpos >= 0) & (kpos < seg)
    negb = jnp.where(valid, 0.0, NEG).astype(F32)

    o, lses = _pair_heads_attention(q_ref[...], kwin[...], vwin[...], lambda h: negb, DIL_HEADS)
    o_ref[...] = o.astype(BF16)
    lane = lax.broadcasted_iota(jnp.int32, (tq, LANES), 1)
    lse_tile = jnp.zeros((tq, LANES), F32)
    for h, lse in enumerate(lses):
        lse_tile = jnp.where(lane == h, lse, lse_tile)
    lse_ref[...] = lse_tile


def _dilated_branch(q, k, v, dil, n_side, tq_max):
    b, t, w = q.shape
    seg = t // dil
    halo = n_side
    assert seg % halo == 0 and halo % 16 == 0
    tq = math.gcd(tq_max, seg)
    assert tq % halo == 0
    hb = tq // halo
    nhalo = seg // halo
    view = lambda a: a.reshape(b, seg, dil * w)
    cur = pl.BlockSpec((None, tq, w), lambda bi, r, i: (bi, i, r))
    prev = pl.BlockSpec((None, halo, w), lambda bi, r, i: (bi, jnp.maximum(i * hb - 1, 0), r))
    nxt = pl.BlockSpec((None, halo, w), lambda bi, r, i: (bi, jnp.minimum((i + 1) * hb, nhalo - 1), r))
    qv, kv, vv = view(q), view(k), view(v)
    o, lse = pl.pallas_call(
        functools.partial(_dil_kernel, seg=seg, tq=tq, halo=halo),
        grid=(b, dil, seg // tq),
        in_specs=[cur, prev, cur, nxt, prev, cur, nxt],
        out_specs=[cur, pl.BlockSpec((None, tq, LANES), lambda bi, r, i: (bi, i, r))],
        out_shape=[jax.ShapeDtypeStruct((b, seg, dil * w), BF16),
                   jax.ShapeDtypeStruct((b, seg, dil * LANES), F32)],
        scratch_shapes=[pltpu.VMEM((tq + 2 * halo, w), BF16), pltpu.VMEM((tq + 2 * halo, w), BF16)],
        compiler_params=_cparams(("parallel", "parallel", "parallel")),
        name=f"dilated_d{dil}",
    )(qv, kv, kv, kv, vv, vv, vv)
    return o.reshape(b * t, w), lse.reshape(b * t, LANES)


def _memkv_kernel(mem_ref, g_ref, w_ref, k_ref, v_ref):
    h = _rms(mem_ref[...], g_ref[...]).astype(BF16)
    kv = jnp.dot(h, w_ref[...], preferred_element_type=F32)
    k_ref[...] = kv[:, :MEM_WIDTH].astype(BF16)
    v_ref[...] = kv[:, MEM_WIDTH:].astype(BF16)


def _mem_kv(mem, g, w):
    b, m, d = mem.shape
    blk = lambda width: pl.BlockSpec((None, m, width), lambda bi: (bi, 0, 0))
    return pl.pallas_call(
        _memkv_kernel,
        grid=(b,),
        in_specs=[blk(d), pl.BlockSpec((1, d), lambda bi: (0, 0)), pl.BlockSpec(w.shape, lambda bi: (0, 0))],
        out_specs=[blk(MEM_WIDTH), blk(MEM_WIDTH)],
        out_shape=[jax.ShapeDtypeStruct((b, m, MEM_WIDTH), BF16)] * 2,
        compiler_params=_cparams(("parallel",)),
        name="mem_kv",
    )(mem, g, w)


def _memattn_kernel(q_ref, k_ref, v_ref, o_ref):
    o, _ = _pair_heads_attention(q_ref[...], k_ref[...], v_ref[...], lambda h: 0.0, MEM_HEADS)
    o_ref[...] = o.astype(BF16)


def _mem_attention(q, k, v, tq):
    b, t, w = q.shape
    m = k.shape[1]
    qs = pl.BlockSpec((None, tq, w), lambda bi, i: (bi, i, 0))
    ks = pl.BlockSpec((None, m, w), lambda bi, i: (bi, 0, 0))
    return pl.pallas_call(
        _memattn_kernel,
        grid=(b, t // tq),
        in_specs=[qs, ks, ks],
        out_specs=qs,
        out_shape=jax.ShapeDtypeStruct((b, t, w), BF16),
        compiler_params=_cparams(("parallel", "parallel")),
        name="mem_attn",
    )(q, k, v)


def _branch_mix(ob_refs, lse_refs, expand_ref):
    lses = [r[...] for r in lse_refs]
    mx = functools.reduce(jnp.maximum, lses)
    es = [jnp.exp(l - mx) for l in lses]
    den = functools.reduce(lambda p, q: p + q, es)
    acc = None
    for e, ob in zip(es, ob_refs):
        wgt = jnp.dot(e / den, expand_ref[...], precision=lax.Precision.HIGHEST,
                      preferred_element_type=F32)
        term = wgt * ob[...].astype(F32)
        acc = term if acc is None else acc + term
    return acc


def _outproj_core(oa_ref, ob_refs, lse_refs, om_ref, x_ref, gm_ref, w_ref, expand_ref):
    gm = gm_ref[...]
    e0, e1 = NA_WIDTH, NA_WIDTH + DIL_WIDTH
    ya = _rms(oa_ref[...].astype(F32), gm[:, :e0]).astype(BF16)
    yb = _rms(_branch_mix(ob_refs, lse_refs, expand_ref), gm[:, e0:e1]).astype(BF16)
    ym = _rms(om_ref[...].astype(F32), gm[:, e1:]).astype(BF16)
    acc = jnp.dot(ya, w_ref[0:e0, :], preferred_element_type=F32)
    acc = acc + jnp.dot(yb, w_ref[e0:e1, :], preferred_element_type=F32)
    acc = acc + jnp.dot(ym, w_ref[e1:, :], preferred_element_type=F32)
    return x_ref[...] + acc


def _outproj_kernel(oa_ref, ob1, ob2, ob3, l1, l2, l3, om_ref, x_ref, gm_ref, w_ref, expand_ref, xo_ref):
    xo_ref[...] = _outproj_core(oa_ref, (ob1, ob2, ob3), (l1, l2, l3), om_ref, x_ref, gm_ref, w_ref, expand_ref)


def _outproj_router_kernel(oa_ref, ob1, ob2, ob3, l1, l2, l3, om_ref, x_ref, gm_ref, w_ref, expand_ref,
                           gf_ref, wr_ref, xo_ref, h_ref, route_ref, *, n_experts):
    xn = _outproj_core(oa_ref, (ob1, ob2, ob3), (l1, l2, l3), om_ref, x_ref, gm_ref, w_ref, expand_ref)
    xo_ref[...] = xn
    h = _rms(xn, gf_ref[...])
    h_ref[...] = h
    logits = jnp.dot(h, wr_ref[...], precision=lax.Precision.HIGHEST, preferred_element_type=F32)
    lane = lax.broadcasted_iota(jnp.int32, logits.shape, 1)
    lg = jnp.where(lane < n_experts, logits, NEG)
    m1 = jnp.max(lg, axis=-1, keepdims=True)
    i1 = jnp.min(jnp.where(lg == m1, lane, LANES), axis=-1, keepdims=True)
    lg2 = jnp.where(lane == i1, NEG, lg)
    m2 = jnp.max(lg2, axis=-1, keepdims=True)
    i2 = jnp.min(jnp.where(lg2 == m2, lane, LANES), axis=-1, keepdims=True)
    e2 = jnp.exp(m2 - m1)
    g1 = 1.0 / (1.0 + e2)
    g2 = e2 / (1.0 + e2)
    route = jnp.where(lane == 0, i1.astype(F32), 0.0)
    route = jnp.where(lane == 1, i2.astype(F32), route)
    route = jnp.where(lane == 2, g1, route)
    route = jnp.where(lane == 3, g2, route)
    route_ref[...] = route


def _head_expand_matrix():
    hd = jnp.arange(DIL_WIDTH) // HEAD_DIM
    return (jnp.arange(LANES)[:, None] == hd[None, :]).astype(F32)


def _outproj(oa, obs, lses, om, x2, gm, w, tm, router=None):
    n, d = x2.shape
    row = lambda width: pl.BlockSpec((tm, width), lambda i: (i, 0))
    full = lambda a: pl.BlockSpec(a.shape, lambda i: (0,) * a.ndim)
    expand = _head_expand_matrix()
    args = [oa, *obs, *lses, om, x2, gm, w, expand]
    in_specs = [row(NA_WIDTH)] + [row(DIL_WIDTH)] * 3 + [row(LANES)] * 3 + [row(MEM_WIDTH), row(d),
                full(gm), full(w), full(expand)]
    if router is None:
        return pl.pallas_call(
            _outproj_kernel, grid=(n // tm,), in_specs=in_specs, out_specs=row(d),
            out_shape=jax.ShapeDtypeStruct((n, d), F32),
            compiler_params=_cparams(("parallel",)), name="outproj",
        )(*args)
    gf, wr, n_experts = router
    return pl.pallas_call(
        functools.partial(_outproj_router_kernel, n_experts=n_experts),
        grid=(n // tm,), in_specs=in_specs + [full(gf), full(wr)],
        out_specs=[row(d), row(d), row(LANES)],
        out_shape=[jax.ShapeDtypeStruct((n, d), F32), jax.ShapeDtypeStruct((n, d), F32),
                   jax.ShapeDtypeStruct((n, LANES), F32)],
        compiler_params=_cparams(("parallel",)), name="outproj_router",
    )(*args, gf, wr)


def _swiglu_step(h, wg_ref, wu_ref, wd_ref):
    gate = jnp.dot(h, wg_ref[...], preferred_element_type=F32)
    up = jnp.dot(h, wu_ref[...], preferred_element_type=F32)
    act = (gate / (1.0 + jnp.exp(-gate))) * up
    return jnp.dot(act.astype(BF16), wd_ref[...], preferred_element_type=F32)


def _dense_ffn_kernel(x_ref, g_ref, wg_ref, wu_ref, wd_ref, o_ref, h_scr, acc_scr):
    k = pl.program_id(1)

    @pl.when(k == 0)
    def _():
        h_scr[...] = _rms(x_ref[...], g_ref[...]).astype(BF16)
        acc_scr[...] = jnp.zeros_like(acc_scr)

    acc_scr[...] += _swiglu_step(h_scr[...], wg_ref, wu_ref, wd_ref)

    @pl.when(k == pl.num_programs(1) - 1)
    def _():
        o_ref[...] = x_ref[...] + acc_scr[...]


def _dense_ffn(x2, g, wg, wu, wd, tm, tf):
    n, d = x2.shape
    f = wg.shape[1]
    row = pl.BlockSpec((tm, d), lambda i, k: (i, 0))
    return pl.pallas_call(
        _dense_ffn_kernel,
        grid=(n // tm, f // tf),
        in_specs=[row, pl.BlockSpec((1, d), lambda i, k: (0, 0)),
                  pl.BlockSpec((d, tf), lambda i, k: (0, k)),
                  pl.BlockSpec((d, tf), lambda i, k: (0, k)),
                  pl.BlockSpec((tf, d), lambda i, k: (k, 0))],
        out_specs=row,
        out_shape=jax.ShapeDtypeStruct((n, d), F32),
        scratch_shapes=[pltpu.VMEM((tm, d), BF16), pltpu.VMEM((tm, d), F32)],
        compiler_params=_cparams(("parallel", "arbitrary")),
        name="dense_ffn",
    )(x2, g, wg, wu, wd)


def _moe_ffn_kernel(be_ref, nused_ref, xb_ref, wg_ref, wu_ref, wd_ref, y_ref, h_scr, acc_scr):
    j = pl.program_id(0)
    k = pl.program_id(1)
    last = pl.num_programs(1) - 1
    used = j < nused_ref[0]

    @pl.when(used & (k == 0))
    def _():
        h_scr[...] = xb_ref[...].astype(BF16)
        acc_scr[...] = jnp.zeros_like(acc_scr)

    @pl.when(used)
    def _():
        acc_scr[...] += _swiglu_step(h_scr[...], wg_ref, wu_ref, wd_ref)

    @pl.when(used & (k == last))
    def _():
        y_ref[...] = acc_scr[...]

    @pl.when(jnp.logical_not(used) & (k == last))
    def _():
        y_ref[...] = jnp.zeros_like(y_ref)


def _moe_ffn(buf, block_expert, n_used, wg, wu, wd, mb, tf):
    p, d = buf.shape
    f = wg.shape[2]
    nk = f // tf
    kk = lambda j, k, be, nu: jnp.where(j < nu[0], k, nk - 1)
    row = pl.BlockSpec((mb, d), lambda j, k, be, nu: (j, 0))
    grid_spec = pltpu.PrefetchScalarGridSpec(
        num_scalar_prefetch=2,
        grid=(p // mb, nk),
        in_specs=[row,
                  pl.BlockSpec((None, d, tf), lambda j, k, be, nu: (be[j], 0, kk(j, k, be, nu))),
                  pl.BlockSpec((None, d, tf), lambda j, k, be, nu: (be[j], 0, kk(j, k, be, nu))),
                  pl.BlockSpec((None, tf, d), lambda j, k, be, nu: (be[j], kk(j, k, be, nu), 0))],
        out_specs=row,
        scratch_shapes=[pltpu.VMEM((mb, d), BF16), pltpu.VMEM((mb, d), F32)],
    )
    return pl.pallas_call(
        _moe_ffn_kernel,
        grid_spec=grid_spec,
        out_shape=jax.ShapeDtypeStruct((p, d), F32),
        compiler_params=_cparams(("parallel", "arbitrary")),
        name="moe_ffn",
    )(block_expert, n_used, buf, wg, wu, wd)


def _gather_kernel(idx_ref, src_ref, out_ref, sem, *, chunk):
    base = pl.program_id(0) * chunk

    def issue(t, carry):
        pltpu.make_async_copy(src_ref.at[pl.ds(idx_ref[0, 0, t], 1)],
                              out_ref.at[pl.ds(base + t, 1)], sem).start()
        return carry

    lax.fori_loop(0, chunk, issue, 0)
    pltpu.make_async_copy(src_ref.at[pl.ds(0, chunk)], out_ref.at[pl.ds(base, chunk)], sem).wait()


def _gather_rows(src, idx, chunk):
    m = idx.shape[0]
    d = src.shape[1]
    chunk = math.gcd(chunk, m)
    idx3 = idx.reshape(m // chunk, 1, chunk)
    return pl.pallas_call(
        functools.partial(_gather_kernel, chunk=chunk),
        grid=(m // chunk,),
        in_specs=[pl.BlockSpec((1, 1, chunk), lambda i: (i, 0, 0), memory_space=pltpu.SMEM),
                  pl.BlockSpec(memory_space=pl.ANY)],
        out_specs=pl.BlockSpec(memory_space=pl.ANY),
        out_shape=jax.ShapeDtypeStruct((m, d), src.dtype),
        scratch_shapes=[pltpu.SemaphoreType.DMA(())],
        compiler_params=_cparams(("arbitrary",)),
        name="gather_rows",
    )(idx3, src)


def _combine_kernel(x_ref, y0_ref, y1_ref, route_ref, o_ref):
    route = route_ref[...]
    o_ref[...] = x_ref[...] + (route[:, 2:3] * y0_ref[...] + route[:, 3:4] * y1_ref[...])


def _combine_norm_kernel(x_ref, y0_ref, y1_ref, route_ref, g_ref, o_ref):
    route = route_ref[...]
    xn = x_ref[...] + (route[:, 2:3] * y0_ref[...] + route[:, 3:4] * y1_ref[...])
    o_ref[...] = _rms(xn, g_ref[...])


def _combine(x2, y0, y1, route, tm, g_final=None):
    n, d = x2.shape
    row = pl.BlockSpec((tm, d), lambda i: (i, 0))
    in_specs = [row, row, row, pl.BlockSpec((tm, LANES), lambda i: (i, 0))]
    args = [x2, y0, y1, route]
    kern = _combine_kernel
    if g_final is not None:
        in_specs.append(pl.BlockSpec((1, d), lambda i: (0, 0)))
        args.append(g_final)
        kern = _combine_norm_kernel
    return pl.pallas_call(
        kern, grid=(n // tm,), in_specs=in_specs, out_specs=row,
        out_shape=jax.ShapeDtypeStruct((n, d), F32),
        compiler_params=_cparams(("parallel",)), name="moe_combine",
    )(*args)


def _final_norm_kernel(x_ref, g_ref, o_ref):
    o_ref[...] = _rms(x_ref[...], g_ref[...])


def _final_norm(x2, g, tm):
    n, d = x2.shape
    row = pl.BlockSpec((tm, d), lambda i: (i, 0))
    return pl.pallas_call(
        _final_norm_kernel, grid=(n // tm,), in_specs=[row, pl.BlockSpec((1, d), lambda i: (0, 0))],
        out_specs=row, out_shape=jax.ShapeDtypeStruct((n, d), F32),
        compiler_params=_cparams(("parallel",)), name="final_norm",
    )(x2, g)


def _routing_plan(route, n_experts, mb):
    n = route.shape[0]
    n_assign = n * TOP_K
    flat_e = route[:, :TOP_K].astype(jnp.int32).reshape(-1)
    onehot = (flat_e[:, None] == jnp.arange(n_experts)[None, :]).astype(jnp.int32)
    csum = jnp.cumsum(onehot, axis=0)
    rank = jnp.sum(csum * onehot, axis=1) - 1
    counts = csum[-1]
    padded = (counts + mb - 1) // mb * mb
    pend = jnp.cumsum(padded)
    pstart = pend - padded
    dest = jnp.sum(pstart[None, :] * onehot, axis=1) + rank
    n_blocks = -(-n_assign // mb) + n_experts
    block_first = jnp.arange(n_blocks) * mb
    block_expert = jnp.minimum(jnp.sum(block_first[:, None] >= pend[None, :], axis=1), n_experts - 1)
    n_used = (pend[-1] // mb).reshape(1)
    src_tok = jnp.zeros((n_blocks * mb,), jnp.int32).at[dest].set(jnp.arange(n_assign, dtype=jnp.int32) // TOP_K)
    return dest.reshape(n, TOP_K), src_tok, block_expert.astype(jnp.int32), n_used.astype(jnp.int32)


def _tiles(n, t):
    return dict(tm_proj=math.gcd(512, t), tm_ffn=math.gcd(1024, n), tf=512, tq_dil=256,
                tq_mem=math.gcd(512, t), moe_block=512, gather_chunk=2048)


@jax.jit
def _forward(x, mem, g_mix_norm, w_in, rpb, g_mem_norm, w_mem_kv, g_mix_out, w_out,
             g_ffn_norm, w_dense_gate, w_dense_up, w_dense_down, w_router,
             w_moe_gate, w_moe_up, w_moe_down, g_final):
    b, t, d = x.shape
    n = b * t
    depth = w_in.shape[0]
    n_experts = w_router.shape[-1]
    ts = _tiles(n, t)
    tf = math.gcd(ts["tf"], w_dense_gate.shape[-1])
    tables = _rope_tables(t)
    row1 = lambda a: a.reshape(1, -1)
    x2 = x.reshape(n, d)
    out = None
    for layer in range(depth):
        qa, ka, va, qb, kb, vb, qm = _inproj(x2, row1(g_mix_norm[layer]), w_in[layer].astype(BF16),
                                             tables, t, ts["tm_proj"])
        shp = lambda a: a.reshape(b, t, a.shape[-1])
        oa = _na_attention(shp(qa), shp(ka), shp(va), _na_bias_table(rpb[layer])).reshape(n, NA_WIDTH)
        obs, lses = [], []
        for window, dil in DIL_BRANCHES:
            o, lse = _dilated_branch(shp(qb), shp(kb), shp(vb), dil, window // (2 * dil), ts["tq_dil"])
            obs.append(o)
            lses.append(lse)
        km, vm = _mem_kv(mem, row1(g_mem_norm[layer]), w_mem_kv[layer].astype(BF16))
        om = _mem_attention(shp(qm), km, vm, ts["tq_mem"]).reshape(n, MEM_WIDTH)
        gm, wo = row1(g_mix_out[layer]), w_out[layer].astype(BF16)
        i = layer // 2
        if layer % 2 == 0:
            x2 = _outproj(oa, obs, lses, om, x2, gm, wo, ts["tm_proj"])
            x2 = _dense_ffn(x2, row1(g_ffn_norm[layer]), w_dense_gate[i].astype(BF16),
                            w_dense_up[i].astype(BF16), w_dense_down[i].astype(BF16), ts["tm_ffn"], tf)
            if layer == depth - 1:
                out = _final_norm(x2, row1(g_final), ts["tm_proj"])
        else:
            wr = jnp.pad(w_router[i], ((0, 0), (0, LANES - n_experts)))
            x2, h, route = _outproj(oa, obs, lses, om, x2, gm, wo, ts["tm_proj"],
                                    router=(row1(g_ffn_norm[layer]), wr, n_experts))
            mb = ts["moe_block"]
            dest, src_tok, block_expert, n_used = _routing_plan(route, n_experts, mb)
            buf = _gather_rows(h, src_tok, ts["gather_chunk"])
            y = _moe_ffn(buf, block_expert, n_used, w_moe_gate[i].astype(BF16), w_moe_up[i].astype(BF16),
                         w_moe_down[i].astype(BF16), mb, tf)
            y0 = _gather_rows(y, dest[:, 0], ts["gather_chunk"])
            y1 = _gather_rows(y, dest[:, 1], ts["gather_chunk"])
            if layer == depth - 1:
                out = _combine(x2, y0, y1, route, ts["tm_proj"], g_final=row1(g_final))
            else:
                x2 = _combine(x2, y0, y1, route, ts["tm_proj"])
    return out.reshape(b, t, d)


def kernel(x, mem, g_mix_norm, w_in, rpb, g_mem_norm, w_mem_kv, g_mix_out, w_out, g_ffn_norm,
           w_dense_gate, w_dense_up, w_dense_down, w_router, w_moe_gate, w_moe_up, w_moe_down, g_final):
    return _forward(x, mem, g_mix_norm, w_in, rpb, g_mem_norm, w_mem_kv, g_mix_out, w_out, g_ffn_norm,
                    w_dense_gate, w_dense_up, w_dense_down, w_router, w_moe_gate, w_moe_up, w_moe_down,
                    g_final)
```

```python
import functools
import math

import jax
import jax.numpy as jnp
from jax import lax
from jax.experimental import pallas as pl
from jax.experimental.pallas import tpu as pltpu

HEAD_DIM = 64
NA_HEADS = 6
DIL_HEADS = 6
MEM_HEADS = 4
NA_WIDTH = NA_HEADS * HEAD_DIM
DIL_WIDTH = DIL_HEADS * HEAD_DIM
MEM_WIDTH = MEM_HEADS * HEAD_DIM
GRID_W = 64
NA_WIN_ROWS = 8
NA_WIN_COLS = 16
DIL_BRANCHES = ((128, 1), (512, 4), (2048, 16))
ROPE_THETA = 500000.0
ROT_DIM = HEAD_DIM // 4
TOP_K = 2
RMS_EPS = 1e-6
ATTN_SCALE = HEAD_DIM ** -0.5

LANES = 128
HEADS_PER_LANE_GROUP = LANES // HEAD_DIM
NEG = -1e30
VMEM_LIMIT = 56 * 1024 * 1024

F32 = jnp.float32
BF16 = jnp.bfloat16


def _cparams(sem):
    return pltpu.CompilerParams(dimension_semantics=sem, vmem_limit_bytes=VMEM_LIMIT)


def _rms(x, g):
    ms = jnp.mean(x * x, axis=-1, keepdims=True)
    return x * lax.rsqrt(ms + RMS_EPS) * g


def _pair_heads_attention(q, k, v, bias_fn, n_heads):
    m_rows = q.shape[0]
    lane = lax.broadcasted_iota(jnp.int32, (m_rows, LANES), 1)
    outs, lses = [], []
    for p in range(n_heads // HEADS_PER_LANE_GROUP):
        sl = slice(p * LANES, (p + 1) * LANES)
        qp, kp, vp = q[:, sl], k[:, sl], v[:, sl]
        o_pair = jnp.zeros((m_rows, LANES), F32)
        for hh in range(HEADS_PER_LANE_GROUP):
            sel = (lane >= hh * HEAD_DIM) & (lane < (hh + 1) * HEAD_DIM)
            qm = jnp.where(sel, qp, jnp.zeros_like(qp))
            s = lax.dot_general(qm, kp, (((1,), (1,)), ((), ())), preferred_element_type=F32)
            s = s + bias_fn(p * HEADS_PER_LANE_GROUP + hh)
            mx = jnp.max(s, axis=-1, keepdims=True)
            e = jnp.exp(s - mx)
            den = jnp.sum(e, axis=-1, keepdims=True)
            o = jnp.dot(e.astype(BF16), vp, preferred_element_type=F32) / den
            o_pair = jnp.where(sel, o, o_pair)
            lses.append(mx + jnp.log(den))
        outs.append(o_pair)
    return jnp.concatenate(outs, axis=-1), lses


def _inproj_kernel(x_ref, g_ref, w_ref, cos_ref, sa_ref, sb_ref,
                   qa_ref, ka_ref, va_ref, qm_ref, *rest, dils):
    dil_refs, scr = rest[:-1], rest[-1]
    tm = x_ref.shape[0]
    h = _rms(x_ref[...], g_ref[...]).astype(BF16)

    def emit_dilated(val, refs):
        groups = val.shape[1] // LANES
        for g in range(groups):
            scr[g] = val[:, g * LANES:(g + 1) * LANES]
        for dil, ref in zip(dils, refs):
            for r in range(dil):
                for g in range(groups):
                    ref[r, :, g * LANES:(g + 1) * LANES] = (
                        scr[g, pl.ds(r, tm // dil, stride=dil), :].astype(BF16))

    def proj(c0, width):
        return jnp.dot(h, w_ref[:, c0:c0 + width], preferred_element_type=F32)

    def rope(a):
        cos, sa, sb = cos_ref[...], sa_ref[...], sb_ref[...]
        half = ROT_DIM // 2
        parts = []
        for gidx in range(a.shape[1] // LANES):
            xg = a[:, gidx * LANES:(gidx + 1) * LANES]
            parts.append(xg * cos + pltpu.roll(xg, LANES - half, 1) * sa + pltpu.roll(xg, half, 1) * sb)
        return jnp.concatenate(parts, axis=-1)

    nd = len(dils)
    c = 0
    qa_ref[...] = (proj(c, NA_WIDTH) * ATTN_SCALE).astype(BF16); c += NA_WIDTH
    ka_ref[...] = proj(c, NA_WIDTH).astype(BF16); c += NA_WIDTH
    va_ref[...] = proj(c, NA_WIDTH).astype(BF16); c += NA_WIDTH
    emit_dilated(rope(proj(c, DIL_WIDTH) * ATTN_SCALE), dil_refs[0:nd]); c += DIL_WIDTH
    emit_dilated(rope(proj(c, DIL_WIDTH)), dil_refs[nd:2 * nd]); c += DIL_WIDTH
    emit_dilated(proj(c, DIL_WIDTH), dil_refs[2 * nd:3 * nd]); c += DIL_WIDTH
    qm_ref[...] = (proj(c, MEM_WIDTH) * ATTN_SCALE).astype(BF16)


def _rope_tables(t):
    half = ROT_DIM // 2
    inv_freq = ROPE_THETA ** (-jnp.arange(0, ROT_DIM, 2, dtype=F32) / ROT_DIM)
    ang = jnp.arange(t, dtype=F32)[:, None] * inv_freq[None, :]
    cos, sin = jnp.cos(ang), jnp.sin(ang)
    ones = jnp.ones((t, HEAD_DIM - ROT_DIM), F32)
    zeros = jnp.zeros((t, HEAD_DIM - ROT_DIM), F32)
    zh = jnp.zeros((t, half), F32)
    cos_h = jnp.concatenate([cos, cos, ones], axis=1)
    sa_h = jnp.concatenate([-sin, zh, zeros], axis=1)
    sb_h = jnp.concatenate([zh, sin, zeros], axis=1)
    tile = lambda a: jnp.tile(a, (1, HEADS_PER_LANE_GROUP))
    return tile(cos_h), tile(sa_h), tile(sb_h)


def _inproj(x2, g, w, tables, t, tm):
    n, d = x2.shape
    b = n // t
    tpb = t // tm
    dils = tuple(dil for _, dil in DIL_BRANCHES)
    assert all(tm % (dil * 16) == 0 for dil in dils)
    row = lambda width: pl.BlockSpec((tm, width), lambda i: (i, 0))
    tab = pl.BlockSpec((tm, LANES), lambda i: (i % tpb, 0))
    plain = [NA_WIDTH] * 3 + [MEM_WIDTH]
    dil_specs = [pl.BlockSpec((None, dil, tm // dil, DIL_WIDTH), lambda i: (i // tpb, 0, i % tpb, 0))
                 for dil in dils] * 3
    dil_shapes = [jax.ShapeDtypeStruct((b, dil, t // dil, DIL_WIDTH), BF16) for dil in dils] * 3
    outs = pl.pallas_call(
        functools.partial(_inproj_kernel, dils=dils),
        grid=(n // tm,),
        in_specs=[row(d), pl.BlockSpec((1, d), lambda i: (0, 0)),
                  pl.BlockSpec(w.shape, lambda i: (0, 0)), tab, tab, tab],
        out_specs=[row(wd) for wd in plain] + dil_specs,
        out_shape=[jax.ShapeDtypeStruct((n, wd), BF16) for wd in plain] + dil_shapes,
        scratch_shapes=[pltpu.VMEM((DIL_WIDTH // LANES, tm, LANES), F32)],
        compiler_params=_cparams(("parallel",)),
        name="inproj",
    )(x2, g, w, *tables)
    nd = len(dils)
    qa, ka, va, qm = outs[:4]
    return qa, ka, va, qm, outs[4:4 + nd], outs[4 + nd:4 + 2 * nd], outs[4 + 2 * nd:]


def _na_bias_table(rpb):
    c = jnp.arange(GRID_W)
    c0 = jnp.clip(c - NA_WIN_COLS // 2, 0, GRID_W - NA_WIN_COLS)
    kc = jnp.arange(GRID_W)
    valid = (kc[None, :] >= c0[:, None]) & (kc[None, :] < c0[:, None] + NA_WIN_COLS)
    coff = kc[None, :] - c[:, None] + (NA_WIN_COLS - 1)
    onehot = (coff[None] == jnp.arange(2 * NA_WIN_COLS - 1)[:, None, None]).astype(F32)
    by_col = jnp.einsum("hrd,dck->hrck", rpb.astype(F32), onehot, precision=lax.Precision.HIGHEST)
    by_col = jnp.where(valid[None, None], by_col, NEG)
    tab = jnp.stack([by_col[:, d:d + NA_WIN_ROWS] for d in range(NA_WIN_ROWS)], axis=1)
    tab = tab.transpose(0, 1, 3, 2, 4)
    return tab.reshape(NA_HEADS * NA_WIN_ROWS, GRID_W, NA_WIN_ROWS * GRID_W)


def _na_kernel(q_ref, kp_ref, kc_ref, kn_ref, vp_ref, vc_ref, vn_ref, bias_ref, o_ref,
               kwin, vwin, *, rows):
    j = pl.program_id(1)
    blk = NA_WIN_ROWS * GRID_W
    for idx, (kr, vr) in enumerate(((kp_ref, vp_ref), (kc_ref, vc_ref), (kn_ref, vn_ref))):
        kwin[idx * blk:(idx + 1) * blk, :] = kr[...]
        vwin[idx * blk:(idx + 1) * blk, :] = vr[...]

    def row_body(i, carry):
        r = j * NA_WIN_ROWS + i
        r0 = jnp.clip(r - NA_WIN_ROWS // 2, 0, rows - NA_WIN_ROWS)
        dlt = r0 - r + (NA_WIN_ROWS - 1)
        start = pl.multiple_of((r0 - (j - 1) * NA_WIN_ROWS) * GRID_W, GRID_W)
        q = q_ref[pl.ds(pl.multiple_of(i * GRID_W, GRID_W), GRID_W), :]
        k = kwin[pl.ds(start, blk), :]
        v = vwin[pl.ds(start, blk), :]
        o, _ = _pair_heads_attention(q, k, v, lambda h: bias_ref[h * NA_WIN_ROWS + dlt], NA_HEADS)
        o_ref[pl.ds(pl.multiple_of(i * GRID_W, GRID_W), GRID_W), :] = o.astype(BF16)
        return carry

    lax.fori_loop(0, NA_WIN_ROWS, row_body, 0)


def _na_attention(q, k, v, bias):
    b, t, w = q.shape
    rows = t // GRID_W
    assert rows % NA_WIN_ROWS == 0 and rows >= NA_WIN_ROWS
    nj = rows // NA_WIN_ROWS
    blk = NA_WIN_ROWS * GRID_W
    cur = pl.BlockSpec((None, blk, w), lambda bi, j: (bi, j, 0))
    prev = pl.BlockSpec((None, blk, w), lambda bi, j: (bi, jnp.maximum(j - 1, 0), 0))
    nxt = pl.BlockSpec((None, blk, w), lambda bi, j: (bi, jnp.minimum(j + 1, nj - 1), 0))
    return pl.pallas_call(
        functools.partial(_na_kernel, rows=rows),
        grid=(b, nj),
        in_specs=[cur, prev, cur, nxt, prev, cur, nxt,
                  pl.BlockSpec(bias.shape, lambda bi, j: (0, 0, 0))],
        out_specs=cur,
        out_shape=jax.ShapeDtypeStruct((b, t, w), BF16),
        scratch_shapes=[pltpu.VMEM((3 * blk, w), BF16), pltpu.VMEM((3 * blk, w), BF16)],
        compiler_params=_cparams(("parallel", "parallel")),
        name="na_attn",
    )(q, k, k, k, v, v, v, bias)


def _dil_kernel(q_ref, kp_ref, kc_ref, kn_ref, vp_ref, vc_ref, vn_ref, o_ref, lse_ref,
                kwin, vwin, *, seg, tq, halo):
    i = pl.program_id(2)
    width = tq + 2 * halo
    kwin[0:halo, :] = kp_ref[...]
    kwin[halo:halo + tq, :] = kc_ref[...]
    kwin[halo + tq:width, :] = kn_ref[...]
    vwin[0:halo, :] = vp_ref[...]
    vwin[halo:halo + tq, :] = vc_ref[...]
    vwin[halo + tq:width, :] = vn_ref[...]

    a = lax.broadcasted_iota(jnp.int32, (tq, width), 0)
    c = lax.broadcasted_iota(jnp.int32, (tq, width), 1)
    kpos = i * tq - halo + c
    valid = (c >= a) & (c <= a + 2 * halo) & (kpos >= 0) & (kpos < seg)
    negb = jnp.where(valid, 0.0, NEG).astype(F32)

    o, lses = _pair_heads_attention(q_ref[...], kwin[...], vwin[...], lambda h: negb, DIL_HEADS)
    o_ref[...] = o.astype(BF16)
    lane = lax.broadcasted_iota(jnp.int32, (tq, LANES), 1)
    lse_tile = jnp.zeros((tq, LANES), F32)
    for h, lse in enumerate(lses):
        lse_tile = jnp.where(lane == h, lse, lse_tile)
    lse_ref[...] = lse_tile


def _dilated_branch(q, k, v, n_side, tq_max):
    b, dil, seg, w = q.shape
    halo = n_side
    assert seg % halo == 0 and halo % 16 == 0
    tq = math.gcd(tq_max, seg)
    assert tq % halo == 0
    hb = tq // halo
    nhalo = seg // halo
    cur = pl.BlockSpec((None, None, tq, w), lambda bi, r, i: (bi, r, i, 0))
    prev = pl.BlockSpec((None, None, halo, w), lambda bi, r, i: (bi, r, jnp.maximum(i * hb - 1, 0), 0))
    nxt = pl.BlockSpec((None, None, halo, w),
                       lambda bi, r, i: (bi, r, jnp.minimum((i + 1) * hb, nhalo - 1), 0))
    return pl.pallas_call(
        functools.partial(_dil_kernel, seg=seg, tq=tq, halo=halo),
        grid=(b, dil, seg // tq),
        in_specs=[cur, prev, cur, nxt, prev, cur, nxt],
        out_specs=[cur, pl.BlockSpec((None, None, tq, LANES), lambda bi, r, i: (bi, r, i, 0))],
        out_shape=[jax.ShapeDtypeStruct((b, dil, seg, w), BF16),
                   jax.ShapeDtypeStruct((b, dil, seg, LANES), F32)],
        scratch_shapes=[pltpu.VMEM((tq + 2 * halo, w), BF16), pltpu.VMEM((tq + 2 * halo, w), BF16)],
        compiler_params=_cparams(("parallel", "parallel", "parallel")),
        name=f"dilated_d{dil}",
    )(q, k, k, k, v, v, v)


def _memkv_kernel(mem_ref, g_ref, w_ref, k_ref, v_ref):
    h = _rms(mem_ref[...], g_ref[...]).astype(BF16)
    kv = jnp.dot(h, w_ref[...], preferred_element_type=F32)
    k_ref[...] = kv[:, :MEM_WIDTH].astype(BF16)
    v_ref[...] = kv[:, MEM_WIDTH:].astype(BF16)


def _mem_kv(mem, g, w):
    b, m, d = mem.shape
    blk = lambda width: pl.BlockSpec((None, m, width), lambda bi: (bi, 0, 0))
    return pl.pallas_call(
        _memkv_kernel,
        grid=(b,),
        in_specs=[blk(d), pl.BlockSpec((1, d), lambda bi: (0, 0)), pl.BlockSpec(w.shape, lambda bi: (0, 0))],
        out_specs=[blk(MEM_WIDTH), blk(MEM_WIDTH)],
        out_shape=[jax.ShapeDtypeStruct((b, m, MEM_WIDTH), BF16)] * 2,
        compiler_params=_cparams(("parallel",)),
        name="mem_kv",
    )(mem, g, w)


def _memattn_kernel(q_ref, k_ref, v_ref, o_ref):
    o, _ = _pair_heads_attention(q_ref[...], k_ref[...], v_ref[...], lambda h: 0.0, MEM_HEADS)
    o_ref[...] = o.astype(BF16)


def _mem_attention(q, k, v, tq):
    b, t, w = q.shape
    m = k.shape[1]
    qs = pl.BlockSpec((None, tq, w), lambda bi, i: (bi, i, 0))
    ks = pl.BlockSpec((None, m, w), lambda bi, i: (bi, 0, 0))
    return pl.pallas_call(
        _memattn_kernel,
        grid=(b, t // tq),
        in_specs=[qs, ks, ks],
        out_specs=qs,
        out_shape=jax.ShapeDtypeStruct((b, t, w), BF16),
        compiler_params=_cparams(("parallel", "parallel")),
        name="mem_attn",
    )(q, k, v)


def _token_order(ref, scr):
    dil = ref.shape[0]
    if dil == 1:
        return ref[0].astype(F32)
    groups, tm, _ = scr.shape
    for r in range(dil):
        for g in range(groups):
            scr[g, pl.ds(r, tm // dil, stride=dil), :] = ref[r, :, g * LANES:(g + 1) * LANES].astype(F32)
    return jnp.concatenate([scr[g] for g in range(groups)], axis=-1)


def _branch_mix(ob_refs, lse_refs, o_scrs, l_scrs, expand_ref):
    lses = [_token_order(r, s) for r, s in zip(lse_refs, l_scrs)]
    mx = functools.reduce(jnp.maximum, lses)
    es = [jnp.exp(l - mx) for l in lses]
    den = functools.reduce(lambda p, q: p + q, es)
    acc = None
    for e, ob, scr in zip(es, ob_refs, o_scrs):
        wgt = jnp.dot(e / den, expand_ref[...], precision=lax.Precision.HIGHEST,
                      preferred_element_type=F32)
        term = wgt * _token_order(ob, scr)
        acc = term if acc is None else acc + term
    return acc


def _outproj_core(oa_ref, ob_refs, lse_refs, om_ref, x_ref, gm_ref, w_ref, expand_ref, o_scrs, l_scrs):
    gm = gm_ref[...]
    e0, e1 = NA_WIDTH, NA_WIDTH + DIL_WIDTH
    ya = _rms(oa_ref[...].astype(F32), gm[:, :e0]).astype(BF16)
    yb = _rms(_branch_mix(ob_refs, lse_refs, o_scrs, l_scrs, expand_ref), gm[:, e0:e1]).astype(BF16)
    ym = _rms(om_ref[...].astype(F32), gm[:, e1:]).astype(BF16)
    acc = jnp.dot(ya, w_ref[0:e0, :], preferred_element_type=F32)
    acc = acc + jnp.dot(yb, w_ref[e0:e1, :], preferred_element_type=F32)
    acc = acc + jnp.dot(ym, w_ref[e1:, :], preferred_element_type=F32)
    return x_ref[...] + acc


def _outproj_kernel(oa_ref, ob1, ob2, ob3, l1, l2, l3, om_ref, x_ref, gm_ref, w_ref, expand_ref, xo_ref,
                    *scrs):
    xo_ref[...] = _outproj_core(oa_ref, (ob1, ob2, ob3), (l1, l2, l3), om_ref, x_ref, gm_ref, w_ref,
                                expand_ref, scrs[0:3], scrs[3:6])


def _store_row_tiles(ref, val):
    m, d = val.shape
    for j in range(d // LANES):
        ref[pl.ds(j, m, stride=d // LANES), :] = val[:, j * LANES:(j + 1) * LANES]


def _outproj_router_kernel(oa_ref, ob1, ob2, ob3, l1, l2, l3, om_ref, x_ref, gm_ref, w_ref, expand_ref,
                           gf_ref, wr_ref, xo_ref, h_ref, route_ref, *scrs, n_experts):
    xn = _outproj_core(oa_ref, (ob1, ob2, ob3), (l1, l2, l3), om_ref, x_ref, gm_ref, w_ref, expand_ref,
                       scrs[0:3], scrs[3:6])
    xo_ref[...] = xn
    h = _rms(xn, gf_ref[...])
    _store_row_tiles(h_ref, h)
    logits = jnp.dot(h, wr_ref[...], precision=lax.Precision.HIGHEST, preferred_element_type=F32)
    lane = lax.broadcasted_iota(jnp.int32, logits.shape, 1)
    lg = jnp.where(lane < n_experts, logits, NEG)
    m1 = jnp.max(lg, axis=-1, keepdims=True)
    i1 = jnp.min(jnp.where(lg == m1, lane, LANES), axis=-1, keepdims=True)
    lg2 = jnp.where(lane == i1, NEG, lg)
    m2 = jnp.max(lg2, axis=-1, keepdims=True)
    i2 = jnp.min(jnp.where(lg2 == m2, lane, LANES), axis=-1, keepdims=True)
    e2 = jnp.exp(m2 - m1)
    g1 = 1.0 / (1.0 + e2)
    g2 = e2 / (1.0 + e2)
    route = jnp.where(lane == 0, i1.astype(F32), 0.0)
    route = jnp.where(lane == 1, i2.astype(F32), route)
    route = jnp.where(lane == 2, g1, route)
    route = jnp.where(lane == 3, g2, route)
    route_ref[...] = route


def _head_expand_matrix():
    hd = jnp.arange(DIL_WIDTH) // HEAD_DIM
    return (jnp.arange(LANES)[:, None] == hd[None, :]).astype(F32)


def _outproj(oa, obs, lses, om, x2, gm, w, t, tm, router=None):
    n, d = x2.shape
    tpb = t // tm
    row = lambda width: pl.BlockSpec((tm, width), lambda i: (i, 0))
    full = lambda a: pl.BlockSpec(a.shape, lambda i: (0,) * a.ndim)
    grouped = lambda a: pl.BlockSpec((None, a.shape[1], tm // a.shape[1], a.shape[3]),
                                     lambda i: (i // tpb, 0, i % tpb, 0))
    expand = _head_expand_matrix()
    args = [oa, *obs, *lses, om, x2, gm, w, expand]
    in_specs = ([row(NA_WIDTH)] + [grouped(a) for a in obs] + [grouped(a) for a in lses]
                + [row(MEM_WIDTH), row(d), full(gm), full(w), full(expand)])
    scratch = [pltpu.VMEM((DIL_WIDTH // LANES, tm, LANES), F32)] * 3 + [pltpu.VMEM((1, tm, LANES), F32)] * 3
    if router is None:
        return pl.pallas_call(
            _outproj_kernel, grid=(n // tm,), in_specs=in_specs, out_specs=row(d),
            out_shape=jax.ShapeDtypeStruct((n, d), F32), scratch_shapes=scratch,
            compiler_params=_cparams(("parallel",)), name="outproj",
        )(*args)
    gf, wr, n_experts = router
    sub = d // LANES
    return pl.pallas_call(
        functools.partial(_outproj_router_kernel, n_experts=n_experts),
        grid=(n // tm,), in_specs=in_specs + [full(gf), full(wr)],
        out_specs=[row(d), pl.BlockSpec((tm * sub, LANES), lambda i: (i, 0)), row(LANES)],
        out_shape=[jax.ShapeDtypeStruct((n, d), F32), jax.ShapeDtypeStruct((n * sub, LANES), F32),
                   jax.ShapeDtypeStruct((n, LANES), F32)],
        scratch_shapes=scratch,
        compiler_params=_cparams(("parallel",)), name="outproj_router",
    )(*args, gf, wr)


def _swiglu_step(h, wg_ref, wu_ref, wd_ref):
    gate = jnp.dot(h, wg_ref[...], preferred_element_type=F32)
    up = jnp.dot(h, wu_ref[...], preferred_element_type=F32)
    act = (gate / (1.0 + jnp.exp(-gate))) * up
    return jnp.dot(act.astype(BF16), wd_ref[...], preferred_element_type=F32)


def _dense_ffn_kernel(x_ref, g_ref, wg_ref, wu_ref, wd_ref, o_ref, h_scr, acc_scr):
    k = pl.program_id(1)

    @pl.when(k == 0)
    def _():
        h_scr[...] = _rms(x_ref[...], g_ref[...]).astype(BF16)
        acc_scr[...] = jnp.zeros_like(acc_scr)

    acc_scr[...] += _swiglu_step(h_scr[...], wg_ref, wu_ref, wd_ref)

    @pl.when(k == pl.num_programs(1) - 1)
    def _():
        o_ref[...] = x_ref[...] + acc_scr[...]


def _dense_ffn(x2, g, wg, wu, wd, tm, tf):
    n, d = x2.shape
    f = wg.shape[1]
    row = pl.BlockSpec((tm, d), lambda i, k: (i, 0))
    return pl.pallas_call(
        _dense_ffn_kernel,
        grid=(n // tm, f // tf),
        in_specs=[row, pl.BlockSpec((1, d), lambda i, k: (0, 0)),
                  pl.BlockSpec((d, tf), lambda i, k: (0, k)),
                  pl.BlockSpec((d, tf), lambda i, k: (0, k)),
                  pl.BlockSpec((tf, d), lambda i, k: (k, 0))],
        out_specs=row,
        out_shape=jax.ShapeDtypeStruct((n, d), F32),
        scratch_shapes=[pltpu.VMEM((tm, d), BF16), pltpu.VMEM((tm, d), F32)],
        compiler_params=_cparams(("parallel", "arbitrary")),
        name="dense_ffn",
    )(x2, g, wg, wu, wd)


def _moe_ffn_kernel(be_ref, nused_ref, xb_ref, wg_ref, wu_ref, wd_ref, y_ref, h_scr, acc_scr):
    j = pl.program_id(0)
    k = pl.program_id(1)
    last = pl.num_programs(1) - 1
    used = j < nused_ref[0]

    mb, d = h_scr.shape
    sub = d // LANES

    @pl.when(used & (k == 0))
    def _():
        for j in range(sub):
            h_scr[:, j * LANES:(j + 1) * LANES] = xb_ref[pl.ds(j, mb, stride=sub), :].astype(BF16)
        acc_scr[...] = jnp.zeros_like(acc_scr)

    @pl.when(used)
    def _():
        acc_scr[...] += _swiglu_step(h_scr[...], wg_ref, wu_ref, wd_ref)

    @pl.when(used & (k == last))
    def _():
        _store_row_tiles(y_ref, acc_scr[...])

    @pl.when(jnp.logical_not(used) & (k == last))
    def _():
        y_ref[...] = jnp.zeros_like(y_ref)


def _moe_ffn(buf, block_expert, n_used, wg, wu, wd, mb, tf):
    d, f = wg.shape[1], wg.shape[2]
    sub = d // LANES
    p = buf.shape[0] // sub
    nk = f // tf
    kk = lambda j, k, be, nu: jnp.where(j < nu[0], k, nk - 1)
    row = pl.BlockSpec((mb * sub, LANES), lambda j, k, be, nu: (j, 0))
    grid_spec = pltpu.PrefetchScalarGridSpec(
        num_scalar_prefetch=2,
        grid=(p // mb, nk),
        in_specs=[row,
                  pl.BlockSpec((None, d, tf), lambda j, k, be, nu: (be[j], 0, kk(j, k, be, nu))),
                  pl.BlockSpec((None, d, tf), lambda j, k, be, nu: (be[j], 0, kk(j, k, be, nu))),
                  pl.BlockSpec((None, tf, d), lambda j, k, be, nu: (be[j], kk(j, k, be, nu), 0))],
        out_specs=row,
        scratch_shapes=[pltpu.VMEM((mb, d), BF16), pltpu.VMEM((mb, d), F32)],
    )
    return pl.pallas_call(
        _moe_ffn_kernel,
        grid_spec=grid_spec,
        out_shape=jax.ShapeDtypeStruct((p * sub, LANES), F32),
        compiler_params=_cparams(("parallel", "arbitrary")),
        name="moe_ffn",
    )(block_expert, n_used, buf, wg, wu, wd)


ISSUE_UNROLL = 8


def _issue_tile_copies(chunk, copy_fn):
    def body(t8, carry):
        for u in range(ISSUE_UNROLL):
            copy_fn(t8 * ISSUE_UNROLL + u).start()
        return carry

    lax.fori_loop(0, chunk // ISSUE_UNROLL, body, 0)


def _wait_chunk(src_ref, out_ref, sem, rows):
    pltpu.make_async_copy(src_ref.at[pl.ds(0, rows)], out_ref.at[pl.ds(0, rows)], sem).wait()


def _tile(ref, idx, sub):
    return ref.at[pl.ds(pl.multiple_of(idx * sub, sub), sub)]


def _dispatch_kernel(slot_ref, src_ref, zero_ref, out_ref, sem, *, chunk, n_assign, sub):
    base = pl.program_id(0) * chunk
    dst = lambda t: _tile(out_ref, slot_ref[0, 0, t], sub)

    @pl.when(base < n_assign)
    def _():
        _issue_tile_copies(chunk, lambda t: pltpu.make_async_copy(
            _tile(src_ref, (base + t) // TOP_K, sub), dst(t), sem))

    @pl.when(base >= n_assign)
    def _():
        _issue_tile_copies(chunk, lambda t: pltpu.make_async_copy(zero_ref, dst(t), sem))

    _wait_chunk(src_ref, out_ref, sem, chunk * sub)


def _dispatch_rows(src, slot, n_assign, chunk, sub):
    p = slot.shape[0]
    chunk = math.gcd(math.gcd(chunk, n_assign), p)
    assert chunk % ISSUE_UNROLL == 0
    any_spec = pl.BlockSpec(memory_space=pl.ANY)
    return pl.pallas_call(
        functools.partial(_dispatch_kernel, chunk=chunk, n_assign=n_assign, sub=sub),
        grid=(p // chunk,),
        in_specs=[pl.BlockSpec((1, 1, chunk), lambda i: (i, 0, 0), memory_space=pltpu.SMEM),
                  any_spec, any_spec],
        out_specs=any_spec,
        out_shape=jax.ShapeDtypeStruct((p * sub, LANES), src.dtype),
        scratch_shapes=[pltpu.SemaphoreType.DMA(())],
        compiler_params=_cparams(("arbitrary",)),
        name="dispatch_rows",
    )(slot.reshape(p // chunk, 1, chunk), src, jnp.zeros((sub, LANES), src.dtype))


def _gather_kernel(idx_ref, src_ref, out_ref, sem, *, chunk, sub):
    base = pl.program_id(0) * chunk
    _issue_tile_copies(chunk, lambda t: pltpu.make_async_copy(
        _tile(src_ref, idx_ref[0, 0, t], sub), _tile(out_ref, base + t, sub), sem))
    _wait_chunk(src_ref, out_ref, sem, chunk * sub)


def _gather_rows(src, idx, chunk, sub):
    m = idx.shape[0]
    chunk = math.gcd(chunk, m)
    assert chunk % ISSUE_UNROLL == 0
    return pl.pallas_call(
        functools.partial(_gather_kernel, chunk=chunk, sub=sub),
        grid=(m // chunk,),
        in_specs=[pl.BlockSpec((1, 1, chunk), lambda i: (i, 0, 0), memory_space=pltpu.SMEM),
                  pl.BlockSpec(memory_space=pl.ANY)],
        out_specs=pl.BlockSpec(memory_space=pl.ANY),
        out_shape=jax.ShapeDtypeStruct((m * sub, LANES), src.dtype),
        scratch_shapes=[pltpu.SemaphoreType.DMA(())],
        compiler_params=_cparams(("arbitrary",)),
        name="gather_rows",
    )(idx.reshape(m // chunk, 1, chunk), src)


def _combine_kernel(x_ref, y_ref, route_ref, *rest, final_norm):
    o_ref = rest[-1]
    tm, d = x_ref.shape
    sub = d // LANES
    route = route_ref[...]
    gates = [route[:, TOP_K + k:TOP_K + k + 1] for k in range(TOP_K)]
    for j in range(sub):
        sl = slice(j * LANES, (j + 1) * LANES)
        mix = None
        for k in range(TOP_K):
            term = gates[k] * y_ref[pl.ds(k * sub + j, tm, stride=TOP_K * sub), :]
            mix = term if mix is None else mix + term
        o_ref[:, sl] = x_ref[:, sl] + mix
    if final_norm:
        o_ref[...] = _rms(o_ref[...], rest[0][...])


def _combine(x2, y, route, tm, g_final=None):
    n, d = x2.shape
    sub = d // LANES
    row = pl.BlockSpec((tm, d), lambda i: (i, 0))
    in_specs = [row, pl.BlockSpec((tm * TOP_K * sub, LANES), lambda i: (i, 0)),
                pl.BlockSpec((tm, LANES), lambda i: (i, 0))]
    args = [x2, y, route]
    if g_final is not None:
        in_specs.append(pl.BlockSpec((1, d), lambda i: (0, 0)))
        args.append(g_final)
    return pl.pallas_call(
        functools.partial(_combine_kernel, final_norm=g_final is not None),
        grid=(n // tm,), in_specs=in_specs, out_specs=row,
        out_shape=jax.ShapeDtypeStruct((n, d), F32),
        compiler_params=_cparams(("parallel",)), name="moe_combine",
    )(*args)


def _final_norm_kernel(x_ref, g_ref, o_ref):
    o_ref[...] = _rms(x_ref[...], g_ref[...])


def _final_norm(x2, g, tm):
    n, d = x2.shape
    row = pl.BlockSpec((tm, d), lambda i: (i, 0))
    return pl.pallas_call(
        _final_norm_kernel, grid=(n // tm,), in_specs=[row, pl.BlockSpec((1, d), lambda i: (0, 0))],
        out_specs=row, out_shape=jax.ShapeDtypeStruct((n, d), F32),
        compiler_params=_cparams(("parallel",)), name="final_norm",
    )(x2, g)


def _routing_plan(route, n_experts, mb):
    n = route.shape[0]
    n_assign = n * TOP_K
    flat_e = route[:, :TOP_K].astype(jnp.int32).reshape(-1)
    onehot = (flat_e[:, None] == jnp.arange(n_experts)[None, :]).astype(jnp.int32)
    csum = jnp.cumsum(onehot, axis=0)
    rank = jnp.sum(csum * onehot, axis=1) - 1
    counts = csum[-1]
    padded = (counts + mb - 1) // mb * mb
    pend = jnp.cumsum(padded)
    pstart = pend - padded
    dest = jnp.sum(pstart[None, :] * onehot, axis=1) + rank
    assert n_assign % mb == 0
    n_blocks = n_assign // mb + n_experts
    block_first = jnp.arange(n_blocks) * mb
    block_expert = jnp.minimum(jnp.sum(block_first[:, None] >= pend[None, :], axis=1), n_experts - 1)
    n_used = (pend[-1] // mb).reshape(1)
    pads = padded - counts
    cpad = jnp.cumsum(pads)
    i = jnp.arange(n_experts * mb)
    grp = jnp.sum(i[:, None] >= cpad[None, :], axis=1)
    grp_hot = (grp[:, None] == jnp.arange(n_experts + 1)[None, :]).astype(jnp.int32)
    first_free = jnp.concatenate([pstart + counts, pend[-1:]])
    before = jnp.concatenate([jnp.zeros((1,), cpad.dtype), cpad])
    free = jnp.sum(grp_hot * (first_free - before)[None, :], axis=1) + i
    slot = jnp.concatenate([dest, free]).astype(jnp.int32)
    return dest.astype(jnp.int32), slot, block_expert.astype(jnp.int32), n_used.astype(jnp.int32)


def _tiles(n, t):
    return dict(tm_proj=math.gcd(512, t), tm_ffn=math.gcd(1024, n), tf=512, tq_dil=256,
                tq_mem=math.gcd(512, t), moe_block=512, gather_chunk=2048)


@jax.jit
def _forward(x, mem, g_mix_norm, w_in, rpb, g_mem_norm, w_mem_kv, g_mix_out, w_out,
             g_ffn_norm, w_dense_gate, w_dense_up, w_dense_down, w_router,
             w_moe_gate, w_moe_up, w_moe_down, g_final):
    b, t, d = x.shape
    n = b * t
    depth = w_in.shape[0]
    n_experts = w_router.shape[-1]
    ts = _tiles(n, t)
    tf = math.gcd(ts["tf"], w_dense_gate.shape[-1])
    tables = _rope_tables(t)
    row1 = lambda a: a.reshape(1, -1)
    x2 = x.reshape(n, d)
    out = None
    sub = d // LANES
    for layer in range(depth):
        qa, ka, va, qm, qbs, kbs, vbs = _inproj(x2, row1(g_mix_norm[layer]), w_in[layer].astype(BF16),
                                                tables, t, ts["tm_proj"])
        shp = lambda a: a.reshape(b, t, a.shape[-1])
        oa = _na_attention(shp(qa), shp(ka), shp(va), _na_bias_table(rpb[layer])).reshape(n, NA_WIDTH)
        obs, lses = [], []
        for (window, dil), qb, kb, vb in zip(DIL_BRANCHES, qbs, kbs, vbs):
            o, lse = _dilated_branch(qb, kb, vb, window // (2 * dil), ts["tq_dil"])
            obs.append(o)
            lses.append(lse)
        km, vm = _mem_kv(mem, row1(g_mem_norm[layer]), w_mem_kv[layer].astype(BF16))
        om = _mem_attention(shp(qm), km, vm, ts["tq_mem"]).reshape(n, MEM_WIDTH)
        gm, wo = row1(g_mix_out[layer]), w_out[layer].astype(BF16)
        i = layer // 2
        if layer % 2 == 0:
            x2 = _outproj(oa, obs, lses, om, x2, gm, wo, t, ts["tm_proj"])
            x2 = _dense_ffn(x2, row1(g_ffn_norm[layer]), w_dense_gate[i].astype(BF16),
                            w_dense_up[i].astype(BF16), w_dense_down[i].astype(BF16), ts["tm_ffn"], tf)
            if layer == depth - 1:
                out = _final_norm(x2, row1(g_final), ts["tm_proj"])
        else:
            wr = jnp.pad(w_router[i], ((0, 0), (0, LANES - n_experts)))
            x2, h, route = _outproj(oa, obs, lses, om, x2, gm, wo, t, ts["tm_proj"],
                                    router=(row1(g_ffn_norm[layer]), wr, n_experts))
            mb = ts["moe_block"]
            dest, slot, block_expert, n_used = _routing_plan(route, n_experts, mb)
            buf = _dispatch_rows(h, slot, n * TOP_K, ts["gather_chunk"], sub)
            y = _moe_ffn(buf, block_expert, n_used, w_moe_gate[i].astype(BF16), w_moe_up[i].astype(BF16),
                         w_moe_down[i].astype(BF16), mb, tf)
            y = _gather_rows(y, dest, ts["gather_chunk"], sub)
            if layer == depth - 1:
                out = _combine(x2, y, route, ts["tm_proj"], g_final=row1(g_final))
            else:
                x2 = _combine(x2, y, route, ts["tm_proj"])
    return out.reshape(b, t, d)


def kernel(x, mem, g_mix_norm, w_in, rpb, g_mem_norm, w_mem_kv, g_mix_out, w_out, g_ffn_norm,
           w_dense_gate, w_dense_up, w_dense_down, w_router, w_moe_gate, w_moe_up, w_moe_down, g_final):
    return _forward(x, mem, g_mix_norm, w_in, rpb, g_mem_norm, w_mem_kv, g_mix_out, w_out, g_ffn_norm,
                    w_dense_gate, w_dense_up, w_dense_down, w_router, w_moe_gate, w_moe_up, w_moe_down,
                    g_final)
```

```python
import functools
import math

import jax
import jax.numpy as jnp
from jax import lax
from jax.experimental import pallas as pl
from jax.experimental.pallas import tpu as pltpu

HEAD_DIM = 64
NA_HEADS = 6
DIL_HEADS = 6
MEM_HEADS = 4
NA_WIDTH = NA_HEADS * HEAD_DIM
DIL_WIDTH = DIL_HEADS * HEAD_DIM
MEM_WIDTH = MEM_HEADS * HEAD_DIM
GRID_W = 64
NA_WIN_ROWS = 8
NA_WIN_COLS = 16
DIL_BRANCHES = ((128, 1), (512, 4), (2048, 16))
ROPE_THETA = 500000.0
ROT_DIM = HEAD_DIM // 4
TOP_K = 2
RMS_EPS = 1e-6
ATTN_SCALE = HEAD_DIM ** -0.5

LANES = 128
HEADS_PER_LANE_GROUP = LANES // HEAD_DIM
NEG = -1e30
VMEM_LIMIT = 56 * 1024 * 1024

F32 = jnp.float32
BF16 = jnp.bfloat16


def _cparams(sem):
    return pltpu.CompilerParams(dimension_semantics=sem, vmem_limit_bytes=VMEM_LIMIT)


def _rms(x, g):
    ms = jnp.mean(x * x, axis=-1, keepdims=True)
    return x * lax.rsqrt(ms + RMS_EPS) * g


def _pair_heads_attention(q, k, v, bias_fn, n_heads):
    m = q.shape[0]
    assert HEADS_PER_LANE_GROUP == 2
    n_pairs = n_heads // HEADS_PER_LANE_GROUP
    lo = lax.broadcasted_iota(jnp.int32, (m, LANES), 1) < HEAD_DIM
    group = lambda a, p: a[:, p * LANES:(p + 1) * LANES]
    s_parts = []
    for p in range(n_pairs):
        qp = group(q, p)
        zero = jnp.zeros_like(qp)
        q2 = jnp.concatenate([jnp.where(lo, qp, zero), jnp.where(lo, zero, qp)], axis=0)
        s_parts.append(lax.dot_general(q2, group(k, p), (((1,), (1,)), ((), ())),
                                       preferred_element_type=F32) + bias_fn(p))
    s = jnp.concatenate(s_parts, axis=0)
    mx = jnp.max(s, axis=-1, keepdims=True)
    e = jnp.exp(s - mx)
    den = jnp.sum(e, axis=-1, keepdims=True)
    eb = e.astype(BF16)
    outs = []
    for p in range(n_pairs):
        rows = slice(p * 2 * m, (p + 1) * 2 * m)
        o2 = jnp.dot(eb[rows], group(v, p), preferred_element_type=F32) / den[rows]
        outs.append(jnp.where(lo, o2[:m], o2[m:]))
    lse = mx + jnp.log(den)
    return jnp.concatenate(outs, axis=-1), [lse[h * m:(h + 1) * m] for h in range(n_heads)]


def _inproj_kernel(x_ref, g_ref, w_ref, cos_ref, sa_ref, sb_ref,
                   qa_ref, ka_ref, va_ref, qm_ref, *rest, dils):
    dil_refs, scr = rest[:-1], rest[-1]
    tm = x_ref.shape[0]
    h = _rms(x_ref[...], g_ref[...]).astype(BF16)

    def emit_dilated(val, refs):
        groups = val.shape[1] // LANES
        for g in range(groups):
            scr[g] = val[:, g * LANES:(g + 1) * LANES]
        for dil, ref in zip(dils, refs):
            for r in range(dil):
                for g in range(groups):
                    ref[r, :, g * LANES:(g + 1) * LANES] = (
                        scr[g, pl.ds(r, tm // dil, stride=dil), :].astype(BF16))

    def proj(c0, width):
        return jnp.dot(h, w_ref[:, c0:c0 + width], preferred_element_type=F32)

    def rope(a):
        cos, sa, sb = cos_ref[...], sa_ref[...], sb_ref[...]
        half = ROT_DIM // 2
        parts = []
        for gidx in range(a.shape[1] // LANES):
            xg = a[:, gidx * LANES:(gidx + 1) * LANES]
            parts.append(xg * cos + pltpu.roll(xg, LANES - half, 1) * sa + pltpu.roll(xg, half, 1) * sb)
        return jnp.concatenate(parts, axis=-1)

    nd = len(dils)
    c = 0
    qa_ref[...] = (proj(c, NA_WIDTH) * ATTN_SCALE).astype(BF16); c += NA_WIDTH
    ka_ref[...] = proj(c, NA_WIDTH).astype(BF16); c += NA_WIDTH
    va_ref[...] = proj(c, NA_WIDTH).astype(BF16); c += NA_WIDTH
    emit_dilated(rope(proj(c, DIL_WIDTH) * ATTN_SCALE), dil_refs[0:nd]); c += DIL_WIDTH
    emit_dilated(rope(proj(c, DIL_WIDTH)), dil_refs[nd:2 * nd]); c += DIL_WIDTH
    emit_dilated(proj(c, DIL_WIDTH), dil_refs[2 * nd:3 * nd]); c += DIL_WIDTH
    qm_ref[...] = (proj(c, MEM_WIDTH) * ATTN_SCALE).astype(BF16)


def _rope_tables(t):
    half = ROT_DIM // 2
    inv_freq = ROPE_THETA ** (-jnp.arange(0, ROT_DIM, 2, dtype=F32) / ROT_DIM)
    ang = jnp.arange(t, dtype=F32)[:, None] * inv_freq[None, :]
    cos, sin = jnp.cos(ang), jnp.sin(ang)
    ones = jnp.ones((t, HEAD_DIM - ROT_DIM), F32)
    zeros = jnp.zeros((t, HEAD_DIM - ROT_DIM), F32)
    zh = jnp.zeros((t, half), F32)
    cos_h = jnp.concatenate([cos, cos, ones], axis=1)
    sa_h = jnp.concatenate([-sin, zh, zeros], axis=1)
    sb_h = jnp.concatenate([zh, sin, zeros], axis=1)
    tile = lambda a: jnp.tile(a, (1, HEADS_PER_LANE_GROUP))
    return tile(cos_h), tile(sa_h), tile(sb_h)


def _inproj(x2, g, w, tables, t, tm):
    n, d = x2.shape
    b = n // t
    tpb = t // tm
    dils = tuple(dil for _, dil in DIL_BRANCHES)
    assert all(tm % (dil * 16) == 0 for dil in dils)
    row = lambda width: pl.BlockSpec((tm, width), lambda i: (i, 0))
    tab = pl.BlockSpec((tm, LANES), lambda i: (i % tpb, 0))
    plain = [NA_WIDTH] * 3 + [MEM_WIDTH]
    dil_specs = [pl.BlockSpec((None, dil, tm // dil, DIL_WIDTH), lambda i: (i // tpb, 0, i % tpb, 0))
                 for dil in dils] * 3
    dil_shapes = [jax.ShapeDtypeStruct((b, dil, t // dil, DIL_WIDTH), BF16) for dil in dils] * 3
    outs = pl.pallas_call(
        functools.partial(_inproj_kernel, dils=dils),
        grid=(n // tm,),
        in_specs=[row(d), pl.BlockSpec((1, d), lambda i: (0, 0)),
                  pl.BlockSpec(w.shape, lambda i: (0, 0)), tab, tab, tab],
        out_specs=[row(wd) for wd in plain] + dil_specs,
        out_shape=[jax.ShapeDtypeStruct((n, wd), BF16) for wd in plain] + dil_shapes,
        scratch_shapes=[pltpu.VMEM((DIL_WIDTH // LANES, tm, LANES), F32)],
        compiler_params=_cparams(("parallel",)),
        name="inproj",
    )(x2, g, w, *tables)
    nd = len(dils)
    qa, ka, va, qm = outs[:4]
    return qa, ka, va, qm, outs[4:4 + nd], outs[4 + nd:4 + 2 * nd], outs[4 + 2 * nd:]


def _na_bias_table(rpb):
    c = jnp.arange(GRID_W)
    c0 = jnp.clip(c - NA_WIN_COLS // 2, 0, GRID_W - NA_WIN_COLS)
    kc = jnp.arange(GRID_W)
    valid = (kc[None, :] >= c0[:, None]) & (kc[None, :] < c0[:, None] + NA_WIN_COLS)
    coff = kc[None, :] - c[:, None] + (NA_WIN_COLS - 1)
    onehot = (coff[None] == jnp.arange(2 * NA_WIN_COLS - 1)[:, None, None]).astype(F32)
    by_col = jnp.einsum("hrd,dck->hrck", rpb.astype(F32), onehot, precision=lax.Precision.HIGHEST)
    by_col = jnp.where(valid[None, None], by_col, NEG)
    tab = jnp.stack([by_col[:, d:d + NA_WIN_ROWS] for d in range(NA_WIN_ROWS)], axis=1)
    tab = tab.transpose(1, 0, 3, 2, 4)
    return tab.reshape(NA_WIN_ROWS * NA_HEADS // HEADS_PER_LANE_GROUP, HEADS_PER_LANE_GROUP * GRID_W,
                       NA_WIN_ROWS * GRID_W)


def _na_kernel(q_ref, kp_ref, kc_ref, kn_ref, vp_ref, vc_ref, vn_ref, bias_ref, o_ref,
               kwin, vwin, *, rows):
    j = pl.program_id(1)
    blk = NA_WIN_ROWS * GRID_W
    for idx, (kr, vr) in enumerate(((kp_ref, vp_ref), (kc_ref, vc_ref), (kn_ref, vn_ref))):
        kwin[idx * blk:(idx + 1) * blk, :] = kr[...]
        vwin[idx * blk:(idx + 1) * blk, :] = vr[...]

    n_pairs = NA_HEADS // HEADS_PER_LANE_GROUP

    def one_row(i):
        r = j * NA_WIN_ROWS + i
        r0 = jnp.clip(r - NA_WIN_ROWS // 2, 0, rows - NA_WIN_ROWS)
        dlt = r0 - r + (NA_WIN_ROWS - 1)
        start = pl.multiple_of((r0 - (j - 1) * NA_WIN_ROWS) * GRID_W, GRID_W)
        q = q_ref[pl.ds(pl.multiple_of(i * GRID_W, GRID_W), GRID_W), :]
        k = kwin[pl.ds(start, blk), :]
        v = vwin[pl.ds(start, blk), :]
        o, _ = _pair_heads_attention(q, k, v, lambda p: bias_ref[dlt * n_pairs + p], NA_HEADS)
        o_ref[pl.ds(pl.multiple_of(i * GRID_W, GRID_W), GRID_W), :] = o.astype(BF16)

    def two_rows(i2, carry):
        one_row(2 * i2)
        one_row(2 * i2 + 1)
        return carry

    lax.fori_loop(0, NA_WIN_ROWS // 2, two_rows, 0)


def _na_attention(q, k, v, bias):
    b, t, w = q.shape
    rows = t // GRID_W
    assert rows % NA_WIN_ROWS == 0 and rows >= NA_WIN_ROWS
    nj = rows // NA_WIN_ROWS
    blk = NA_WIN_ROWS * GRID_W
    cur = pl.BlockSpec((None, blk, w), lambda bi, j: (bi, j, 0))
    prev = pl.BlockSpec((None, blk, w), lambda bi, j: (bi, jnp.maximum(j - 1, 0), 0))
    nxt = pl.BlockSpec((None, blk, w), lambda bi, j: (bi, jnp.minimum(j + 1, nj - 1), 0))
    return pl.pallas_call(
        functools.partial(_na_kernel, rows=rows),
        grid=(b, nj),
        in_specs=[cur, prev, cur, nxt, prev, cur, nxt,
                  pl.BlockSpec(bias.shape, lambda bi, j: (0, 0, 0))],
        out_specs=cur,
        out_shape=jax.ShapeDtypeStruct((b, t, w), BF16),
        scratch_shapes=[pltpu.VMEM((3 * blk, w), BF16), pltpu.VMEM((3 * blk, w), BF16)],
        compiler_params=_cparams(("parallel", "parallel")),
        name="na_attn",
    )(q, k, k, k, v, v, v, bias)


def _dil_kernel(q_ref, kp_ref, kc_ref, kn_ref, vp_ref, vc_ref, vn_ref, o_ref, lse_ref,
                kwin, vwin, *, seg, tq, halo):
    i = pl.program_id(2)
    width = tq + 2 * halo
    kwin[0:halo, :] = kp_ref[...]
    kwin[halo:halo + tq, :] = kc_ref[...]
    kwin[halo + tq:width, :] = kn_ref[...]
    vwin[0:halo, :] = vp_ref[...]
    vwin[halo:halo + tq, :] = vc_ref[...]
    vwin[halo + tq:width, :] = vn_ref[...]

    a = lax.broadcasted_iota(jnp.int32, (tq, width), 0)
    c = lax.broadcasted_iota(jnp.int32, (tq, width), 1)
    kpos = i * tq - halo + c
    valid = (c >= a) & (c <= a + 2 * halo) & (kpos >= 0) & (kpos < seg)
    negb = jnp.where(valid, 0.0, NEG).astype(F32)
    negb2 = jnp.concatenate([negb] * HEADS_PER_LANE_GROUP, axis=0)

    o, lses = _pair_heads_attention(q_ref[...], kwin[...], vwin[...], lambda p: negb2, DIL_HEADS)
    o_ref[...] = o.astype(BF16)
    lane = lax.broadcasted_iota(jnp.int32, (tq, LANES), 1)
    lse_tile = jnp.zeros((tq, LANES), F32)
    for h, lse in enumerate(lses):
        lse_tile = jnp.where(lane == h, lse, lse_tile)
    lse_ref[...] = lse_tile


def _dilated_branch(q, k, v, n_side, tq_max):
    b, dil, seg, w = q.shape
    halo = n_side
    assert seg % halo == 0 and halo % 16 == 0
    tq = math.gcd(tq_max, seg)
    assert tq % halo == 0
    hb = tq // halo
    nhalo = seg // halo
    cur = pl.BlockSpec((None, None, tq, w), lambda bi, r, i: (bi, r, i, 0))
    prev = pl.BlockSpec((None, None, halo, w), lambda bi, r, i: (bi, r, jnp.maximum(i * hb - 1, 0), 0))
    nxt = pl.BlockSpec((None, None, halo, w),
                       lambda bi, r, i: (bi, r, jnp.minimum((i + 1) * hb, nhalo - 1), 0))
    return pl.pallas_call(
        functools.partial(_dil_kernel, seg=seg, tq=tq, halo=halo),
        grid=(b, dil, seg // tq),
        in_specs=[cur, prev, cur, nxt, prev, cur, nxt],
        out_specs=[cur, pl.BlockSpec((None, None, tq, LANES), lambda bi, r, i: (bi, r, i, 0))],
        out_shape=[jax.ShapeDtypeStruct((b, dil, seg, w), BF16),
                   jax.ShapeDtypeStruct((b, dil, seg, LANES), F32)],
        scratch_shapes=[pltpu.VMEM((tq + 2 * halo, w), BF16), pltpu.VMEM((tq + 2 * halo, w), BF16)],
        compiler_params=_cparams(("parallel", "parallel", "parallel")),
        name=f"dilated_d{dil}",
    )(q, k, k, k, v, v, v)


def _memkv_kernel(mem_ref, g_ref, w_ref, k_ref, v_ref):
    h = _rms(mem_ref[...], g_ref[...]).astype(BF16)
    kv = jnp.dot(h, w_ref[...], preferred_element_type=F32)
    k_ref[...] = kv[:, :MEM_WIDTH].astype(BF16)
    v_ref[...] = kv[:, MEM_WIDTH:].astype(BF16)


def _mem_kv(mem, g, w):
    b, m, d = mem.shape
    blk = lambda width: pl.BlockSpec((None, m, width), lambda bi: (bi, 0, 0))
    return pl.pallas_call(
        _memkv_kernel,
        grid=(b,),
        in_specs=[blk(d), pl.BlockSpec((1, d), lambda bi: (0, 0)), pl.BlockSpec(w.shape, lambda bi: (0, 0))],
        out_specs=[blk(MEM_WIDTH), blk(MEM_WIDTH)],
        out_shape=[jax.ShapeDtypeStruct((b, m, MEM_WIDTH), BF16)] * 2,
        compiler_params=_cparams(("parallel",)),
        name="mem_kv",
    )(mem, g, w)


def _memattn_kernel(q_ref, k_ref, v_ref, o_ref):
    o, _ = _pair_heads_attention(q_ref[...], k_ref[...], v_ref[...], lambda p: 0.0, MEM_HEADS)
    o_ref[...] = o.astype(BF16)


def _mem_attention(q, k, v, tq):
    b, t, w = q.shape
    m = k.shape[1]
    qs = pl.BlockSpec((None, tq, w), lambda bi, i: (bi, i, 0))
    ks = pl.BlockSpec((None, m, w), lambda bi, i: (bi, 0, 0))
    return pl.pallas_call(
        _memattn_kernel,
        grid=(b, t // tq),
        in_specs=[qs, ks, ks],
        out_specs=qs,
        out_shape=jax.ShapeDtypeStruct((b, t, w), BF16),
        compiler_params=_cparams(("parallel", "parallel")),
        name="mem_attn",
    )(q, k, v)


def _token_order(ref, scr):
    dil = ref.shape[0]
    if dil == 1:
        return ref[0].astype(F32)
    groups, tm, _ = scr.shape
    for r in range(dil):
        for g in range(groups):
            scr[g, pl.ds(r, tm // dil, stride=dil), :] = ref[r, :, g * LANES:(g + 1) * LANES].astype(F32)
    return jnp.concatenate([scr[g] for g in range(groups)], axis=-1)


def _branch_mix(ob_refs, lse_refs, o_scrs, l_scrs, expand_ref):
    lses = [_token_order(r, s) for r, s in zip(lse_refs, l_scrs)]
    mx = functools.reduce(jnp.maximum, lses)
    es = [jnp.exp(l - mx) for l in lses]
    den = functools.reduce(lambda p, q: p + q, es)
    acc = None
    for e, ob, scr in zip(es, ob_refs, o_scrs):
        wgt = jnp.dot(e / den, expand_ref[...], precision=lax.Precision.HIGHEST,
                      preferred_element_type=F32)
        term = wgt * _token_order(ob, scr)
        acc = term if acc is None else acc + term
    return acc


def _outproj_core(oa_ref, ob_refs, lse_refs, om_ref, x_ref, gm_ref, w_ref, expand_ref, o_scrs, l_scrs):
    gm = gm_ref[...]
    e0, e1 = NA_WIDTH, NA_WIDTH + DIL_WIDTH
    ya = _rms(oa_ref[...].astype(F32), gm[:, :e0]).astype(BF16)
    yb = _rms(_branch_mix(ob_refs, lse_refs, o_scrs, l_scrs, expand_ref), gm[:, e0:e1]).astype(BF16)
    ym = _rms(om_ref[...].astype(F32), gm[:, e1:]).astype(BF16)
    acc = jnp.dot(ya, w_ref[0:e0, :], preferred_element_type=F32)
    acc = acc + jnp.dot(yb, w_ref[e0:e1, :], preferred_element_type=F32)
    acc = acc + jnp.dot(ym, w_ref[e1:, :], preferred_element_type=F32)
    return x_ref[...] + acc


def _outproj_kernel(oa_ref, ob1, ob2, ob3, l1, l2, l3, om_ref, x_ref, gm_ref, w_ref, expand_ref, xo_ref,
                    *scrs):
    xo_ref[...] = _outproj_core(oa_ref, (ob1, ob2, ob3), (l1, l2, l3), om_ref, x_ref, gm_ref, w_ref,
                                expand_ref, scrs[0:3], scrs[3:6])


def _store_row_tiles(ref, val):
    m, d = val.shape
    for j in range(d // LANES):
        ref[pl.ds(j, m, stride=d // LANES), :] = val[:, j * LANES:(j + 1) * LANES]


def _outproj_router_kernel(oa_ref, ob1, ob2, ob3, l1, l2, l3, om_ref, x_ref, gm_ref, w_ref, expand_ref,
                           gf_ref, wr_ref, xo_ref, h_ref, route_ref, *scrs, n_experts):
    xn = _outproj_core(oa_ref, (ob1, ob2, ob3), (l1, l2, l3), om_ref, x_ref, gm_ref, w_ref, expand_ref,
                       scrs[0:3], scrs[3:6])
    xo_ref[...] = xn
    h = _rms(xn, gf_ref[...])
    _store_row_tiles(h_ref, h)
    logits = jnp.dot(h, wr_ref[...], precision=lax.Precision.HIGHEST, preferred_element_type=F32)
    lane = lax.broadcasted_iota(jnp.int32, logits.shape, 1)
    lg = jnp.where(lane < n_experts, logits, NEG)
    m1 = jnp.max(lg, axis=-1, keepdims=True)
    i1 = jnp.min(jnp.where(lg == m1, lane, LANES), axis=-1, keepdims=True)
    lg2 = jnp.where(lane == i1, NEG, lg)
    m2 = jnp.max(lg2, axis=-1, keepdims=True)
    i2 = jnp.min(jnp.where(lg2 == m2, lane, LANES), axis=-1, keepdims=True)
    e2 = jnp.exp(m2 - m1)
    g1 = 1.0 / (1.0 + e2)
    g2 = e2 / (1.0 + e2)
    route = jnp.where(lane == 0, i1.astype(F32), 0.0)
    route = jnp.where(lane == 1, i2.astype(F32), route)
    route = jnp.where(lane == 2, g1, route)
    route = jnp.where(lane == 3, g2, route)
    route_ref[...] = route


def _head_expand_matrix():
    hd = jnp.arange(DIL_WIDTH) // HEAD_DIM
    return (jnp.arange(LANES)[:, None] == hd[None, :]).astype(F32)


def _outproj(oa, obs, lses, om, x2, gm, w, t, tm, router=None):
    n, d = x2.shape
    tpb = t // tm
    row = lambda width: pl.BlockSpec((tm, width), lambda i: (i, 0))
    full = lambda a: pl.BlockSpec(a.shape, lambda i: (0,) * a.ndim)
    grouped = lambda a: pl.BlockSpec((None, a.shape[1], tm // a.shape[1], a.shape[3]),
                                     lambda i: (i // tpb, 0, i % tpb, 0))
    expand = _head_expand_matrix()
    args = [oa, *obs, *lses, om, x2, gm, w, expand]
    in_specs = ([row(NA_WIDTH)] + [grouped(a) for a in obs] + [grouped(a) for a in lses]
                + [row(MEM_WIDTH), row(d), full(gm), full(w), full(expand)])
    scratch = [pltpu.VMEM((DIL_WIDTH // LANES, tm, LANES), F32)] * 3 + [pltpu.VMEM((1, tm, LANES), F32)] * 3
    if router is None:
        return pl.pallas_call(
            _outproj_kernel, grid=(n // tm,), in_specs=in_specs, out_specs=row(d),
            out_shape=jax.ShapeDtypeStruct((n, d), F32), scratch_shapes=scratch,
            compiler_params=_cparams(("parallel",)), name="outproj",
        )(*args)
    gf, wr, n_experts = router
    sub = d // LANES
    return pl.pallas_call(
        functools.partial(_outproj_router_kernel, n_experts=n_experts),
        grid=(n // tm,), in_specs=in_specs + [full(gf), full(wr)],
        out_specs=[row(d), pl.BlockSpec((tm * sub, LANES), lambda i: (i, 0)), row(LANES)],
        out_shape=[jax.ShapeDtypeStruct((n, d), F32), jax.ShapeDtypeStruct((n * sub, LANES), F32),
                   jax.ShapeDtypeStruct((n, LANES), F32)],
        scratch_shapes=scratch,
        compiler_params=_cparams(("parallel",)), name="outproj_router",
    )(*args, gf, wr)


def _swiglu_step(h, wg_ref, wu_ref, wd_ref):
    gate = jnp.dot(h, wg_ref[...], preferred_element_type=F32)
    up = jnp.dot(h, wu_ref[...], preferred_element_type=F32)
    act = (gate / (1.0 + jnp.exp(-gate))) * up
    return jnp.dot(act.astype(BF16), wd_ref[...], preferred_element_type=F32)


def _dense_ffn_kernel(x_ref, g_ref, wg_ref, wu_ref, wd_ref, o_ref, h_scr, acc_scr):
    k = pl.program_id(1)

    @pl.when(k == 0)
    def _():
        h_scr[...] = _rms(x_ref[...], g_ref[...]).astype(BF16)
        acc_scr[...] = jnp.zeros_like(acc_scr)

    acc_scr[...] += _swiglu_step(h_scr[...], wg_ref, wu_ref, wd_ref)

    @pl.when(k == pl.num_programs(1) - 1)
    def _():
        o_ref[...] = x_ref[...] + acc_scr[...]


def _dense_ffn(x2, g, wg, wu, wd, tm, tf):
    n, d = x2.shape
    f = wg.shape[1]
    row = pl.BlockSpec((tm, d), lambda i, k: (i, 0))
    return pl.pallas_call(
        _dense_ffn_kernel,
        grid=(n // tm, f // tf),
        in_specs=[row, pl.BlockSpec((1, d), lambda i, k: (0, 0)),
                  pl.BlockSpec((d, tf), lambda i, k: (0, k)),
                  pl.BlockSpec((d, tf), lambda i, k: (0, k)),
                  pl.BlockSpec((tf, d), lambda i, k: (k, 0))],
        out_specs=row,
        out_shape=jax.ShapeDtypeStruct((n, d), F32),
        scratch_shapes=[pltpu.VMEM((tm, d), BF16), pltpu.VMEM((tm, d), F32)],
        compiler_params=_cparams(("parallel", "arbitrary")),
        name="dense_ffn",
    )(x2, g, wg, wu, wd)


def _moe_ffn_kernel(be_ref, nused_ref, xb_ref, wg_ref, wu_ref, wd_ref, y_ref, h_scr, acc_scr):
    j = pl.program_id(0)
    k = pl.program_id(1)
    last = pl.num_programs(1) - 1
    used = j < nused_ref[0]

    mb, d = h_scr.shape
    sub = d // LANES

    @pl.when(used & (k == 0))
    def _():
        for j in range(sub):
            h_scr[:, j * LANES:(j + 1) * LANES] = xb_ref[pl.ds(j, mb, stride=sub), :].astype(BF16)
        acc_scr[...] = jnp.zeros_like(acc_scr)

    @pl.when(used)
    def _():
        acc_scr[...] += _swiglu_step(h_scr[...], wg_ref, wu_ref, wd_ref)

    @pl.when(used & (k == last))
    def _():
        _store_row_tiles(y_ref, acc_scr[...])

    @pl.when(jnp.logical_not(used) & (k == last))
    def _():
        y_ref[...] = jnp.zeros_like(y_ref)


def _moe_ffn(buf, block_expert, n_used, wg, wu, wd, mb, tf):
    d, f = wg.shape[1], wg.shape[2]
    sub = d // LANES
    p = buf.shape[0] // sub
    nk = f // tf
    kk = lambda j, k, be, nu: jnp.where(j < nu[0], k, nk - 1)
    row = pl.BlockSpec((mb * sub, LANES), lambda j, k, be, nu: (j, 0))
    grid_spec = pltpu.PrefetchScalarGridSpec(
        num_scalar_prefetch=2,
        grid=(p // mb, nk),
        in_specs=[row,
                  pl.BlockSpec((None, d, tf), lambda j, k, be, nu: (be[j], 0, kk(j, k, be, nu))),
                  pl.BlockSpec((None, d, tf), lambda j, k, be, nu: (be[j], 0, kk(j, k, be, nu))),
                  pl.BlockSpec((None, tf, d), lambda j, k, be, nu: (be[j], kk(j, k, be, nu), 0))],
        out_specs=row,
        scratch_shapes=[pltpu.VMEM((mb, d), BF16), pltpu.VMEM((mb, d), F32)],
    )
    return pl.pallas_call(
        _moe_ffn_kernel,
        grid_spec=grid_spec,
        out_shape=jax.ShapeDtypeStruct((p * sub, LANES), F32),
        compiler_params=_cparams(("parallel", "arbitrary")),
        name="moe_ffn",
    )(block_expert, n_used, buf, wg, wu, wd)


ISSUE_UNROLL = 8


def _issue_tile_copies(chunk, copy_fn):
    def body(t8, carry):
        for u in range(ISSUE_UNROLL):
            copy_fn(t8 * ISSUE_UNROLL + u).start()
        return carry

    lax.fori_loop(0, chunk // ISSUE_UNROLL, body, 0)


def _tile(ref, idx, sub):
    return ref.at[pl.ds(pl.multiple_of(idx * sub, sub), sub)]


def _dispatch_kernel(slot_ref, h_ref, out_ref, zero_scr, sem, *, n_tok_steps, sub):
    i = pl.program_id(0)
    n_copies = slot_ref.shape[-1]
    dst = lambda e: _tile(out_ref, slot_ref[0, 0, e], sub)

    @pl.when(i == 0)
    def _():
        zero_scr[...] = jnp.zeros_like(zero_scr)

    @pl.when(i < n_tok_steps)
    def _():
        _issue_tile_copies(n_copies, lambda e: pltpu.make_async_copy(_tile(h_ref, e // TOP_K, sub), dst(e), sem))

    @pl.when(i >= n_tok_steps)
    def _():
        _issue_tile_copies(n_copies, lambda e: pltpu.make_async_copy(zero_scr, dst(e), sem))

    for _ in range(TOP_K):
        pltpu.make_async_copy(h_ref, out_ref.at[pl.ds(0, h_ref.shape[0])], sem).wait()


def _dispatch_rows(h, slot, tm, sub):
    n = h.shape[0] // sub
    p = slot.shape[0]
    n_copies = tm * TOP_K
    assert p % n_copies == 0 and n % tm == 0 and n_copies % ISSUE_UNROLL == 0
    n_tok_steps = n // tm
    return pl.pallas_call(
        functools.partial(_dispatch_kernel, n_tok_steps=n_tok_steps, sub=sub),
        grid=(p // n_copies,),
        in_specs=[pl.BlockSpec((1, 1, n_copies), lambda i: (i, 0, 0), memory_space=pltpu.SMEM),
                  pl.BlockSpec((tm * sub, LANES), lambda i: (jnp.minimum(i, n_tok_steps - 1), 0))],
        out_specs=pl.BlockSpec(memory_space=pl.ANY),
        out_shape=jax.ShapeDtypeStruct((p * sub, LANES), h.dtype),
        scratch_shapes=[pltpu.VMEM((sub, LANES), h.dtype), pltpu.SemaphoreType.DMA(())],
        compiler_params=_cparams(("arbitrary",)),
        name="dispatch_rows",
    )(slot.reshape(p // n_copies, 1, n_copies), h)


def _combine_kernel(cur_ref, nxt_ref, x_ref, route_ref, y_ref, *rest, final_norm, sub):
    o_ref, ybuf, sem = rest[-3:]
    i = pl.program_id(0)
    tm = x_ref.shape[0]
    n_copies = tm * TOP_K
    par = i % 2

    def fetch(idx_ref, buf):
        _issue_tile_copies(n_copies, lambda e: pltpu.make_async_copy(
            _tile(y_ref, idx_ref[0, 0, e], sub), _tile(ybuf.at[buf], e, sub), sem.at[buf]))

    @pl.when(i == 0)
    def _():
        fetch(cur_ref, 0)

    @pl.when(i + 1 < pl.num_programs(0))
    def _():
        fetch(nxt_ref, 1 - par)

    pltpu.make_async_copy(ybuf.at[par], ybuf.at[par], sem.at[par]).wait()

    route = route_ref[...]
    gates = [route[:, TOP_K + k:TOP_K + k + 1] for k in range(TOP_K)]
    for j in range(sub):
        sl = slice(j * LANES, (j + 1) * LANES)
        mix = None
        for k in range(TOP_K):
            term = gates[k] * ybuf[par, pl.ds(k * sub + j, tm, stride=TOP_K * sub), :]
            mix = term if mix is None else mix + term
        o_ref[:, sl] = x_ref[:, sl] + mix
    if final_norm:
        o_ref[...] = _rms(o_ref[...], rest[0][...])


def _combine(x2, y, dest, route, tm, g_final=None):
    n, d = x2.shape
    sub = d // LANES
    n_copies = tm * TOP_K
    steps = n // tm
    assert n_copies % ISSUE_UNROLL == 0
    dest3 = dest.reshape(steps, 1, n_copies)
    row = pl.BlockSpec((tm, d), lambda i: (i, 0))
    in_specs = [pl.BlockSpec((1, 1, n_copies), lambda i: (i, 0, 0), memory_space=pltpu.SMEM),
                pl.BlockSpec((1, 1, n_copies), lambda i: (jnp.minimum(i + 1, steps - 1), 0, 0),
                             memory_space=pltpu.SMEM),
                row, pl.BlockSpec((tm, LANES), lambda i: (i, 0)), pl.BlockSpec(memory_space=pl.ANY)]
    args = [dest3, dest3, x2, route, y]
    if g_final is not None:
        in_specs.append(pl.BlockSpec((1, d), lambda i: (0, 0)))
        args.append(g_final)
    return pl.pallas_call(
        functools.partial(_combine_kernel, final_norm=g_final is not None, sub=sub),
        grid=(steps,), in_specs=in_specs, out_specs=row,
        out_shape=jax.ShapeDtypeStruct((n, d), F32),
        scratch_shapes=[pltpu.VMEM((2, n_copies * sub, LANES), F32), pltpu.SemaphoreType.DMA((2,))],
        compiler_params=_cparams(("arbitrary",)), name="moe_combine",
    )(*args)


def _final_norm_kernel(x_ref, g_ref, o_ref):
    o_ref[...] = _rms(x_ref[...], g_ref[...])


def _final_norm(x2, g, tm):
    n, d = x2.shape
    row = pl.BlockSpec((tm, d), lambda i: (i, 0))
    return pl.pallas_call(
        _final_norm_kernel, grid=(n // tm,), in_specs=[row, pl.BlockSpec((1, d), lambda i: (0, 0))],
        out_specs=row, out_shape=jax.ShapeDtypeStruct((n, d), F32),
        compiler_params=_cparams(("parallel",)), name="final_norm",
    )(x2, g)


def _routing_plan(route, n_experts, mb):
    n = route.shape[0]
    n_assign = n * TOP_K
    flat_e = route[:, :TOP_K].astype(jnp.int32).reshape(-1)
    onehot = (flat_e[:, None] == jnp.arange(n_experts)[None, :]).astype(jnp.int32)
    csum = jnp.cumsum(onehot, axis=0)
    rank = jnp.sum(csum * onehot, axis=1) - 1
    counts = csum[-1]
    padded = (counts + mb - 1) // mb * mb
    pend = jnp.cumsum(padded)
    pstart = pend - padded
    dest = jnp.sum(pstart[None, :] * onehot, axis=1) + rank
    assert n_assign % mb == 0
    n_blocks = n_assign // mb + n_experts
    block_first = jnp.arange(n_blocks) * mb
    block_expert = jnp.minimum(jnp.sum(block_first[:, None] >= pend[None, :], axis=1), n_experts - 1)
    n_used = (pend[-1] // mb).reshape(1)
    pads = padded - counts
    cpad = jnp.cumsum(pads)
    i = jnp.arange(n_experts * mb)
    grp = jnp.sum(i[:, None] >= cpad[None, :], axis=1)
    grp_hot = (grp[:, None] == jnp.arange(n_experts + 1)[None, :]).astype(jnp.int32)
    first_free = jnp.concatenate([pstart + counts, pend[-1:]])
    before = jnp.concatenate([jnp.zeros((1,), cpad.dtype), cpad])
    free = jnp.sum(grp_hot * (first_free - before)[None, :], axis=1) + i
    slot = jnp.concatenate([dest, free]).astype(jnp.int32)
    return dest.astype(jnp.int32), slot, block_expert.astype(jnp.int32), n_used.astype(jnp.int32)


def _tiles(n, t):
    return dict(tm_proj=math.gcd(512, t), tm_ffn=math.gcd(1024, n), tf=512, tq_dil=256,
                tq_mem=math.gcd(512, t), moe_block=512)


@jax.jit
def _forward(x, mem, g_mix_norm, w_in, rpb, g_mem_norm, w_mem_kv, g_mix_out, w_out,
             g_ffn_norm, w_dense_gate, w_dense_up, w_dense_down, w_router,
             w_moe_gate, w_moe_up, w_moe_down, g_final):
    b, t, d = x.shape
    n = b * t
    depth = w_in.shape[0]
    n_experts = w_router.shape[-1]
    ts = _tiles(n, t)
    tf = math.gcd(ts["tf"], w_dense_gate.shape[-1])
    tables = _rope_tables(t)
    row1 = lambda a: a.reshape(1, -1)
    x2 = x.reshape(n, d)
    out = None
    sub = d // LANES
    for layer in range(depth):
        qa, ka, va, qm, qbs, kbs, vbs = _inproj(x2, row1(g_mix_norm[layer]), w_in[layer].astype(BF16),
                                                tables, t, ts["tm_proj"])
        shp = lambda a: a.reshape(b, t, a.shape[-1])
        oa = _na_attention(shp(qa), shp(ka), shp(va), _na_bias_table(rpb[layer])).reshape(n, NA_WIDTH)
        obs, lses = [], []
        for (window, dil), qb, kb, vb in zip(DIL_BRANCHES, qbs, kbs, vbs):
            o, lse = _dilated_branch(qb, kb, vb, window // (2 * dil), ts["tq_dil"])
            obs.append(o)
            lses.append(lse)
        km, vm = _mem_kv(mem, row1(g_mem_norm[layer]), w_mem_kv[layer].astype(BF16))
        om = _mem_attention(shp(qm), km, vm, ts["tq_mem"]).reshape(n, MEM_WIDTH)
        gm, wo = row1(g_mix_out[layer]), w_out[layer].astype(BF16)
        i = layer // 2
        if layer % 2 == 0:
            x2 = _outproj(oa, obs, lses, om, x2, gm, wo, t, ts["tm_proj"])
            x2 = _dense_ffn(x2, row1(g_ffn_norm[layer]), w_dense_gate[i].astype(BF16),
                            w_dense_up[i].astype(BF16), w_dense_down[i].astype(BF16), ts["tm_ffn"], tf)
            if layer == depth - 1:
                out = _final_norm(x2, row1(g_final), ts["tm_proj"])
        else:
            wr = jnp.pad(w_router[i], ((0, 0), (0, LANES - n_experts)))
            x2, h, route = _outproj(oa, obs, lses, om, x2, gm, wo, t, ts["tm_proj"],
                                    router=(row1(g_ffn_norm[layer]), wr, n_experts))
            mb = ts["moe_block"]
            dest, slot, block_expert, n_used = _routing_plan(route, n_experts, mb)
            buf = _dispatch_rows(h, slot, ts["tm_proj"], sub)
            y = _moe_ffn(buf, block_expert, n_used, w_moe_gate[i].astype(BF16), w_moe_up[i].astype(BF16),
                         w_moe_down[i].astype(BF16), mb, tf)
            if layer == depth - 1:
                out = _combine(x2, y, dest, route, ts["tm_proj"], g_final=row1(g_final))
            else:
                x2 = _combine(x2, y, dest, route, ts["tm_proj"])
    return out.reshape(b, t, d)


def kernel(x, mem, g_mix_norm, w_in, rpb, g_mem_norm, w_mem_kv, g_mix_out, w_out, g_ffn_norm,
           w_dense_gate, w_dense_up, w_dense_down, w_router, w_moe_gate, w_moe_up, w_moe_down, g_final):
    return _forward(x, mem, g_mix_norm, w_in, rpb, g_mem_norm, w_mem_kv, g_mix_out, w_out, g_ffn_norm,
                    w_dense_gate, w_dense_up, w_dense_down, w_router, w_moe_gate, w_moe_up, w_moe_down,
                    g_final)
```

```python
import functools
import math

import jax
import jax.numpy as jnp
from jax import lax
from jax.experimental import pallas as pl
from jax.experimental.pallas import tpu as pltpu

HEAD_DIM = 64
NA_HEADS = 6
DIL_HEADS = 6
MEM_HEADS = 4
NA_WIDTH = NA_HEADS * HEAD_DIM
DIL_WIDTH = DIL_HEADS * HEAD_DIM
MEM_WIDTH = MEM_HEADS * HEAD_DIM
GRID_W = 64
NA_WIN_ROWS = 8
NA_WIN_COLS = 16
DIL_BRANCHES = ((128, 1), (512, 4), (2048, 16))
ROPE_THETA = 500000.0
ROT_DIM = HEAD_DIM // 4
TOP_K = 2
RMS_EPS = 1e-6
ATTN_SCALE = HEAD_DIM ** -0.5

LANES = 128
HEADS_PER_LANE_GROUP = LANES // HEAD_DIM
NEG = -1e30
VMEM_LIMIT = 56 * 1024 * 1024

F32 = jnp.float32
BF16 = jnp.bfloat16


def _cparams(sem):
    return pltpu.CompilerParams(dimension_semantics=sem, vmem_limit_bytes=VMEM_LIMIT)


def _rms(x, g):
    ms = jnp.mean(x * x, axis=-1, keepdims=True)
    return x * lax.rsqrt(ms + RMS_EPS) * g


def _pair_heads_attention(q, k, v, bias_fn, n_heads):
    m = q.shape[0]
    assert HEADS_PER_LANE_GROUP == 2
    n_pairs = n_heads // HEADS_PER_LANE_GROUP
    lo = lax.broadcasted_iota(jnp.int32, (m, LANES), 1) < HEAD_DIM
    group = lambda a, p: a[:, p * LANES:(p + 1) * LANES]
    s_parts = []
    for p in range(n_pairs):
        qp = group(q, p)
        zero = jnp.zeros_like(qp)
        q2 = jnp.concatenate([jnp.where(lo, qp, zero), jnp.where(lo, zero, qp)], axis=0)
        s_parts.append(lax.dot_general(q2, group(k, p), (((1,), (1,)), ((), ())),
                                       preferred_element_type=F32) + bias_fn(p))
    s = jnp.concatenate(s_parts, axis=0)
    mx = jnp.max(s, axis=-1, keepdims=True)
    e = jnp.exp(s - mx)
    den = jnp.sum(e, axis=-1, keepdims=True)
    eb = e.astype(BF16)
    outs = []
    for p in range(n_pairs):
        rows = slice(p * 2 * m, (p + 1) * 2 * m)
        o2 = jnp.dot(eb[rows], group(v, p), preferred_element_type=F32) / den[rows]
        outs.append(jnp.where(lo, o2[:m], o2[m:]))
    lse = mx + jnp.log(den)
    return jnp.concatenate(outs, axis=-1), [lse[h * m:(h + 1) * m] for h in range(n_heads)]


def _inproj_kernel(x_ref, g_ref, w_ref, cos_ref, sa_ref, sb_ref,
                   qa_ref, ka_ref, va_ref, qm_ref, *rest, dils):
    dil_refs, scr = rest[:-1], rest[-1]
    tm = x_ref.shape[0]
    h = _rms(x_ref[...], g_ref[...]).astype(BF16)

    def emit_dilated(val, refs):
        groups = val.shape[1] // LANES
        for g in range(groups):
            scr[g] = val[:, g * LANES:(g + 1) * LANES]
        for dil, ref in zip(dils, refs):
            for r in range(dil):
                for g in range(groups):
                    ref[r, :, g * LANES:(g + 1) * LANES] = (
                        scr[g, pl.ds(r, tm // dil, stride=dil), :].astype(BF16))

    def proj(c0, width):
        return jnp.dot(h, w_ref[:, c0:c0 + width], preferred_element_type=F32)

    def rope(a):
        cos, sa, sb = cos_ref[...], sa_ref[...], sb_ref[...]
        half = ROT_DIM // 2
        parts = []
        for gidx in range(a.shape[1] // LANES):
            xg = a[:, gidx * LANES:(gidx + 1) * LANES]
            parts.append(xg * cos + pltpu.roll(xg, LANES - half, 1) * sa + pltpu.roll(xg, half, 1) * sb)
        return jnp.concatenate(parts, axis=-1)

    nd = len(dils)
    c = 0
    qa_ref[...] = (proj(c, NA_WIDTH) * ATTN_SCALE).astype(BF16); c += NA_WIDTH
    ka_ref[...] = proj(c, NA_WIDTH).astype(BF16); c += NA_WIDTH
    va_ref[...] = proj(c, NA_WIDTH).astype(BF16); c += NA_WIDTH
    emit_dilated(rope(proj(c, DIL_WIDTH) * ATTN_SCALE), dil_refs[0:nd]); c += DIL_WIDTH
    emit_dilated(rope(proj(c, DIL_WIDTH)), dil_refs[nd:2 * nd]); c += DIL_WIDTH
    emit_dilated(proj(c, DIL_WIDTH), dil_refs[2 * nd:3 * nd]); c += DIL_WIDTH
    qm_ref[...] = (proj(c, MEM_WIDTH) * ATTN_SCALE).astype(BF16)


def _rope_tables(t):
    half = ROT_DIM // 2
    inv_freq = ROPE_THETA ** (-jnp.arange(0, ROT_DIM, 2, dtype=F32) / ROT_DIM)
    ang = jnp.arange(t, dtype=F32)[:, None] * inv_freq[None, :]
    cos, sin = jnp.cos(ang), jnp.sin(ang)
    ones = jnp.ones((t, HEAD_DIM - ROT_DIM), F32)
    zeros = jnp.zeros((t, HEAD_DIM - ROT_DIM), F32)
    zh = jnp.zeros((t, half), F32)
    cos_h = jnp.concatenate([cos, cos, ones], axis=1)
    sa_h = jnp.concatenate([-sin, zh, zeros], axis=1)
    sb_h = jnp.concatenate([zh, sin, zeros], axis=1)
    tile = lambda a: jnp.tile(a, (1, HEADS_PER_LANE_GROUP))
    return tile(cos_h), tile(sa_h), tile(sb_h)


def _inproj(x2, g, w, tables, t, tm):
    n, d = x2.shape
    b = n // t
    tpb = t // tm
    dils = tuple(dil for _, dil in DIL_BRANCHES)
    assert all(tm % (dil * 16) == 0 for dil in dils)
    row = lambda width: pl.BlockSpec((tm, width), lambda i: (i, 0))
    tab = pl.BlockSpec((tm, LANES), lambda i: (i % tpb, 0))
    plain = [NA_WIDTH] * 3 + [MEM_WIDTH]
    dil_specs = [pl.BlockSpec((None, dil, tm // dil, DIL_WIDTH), lambda i: (i // tpb, 0, i % tpb, 0))
                 for dil in dils] * 3
    dil_shapes = [jax.ShapeDtypeStruct((b, dil, t // dil, DIL_WIDTH), BF16) for dil in dils] * 3
    outs = pl.pallas_call(
        functools.partial(_inproj_kernel, dils=dils),
        grid=(n // tm,),
        in_specs=[row(d), pl.BlockSpec((1, d), lambda i: (0, 0)),
                  pl.BlockSpec(w.shape, lambda i: (0, 0)), tab, tab, tab],
        out_specs=[row(wd) for wd in plain] + dil_specs,
        out_shape=[jax.ShapeDtypeStruct((n, wd), BF16) for wd in plain] + dil_shapes,
        scratch_shapes=[pltpu.VMEM((DIL_WIDTH // LANES, tm, LANES), F32)],
        compiler_params=_cparams(("parallel",)),
        name="inproj",
    )(x2, g, w, *tables)
    nd = len(dils)
    qa, ka, va, qm = outs[:4]
    return qa, ka, va, qm, outs[4:4 + nd], outs[4 + nd:4 + 2 * nd], outs[4 + 2 * nd:]


def _na_bias_table(rpb):
    c = jnp.arange(GRID_W)
    c0 = jnp.clip(c - NA_WIN_COLS // 2, 0, GRID_W - NA_WIN_COLS)
    kc = jnp.arange(GRID_W)
    valid = (kc[None, :] >= c0[:, None]) & (kc[None, :] < c0[:, None] + NA_WIN_COLS)
    coff = kc[None, :] - c[:, None] + (NA_WIN_COLS - 1)
    onehot = (coff[None] == jnp.arange(2 * NA_WIN_COLS - 1)[:, None, None]).astype(F32)
    by_col = jnp.einsum("hrd,dck->hrck", rpb.astype(F32), onehot, precision=lax.Precision.HIGHEST)
    by_col = jnp.where(valid[None, None], by_col, NEG)
    tab = jnp.stack([by_col[:, d:d + NA_WIN_ROWS] for d in range(NA_WIN_ROWS)], axis=1)
    tab = tab.transpose(1, 0, 3, 2, 4)
    return tab.reshape(NA_WIN_ROWS * NA_HEADS // HEADS_PER_LANE_GROUP, HEADS_PER_LANE_GROUP * GRID_W,
                       NA_WIN_ROWS * GRID_W)


def _na_kernel(q_ref, kp_ref, kc_ref, kn_ref, vp_ref, vc_ref, vn_ref, bias_ref, o_ref,
               kwin, vwin, *, rows):
    j = pl.program_id(1)
    blk = NA_WIN_ROWS * GRID_W
    for idx, (kr, vr) in enumerate(((kp_ref, vp_ref), (kc_ref, vc_ref), (kn_ref, vn_ref))):
        kwin[idx * blk:(idx + 1) * blk, :] = kr[...]
        vwin[idx * blk:(idx + 1) * blk, :] = vr[...]

    n_pairs = NA_HEADS // HEADS_PER_LANE_GROUP

    def one_row(i):
        r = j * NA_WIN_ROWS + i
        r0 = jnp.clip(r - NA_WIN_ROWS // 2, 0, rows - NA_WIN_ROWS)
        dlt = r0 - r + (NA_WIN_ROWS - 1)
        start = pl.multiple_of((r0 - (j - 1) * NA_WIN_ROWS) * GRID_W, GRID_W)
        q = q_ref[pl.ds(pl.multiple_of(i * GRID_W, GRID_W), GRID_W), :]
        k = kwin[pl.ds(start, blk), :]
        v = vwin[pl.ds(start, blk), :]
        o, _ = _pair_heads_attention(q, k, v, lambda p: bias_ref[dlt * n_pairs + p], NA_HEADS)
        o_ref[pl.ds(pl.multiple_of(i * GRID_W, GRID_W), GRID_W), :] = o.astype(BF16)

    def two_rows(i2, carry):
        one_row(2 * i2)
        one_row(2 * i2 + 1)
        return carry

    lax.fori_loop(0, NA_WIN_ROWS // 2, two_rows, 0)


def _na_attention(q, k, v, bias):
    b, t, w = q.shape
    rows = t // GRID_W
    assert rows % NA_WIN_ROWS == 0 and rows >= NA_WIN_ROWS
    nj = rows // NA_WIN_ROWS
    blk = NA_WIN_ROWS * GRID_W
    cur = pl.BlockSpec((None, blk, w), lambda bi, j: (bi, j, 0))
    prev = pl.BlockSpec((None, blk, w), lambda bi, j: (bi, jnp.maximum(j - 1, 0), 0))
    nxt = pl.BlockSpec((None, blk, w), lambda bi, j: (bi, jnp.minimum(j + 1, nj - 1), 0))
    return pl.pallas_call(
        functools.partial(_na_kernel, rows=rows),
        grid=(b, nj),
        in_specs=[cur, prev, cur, nxt, prev, cur, nxt,
                  pl.BlockSpec(bias.shape, lambda bi, j: (0, 0, 0))],
        out_specs=cur,
        out_shape=jax.ShapeDtypeStruct((b, t, w), BF16),
        scratch_shapes=[pltpu.VMEM((3 * blk, w), BF16), pltpu.VMEM((3 * blk, w), BF16)],
        compiler_params=_cparams(("parallel", "parallel")),
        name="na_attn",
    )(q, k, k, k, v, v, v, bias)


def _dil_kernel(q_ref, kp_ref, kc_ref, kn_ref, vp_ref, vc_ref, vn_ref, o_ref, lse_ref,
                kwin, vwin, *, seg, tq, halo):
    i = pl.program_id(2)
    width = tq + 2 * halo
    kwin[0:halo, :] = kp_ref[...]
    kwin[halo:halo + tq, :] = kc_ref[...]
    kwin[halo + tq:width, :] = kn_ref[...]
    vwin[0:halo, :] = vp_ref[...]
    vwin[halo:halo + tq, :] = vc_ref[...]
    vwin[halo + tq:width, :] = vn_ref[...]

    sq = math.gcd(tq, 2 * halo)
    sw = sq + 2 * halo
    a = lax.broadcasted_iota(jnp.int32, (sq, sw), 0)
    c = lax.broadcasted_iota(jnp.int32, (sq, sw), 1)
    band = (c >= a) & (c <= a + 2 * halo)
    lo = lax.broadcasted_iota(jnp.int32, (sq, LANES), 1) < HEAD_DIM

    def sub_block(s):
        row0 = pl.multiple_of(s * sq, sq)
        first_key = i * tq + s * sq - halo
        valid = band & (c >= -first_key) & (c < seg - first_key)
        negb = jnp.where(valid, 0.0, NEG).astype(F32)
        negb2 = jnp.concatenate([negb] * HEADS_PER_LANE_GROUP, axis=0)
        o, lses = _pair_heads_attention(q_ref[pl.ds(row0, sq), :], kwin[pl.ds(row0, sw), :],
                                        vwin[pl.ds(row0, sw), :], lambda p: negb2, DIL_HEADS)
        o_ref[pl.ds(row0, sq), :] = o.astype(BF16)
        lse_ref[pl.ds(row0, sq), :] = jnp.concatenate(
            [jnp.where(lo, lses[2 * p], lses[2 * p + 1]) for p in range(DIL_HEADS // 2)], axis=-1)

    n_sub = tq // sq
    if n_sub % 2 == 0:
        def two_sub_blocks(s2, carry):
            sub_block(2 * s2)
            sub_block(2 * s2 + 1)
            return carry
        lax.fori_loop(0, n_sub // 2, two_sub_blocks, 0)
    else:
        for s in range(n_sub):
            sub_block(s)


def _dilated_branch(q, k, v, n_side, tq_max):
    b, dil, seg, w = q.shape
    halo = n_side
    assert seg % halo == 0 and halo % 16 == 0
    tq = math.gcd(tq_max, seg)
    assert tq % halo == 0
    hb = tq // halo
    nhalo = seg // halo
    cur = pl.BlockSpec((None, None, tq, w), lambda bi, r, i: (bi, r, i, 0))
    prev = pl.BlockSpec((None, None, halo, w), lambda bi, r, i: (bi, r, jnp.maximum(i * hb - 1, 0), 0))
    nxt = pl.BlockSpec((None, None, halo, w),
                       lambda bi, r, i: (bi, r, jnp.minimum((i + 1) * hb, nhalo - 1), 0))
    return pl.pallas_call(
        functools.partial(_dil_kernel, seg=seg, tq=tq, halo=halo),
        grid=(b, dil, seg // tq),
        in_specs=[cur, prev, cur, nxt, prev, cur, nxt],
        out_specs=[cur, cur],
        out_shape=[jax.ShapeDtypeStruct((b, dil, seg, w), BF16),
                   jax.ShapeDtypeStruct((b, dil, seg, w), F32)],
        scratch_shapes=[pltpu.VMEM((tq + 2 * halo, w), BF16), pltpu.VMEM((tq + 2 * halo, w), BF16)],
        compiler_params=_cparams(("parallel", "parallel", "parallel")),
        name=f"dilated_d{dil}",
    )(q, k, k, k, v, v, v)


def _memkv_kernel(mem_ref, g_ref, w_ref, k_ref, v_ref):
    h = _rms(mem_ref[...], g_ref[...]).astype(BF16)
    kv = jnp.dot(h, w_ref[...], preferred_element_type=F32)
    k_ref[...] = kv[:, :MEM_WIDTH].astype(BF16)
    v_ref[...] = kv[:, MEM_WIDTH:].astype(BF16)


def _mem_kv(mem, g, w):
    b, m, d = mem.shape
    blk = lambda width: pl.BlockSpec((None, m, width), lambda bi: (bi, 0, 0))
    return pl.pallas_call(
        _memkv_kernel,
        grid=(b,),
        in_specs=[blk(d), pl.BlockSpec((1, d), lambda bi: (0, 0)), pl.BlockSpec(w.shape, lambda bi: (0, 0))],
        out_specs=[blk(MEM_WIDTH), blk(MEM_WIDTH)],
        out_shape=[jax.ShapeDtypeStruct((b, m, MEM_WIDTH), BF16)] * 2,
        compiler_params=_cparams(("parallel",)),
        name="mem_kv",
    )(mem, g, w)


def _memattn_kernel(q_ref, k_ref, v_ref, o_ref):
    o, _ = _pair_heads_attention(q_ref[...], k_ref[...], v_ref[...], lambda p: 0.0, MEM_HEADS)
    o_ref[...] = o.astype(BF16)


def _mem_attention(q, k, v, tq):
    b, t, w = q.shape
    m = k.shape[1]
    qs = pl.BlockSpec((None, tq, w), lambda bi, i: (bi, i, 0))
    ks = pl.BlockSpec((None, m, w), lambda bi, i: (bi, 0, 0))
    return pl.pallas_call(
        _memattn_kernel,
        grid=(b, t // tq),
        in_specs=[qs, ks, ks],
        out_specs=qs,
        out_shape=jax.ShapeDtypeStruct((b, t, w), BF16),
        compiler_params=_cparams(("parallel", "parallel")),
        name="mem_attn",
    )(q, k, v)


def _token_order(ref, scr):
    dil = ref.shape[0]
    if dil == 1:
        return ref[0].astype(F32)
    groups, tm, _ = scr.shape
    for r in range(dil):
        for g in range(groups):
            scr[g, pl.ds(r, tm // dil, stride=dil), :] = ref[r, :, g * LANES:(g + 1) * LANES].astype(F32)
    return jnp.concatenate([scr[g] for g in range(groups)], axis=-1)


def _branch_mix(ob_refs, lse_refs, o_scrs, l_scrs):
    lses = [_token_order(r, s) for r, s in zip(lse_refs, l_scrs)]
    mx = functools.reduce(jnp.maximum, lses)
    es = [jnp.exp(l - mx) for l in lses]
    den = functools.reduce(lambda p, q: p + q, es)
    acc = None
    for e, ob, scr in zip(es, ob_refs, o_scrs):
        term = (e / den) * _token_order(ob, scr)
        acc = term if acc is None else acc + term
    return acc


def _outproj_core(oa_ref, ob_refs, lse_refs, om_ref, x_ref, gm_ref, w_ref, o_scrs, l_scrs):
    gm = gm_ref[...]
    e0, e1 = NA_WIDTH, NA_WIDTH + DIL_WIDTH
    ya = _rms(oa_ref[...].astype(F32), gm[:, :e0]).astype(BF16)
    yb = _rms(_branch_mix(ob_refs, lse_refs, o_scrs, l_scrs), gm[:, e0:e1]).astype(BF16)
    ym = _rms(om_ref[...].astype(F32), gm[:, e1:]).astype(BF16)
    acc = jnp.dot(ya, w_ref[0:e0, :], preferred_element_type=F32)
    acc = acc + jnp.dot(yb, w_ref[e0:e1, :], preferred_element_type=F32)
    acc = acc + jnp.dot(ym, w_ref[e1:, :], preferred_element_type=F32)
    return x_ref[...] + acc


def _split_scratch(scrs, dils):
    it = iter(scrs)
    o_scrs = [next(it) if dil > 1 else None for dil in dils]
    l_scrs = [next(it) if dil > 1 else None for dil in dils]
    return o_scrs, l_scrs


def _outproj_kernel(oa_ref, ob1, ob2, ob3, l1, l2, l3, om_ref, x_ref, gm_ref, w_ref, xo_ref, *scrs, dils):
    xo_ref[...] = _outproj_core(oa_ref, (ob1, ob2, ob3), (l1, l2, l3), om_ref, x_ref, gm_ref, w_ref,
                                *_split_scratch(scrs, dils))


def _store_row_tiles(ref, val):
    m, d = val.shape
    for j in range(d // LANES):
        ref[pl.ds(j, m, stride=d // LANES), :] = val[:, j * LANES:(j + 1) * LANES]


def _outproj_router_kernel(oa_ref, ob1, ob2, ob3, l1, l2, l3, om_ref, x_ref, gm_ref, w_ref,
                           gf_ref, wr_ref, xo_ref, h_ref, route_ref, *scrs, n_experts, dils):
    xn = _outproj_core(oa_ref, (ob1, ob2, ob3), (l1, l2, l3), om_ref, x_ref, gm_ref, w_ref,
                       *_split_scratch(scrs, dils))
    xo_ref[...] = xn
    h = _rms(xn, gf_ref[...])
    _store_row_tiles(h_ref, h)
    logits = jnp.dot(h.astype(BF16), wr_ref[...], preferred_element_type=F32)
    lane = lax.broadcasted_iota(jnp.int32, logits.shape, 1)
    lg = jnp.where(lane < n_experts, logits, NEG)
    m1 = jnp.max(lg, axis=-1, keepdims=True)
    i1 = jnp.min(jnp.where(lg == m1, lane, LANES), axis=-1, keepdims=True)
    lg2 = jnp.where(lane == i1, NEG, lg)
    m2 = jnp.max(lg2, axis=-1, keepdims=True)
    i2 = jnp.min(jnp.where(lg2 == m2, lane, LANES), axis=-1, keepdims=True)
    e2 = jnp.exp(m2 - m1)
    g1 = 1.0 / (1.0 + e2)
    g2 = e2 / (1.0 + e2)
    route = jnp.where(lane == 0, i1.astype(F32), 0.0)
    route = jnp.where(lane == 1, i2.astype(F32), route)
    route = jnp.where(lane == 2, g1, route)
    route = jnp.where(lane == 3, g2, route)
    route_ref[...] = route


def _outproj(oa, obs, lses, om, x2, gm, w, t, tm, router=None):
    n, d = x2.shape
    tpb = t // tm
    row = lambda width: pl.BlockSpec((tm, width), lambda i: (i, 0))
    full = lambda a: pl.BlockSpec(a.shape, lambda i: (0,) * a.ndim)
    grouped = lambda a: pl.BlockSpec((None, a.shape[1], tm // a.shape[1], a.shape[3]),
                                     lambda i: (i // tpb, 0, i % tpb, 0))
    dils = tuple(a.shape[1] for a in obs)
    args = [oa, *obs, *lses, om, x2, gm, w]
    in_specs = ([row(NA_WIDTH)] + [grouped(a) for a in obs] + [grouped(a) for a in lses]
                + [row(MEM_WIDTH), row(d), full(gm), full(w)])
    scratch = [pltpu.VMEM((DIL_WIDTH // LANES, tm, LANES), F32)] * (2 * sum(dil > 1 for dil in dils))
    if router is None:
        return pl.pallas_call(
            functools.partial(_outproj_kernel, dils=dils),
            grid=(n // tm,), in_specs=in_specs, out_specs=row(d),
            out_shape=jax.ShapeDtypeStruct((n, d), F32), scratch_shapes=scratch,
            compiler_params=_cparams(("parallel",)), name="outproj",
        )(*args)
    gf, wr, n_experts = router
    sub = d // LANES
    return pl.pallas_call(
        functools.partial(_outproj_router_kernel, n_experts=n_experts, dils=dils),
        grid=(n // tm,), in_specs=in_specs + [full(gf), full(wr)],
        out_specs=[row(d), pl.BlockSpec((tm * sub, LANES), lambda i: (i, 0)), row(LANES)],
        out_shape=[jax.ShapeDtypeStruct((n, d), F32), jax.ShapeDtypeStruct((n * sub, LANES), F32),
                   jax.ShapeDtypeStruct((n, LANES), F32)],
        scratch_shapes=scratch,
        compiler_params=_cparams(("parallel",)), name="outproj_router",
    )(*args, gf, wr)


def _swiglu_step(h, wg_ref, wu_ref, wd_ref):
    gate = jnp.dot(h, wg_ref[...], preferred_element_type=F32)
    up = jnp.dot(h, wu_ref[...], preferred_element_type=F32)
    act = (gate / (1.0 + jnp.exp(-gate))) * up
    return jnp.dot(act.astype(BF16), wd_ref[...], preferred_element_type=F32)


def _dense_ffn_kernel(x_ref, g_ref, wg_ref, wu_ref, wd_ref, o_ref, h_scr, acc_scr):
    k = pl.program_id(1)

    @pl.when(k == 0)
    def _():
        h_scr[...] = _rms(x_ref[...], g_ref[...]).astype(BF16)
        acc_scr[...] = jnp.zeros_like(acc_scr)

    acc_scr[...] += _swiglu_step(h_scr[...], wg_ref, wu_ref, wd_ref)

    @pl.when(k == pl.num_programs(1) - 1)
    def _():
        o_ref[...] = x_ref[...] + acc_scr[...]


def _dense_ffn(x2, g, wg, wu, wd, tm, tf):
    n, d = x2.shape
    f = wg.shape[1]
    row = pl.BlockSpec((tm, d), lambda i, k: (i, 0))
    return pl.pallas_call(
        _dense_ffn_kernel,
        grid=(n // tm, f // tf),
        in_specs=[row, pl.BlockSpec((1, d), lambda i, k: (0, 0)),
                  pl.BlockSpec((d, tf), lambda i, k: (0, k)),
                  pl.BlockSpec((d, tf), lambda i, k: (0, k)),
                  pl.BlockSpec((tf, d), lambda i, k: (k, 0))],
        out_specs=row,
        out_shape=jax.ShapeDtypeStruct((n, d), F32),
        scratch_shapes=[pltpu.VMEM((tm, d), BF16), pltpu.VMEM((tm, d), F32)],
        compiler_params=_cparams(("parallel", "arbitrary")),
        name="dense_ffn",
    )(x2, g, wg, wu, wd)


def _moe_ffn_kernel(be_ref, nused_ref, xb_ref, wg_ref, wu_ref, wd_ref, y_ref, h_scr, acc_scr):
    j = pl.program_id(0)
    k = pl.program_id(1)
    last = pl.num_programs(1) - 1
    used = j < nused_ref[0]

    mb, d = h_scr.shape
    sub = d // LANES

    @pl.when(used & (k == 0))
    def _():
        for j in range(sub):
            h_scr[:, j * LANES:(j + 1) * LANES] = xb_ref[pl.ds(j, mb, stride=sub), :].astype(BF16)
        acc_scr[...] = jnp.zeros_like(acc_scr)

    @pl.when(used)
    def _():
        acc_scr[...] += _swiglu_step(h_scr[...], wg_ref, wu_ref, wd_ref)

    @pl.when(used & (k == last))
    def _():
        _store_row_tiles(y_ref, acc_scr[...])

    @pl.when(jnp.logical_not(used) & (k == last))
    def _():
        y_ref[...] = jnp.zeros_like(y_ref)


def _moe_ffn(buf, block_expert, n_used, wg, wu, wd, mb, tf):
    d, f = wg.shape[1], wg.shape[2]
    sub = d // LANES
    p = buf.shape[0] // sub
    nk = f // tf
    kk = lambda j, k, be, nu: jnp.where(j < nu[0], k, nk - 1)
    row = pl.BlockSpec((mb * sub, LANES), lambda j, k, be, nu: (j, 0))
    grid_spec = pltpu.PrefetchScalarGridSpec(
        num_scalar_prefetch=2,
        grid=(p // mb, nk),
        in_specs=[row,
                  pl.BlockSpec((None, d, tf), lambda j, k, be, nu: (be[j], 0, kk(j, k, be, nu))),
                  pl.BlockSpec((None, d, tf), lambda j, k, be, nu: (be[j], 0, kk(j, k, be, nu))),
                  pl.BlockSpec((None, tf, d), lambda j, k, be, nu: (be[j], kk(j, k, be, nu), 0))],
        out_specs=row,
        scratch_shapes=[pltpu.VMEM((mb, d), BF16), pltpu.VMEM((mb, d), F32)],
    )
    return pl.pallas_call(
        _moe_ffn_kernel,
        grid_spec=grid_spec,
        out_shape=jax.ShapeDtypeStruct((p * sub, LANES), F32),
        compiler_params=_cparams(("parallel", "arbitrary")),
        name="moe_ffn",
    )(block_expert, n_used, buf, wg, wu, wd)


ISSUE_UNROLL = 8


def _issue_tile_copies(chunk, copy_fn):
    def body(t8, carry):
        for u in range(ISSUE_UNROLL):
            copy_fn(t8 * ISSUE_UNROLL + u).start()
        return carry

    lax.fori_loop(0, chunk // ISSUE_UNROLL, body, 0)


def _tile(ref, idx, sub):
    return ref.at[pl.ds(pl.multiple_of(idx * sub, sub), sub)]


def _dispatch_kernel(slot_ref, h_ref, out_ref, zero_scr, sem, *, n_tok_steps, sub):
    i = pl.program_id(0)
    n_copies = slot_ref.shape[-1]
    dst = lambda e: _tile(out_ref, slot_ref[0, 0, e], sub)

    @pl.when(i == 0)
    def _():
        zero_scr[...] = jnp.zeros_like(zero_scr)

    @pl.when(i < n_tok_steps)
    def _():
        _issue_tile_copies(n_copies, lambda e: pltpu.make_async_copy(_tile(h_ref, e // TOP_K, sub), dst(e), sem))

    @pl.when(i >= n_tok_steps)
    def _():
        _issue_tile_copies(n_copies, lambda e: pltpu.make_async_copy(zero_scr, dst(e), sem))

    for _ in range(TOP_K):
        pltpu.make_async_copy(h_ref, out_ref.at[pl.ds(0, h_ref.shape[0])], sem).wait()


def _dispatch_rows(h, slot, tm, sub):
    n = h.shape[0] // sub
    p = slot.shape[0]
    n_copies = tm * TOP_K
    assert p % n_copies == 0 and n % tm == 0 and n_copies % ISSUE_UNROLL == 0
    n_tok_steps = n // tm
    return pl.pallas_call(
        functools.partial(_dispatch_kernel, n_tok_steps=n_tok_steps, sub=sub),
        grid=(p // n_copies,),
        in_specs=[pl.BlockSpec((1, 1, n_copies), lambda i: (i, 0, 0), memory_space=pltpu.SMEM),
                  pl.BlockSpec((tm * sub, LANES), lambda i: (jnp.minimum(i, n_tok_steps - 1), 0))],
        out_specs=pl.BlockSpec(memory_space=pl.ANY),
        out_shape=jax.ShapeDtypeStruct((p * sub, LANES), h.dtype),
        scratch_shapes=[pltpu.VMEM((sub, LANES), h.dtype), pltpu.SemaphoreType.DMA(())],
        compiler_params=_cparams(("arbitrary",)),
        name="dispatch_rows",
    )(slot.reshape(p // n_copies, 1, n_copies), h)


def _combine_kernel(cur_ref, nxt_ref, x_ref, route_ref, y_ref, *rest, final_norm, sub):
    o_ref, ybuf, sem = rest[-3:]
    i = pl.program_id(0)
    tm = x_ref.shape[0]
    n_copies = tm * TOP_K
    par = i % 2

    def fetch(idx_ref, buf):
        _issue_tile_copies(n_copies, lambda e: pltpu.make_async_copy(
            _tile(y_ref, idx_ref[0, 0, e], sub), _tile(ybuf.at[buf], e, sub), sem.at[buf]))

    @pl.when(i == 0)
    def _():
        fetch(cur_ref, 0)

    @pl.when(i + 1 < pl.num_programs(0))
    def _():
        fetch(nxt_ref, 1 - par)

    pltpu.make_async_copy(ybuf.at[par], ybuf.at[par], sem.at[par]).wait()

    route = route_ref[...]
    gates = [route[:, TOP_K + k:TOP_K + k + 1] for k in range(TOP_K)]
    for j in range(sub):
        sl = slice(j * LANES, (j + 1) * LANES)
        mix = None
        for k in range(TOP_K):
            term = gates[k] * ybuf[par, pl.ds(k * sub + j, tm, stride=TOP_K * sub), :]
            mix = term if mix is None else mix + term
        o_ref[:, sl] = x_ref[:, sl] + mix
    if final_norm:
        o_ref[...] = _rms(o_ref[...], rest[0][...])


def _combine(x2, y, dest, route, tm, g_final=None):
    n, d = x2.shape
    sub = d // LANES
    n_copies = tm * TOP_K
    steps = n // tm
    assert n_copies % ISSUE_UNROLL == 0
    dest3 = dest.reshape(steps, 1, n_copies)
    row = pl.BlockSpec((tm, d), lambda i: (i, 0))
    in_specs = [pl.BlockSpec((1, 1, n_copies), lambda i: (i, 0, 0), memory_space=pltpu.SMEM),
                pl.BlockSpec((1, 1, n_copies), lambda i: (jnp.minimum(i + 1, steps - 1), 0, 0),
                             memory_space=pltpu.SMEM),
                row, pl.BlockSpec((tm, LANES), lambda i: (i, 0)), pl.BlockSpec(memory_space=pl.ANY)]
    args = [dest3, dest3, x2, route, y]
    if g_final is not None:
        in_specs.append(pl.BlockSpec((1, d), lambda i: (0, 0)))
        args.append(g_final)
    return pl.pallas_call(
        functools.partial(_combine_kernel, final_norm=g_final is not None, sub=sub),
        grid=(steps,), in_specs=in_specs, out_specs=row,
        out_shape=jax.ShapeDtypeStruct((n, d), F32),
        scratch_shapes=[pltpu.VMEM((2, n_copies * sub, LANES), F32), pltpu.SemaphoreType.DMA((2,))],
        compiler_params=_cparams(("arbitrary",)), name="moe_combine",
    )(*args)


def _final_norm_kernel(x_ref, g_ref, o_ref):
    o_ref[...] = _rms(x_ref[...], g_ref[...])


def _final_norm(x2, g, tm):
    n, d = x2.shape
    row = pl.BlockSpec((tm, d), lambda i: (i, 0))
    return pl.pallas_call(
        _final_norm_kernel, grid=(n // tm,), in_specs=[row, pl.BlockSpec((1, d), lambda i: (0, 0))],
        out_specs=row, out_shape=jax.ShapeDtypeStruct((n, d), F32),
        compiler_params=_cparams(("parallel",)), name="final_norm",
    )(x2, g)


def _routing_plan(route, n_experts, mb):
    n = route.shape[0]
    n_assign = n * TOP_K
    flat_e = route[:, :TOP_K].astype(jnp.int32).reshape(-1)
    onehot = (flat_e[:, None] == jnp.arange(n_experts)[None, :]).astype(jnp.int32)
    csum = jnp.cumsum(onehot, axis=0)
    rank = jnp.sum(csum * onehot, axis=1) - 1
    counts = csum[-1]
    padded = (counts + mb - 1) // mb * mb
    pend = jnp.cumsum(padded)
    pstart = pend - padded
    dest = jnp.sum(pstart[None, :] * onehot, axis=1) + rank
    assert n_assign % mb == 0
    n_blocks = n_assign // mb + n_experts
    block_first = jnp.arange(n_blocks) * mb
    block_expert = jnp.minimum(jnp.sum(block_first[:, None] >= pend[None, :], axis=1), n_experts - 1)
    n_used = (pend[-1] // mb).reshape(1)
    pads = padded - counts
    cpad = jnp.cumsum(pads)
    i = jnp.arange(n_experts * mb)
    grp = jnp.sum(i[:, None] >= cpad[None, :], axis=1)
    grp_hot = (grp[:, None] == jnp.arange(n_experts + 1)[None, :]).astype(jnp.int32)
    first_free = jnp.concatenate([pstart + counts, pend[-1:]])
    before = jnp.concatenate([jnp.zeros((1,), cpad.dtype), cpad])
    free = jnp.sum(grp_hot * (first_free - before)[None, :], axis=1) + i
    slot = jnp.concatenate([dest, free]).astype(jnp.int32)
    return dest.astype(jnp.int32), slot, block_expert.astype(jnp.int32), n_used.astype(jnp.int32)


def _tiles(n, t):
    return dict(tm_proj=math.gcd(512, t), tm_ffn=math.gcd(1024, n), tf=512, tq_dil=1024,
                tq_mem=math.gcd(512, t), moe_block=math.gcd(1024, n))


@jax.jit
def _forward(x, mem, g_mix_norm, w_in, rpb, g_mem_norm, w_mem_kv, g_mix_out, w_out,
             g_ffn_norm, w_dense_gate, w_dense_up, w_dense_down, w_router,
             w_moe_gate, w_moe_up, w_moe_down, g_final):
    b, t, d = x.shape
    n = b * t
    depth = w_in.shape[0]
    n_experts = w_router.shape[-1]
    ts = _tiles(n, t)
    tf = math.gcd(ts["tf"], w_dense_gate.shape[-1])
    tables = _rope_tables(t)
    row1 = lambda a: a.reshape(1, -1)
    x2 = x.reshape(n, d)
    out = None
    sub = d // LANES
    for layer in range(depth):
        qa, ka, va, qm, qbs, kbs, vbs = _inproj(x2, row1(g_mix_norm[layer]), w_in[layer].astype(BF16),
                                                tables, t, ts["tm_proj"])
        shp = lambda a: a.reshape(b, t, a.shape[-1])
        oa = _na_attention(shp(qa), shp(ka), shp(va), _na_bias_table(rpb[layer])).reshape(n, NA_WIDTH)
        obs, lses = [], []
        for (window, dil), qb, kb, vb in zip(DIL_BRANCHES, qbs, kbs, vbs):
            o, lse = _dilated_branch(qb, kb, vb, window // (2 * dil), ts["tq_dil"])
            obs.append(o)
            lses.append(lse)
        km, vm = _mem_kv(mem, row1(g_mem_norm[layer]), w_mem_kv[layer].astype(BF16))
        om = _mem_attention(shp(qm), km, vm, ts["tq_mem"]).reshape(n, MEM_WIDTH)
        gm, wo = row1(g_mix_out[layer]), w_out[layer].astype(BF16)
        i = layer // 2
        if layer % 2 == 0:
            x2 = _outproj(oa, obs, lses, om, x2, gm, wo, t, ts["tm_proj"])
            x2 = _dense_ffn(x2, row1(g_ffn_norm[layer]), w_dense_gate[i].astype(BF16),
                            w_dense_up[i].astype(BF16), w_dense_down[i].astype(BF16), ts["tm_ffn"], tf)
            if layer == depth - 1:
                out = _final_norm(x2, row1(g_final), ts["tm_proj"])
        else:
            wr = jnp.pad(w_router[i], ((0, 0), (0, LANES - n_experts))).astype(BF16)
            x2, h, route = _outproj(oa, obs, lses, om, x2, gm, wo, t, ts["tm_proj"],
                                    router=(row1(g_ffn_norm[layer]), wr, n_experts))
            mb = ts["moe_block"]
            dest, slot, block_expert, n_used = _routing_plan(route, n_experts, mb)
            buf = _dispatch_rows(h, slot, ts["tm_proj"], sub)
            y = _moe_ffn(buf, block_expert, n_used, w_moe_gate[i].astype(BF16), w_moe_up[i].astype(BF16),
                         w_moe_down[i].astype(BF16), mb, tf)
            if layer == depth - 1:
                out = _combine(x2, y, dest, route, ts["tm_proj"], g_final=row1(g_final))
            else:
                x2 = _combine(x2, y, dest, route, ts["tm_proj"])
    return out.reshape(b, t, d)


def kernel(x, mem, g_mix_norm, w_in, rpb, g_mem_norm, w_mem_kv, g_mix_out, w_out, g_ffn_norm,
           w_dense_gate, w_dense_up, w_dense_down, w_router, w_moe_gate, w_moe_up, w_moe_down, g_final):
    return _forward(x, mem, g_mix_norm, w_in, rpb, g_mem_norm, w_mem_kv, g_mix_out, w_out, g_ffn_norm,
                    w_dense_gate, w_dense_up, w_dense_down, w_router, w_moe_gate, w_moe_up, w_moe_down,
                    g_final)
```

```python
import functools
import math

import jax
import jax.numpy as jnp
from jax import lax
from jax.experimental import pallas as pl
from jax.experimental.pallas import tpu as pltpu

HEAD_DIM = 64
NA_HEADS = 6
DIL_HEADS = 6
MEM_HEADS = 4
NA_WIDTH = NA_HEADS * HEAD_DIM
DIL_WIDTH = DIL_HEADS * HEAD_DIM
MEM_WIDTH = MEM_HEADS * HEAD_DIM
GRID_W = 64
NA_WIN_ROWS = 8
NA_WIN_COLS = 16
DIL_BRANCHES = ((128, 1), (512, 4), (2048, 16))
ROPE_THETA = 500000.0
ROT_DIM = HEAD_DIM // 4
TOP_K = 2
RMS_EPS = 1e-6
ATTN_SCALE = HEAD_DIM ** -0.5

LANES = 128
HEADS_PER_LANE_GROUP = LANES // HEAD_DIM
NEG = -1e30
VMEM_LIMIT = 56 * 1024 * 1024

F32 = jnp.float32
BF16 = jnp.bfloat16


def _cparams(sem):
    return pltpu.CompilerParams(dimension_semantics=sem, vmem_limit_bytes=VMEM_LIMIT)


def _rms(x, g):
    ms = jnp.mean(x * x, axis=-1, keepdims=True)
    return x * lax.rsqrt(ms + RMS_EPS) * g


def _pair_heads_attention(q, k, v, bias_fn, n_heads):
    m = q.shape[0]
    assert HEADS_PER_LANE_GROUP == 2
    n_pairs = n_heads // HEADS_PER_LANE_GROUP
    lo = lax.broadcasted_iota(jnp.int32, (m, LANES), 1) < HEAD_DIM
    group = lambda a, p: a[:, p * LANES:(p + 1) * LANES]
    s_parts = []
    for p in range(n_pairs):
        qp = group(q, p)
        zero = jnp.zeros_like(qp)
        q2 = jnp.concatenate([jnp.where(lo, qp, zero), jnp.where(lo, zero, qp)], axis=0)
        s_parts.append(lax.dot_general(q2, group(k, p), (((1,), (1,)), ((), ())),
                                       preferred_element_type=F32) + bias_fn(p))
    s = jnp.concatenate(s_parts, axis=0)
    mx = jnp.max(s, axis=-1, keepdims=True)
    e = jnp.exp(s - mx)
    den = jnp.sum(e, axis=-1, keepdims=True)
    eb = e.astype(BF16)
    outs = []
    for p in range(n_pairs):
        rows = slice(p * 2 * m, (p + 1) * 2 * m)
        o2 = jnp.dot(eb[rows], group(v, p), preferred_element_type=F32) / den[rows]
        outs.append(jnp.where(lo, o2[:m], o2[m:]))
    lse = mx + jnp.log(den)
    return jnp.concatenate(outs, axis=-1), [lse[h * m:(h + 1) * m] for h in range(n_heads)]


def _inproj_kernel(x_ref, g_ref, w_ref, cos_ref, sa_ref, sb_ref,
                   qa_ref, ka_ref, va_ref, qm_ref, *rest, dils):
    dil_refs, scr = rest[:-1], rest[-1]
    tm = x_ref.shape[0]
    h = _rms(x_ref[...], g_ref[...]).astype(BF16)

    def emit_dilated(val, refs):
        groups = val.shape[1] // LANES
        for g in range(groups):
            scr[g] = val[:, g * LANES:(g + 1) * LANES]
        for dil, ref in zip(dils, refs):
            for r in range(dil):
                for g in range(groups):
                    ref[r, :, g * LANES:(g + 1) * LANES] = (
                        scr[g, pl.ds(r, tm // dil, stride=dil), :].astype(BF16))

    def proj(c0, width):
        return jnp.dot(h, w_ref[:, c0:c0 + width], preferred_element_type=F32)

    def rope(a):
        cos, sa, sb = cos_ref[...], sa_ref[...], sb_ref[...]
        half = ROT_DIM // 2
        parts = []
        for gidx in range(a.shape[1] // LANES):
            xg = a[:, gidx * LANES:(gidx + 1) * LANES]
            parts.append(xg * cos + pltpu.roll(xg, LANES - half, 1) * sa + pltpu.roll(xg, half, 1) * sb)
        return jnp.concatenate(parts, axis=-1)

    nd = len(dils)
    c = 0
    qa_ref[...] = (proj(c, NA_WIDTH) * ATTN_SCALE).astype(BF16); c += NA_WIDTH
    ka_ref[...] = proj(c, NA_WIDTH).astype(BF16); c += NA_WIDTH
    va_ref[...] = proj(c, NA_WIDTH).astype(BF16); c += NA_WIDTH
    emit_dilated(rope(proj(c, DIL_WIDTH) * ATTN_SCALE), dil_refs[0:nd]); c += DIL_WIDTH
    emit_dilated(rope(proj(c, DIL_WIDTH)), dil_refs[nd:2 * nd]); c += DIL_WIDTH
    emit_dilated(proj(c, DIL_WIDTH), dil_refs[2 * nd:3 * nd]); c += DIL_WIDTH
    qm_ref[...] = (proj(c, MEM_WIDTH) * ATTN_SCALE).astype(BF16)


def _rope_tables(t):
    half = ROT_DIM // 2
    inv_freq = ROPE_THETA ** (-jnp.arange(0, ROT_DIM, 2, dtype=F32) / ROT_DIM)
    ang = jnp.arange(t, dtype=F32)[:, None] * inv_freq[None, :]
    cos, sin = jnp.cos(ang), jnp.sin(ang)
    ones = jnp.ones((t, HEAD_DIM - ROT_DIM), F32)
    zeros = jnp.zeros((t, HEAD_DIM - ROT_DIM), F32)
    zh = jnp.zeros((t, half), F32)
    cos_h = jnp.concatenate([cos, cos, ones], axis=1)
    sa_h = jnp.concatenate([-sin, zh, zeros], axis=1)
    sb_h = jnp.concatenate([zh, sin, zeros], axis=1)
    tile = lambda a: jnp.tile(a, (1, HEADS_PER_LANE_GROUP))
    return tile(cos_h), tile(sa_h), tile(sb_h)


def _inproj(x2, g, w, tables, t, tm):
    n, d = x2.shape
    b = n // t
    tpb = t // tm
    dils = tuple(dil for _, dil in DIL_BRANCHES)
    assert all(tm % (dil * 16) == 0 for dil in dils)
    row = lambda width: pl.BlockSpec((tm, width), lambda i: (i, 0))
    tab = pl.BlockSpec((tm, LANES), lambda i: (i % tpb, 0))
    plain = [NA_WIDTH] * 3 + [MEM_WIDTH]
    dil_specs = [pl.BlockSpec((None, dil, tm // dil, DIL_WIDTH), lambda i: (i // tpb, 0, i % tpb, 0))
                 for dil in dils] * 3
    dil_shapes = [jax.ShapeDtypeStruct((b, dil, t // dil, DIL_WIDTH), BF16) for dil in dils] * 3
    outs = pl.pallas_call(
        functools.partial(_inproj_kernel, dils=dils),
        grid=(n // tm,),
        in_specs=[row(d), pl.BlockSpec((1, d), lambda i: (0, 0)),
                  pl.BlockSpec(w.shape, lambda i: (0, 0)), tab, tab, tab],
        out_specs=[row(wd) for wd in plain] + dil_specs,
        out_shape=[jax.ShapeDtypeStruct((n, wd), BF16) for wd in plain] + dil_shapes,
        scratch_shapes=[pltpu.VMEM((DIL_WIDTH // LANES, tm, LANES), F32)],
        compiler_params=_cparams(("parallel",)),
        name="inproj",
    )(x2, g, w, *tables)
    nd = len(dils)
    qa, ka, va, qm = outs[:4]
    return qa, ka, va, qm, outs[4:4 + nd], outs[4 + nd:4 + 2 * nd], outs[4 + 2 * nd:]


def _na_bias_table(rpb):
    c = jnp.arange(GRID_W)
    c0 = jnp.clip(c - NA_WIN_COLS // 2, 0, GRID_W - NA_WIN_COLS)
    kc = jnp.arange(GRID_W)
    valid = (kc[None, :] >= c0[:, None]) & (kc[None, :] < c0[:, None] + NA_WIN_COLS)
    coff = kc[None, :] - c[:, None] + (NA_WIN_COLS - 1)
    onehot = (coff[None] == jnp.arange(2 * NA_WIN_COLS - 1)[:, None, None]).astype(F32)
    by_col = jnp.einsum("hrd,dck->hrck", rpb.astype(F32), onehot, precision=lax.Precision.HIGHEST)
    by_col = jnp.where(valid[None, None], by_col, NEG)
    tab = jnp.stack([by_col[:, d:d + NA_WIN_ROWS] for d in range(NA_WIN_ROWS)], axis=1)
    tab = tab.transpose(1, 0, 3, 2, 4)
    return tab.reshape(NA_WIN_ROWS * NA_HEADS // HEADS_PER_LANE_GROUP, HEADS_PER_LANE_GROUP * GRID_W,
                       NA_WIN_ROWS * GRID_W)


def _na_kernel(q_ref, kp_ref, kc_ref, kn_ref, vp_ref, vc_ref, vn_ref, bias_ref, o_ref,
               kwin, vwin, *, rows):
    j = pl.program_id(1)
    blk = NA_WIN_ROWS * GRID_W
    for idx, (kr, vr) in enumerate(((kp_ref, vp_ref), (kc_ref, vc_ref), (kn_ref, vn_ref))):
        kwin[idx * blk:(idx + 1) * blk, :] = kr[...]
        vwin[idx * blk:(idx + 1) * blk, :] = vr[...]

    n_pairs = NA_HEADS // HEADS_PER_LANE_GROUP

    def one_row(i):
        r = j * NA_WIN_ROWS + i
        r0 = jnp.clip(r - NA_WIN_ROWS // 2, 0, rows - NA_WIN_ROWS)
        dlt = r0 - r + (NA_WIN_ROWS - 1)
        start = pl.multiple_of((r0 - (j - 1) * NA_WIN_ROWS) * GRID_W, GRID_W)
        q = q_ref[pl.ds(pl.multiple_of(i * GRID_W, GRID_W), GRID_W), :]
        k = kwin[pl.ds(start, blk), :]
        v = vwin[pl.ds(start, blk), :]
        o, _ = _pair_heads_attention(q, k, v, lambda p: bias_ref[dlt * n_pairs + p], NA_HEADS)
        o_ref[pl.ds(pl.multiple_of(i * GRID_W, GRID_W), GRID_W), :] = o.astype(BF16)

    def two_rows(i2, carry):
        one_row(2 * i2)
        one_row(2 * i2 + 1)
        return carry

    lax.fori_loop(0, NA_WIN_ROWS // 2, two_rows, 0)


def _na_attention(q, k, v, bias):
    b, t, w = q.shape
    rows = t // GRID_W
    assert rows % NA_WIN_ROWS == 0 and rows >= NA_WIN_ROWS
    nj = rows // NA_WIN_ROWS
    blk = NA_WIN_ROWS * GRID_W
    cur = pl.BlockSpec((None, blk, w), lambda bi, j: (bi, j, 0))
    prev = pl.BlockSpec((None, blk, w), lambda bi, j: (bi, jnp.maximum(j - 1, 0), 0))
    nxt = pl.BlockSpec((None, blk, w), lambda bi, j: (bi, jnp.minimum(j + 1, nj - 1), 0))
    return pl.pallas_call(
        functools.partial(_na_kernel, rows=rows),
        grid=(b, nj),
        in_specs=[cur, prev, cur, nxt, prev, cur, nxt,
                  pl.BlockSpec(bias.shape, lambda bi, j: (0, 0, 0))],
        out_specs=cur,
        out_shape=jax.ShapeDtypeStruct((b, t, w), BF16),
        scratch_shapes=[pltpu.VMEM((3 * blk, w), BF16), pltpu.VMEM((3 * blk, w), BF16)],
        compiler_params=_cparams(("parallel", "parallel")),
        name="na_attn",
    )(q, k, k, k, v, v, v, bias)


def _dil_kernel(q_ref, kp_ref, kc_ref, kn_ref, vp_ref, vc_ref, vn_ref, o_ref, lse_ref,
                kwin, vwin, *, seg, tq, halo):
    i = pl.program_id(2)
    width = tq + 2 * halo
    kwin[0:halo, :] = kp_ref[...]
    kwin[halo:halo + tq, :] = kc_ref[...]
    kwin[halo + tq:width, :] = kn_ref[...]
    vwin[0:halo, :] = vp_ref[...]
    vwin[halo:halo + tq, :] = vc_ref[...]
    vwin[halo + tq:width, :] = vn_ref[...]

    sq = math.gcd(tq, 2 * halo)
    sw = sq + 2 * halo
    a = lax.broadcasted_iota(jnp.int32, (sq, sw), 0)
    c = lax.broadcasted_iota(jnp.int32, (sq, sw), 1)
    band = (c >= a) & (c <= a + 2 * halo)
    lo = lax.broadcasted_iota(jnp.int32, (sq, LANES), 1) < HEAD_DIM

    def sub_block(s):
        row0 = pl.multiple_of(s * sq, sq)
        first_key = i * tq + s * sq - halo
        valid = band & (c >= -first_key) & (c < seg - first_key)
        negb = jnp.where(valid, 0.0, NEG).astype(F32)
        negb2 = jnp.concatenate([negb] * HEADS_PER_LANE_GROUP, axis=0)
        o, lses = _pair_heads_attention(q_ref[pl.ds(row0, sq), :], kwin[pl.ds(row0, sw), :],
                                        vwin[pl.ds(row0, sw), :], lambda p: negb2, DIL_HEADS)
        o_ref[pl.ds(row0, sq), :] = o.astype(BF16)
        lse_ref[pl.ds(row0, sq), :] = jnp.concatenate(
            [jnp.where(lo, lses[2 * p], lses[2 * p + 1]) for p in range(DIL_HEADS // 2)], axis=-1)

    n_sub = tq // sq
    if n_sub % 2 == 0:
        def two_sub_blocks(s2, carry):
            sub_block(2 * s2)
            sub_block(2 * s2 + 1)
            return carry
        lax.fori_loop(0, n_sub // 2, two_sub_blocks, 0)
    else:
        for s in range(n_sub):
            sub_block(s)


def _dilated_branch(q, k, v, n_side, tq_max):
    b, dil, seg, w = q.shape
    halo = n_side
    assert seg % halo == 0 and halo % 16 == 0
    tq = math.gcd(tq_max, seg)
    assert tq % halo == 0
    hb = tq // halo
    nhalo = seg // halo
    cur = pl.BlockSpec((None, None, tq, w), lambda bi, r, i: (bi, r, i, 0))
    prev = pl.BlockSpec((None, None, halo, w), lambda bi, r, i: (bi, r, jnp.maximum(i * hb - 1, 0), 0))
    nxt = pl.BlockSpec((None, None, halo, w),
                       lambda bi, r, i: (bi, r, jnp.minimum((i + 1) * hb, nhalo - 1), 0))
    return pl.pallas_call(
        functools.partial(_dil_kernel, seg=seg, tq=tq, halo=halo),
        grid=(b, dil, seg // tq),
        in_specs=[cur, prev, cur, nxt, prev, cur, nxt],
        out_specs=[cur, cur],
        out_shape=[jax.ShapeDtypeStruct((b, dil, seg, w), BF16),
                   jax.ShapeDtypeStruct((b, dil, seg, w), F32)],
        scratch_shapes=[pltpu.VMEM((tq + 2 * halo, w), BF16), pltpu.VMEM((tq + 2 * halo, w), BF16)],
        compiler_params=_cparams(("parallel", "parallel", "parallel")),
        name=f"dilated_d{dil}",
    )(q, k, k, k, v, v, v)


def _memkv_kernel(mem_ref, g_ref, w_ref, k_ref, v_ref):
    h = _rms(mem_ref[...], g_ref[...]).astype(BF16)
    kv = jnp.dot(h, w_ref[...], preferred_element_type=F32)
    k_ref[...] = kv[:, :MEM_WIDTH].astype(BF16)
    v_ref[...] = kv[:, MEM_WIDTH:].astype(BF16)


def _mem_kv(mem, g, w):
    b, m, d = mem.shape
    blk = lambda width: pl.BlockSpec((None, m, width), lambda bi: (bi, 0, 0))
    return pl.pallas_call(
        _memkv_kernel,
        grid=(b,),
        in_specs=[blk(d), pl.BlockSpec((1, d), lambda bi: (0, 0)), pl.BlockSpec(w.shape, lambda bi: (0, 0))],
        out_specs=[blk(MEM_WIDTH), blk(MEM_WIDTH)],
        out_shape=[jax.ShapeDtypeStruct((b, m, MEM_WIDTH), BF16)] * 2,
        compiler_params=_cparams(("parallel",)),
        name="mem_kv",
    )(mem, g, w)


def _memattn_kernel(q_ref, k_ref, v_ref, o_ref):
    o, _ = _pair_heads_attention(q_ref[...], k_ref[...], v_ref[...], lambda p: 0.0, MEM_HEADS)
    o_ref[...] = o.astype(BF16)


def _mem_attention(q, k, v, tq):
    b, t, w = q.shape
    m = k.shape[1]
    qs = pl.BlockSpec((None, tq, w), lambda bi, i: (bi, i, 0))
    ks = pl.BlockSpec((None, m, w), lambda bi, i: (bi, 0, 0))
    return pl.pallas_call(
        _memattn_kernel,
        grid=(b, t // tq),
        in_specs=[qs, ks, ks],
        out_specs=qs,
        out_shape=jax.ShapeDtypeStruct((b, t, w), BF16),
        compiler_params=_cparams(("parallel", "parallel")),
        name="mem_attn",
    )(q, k, v)


def _token_order(ref, scr):
    dil = ref.shape[0]
    if dil == 1:
        return ref[0].astype(F32)
    groups, tm, _ = scr.shape
    for r in range(dil):
        for g in range(groups):
            scr[g, pl.ds(r, tm // dil, stride=dil), :] = ref[r, :, g * LANES:(g + 1) * LANES].astype(F32)
    return jnp.concatenate([scr[g] for g in range(groups)], axis=-1)


def _branch_mix(ob_refs, lse_refs, o_scrs, l_scrs):
    lses = [_token_order(r, s) for r, s in zip(lse_refs, l_scrs)]
    mx = functools.reduce(jnp.maximum, lses)
    es = [jnp.exp(l - mx) for l in lses]
    den = functools.reduce(lambda p, q: p + q, es)
    acc = None
    for e, ob, scr in zip(es, ob_refs, o_scrs):
        term = (e / den) * _token_order(ob, scr)
        acc = term if acc is None else acc + term
    return acc


def _outproj_core(oa_ref, ob_refs, lse_refs, om_ref, x_ref, gm_ref, w_ref, o_scrs, l_scrs):
    gm = gm_ref[...]
    e0, e1 = NA_WIDTH, NA_WIDTH + DIL_WIDTH
    ya = _rms(oa_ref[...].astype(F32), gm[:, :e0]).astype(BF16)
    yb = _rms(_branch_mix(ob_refs, lse_refs, o_scrs, l_scrs), gm[:, e0:e1]).astype(BF16)
    ym = _rms(om_ref[...].astype(F32), gm[:, e1:]).astype(BF16)
    acc = jnp.dot(ya, w_ref[0:e0, :], preferred_element_type=F32)
    acc = acc + jnp.dot(yb, w_ref[e0:e1, :], preferred_element_type=F32)
    acc = acc + jnp.dot(ym, w_ref[e1:, :], preferred_element_type=F32)
    return x_ref[...] + acc


def _split_scratch(scrs, dils):
    it = iter(scrs)
    o_scrs = [next(it) if dil > 1 else None for dil in dils]
    l_scrs = [next(it) if dil > 1 else None for dil in dils]
    return o_scrs, l_scrs


def _outproj_kernel(oa_ref, ob1, ob2, ob3, l1, l2, l3, om_ref, x_ref, gm_ref, w_ref, xo_ref, *scrs, dils):
    xo_ref[...] = _outproj_core(oa_ref, (ob1, ob2, ob3), (l1, l2, l3), om_ref, x_ref, gm_ref, w_ref,
                                *_split_scratch(scrs, dils))


def _store_row_tiles(ref, val):
    m, d = val.shape
    for j in range(d // LANES):
        ref[pl.ds(j, m, stride=d // LANES), :] = val[:, j * LANES:(j + 1) * LANES]


def _outproj_router_kernel(oa_ref, ob1, ob2, ob3, l1, l2, l3, om_ref, x_ref, gm_ref, w_ref,
                           gf_ref, wr_ref, xo_ref, h_ref, route_ref, *scrs, n_experts, dils):
    xn = _outproj_core(oa_ref, (ob1, ob2, ob3), (l1, l2, l3), om_ref, x_ref, gm_ref, w_ref,
                       *_split_scratch(scrs, dils))
    xo_ref[...] = xn
    h = _rms(xn, gf_ref[...])
    _store_row_tiles(h_ref, h)
    logits = jnp.dot(h.astype(BF16), wr_ref[...], preferred_element_type=F32)
    lane = lax.broadcasted_iota(jnp.int32, logits.shape, 1)
    lg = jnp.where(lane < n_experts, logits, NEG)
    m1 = jnp.max(lg, axis=-1, keepdims=True)
    i1 = jnp.min(jnp.where(lg == m1, lane, LANES), axis=-1, keepdims=True)
    lg2 = jnp.where(lane == i1, NEG, lg)
    m2 = jnp.max(lg2, axis=-1, keepdims=True)
    i2 = jnp.min(jnp.where(lg2 == m2, lane, LANES), axis=-1, keepdims=True)
    e2 = jnp.exp(m2 - m1)
    g1 = 1.0 / (1.0 + e2)
    g2 = e2 / (1.0 + e2)
    route = jnp.where(lane == 0, i1.astype(F32), 0.0)
    route = jnp.where(lane == 1, i2.astype(F32), route)
    route = jnp.where(lane == 2, g1, route)
    route = jnp.where(lane == 3, g2, route)
    route_ref[...] = route


def _outproj(oa, obs, lses, om, x2, gm, w, t, tm, router=None):
    n, d = x2.shape
    tpb = t // tm
    row = lambda width: pl.BlockSpec((tm, width), lambda i: (i, 0))
    full = lambda a: pl.BlockSpec(a.shape, lambda i: (0,) * a.ndim)
    grouped = lambda a: pl.BlockSpec((None, a.shape[1], tm // a.shape[1], a.shape[3]),
                                     lambda i: (i // tpb, 0, i % tpb, 0))
    dils = tuple(a.shape[1] for a in obs)
    args = [oa, *obs, *lses, om, x2, gm, w]
    in_specs = ([row(NA_WIDTH)] + [grouped(a) for a in obs] + [grouped(a) for a in lses]
                + [row(MEM_WIDTH), row(d), full(gm), full(w)])
    scratch = [pltpu.VMEM((DIL_WIDTH // LANES, tm, LANES), F32)] * (2 * sum(dil > 1 for dil in dils))
    if router is None:
        return pl.pallas_call(
            functools.partial(_outproj_kernel, dils=dils),
            grid=(n // tm,), in_specs=in_specs, out_specs=row(d),
            out_shape=jax.ShapeDtypeStruct((n, d), F32), scratch_shapes=scratch,
            compiler_params=_cparams(("parallel",)), name="outproj",
        )(*args)
    gf, wr, n_experts = router
    sub = d // LANES
    return pl.pallas_call(
        functools.partial(_outproj_router_kernel, n_experts=n_experts, dils=dils),
        grid=(n // tm,), in_specs=in_specs + [full(gf), full(wr)],
        out_specs=[row(d), pl.BlockSpec((tm * sub, LANES), lambda i: (i, 0)), row(LANES)],
        out_shape=[jax.ShapeDtypeStruct((n, d), F32), jax.ShapeDtypeStruct((n * sub, LANES), F32),
                   jax.ShapeDtypeStruct((n, LANES), F32)],
        scratch_shapes=scratch,
        compiler_params=_cparams(("parallel",)), name="outproj_router",
    )(*args, gf, wr)


def _swiglu_step(h, wg_ref, wu_ref, wd_ref):
    gate = jnp.dot(h, wg_ref[...], preferred_element_type=F32)
    up = jnp.dot(h, wu_ref[...], preferred_element_type=F32)
    act = (gate / (1.0 + jnp.exp(-gate))) * up
    return jnp.dot(act.astype(BF16), wd_ref[...], preferred_element_type=F32)


def _dense_ffn_kernel(x_ref, g_ref, wg_ref, wu_ref, wd_ref, o_ref, h_scr, acc_scr):
    k = pl.program_id(1)

    @pl.when(k == 0)
    def _():
        h_scr[...] = _rms(x_ref[...], g_ref[...]).astype(BF16)
        acc_scr[...] = jnp.zeros_like(acc_scr)

    acc_scr[...] += _swiglu_step(h_scr[...], wg_ref, wu_ref, wd_ref)

    @pl.when(k == pl.num_programs(1) - 1)
    def _():
        o_ref[...] = x_ref[...] + acc_scr[...]


def _dense_ffn(x2, g, wg, wu, wd, tm, tf):
    n, d = x2.shape
    f = wg.shape[1]
    row = pl.BlockSpec((tm, d), lambda i, k: (i, 0))
    return pl.pallas_call(
        _dense_ffn_kernel,
        grid=(n // tm, f // tf),
        in_specs=[row, pl.BlockSpec((1, d), lambda i, k: (0, 0)),
                  pl.BlockSpec((d, tf), lambda i, k: (0, k)),
                  pl.BlockSpec((d, tf), lambda i, k: (0, k)),
                  pl.BlockSpec((tf, d), lambda i, k: (k, 0))],
        out_specs=row,
        out_shape=jax.ShapeDtypeStruct((n, d), F32),
        scratch_shapes=[pltpu.VMEM((tm, d), BF16), pltpu.VMEM((tm, d), F32)],
        compiler_params=_cparams(("parallel", "arbitrary")),
        name="dense_ffn",
    )(x2, g, wg, wu, wd)


def _moe_ffn_kernel(be_ref, nused_ref, xb_ref, wg_ref, wu_ref, wd_ref, y_ref, h_scr, acc_scr):
    j = pl.program_id(0)
    k = pl.program_id(1)
    last = pl.num_programs(1) - 1
    used = j < nused_ref[0]

    mb, d = h_scr.shape
    sub = d // LANES

    @pl.when(used & (k == 0))
    def _():
        for j in range(sub):
            h_scr[:, j * LANES:(j + 1) * LANES] = xb_ref[pl.ds(j, mb, stride=sub), :].astype(BF16)
        acc_scr[...] = jnp.zeros_like(acc_scr)

    @pl.when(used)
    def _():
        acc_scr[...] += _swiglu_step(h_scr[...], wg_ref, wu_ref, wd_ref)

    @pl.when(used & (k == last))
    def _():
        _store_row_tiles(y_ref, acc_scr[...])

    @pl.when(jnp.logical_not(used) & (k == last))
    def _():
        y_ref[...] = jnp.zeros_like(y_ref)


def _moe_ffn(buf, block_expert, n_used, wg, wu, wd, mb, tf):
    d, f = wg.shape[1], wg.shape[2]
    sub = d // LANES
    p = buf.shape[0] // sub
    nk = f // tf
    kk = lambda j, k, be, nu: jnp.where(j < nu[0], k, nk - 1)
    row = pl.BlockSpec((mb * sub, LANES), lambda j, k, be, nu: (j, 0))
    grid_spec = pltpu.PrefetchScalarGridSpec(
        num_scalar_prefetch=2,
        grid=(p // mb, nk),
        in_specs=[row,
                  pl.BlockSpec((None, d, tf), lambda j, k, be, nu: (be[j], 0, kk(j, k, be, nu))),
                  pl.BlockSpec((None, d, tf), lambda j, k, be, nu: (be[j], 0, kk(j, k, be, nu))),
                  pl.BlockSpec((None, tf, d), lambda j, k, be, nu: (be[j], kk(j, k, be, nu), 0))],
        out_specs=row,
        scratch_shapes=[pltpu.VMEM((mb, d), BF16), pltpu.VMEM((mb, d), F32)],
    )
    return pl.pallas_call(
        _moe_ffn_kernel,
        grid_spec=grid_spec,
        out_shape=jax.ShapeDtypeStruct((p * sub, LANES), F32),
        compiler_params=_cparams(("parallel", "arbitrary")),
        name="moe_ffn",
    )(block_expert, n_used, buf, wg, wu, wd)


ISSUE_UNROLL = 8


def _issue_tile_copies(chunk, copy_fn):
    def body(trip, carry):
        for u in range(ISSUE_UNROLL):
            copy_fn(trip, u).start(priority=u % 2)
        return carry

    lax.fori_loop(0, chunk // ISSUE_UNROLL, body, 0)


def _tile(ref, idx, sub):
    return ref.at[pl.ds(pl.multiple_of(idx * sub, sub), sub)]


def _dispatch_kernel(slot_ref, h_ref, out_ref, zero_scr, sem, *, n_tok_steps, sub):
    i = pl.program_id(0)
    n_copies = slot_ref.shape[-1]
    dst = lambda trip, u: _tile(out_ref, slot_ref[0, 0, trip * ISSUE_UNROLL + u], sub)
    tok = lambda trip, u: trip * (ISSUE_UNROLL // TOP_K) + u // TOP_K

    @pl.when(i == 0)
    def _():
        zero_scr[...] = jnp.zeros_like(zero_scr)

    @pl.when(i < n_tok_steps)
    def _():
        _issue_tile_copies(n_copies, lambda trip, u: pltpu.make_async_copy(
            _tile(h_ref, tok(trip, u), sub), dst(trip, u), sem))

    @pl.when(i >= n_tok_steps)
    def _():
        _issue_tile_copies(n_copies, lambda trip, u: pltpu.make_async_copy(zero_scr, dst(trip, u), sem))

    for _ in range(TOP_K):
        pltpu.make_async_copy(h_ref, out_ref.at[pl.ds(0, h_ref.shape[0])], sem).wait()


def _dispatch_rows(h, slot, tm, sub):
    n = h.shape[0] // sub
    p = slot.shape[0]
    n_copies = tm * TOP_K
    assert p % n_copies == 0 and n % tm == 0 and n_copies % ISSUE_UNROLL == 0
    n_tok_steps = n // tm
    return pl.pallas_call(
        functools.partial(_dispatch_kernel, n_tok_steps=n_tok_steps, sub=sub),
        grid=(p // n_copies,),
        in_specs=[pl.BlockSpec((1, 1, n_copies), lambda i: (i, 0, 0), memory_space=pltpu.SMEM),
                  pl.BlockSpec((tm * sub, LANES), lambda i: (jnp.minimum(i, n_tok_steps - 1), 0))],
        out_specs=pl.BlockSpec(memory_space=pl.ANY),
        out_shape=jax.ShapeDtypeStruct((p * sub, LANES), h.dtype),
        scratch_shapes=[pltpu.VMEM((sub, LANES), h.dtype), pltpu.SemaphoreType.DMA(())],
        compiler_params=_cparams(("arbitrary",)),
        name="dispatch_rows",
    )(slot.reshape(p // n_copies, 1, n_copies), h)


def _combine_kernel(cur_ref, nxt_ref, x_ref, route_ref, y_ref, *rest, final_norm, sub):
    o_ref, ybuf, sem = rest[-3:]
    i = pl.program_id(0)
    tm = x_ref.shape[0]
    n_copies = tm * TOP_K
    par = i % 2

    def fetch(idx_ref, buf):
        def copy(trip, u):
            tok = trip * (ISSUE_UNROLL // TOP_K) + u // TOP_K
            return pltpu.make_async_copy(_tile(y_ref, idx_ref[0, 0, trip * ISSUE_UNROLL + u], sub),
                                         _tile(ybuf.at[buf, u % TOP_K], tok, sub), sem.at[buf])
        _issue_tile_copies(n_copies, copy)

    @pl.when(i == 0)
    def _():
        fetch(cur_ref, 0)

    @pl.when(i + 1 < pl.num_programs(0))
    def _():
        fetch(nxt_ref, 1 - par)

    pltpu.make_async_copy(ybuf.at[par], ybuf.at[par], sem.at[par]).wait()

    route = route_ref[...]
    gates = [route[:, TOP_K + k:TOP_K + k + 1] for k in range(TOP_K)]
    for j in range(sub):
        sl = slice(j * LANES, (j + 1) * LANES)
        mix = None
        for k in range(TOP_K):
            term = gates[k] * ybuf[par, k, pl.ds(j, tm, stride=sub), :]
            mix = term if mix is None else mix + term
        o_ref[:, sl] = x_ref[:, sl] + mix
    if final_norm:
        o_ref[...] = _rms(o_ref[...], rest[0][...])


def _combine(x2, y, dest, route, tm, g_final=None):
    n, d = x2.shape
    sub = d // LANES
    n_copies = tm * TOP_K
    steps = n // tm
    assert n_copies % ISSUE_UNROLL == 0 and ISSUE_UNROLL % TOP_K == 0
    dest3 = dest.reshape(steps, 1, n_copies)
    row = pl.BlockSpec((tm, d), lambda i: (i, 0))
    in_specs = [pl.BlockSpec((1, 1, n_copies), lambda i: (i, 0, 0), memory_space=pltpu.SMEM),
                pl.BlockSpec((1, 1, n_copies), lambda i: (jnp.minimum(i + 1, steps - 1), 0, 0),
                             memory_space=pltpu.SMEM),
                row, pl.BlockSpec((tm, LANES), lambda i: (i, 0)), pl.BlockSpec(memory_space=pl.ANY)]
    args = [dest3, dest3, x2, route, y]
    if g_final is not None:
        in_specs.append(pl.BlockSpec((1, d), lambda i: (0, 0)))
        args.append(g_final)
    return pl.pallas_call(
        functools.partial(_combine_kernel, final_norm=g_final is not None, sub=sub),
        grid=(steps,), in_specs=in_specs, out_specs=row,
        out_shape=jax.ShapeDtypeStruct((n, d), F32),
        scratch_shapes=[pltpu.VMEM((2, TOP_K, tm * sub, LANES), F32), pltpu.SemaphoreType.DMA((2,))],
        compiler_params=_cparams(("arbitrary",)), name="moe_combine",
    )(*args)


def _final_norm_kernel(x_ref, g_ref, o_ref):
    o_ref[...] = _rms(x_ref[...], g_ref[...])


def _final_norm(x2, g, tm):
    n, d = x2.shape
    row = pl.BlockSpec((tm, d), lambda i: (i, 0))
    return pl.pallas_call(
        _final_norm_kernel, grid=(n // tm,), in_specs=[row, pl.BlockSpec((1, d), lambda i: (0, 0))],
        out_specs=row, out_shape=jax.ShapeDtypeStruct((n, d), F32),
        compiler_params=_cparams(("parallel",)), name="final_norm",
    )(x2, g)


def _routing_plan(route, n_experts, mb):
    n = route.shape[0]
    n_assign = n * TOP_K
    flat_e = route[:, :TOP_K].astype(jnp.int32).reshape(-1)
    onehot = (flat_e[:, None] == jnp.arange(n_experts)[None, :]).astype(jnp.int32)
    csum = jnp.cumsum(onehot, axis=0)
    rank = jnp.sum(csum * onehot, axis=1) - 1
    counts = csum[-1]
    padded = (counts + mb - 1) // mb * mb
    pend = jnp.cumsum(padded)
    pstart = pend - padded
    dest = jnp.sum(pstart[None, :] * onehot, axis=1) + rank
    assert n_assign % mb == 0
    n_blocks = n_assign // mb + n_experts
    block_first = jnp.arange(n_blocks) * mb
    block_expert = jnp.minimum(jnp.sum(block_first[:, None] >= pend[None, :], axis=1), n_experts - 1)
    n_used = (pend[-1] // mb).reshape(1)
    pads = padded - counts
    cpad = jnp.cumsum(pads)
    i = jnp.arange(n_experts * mb)
    grp = jnp.sum(i[:, None] >= cpad[None, :], axis=1)
    grp_hot = (grp[:, None] == jnp.arange(n_experts + 1)[None, :]).astype(jnp.int32)
    first_free = jnp.concatenate([pstart + counts, pend[-1:]])
    before = jnp.concatenate([jnp.zeros((1,), cpad.dtype), cpad])
    free = jnp.sum(grp_hot * (first_free - before)[None, :], axis=1) + i
    slot = jnp.concatenate([dest, free]).astype(jnp.int32)
    return dest.astype(jnp.int32), slot, block_expert.astype(jnp.int32), n_used.astype(jnp.int32)


def _tiles(n, t):
    return dict(tm_in=math.gcd(1024, t), tm_proj=math.gcd(512, t), tm_ffn=math.gcd(1024, n), tf=512,
                tq_dil=1024, tq_mem=math.gcd(512, t), moe_block=math.gcd(1024, n))


@jax.jit
def _forward(x, mem, g_mix_norm, w_in, rpb, g_mem_norm, w_mem_kv, g_mix_out, w_out,
             g_ffn_norm, w_dense_gate, w_dense_up, w_dense_down, w_router,
             w_moe_gate, w_moe_up, w_moe_down, g_final):
    b, t, d = x.shape
    n = b * t
    depth = w_in.shape[0]
    n_experts = w_router.shape[-1]
    ts = _tiles(n, t)
    tf = math.gcd(ts["tf"], w_dense_gate.shape[-1])
    tables = _rope_tables(t)
    row1 = lambda a: a.reshape(1, -1)
    x2 = x.reshape(n, d)
    out = None
    sub = d // LANES
    for layer in range(depth):
        qa, ka, va, qm, qbs, kbs, vbs = _inproj(x2, row1(g_mix_norm[layer]), w_in[layer].astype(BF16),
                                                tables, t, ts["tm_in"])
        shp = lambda a: a.reshape(b, t, a.shape[-1])
        oa = _na_attention(shp(qa), shp(ka), shp(va), _na_bias_table(rpb[layer])).reshape(n, NA_WIDTH)
        obs, lses = [], []
        for (window, dil), qb, kb, vb in zip(DIL_BRANCHES, qbs, kbs, vbs):
            o, lse = _dilated_branch(qb, kb, vb, window // (2 * dil), ts["tq_dil"])
            obs.append(o)
            lses.append(lse)
        km, vm = _mem_kv(mem, row1(g_mem_norm[layer]), w_mem_kv[layer].astype(BF16))
        om = _mem_attention(shp(qm), km, vm, ts["tq_mem"]).reshape(n, MEM_WIDTH)
        gm, wo = row1(g_mix_out[layer]), w_out[layer].astype(BF16)
        i = layer // 2
        if layer % 2 == 0:
            x2 = _outproj(oa, obs, lses, om, x2, gm, wo, t, ts["tm_proj"])
            x2 = _dense_ffn(x2, row1(g_ffn_norm[layer]), w_dense_gate[i].astype(BF16),
                            w_dense_up[i].astype(BF16), w_dense_down[i].astype(BF16), ts["tm_ffn"], tf)
            if layer == depth - 1:
                out = _final_norm(x2, row1(g_final), ts["tm_proj"])
        else:
            wr = jnp.pad(w_router[i], ((0, 0), (0, LANES - n_experts))).astype(BF16)
            x2, h, route = _outproj(oa, obs, lses, om, x2, gm, wo, t, ts["tm_proj"],
                                    router=(row1(g_ffn_norm[layer]), wr, n_experts))
            mb = ts["moe_block"]
            dest, slot, block_expert, n_used = _routing_plan(route, n_experts, mb)
            buf = _dispatch_rows(h, slot, ts["tm_proj"], sub)
            y = _moe_ffn(buf, block_expert, n_used, w_moe_gate[i].astype(BF16), w_moe_up[i].astype(BF16),
                         w_moe_down[i].astype(BF16), mb, tf)
            if layer == depth - 1:
                out = _combine(x2, y, dest, route, ts["tm_proj"], g_final=row1(g_final))
            else:
                x2 = _combine(x2, y, dest, route, ts["tm_proj"])
    return out.reshape(b, t, d)


def kernel(x, mem, g_mix_norm, w_in, rpb, g_mem_norm, w_mem_kv, g_mix_out, w_out, g_ffn_norm,
           w_dense_gate, w_dense_up, w_dense_down, w_router, w_moe_gate, w_moe_up, w_moe_down, g_final):
    return _forward(x, mem, g_mix_norm, w_in, rpb, g_mem_norm, w_mem_kv, g_mix_out, w_out, g_ffn_norm,
                    w_dense_gate, w_dense_up, w_dense_down, w_router, w_moe_gate, w_moe_up, w_moe_down,
                    g_final)
```

```python
import functools
import math

import jax
import jax.numpy as jnp
from jax import lax
from jax.experimental import pallas as pl
from jax.experimental.pallas import tpu as pltpu

HEAD_DIM = 64
NA_HEADS = 6
DIL_HEADS = 6
MEM_HEADS = 4
NA_WIDTH = NA_HEADS * HEAD_DIM
DIL_WIDTH = DIL_HEADS * HEAD_DIM
MEM_WIDTH = MEM_HEADS * HEAD_DIM
GRID_W = 64
NA_WIN_ROWS = 8
NA_WIN_COLS = 16
DIL_BRANCHES = ((128, 1), (512, 4), (2048, 16))
ROPE_THETA = 500000.0
ROT_DIM = HEAD_DIM // 4
TOP_K = 2
RMS_EPS = 1e-6
ATTN_SCALE = HEAD_DIM ** -0.5
LOG2E = math.log2(math.e)
Q_SCALE = ATTN_SCALE * LOG2E

LANES = 128
HEADS_PER_LANE_GROUP = LANES // HEAD_DIM
NEG = -1e30
ROWS_PER_TRIP = 4
VMEM_LIMIT = 56 * 1024 * 1024

F32 = jnp.float32
BF16 = jnp.bfloat16


def _cparams(sem):
    return pltpu.CompilerParams(dimension_semantics=sem, vmem_limit_bytes=VMEM_LIMIT)


def _rms(x, g):
    ms = jnp.mean(x * x, axis=-1, keepdims=True)
    return x * lax.rsqrt(ms + RMS_EPS) * g


def _pair_heads_attention(q, k, v, bias_fn, n_heads):
    results = []
    _attention_pipelined([(lambda: (q, k, v, bias_fn), lambda o, lses: results.append((o, lses)))], n_heads)
    return results[0]


def _lane_group(a, p):
    return a[:, p * LANES:(p + 1) * LANES]


def _attn_scores(q, k, bias_fn, n_heads):
    m = q.shape[0]
    assert HEADS_PER_LANE_GROUP == 2
    lo = lax.broadcasted_iota(jnp.int32, (m, LANES), 1) < HEAD_DIM
    s_parts = []
    for p in range(n_heads // HEADS_PER_LANE_GROUP):
        qp = _lane_group(q, p)
        zero = jnp.zeros_like(qp)
        q2 = jnp.concatenate([jnp.where(lo, qp, zero), jnp.where(lo, zero, qp)], axis=0)
        s_parts.append(lax.dot_general(q2, _lane_group(k, p), (((1,), (1,)), ((), ())),
                                       preferred_element_type=F32) + bias_fn(p))
    return jnp.concatenate(s_parts, axis=0)


def _attn_softmax(s):
    mx = jnp.max(s, axis=-1, keepdims=True)
    e = jnp.exp2(s - mx)
    den = jnp.sum(e, axis=-1, keepdims=True)
    return e.astype(BF16), den, mx


def _attn_values(eb, den, mx, v, n_heads):
    m = eb.shape[0] // n_heads
    lo = lax.broadcasted_iota(jnp.int32, (m, LANES), 1) < HEAD_DIM
    outs = []
    for p in range(n_heads // HEADS_PER_LANE_GROUP):
        rows = slice(p * 2 * m, (p + 1) * 2 * m)
        o2 = jnp.dot(eb[rows], _lane_group(v, p), preferred_element_type=F32) / den[rows]
        outs.append(jnp.where(lo, o2[:m], o2[m:]))
    lse = mx + jnp.log2(den)
    return jnp.concatenate(outs, axis=-1), [lse[h * m:(h + 1) * m] for h in range(n_heads)]


def _attention_pipelined(items, n_heads, skew=1):
    n = len(items)
    loaded, scores, soft = {}, {}, {}
    for step in range(n + 2 * skew):
        if step < n:
            loaded[step] = items[step][0]()
            q, k, _, bias_fn = loaded[step]
            scores[step] = _attn_scores(q, k, bias_fn, n_heads)
        if 0 <= step - skew < n:
            soft[step - skew] = _attn_softmax(scores.pop(step - skew))
        if 0 <= step - 2 * skew < n:
            eb, den, mx = soft.pop(step - 2 * skew)
            items[step - 2 * skew][1](*_attn_values(eb, den, mx, loaded.pop(step - 2 * skew)[2], n_heads))


def _inproj_kernel(x_ref, g_ref, w_ref, cos_ref, sa_ref, sb_ref,
                   qa_ref, ka_ref, va_ref, qm_ref, *rest, dils):
    dil_refs, scr = rest[:-1], rest[-1]
    tm = x_ref.shape[0]
    h = _rms(x_ref[...], g_ref[...]).astype(BF16)

    def emit_dilated(val, refs):
        groups = val.shape[1] // LANES
        for g in range(groups):
            scr[g] = val[:, g * LANES:(g + 1) * LANES]
        for dil, ref in zip(dils, refs):
            for r in range(dil):
                for g in range(groups):
                    ref[r, :, g * LANES:(g + 1) * LANES] = (
                        scr[g, pl.ds(r, tm // dil, stride=dil), :].astype(BF16))

    def proj_pair(c0, width_a, width_b):
        both = jnp.dot(h, w_ref[:, c0:c0 + width_a + width_b], preferred_element_type=F32)
        return both[:, :width_a], both[:, width_a:]

    def rope(a):
        cos, sa, sb = cos_ref[...], sa_ref[...], sb_ref[...]
        half = ROT_DIM // 2
        parts = []
        for gidx in range(a.shape[1] // LANES):
            xg = a[:, gidx * LANES:(gidx + 1) * LANES]
            parts.append(xg * cos + pltpu.roll(xg, LANES - half, 1) * sa + pltpu.roll(xg, half, 1) * sb)
        return jnp.concatenate(parts, axis=-1)

    nd = len(dils)
    assert (2 * NA_WIDTH) % 256 == 0 and (NA_WIDTH + DIL_WIDTH) % 256 == 0 and (2 * DIL_WIDTH) % 256 == 0
    qa, ka = proj_pair(0, NA_WIDTH, NA_WIDTH)
    qa_ref[...] = (qa * Q_SCALE).astype(BF16)
    ka_ref[...] = ka.astype(BF16)
    va, qb = proj_pair(2 * NA_WIDTH, NA_WIDTH, DIL_WIDTH)
    va_ref[...] = va.astype(BF16)
    emit_dilated(rope(qb * Q_SCALE), dil_refs[0:nd])
    kb, vb = proj_pair(3 * NA_WIDTH + DIL_WIDTH, DIL_WIDTH, DIL_WIDTH)
    emit_dilated(rope(kb), dil_refs[nd:2 * nd])
    emit_dilated(vb, dil_refs[2 * nd:3 * nd])
    c = 3 * NA_WIDTH + 3 * DIL_WIDTH
    qm_ref[...] = (jnp.dot(h, w_ref[:, c:c + MEM_WIDTH], preferred_element_type=F32) * Q_SCALE).astype(BF16)


def _rope_tables(t):
    half = ROT_DIM // 2
    inv_freq = ROPE_THETA ** (-jnp.arange(0, ROT_DIM, 2, dtype=F32) / ROT_DIM)
    ang = jnp.arange(t, dtype=F32)[:, None] * inv_freq[None, :]
    cos, sin = jnp.cos(ang), jnp.sin(ang)
    ones = jnp.ones((t, HEAD_DIM - ROT_DIM), F32)
    zeros = jnp.zeros((t, HEAD_DIM - ROT_DIM), F32)
    zh = jnp.zeros((t, half), F32)
    cos_h = jnp.concatenate([cos, cos, ones], axis=1)
    sa_h = jnp.concatenate([-sin, zh, zeros], axis=1)
    sb_h = jnp.concatenate([zh, sin, zeros], axis=1)
    tile = lambda a: jnp.tile(a, (1, HEADS_PER_LANE_GROUP))
    return tile(cos_h), tile(sa_h), tile(sb_h)


def _inproj(x2, g, w, tables, t, tm):
    n, d = x2.shape
    b = n // t
    tpb = t // tm
    dils = tuple(dil for _, dil in DIL_BRANCHES)
    assert all(tm % (dil * 16) == 0 for dil in dils)
    row = lambda width: pl.BlockSpec((tm, width), lambda i: (i, 0))
    tab = pl.BlockSpec((tm, LANES), lambda i: (i % tpb, 0))
    plain = [NA_WIDTH] * 3 + [MEM_WIDTH]
    dil_specs = [pl.BlockSpec((None, dil, tm // dil, DIL_WIDTH), lambda i: (i // tpb, 0, i % tpb, 0))
                 for dil in dils] * 3
    dil_shapes = [jax.ShapeDtypeStruct((b, dil, t // dil, DIL_WIDTH), BF16) for dil in dils] * 3
    outs = pl.pallas_call(
        functools.partial(_inproj_kernel, dils=dils),
        grid=(n // tm,),
        in_specs=[row(d), pl.BlockSpec((1, d), lambda i: (0, 0)),
                  pl.BlockSpec(w.shape, lambda i: (0, 0)), tab, tab, tab],
        out_specs=[row(wd) for wd in plain] + dil_specs,
        out_shape=[jax.ShapeDtypeStruct((n, wd), BF16) for wd in plain] + dil_shapes,
        scratch_shapes=[pltpu.VMEM((DIL_WIDTH // LANES, tm, LANES), F32)],
        compiler_params=_cparams(("parallel",)),
        name="inproj",
    )(x2, g, w, *tables)
    nd = len(dils)
    qa, ka, va, qm = outs[:4]
    return qa, ka, va, qm, outs[4:4 + nd], outs[4 + nd:4 + 2 * nd], outs[4 + 2 * nd:]


def _na_bias_table(rpb):
    c = jnp.arange(GRID_W)
    c0 = jnp.clip(c - NA_WIN_COLS // 2, 0, GRID_W - NA_WIN_COLS)
    kc = jnp.arange(GRID_W)
    valid = (kc[None, :] >= c0[:, None]) & (kc[None, :] < c0[:, None] + NA_WIN_COLS)
    coff = kc[None, :] - c[:, None] + (NA_WIN_COLS - 1)
    onehot = (coff[None] == jnp.arange(2 * NA_WIN_COLS - 1)[:, None, None]).astype(F32)
    by_col = jnp.einsum("hrd,dck->hrck", rpb.astype(F32), onehot, precision=lax.Precision.HIGHEST)
    by_col = jnp.where(valid[None, None], by_col * LOG2E, NEG)
    tab = jnp.stack([by_col[:, d:d + NA_WIN_ROWS] for d in range(NA_WIN_ROWS)], axis=1)
    tab = tab.transpose(1, 0, 3, 2, 4)
    return tab.reshape(NA_WIN_ROWS * NA_HEADS // HEADS_PER_LANE_GROUP, HEADS_PER_LANE_GROUP * GRID_W,
                       NA_WIN_ROWS * GRID_W)


def _na_kernel(q_ref, kp_ref, kc_ref, kn_ref, vp_ref, vc_ref, vn_ref, bias_ref, o_ref,
               kwin, vwin, *, rows):
    j = pl.program_id(1)
    blk = NA_WIN_ROWS * GRID_W
    for idx, (kr, vr) in enumerate(((kp_ref, vp_ref), (kc_ref, vc_ref), (kn_ref, vn_ref))):
        kwin[idx * blk:(idx + 1) * blk, :] = kr[...]
        vwin[idx * blk:(idx + 1) * blk, :] = vr[...]

    n_pairs = NA_HEADS // HEADS_PER_LANE_GROUP

    def one_row(i):
        q_rows = pl.ds(pl.multiple_of(i * GRID_W, GRID_W), GRID_W)

        def load():
            r = j * NA_WIN_ROWS + i
            r0 = jnp.clip(r - NA_WIN_ROWS // 2, 0, rows - NA_WIN_ROWS)
            dlt = r0 - r + (NA_WIN_ROWS - 1)
            start = pl.multiple_of((r0 - (j - 1) * NA_WIN_ROWS) * GRID_W, GRID_W)
            return (q_ref[q_rows, :], kwin[pl.ds(start, blk), :], vwin[pl.ds(start, blk), :],
                    lambda p: bias_ref[dlt * n_pairs + p])

        def store(o, lses):
            o_ref[q_rows, :] = o.astype(BF16)

        return load, store

    def row_group(ig, carry):
        _attention_pipelined([one_row(ig * ROWS_PER_TRIP + u) for u in range(ROWS_PER_TRIP)], NA_HEADS)
        return carry

    lax.fori_loop(0, NA_WIN_ROWS // ROWS_PER_TRIP, row_group, 0)


def _na_attention(q, k, v, bias):
    b, t, w = q.shape
    rows = t // GRID_W
    assert rows % NA_WIN_ROWS == 0 and rows >= NA_WIN_ROWS
    nj = rows // NA_WIN_ROWS
    blk = NA_WIN_ROWS * GRID_W
    cur = pl.BlockSpec((None, blk, w), lambda bi, j: (bi, j, 0))
    prev = pl.BlockSpec((None, blk, w), lambda bi, j: (bi, jnp.maximum(j - 1, 0), 0))
    nxt = pl.BlockSpec((None, blk, w), lambda bi, j: (bi, jnp.minimum(j + 1, nj - 1), 0))
    return pl.pallas_call(
        functools.partial(_na_kernel, rows=rows),
        grid=(b, nj),
        in_specs=[cur, prev, cur, nxt, prev, cur, nxt,
                  pl.BlockSpec(bias.shape, lambda bi, j: (0, 0, 0))],
        out_specs=cur,
        out_shape=jax.ShapeDtypeStruct((b, t, w), BF16),
        scratch_shapes=[pltpu.VMEM((3 * blk, w), BF16), pltpu.VMEM((3 * blk, w), BF16)],
        compiler_params=_cparams(("parallel", "parallel")),
        name="na_attn",
    )(q, k, k, k, v, v, v, bias)


def _dil_kernel(q_ref, kp_ref, kc_ref, kn_ref, vp_ref, vc_ref, vn_ref, o_ref, lse_ref,
                kwin, vwin, *, seg, tq, halo):
    i = pl.program_id(2)
    width = tq + 2 * halo
    kwin[0:halo, :] = kp_ref[...]
    kwin[halo:halo + tq, :] = kc_ref[...]
    kwin[halo + tq:width, :] = kn_ref[...]
    vwin[0:halo, :] = vp_ref[...]
    vwin[halo:halo + tq, :] = vc_ref[...]
    vwin[halo + tq:width, :] = vn_ref[...]

    sq = math.gcd(tq, 2 * halo)
    sw = sq + 2 * halo
    a = lax.broadcasted_iota(jnp.int32, (sq, sw), 0)
    c = lax.broadcasted_iota(jnp.int32, (sq, sw), 1)
    band = (c >= a) & (c <= a + 2 * halo)
    lo = lax.broadcasted_iota(jnp.int32, (sq, LANES), 1) < HEAD_DIM

    def sub_block(s):
        row0 = pl.multiple_of(s * sq, sq)

        def load():
            first_key = i * tq + s * sq - halo
            valid = band & (c >= -first_key) & (c < seg - first_key)
            negb = jnp.where(valid, 0.0, NEG).astype(F32)
            negb2 = jnp.concatenate([negb] * HEADS_PER_LANE_GROUP, axis=0)
            return (q_ref[pl.ds(row0, sq), :], kwin[pl.ds(row0, sw), :], vwin[pl.ds(row0, sw), :],
                    lambda p: negb2)

        def store(o, lses):
            o_ref[pl.ds(row0, sq), :] = o.astype(BF16)
            lse_ref[pl.ds(row0, sq), :] = jnp.concatenate(
                [jnp.where(lo, lses[2 * p], lses[2 * p + 1]) for p in range(DIL_HEADS // 2)], axis=-1)

        return load, store

    n_sub = tq // sq
    per_trip = math.gcd(n_sub, ROWS_PER_TRIP)

    def sub_block_group(sg, carry):
        _attention_pipelined([sub_block(sg * per_trip + u) for u in range(per_trip)], DIL_HEADS, skew=0)
        return carry

    lax.fori_loop(0, n_sub // per_trip, sub_block_group, 0)


def _dilated_branch(q, k, v, n_side, tq_max):
    b, dil, seg, w = q.shape
    halo = n_side
    assert seg % halo == 0 and halo % 16 == 0
    tq = math.gcd(tq_max, seg)
    assert tq % halo == 0
    hb = tq // halo
    nhalo = seg // halo
    cur = pl.BlockSpec((None, None, tq, w), lambda bi, r, i: (bi, r, i, 0))
    prev = pl.BlockSpec((None, None, halo, w), lambda bi, r, i: (bi, r, jnp.maximum(i * hb - 1, 0), 0))
    nxt = pl.BlockSpec((None, None, halo, w),
                       lambda bi, r, i: (bi, r, jnp.minimum((i + 1) * hb, nhalo - 1), 0))
    return pl.pallas_call(
        functools.partial(_dil_kernel, seg=seg, tq=tq, halo=halo),
        grid=(b, dil, seg // tq),
        in_specs=[cur, prev, cur, nxt, prev, cur, nxt],
        out_specs=[cur, cur],
        out_shape=[jax.ShapeDtypeStruct((b, dil, seg, w), BF16),
                   jax.ShapeDtypeStruct((b, dil, seg, w), F32)],
        scratch_shapes=[pltpu.VMEM((tq + 2 * halo, w), BF16), pltpu.VMEM((tq + 2 * halo, w), BF16)],
        compiler_params=_cparams(("parallel", "parallel", "parallel")),
        name=f"dilated_d{dil}",
    )(q, k, k, k, v, v, v)


def _memkv_kernel(mem_ref, g_ref, w_ref, k_ref, v_ref):
    h = _rms(mem_ref[...], g_ref[...]).astype(BF16)
    kv = jnp.dot(h, w_ref[...], preferred_element_type=F32)
    k_ref[...] = kv[:, :MEM_WIDTH].astype(BF16)
    v_ref[...] = kv[:, MEM_WIDTH:].astype(BF16)


def _mem_kv(mem, g, w):
    b, m, d = mem.shape
    blk = lambda width: pl.BlockSpec((None, m, width), lambda bi: (bi, 0, 0))
    return pl.pallas_call(
        _memkv_kernel,
        grid=(b,),
        in_specs=[blk(d), pl.BlockSpec((1, d), lambda bi: (0, 0)), pl.BlockSpec(w.shape, lambda bi: (0, 0))],
        out_specs=[blk(MEM_WIDTH), blk(MEM_WIDTH)],
        out_shape=[jax.ShapeDtypeStruct((b, m, MEM_WIDTH), BF16)] * 2,
        compiler_params=_cparams(("parallel",)),
        name="mem_kv",
    )(mem, g, w)


def _memattn_kernel(q_ref, k_ref, v_ref, o_ref):
    o, _ = _pair_heads_attention(q_ref[...], k_ref[...], v_ref[...], lambda p: 0.0, MEM_HEADS)
    o_ref[...] = o.astype(BF16)


def _mem_attention(q, k, v, tq):
    b, t, w = q.shape
    m = k.shape[1]
    qs = pl.BlockSpec((None, tq, w), lambda bi, i: (bi, i, 0))
    ks = pl.BlockSpec((None, m, w), lambda bi, i: (bi, 0, 0))
    return pl.pallas_call(
        _memattn_kernel,
        grid=(b, t // tq),
        in_specs=[qs, ks, ks],
        out_specs=qs,
        out_shape=jax.ShapeDtypeStruct((b, t, w), BF16),
        compiler_params=_cparams(("parallel", "parallel")),
        name="mem_attn",
    )(q, k, v)


def _token_order(ref, scr):
    dil = ref.shape[0]
    if dil == 1:
        return ref[0].astype(F32)
    groups, tm, _ = scr.shape
    for r in range(dil):
        for g in range(groups):
            scr[g, pl.ds(r, tm // dil, stride=dil), :] = ref[r, :, g * LANES:(g + 1) * LANES].astype(F32)
    return jnp.concatenate([scr[g] for g in range(groups)], axis=-1)


def _branch_mix(ob_refs, lse_refs, o_scrs, l_scrs):
    lses = [_token_order(r, s) for r, s in zip(lse_refs, l_scrs)]
    mx = functools.reduce(jnp.maximum, lses)
    es = [jnp.exp2(l - mx) for l in lses]
    den = functools.reduce(lambda p, q: p + q, es)
    acc = None
    for e, ob, scr in zip(es, ob_refs, o_scrs):
        term = (e / den) * _token_order(ob, scr)
        acc = term if acc is None else acc + term
    return acc


def _outproj_core(oa_ref, ob_refs, lse_refs, om_ref, x_ref, gm_ref, w_ref, o_scrs, l_scrs):
    gm = gm_ref[...]
    e0, e1 = NA_WIDTH, NA_WIDTH + DIL_WIDTH
    ya = _rms(oa_ref[...].astype(F32), gm[:, :e0]).astype(BF16)
    yb = _rms(_branch_mix(ob_refs, lse_refs, o_scrs, l_scrs), gm[:, e0:e1]).astype(BF16)
    ym = _rms(om_ref[...].astype(F32), gm[:, e1:]).astype(BF16)
    y = jnp.concatenate([ya, yb, ym], axis=-1)
    return x_ref[...] + jnp.dot(y, w_ref[...], preferred_element_type=F32)


def _split_scratch(scrs, dils):
    it = iter(scrs)
    o_scrs = [next(it) if dil > 1 else None for dil in dils]
    l_scrs = [next(it) if dil > 1 else None for dil in dils]
    return o_scrs, l_scrs


def _outproj_kernel(oa_ref, ob1, ob2, ob3, l1, l2, l3, om_ref, x_ref, gm_ref, w_ref, xo_ref, *scrs, dils):
    xo_ref[...] = _outproj_core(oa_ref, (ob1, ob2, ob3), (l1, l2, l3), om_ref, x_ref, gm_ref, w_ref,
                                *_split_scratch(scrs, dils))


def _store_row_tiles(ref, val):
    m, d = val.shape
    for j in range(d // LANES):
        ref[pl.ds(j, m, stride=d // LANES), :] = val[:, j * LANES:(j + 1) * LANES]


def _outproj_router_kernel(oa_ref, ob1, ob2, ob3, l1, l2, l3, om_ref, x_ref, gm_ref, w_ref,
                           gf_ref, wr_ref, xo_ref, h_ref, route_ref, *scrs, n_experts, dils):
    xn = _outproj_core(oa_ref, (ob1, ob2, ob3), (l1, l2, l3), om_ref, x_ref, gm_ref, w_ref,
                       *_split_scratch(scrs, dils))
    xo_ref[...] = xn
    h = _rms(xn, gf_ref[...])
    _store_row_tiles(h_ref, h)
    logits = jnp.dot(h.astype(BF16), wr_ref[...], preferred_element_type=F32)
    lane = lax.broadcasted_iota(jnp.int32, logits.shape, 1)
    lg = jnp.where(lane < n_experts, logits, NEG)
    m1 = jnp.max(lg, axis=-1, keepdims=True)
    i1 = jnp.min(jnp.where(lg == m1, lane, LANES), axis=-1, keepdims=True)
    lg2 = jnp.where(lane == i1, NEG, lg)
    m2 = jnp.max(lg2, axis=-1, keepdims=True)
    i2 = jnp.min(jnp.where(lg2 == m2, lane, LANES), axis=-1, keepdims=True)
    e2 = jnp.exp(m2 - m1)
    g1 = 1.0 / (1.0 + e2)
    g2 = e2 / (1.0 + e2)
    route = jnp.where(lane == 0, i1.astype(F32), 0.0)
    route = jnp.where(lane == 1, i2.astype(F32), route)
    route = jnp.where(lane == 2, g1, route)
    route = jnp.where(lane == 3, g2, route)
    route_ref[...] = route


def _outproj(oa, obs, lses, om, x2, gm, w, t, tm, router=None):
    n, d = x2.shape
    tpb = t // tm
    row = lambda width: pl.BlockSpec((tm, width), lambda i: (i, 0))
    full = lambda a: pl.BlockSpec(a.shape, lambda i: (0,) * a.ndim)
    grouped = lambda a: pl.BlockSpec((None, a.shape[1], tm // a.shape[1], a.shape[3]),
                                     lambda i: (i // tpb, 0, i % tpb, 0))
    dils = tuple(a.shape[1] for a in obs)
    args = [oa, *obs, *lses, om, x2, gm, w]
    in_specs = ([row(NA_WIDTH)] + [grouped(a) for a in obs] + [grouped(a) for a in lses]
                + [row(MEM_WIDTH), row(d), full(gm), full(w)])
    scratch = [pltpu.VMEM((DIL_WIDTH // LANES, tm, LANES), F32)] * (2 * sum(dil > 1 for dil in dils))
    if router is None:
        return pl.pallas_call(
            functools.partial(_outproj_kernel, dils=dils),
            grid=(n // tm,), in_specs=in_specs, out_specs=row(d),
            out_shape=jax.ShapeDtypeStruct((n, d), F32), scratch_shapes=scratch,
            compiler_params=_cparams(("parallel",)), name="outproj",
        )(*args)
    gf, wr, n_experts = router
    sub = d // LANES
    return pl.pallas_call(
        functools.partial(_outproj_router_kernel, n_experts=n_experts, dils=dils),
        grid=(n // tm,), in_specs=in_specs + [full(gf), full(wr)],
        out_specs=[row(d), pl.BlockSpec((tm * sub, LANES), lambda i: (i, 0)), row(LANES)],
        out_shape=[jax.ShapeDtypeStruct((n, d), F32), jax.ShapeDtypeStruct((n * sub, LANES), F32),
                   jax.ShapeDtypeStruct((n, LANES), F32)],
        scratch_shapes=scratch,
        compiler_params=_cparams(("parallel",)), name="outproj_router",
    )(*args, gf, wr)


def _swiglu_step(h, wg_ref, wu_ref, wd_ref):
    gate = jnp.dot(h, wg_ref[...], preferred_element_type=F32)
    up = jnp.dot(h, wu_ref[...], preferred_element_type=F32)
    act = (gate / (1.0 + jnp.exp(-gate))) * up
    return jnp.dot(act.astype(BF16), wd_ref[...], preferred_element_type=F32)


def _dense_ffn_kernel(x_ref, g_ref, wg_ref, wu_ref, wd_ref, o_ref, h_scr, acc_scr):
    k = pl.program_id(1)

    @pl.when(k == 0)
    def _():
        h_scr[...] = _rms(x_ref[...], g_ref[...]).astype(BF16)
        acc_scr[...] = jnp.zeros_like(acc_scr)

    acc_scr[...] += _swiglu_step(h_scr[...], wg_ref, wu_ref, wd_ref)

    @pl.when(k == pl.num_programs(1) - 1)
    def _():
        o_ref[...] = x_ref[...] + acc_scr[...]


def _dense_ffn(x2, g, wg, wu, wd, tm, tf):
    n, d = x2.shape
    f = wg.shape[1]
    row = pl.BlockSpec((tm, d), lambda i, k: (i, 0))
    return pl.pallas_call(
        _dense_ffn_kernel,
        grid=(n // tm, f // tf),
        in_specs=[row, pl.BlockSpec((1, d), lambda i, k: (0, 0)),
                  pl.BlockSpec((d, tf), lambda i, k: (0, k)),
                  pl.BlockSpec((d, tf), lambda i, k: (0, k)),
                  pl.BlockSpec((tf, d), lambda i, k: (k, 0))],
        out_specs=row,
        out_shape=jax.ShapeDtypeStruct((n, d), F32),
        scratch_shapes=[pltpu.VMEM((tm, d), BF16), pltpu.VMEM((tm, d), F32)],
        compiler_params=_cparams(("parallel", "arbitrary")),
        name="dense_ffn",
    )(x2, g, wg, wu, wd)


def _moe_ffn_kernel(be_ref, nused_ref, xb_ref, wg_ref, wu_ref, wd_ref, y_ref, h_scr, acc_scr):
    j = pl.program_id(0)
    k = pl.program_id(1)
    last = pl.num_programs(1) - 1
    used = j < nused_ref[0]

    mb, d = h_scr.shape
    sub = d // LANES

    @pl.when(used & (k == 0))
    def _():
        for j in range(sub):
            h_scr[:, j * LANES:(j + 1) * LANES] = xb_ref[pl.ds(j, mb, stride=sub), :].astype(BF16)
        acc_scr[...] = jnp.zeros_like(acc_scr)

    @pl.when(used)
    def _():
        acc_scr[...] += _swiglu_step(h_scr[...], wg_ref, wu_ref, wd_ref)

    @pl.when(used & (k == last))
    def _():
        _store_row_tiles(y_ref, acc_scr[...])

    @pl.when(jnp.logical_not(used) & (k == last))
    def _():
        y_ref[...] = jnp.zeros_like(y_ref)


def _moe_ffn(buf, block_expert, n_used, wg, wu, wd, mb, tf):
    d, f = wg.shape[1], wg.shape[2]
    sub = d // LANES
    p = buf.shape[0] // sub
    nk = f // tf
    kk = lambda j, k, be, nu: jnp.where(j < nu[0], k, nk - 1)
    row = pl.BlockSpec((mb * sub, LANES), lambda j, k, be, nu: (j, 0))
    grid_spec = pltpu.PrefetchScalarGridSpec(
        num_scalar_prefetch=2,
        grid=(p // mb, nk),
        in_specs=[row,
                  pl.BlockSpec((None, d, tf), lambda j, k, be, nu: (be[j], 0, kk(j, k, be, nu))),
                  pl.BlockSpec((None, d, tf), lambda j, k, be, nu: (be[j], 0, kk(j, k, be, nu))),
                  pl.BlockSpec((None, tf, d), lambda j, k, be, nu: (be[j], kk(j, k, be, nu), 0))],
        out_specs=row,
        scratch_shapes=[pltpu.VMEM((mb, d), BF16), pltpu.VMEM((mb, d), F32)],
    )
    return pl.pallas_call(
        _moe_ffn_kernel,
        grid_spec=grid_spec,
        out_shape=jax.ShapeDtypeStruct((p * sub, LANES), F32),
        compiler_params=_cparams(("parallel", "arbitrary")),
        name="moe_ffn",
    )(block_expert, n_used, buf, wg, wu, wd)


ISSUE_UNROLL = 8


def _issue_tile_copies(chunk, copy_fn):
    def body(trip, carry):
        for u in range(ISSUE_UNROLL):
            copy_fn(trip, u).start(priority=u % 2)
        return carry

    lax.fori_loop(0, chunk // ISSUE_UNROLL, body, 0)


def _tile(ref, idx, sub):
    return ref.at[pl.ds(pl.multiple_of(idx * sub, sub), sub)]


def _dispatch_kernel(slot_ref, h_ref, out_ref, zero_scr, sem, *, n_tok_steps, sub):
    i = pl.program_id(0)
    n_copies = slot_ref.shape[-1]
    dst = lambda trip, u: _tile(out_ref, slot_ref[0, 0, trip * ISSUE_UNROLL + u], sub)
    tok = lambda trip, u: trip * (ISSUE_UNROLL // TOP_K) + u // TOP_K

    @pl.when(i == 0)
    def _():
        zero_scr[...] = jnp.zeros_like(zero_scr)

    @pl.when(i < n_tok_steps)
    def _():
        _issue_tile_copies(n_copies, lambda trip, u: pltpu.make_async_copy(
            _tile(h_ref, tok(trip, u), sub), dst(trip, u), sem))

    @pl.when(i >= n_tok_steps)
    def _():
        _issue_tile_copies(n_copies, lambda trip, u: pltpu.make_async_copy(zero_scr, dst(trip, u), sem))

    for _ in range(TOP_K):
        pltpu.make_async_copy(h_ref, out_ref.at[pl.ds(0, h_ref.shape[0])], sem).wait()


def _dispatch_rows(h, slot, tm, sub):
    n = h.shape[0] // sub
    p = slot.shape[0]
    n_copies = tm * TOP_K
    assert p % n_copies == 0 and n % tm == 0 and n_copies % ISSUE_UNROLL == 0
    n_tok_steps = n // tm
    return pl.pallas_call(
        functools.partial(_dispatch_kernel, n_tok_steps=n_tok_steps, sub=sub),
        grid=(p // n_copies,),
        in_specs=[pl.BlockSpec((1, 1, n_copies), lambda i: (i, 0, 0), memory_space=pltpu.SMEM),
                  pl.BlockSpec((tm * sub, LANES), lambda i: (jnp.minimum(i, n_tok_steps - 1), 0))],
        out_specs=pl.BlockSpec(memory_space=pl.ANY),
        out_shape=jax.ShapeDtypeStruct((p * sub, LANES), h.dtype),
        scratch_shapes=[pltpu.VMEM((sub, LANES), h.dtype), pltpu.SemaphoreType.DMA(())],
        compiler_params=_cparams(("arbitrary",)),
        name="dispatch_rows",
    )(slot.reshape(p // n_copies, 1, n_copies), h)


def _combine_kernel(cur_ref, nxt_ref, x_ref, route_ref, y_ref, *rest, final_norm, sub):
    o_ref, ybuf, sem = rest[-3:]
    i = pl.program_id(0)
    tm = x_ref.shape[0]
    n_copies = tm * TOP_K
    par = i % 2

    def fetch(idx_ref, buf):
        def copy(trip, u):
            tok = trip * (ISSUE_UNROLL // TOP_K) + u // TOP_K
            return pltpu.make_async_copy(_tile(y_ref, idx_ref[0, 0, trip * ISSUE_UNROLL + u], sub),
                                         _tile(ybuf.at[buf, u % TOP_K], tok, sub), sem.at[buf])
        _issue_tile_copies(n_copies, copy)

    @pl.when(i == 0)
    def _():
        fetch(cur_ref, 0)

    @pl.when(i + 1 < pl.num_programs(0))
    def _():
        fetch(nxt_ref, 1 - par)

    pltpu.make_async_copy(ybuf.at[par], ybuf.at[par], sem.at[par]).wait()

    route = route_ref[...]
    gates = [route[:, TOP_K + k:TOP_K + k + 1] for k in range(TOP_K)]
    for j in range(sub):
        sl = slice(j * LANES, (j + 1) * LANES)
        mix = None
        for k in range(TOP_K):
            term = gates[k] * ybuf[par, k, pl.ds(j, tm, stride=sub), :]
            mix = term if mix is None else mix + term
        o_ref[:, sl] = x_ref[:, sl] + mix
    if final_norm:
        o_ref[...] = _rms(o_ref[...], rest[0][...])


def _combine(x2, y, dest, route, tm, g_final=None):
    n, d = x2.shape
    sub = d // LANES
    n_copies = tm * TOP_K
    steps = n // tm
    assert n_copies % ISSUE_UNROLL == 0 and ISSUE_UNROLL % TOP_K == 0
    dest3 = dest.reshape(steps, 1, n_copies)
    row = pl.BlockSpec((tm, d), lambda i: (i, 0))
    in_specs = [pl.BlockSpec((1, 1, n_copies), lambda i: (i, 0, 0), memory_space=pltpu.SMEM),
                pl.BlockSpec((1, 1, n_copies), lambda i: (jnp.minimum(i + 1, steps - 1), 0, 0),
                             memory_space=pltpu.SMEM),
                row, pl.BlockSpec((tm, LANES), lambda i: (i, 0)), pl.BlockSpec(memory_space=pl.ANY)]
    args = [dest3, dest3, x2, route, y]
    if g_final is not None:
        in_specs.append(pl.BlockSpec((1, d), lambda i: (0, 0)))
        args.append(g_final)
    return pl.pallas_call(
        functools.partial(_combine_kernel, final_norm=g_final is not None, sub=sub),
        grid=(steps,), in_specs=in_specs, out_specs=row,
        out_shape=jax.ShapeDtypeStruct((n, d), F32),
        scratch_shapes=[pltpu.VMEM((2, TOP_K, tm * sub, LANES), F32), pltpu.SemaphoreType.DMA((2,))],
        compiler_params=_cparams(("arbitrary",)), name="moe_combine",
    )(*args)


def _final_norm_kernel(x_ref, g_ref, o_ref):
    o_ref[...] = _rms(x_ref[...], g_ref[...])


def _final_norm(x2, g, tm):
    n, d = x2.shape
    row = pl.BlockSpec((tm, d), lambda i: (i, 0))
    return pl.pallas_call(
        _final_norm_kernel, grid=(n // tm,), in_specs=[row, pl.BlockSpec((1, d), lambda i: (0, 0))],
        out_specs=row, out_shape=jax.ShapeDtypeStruct((n, d), F32),
        compiler_params=_cparams(("parallel",)), name="final_norm",
    )(x2, g)


def _routing_plan(route, n_experts, mb):
    n = route.shape[0]
    n_assign = n * TOP_K
    flat_e = route[:, :TOP_K].astype(jnp.int32).reshape(-1)
    onehot = (flat_e[:, None] == jnp.arange(n_experts)[None, :]).astype(jnp.int32)
    csum = jnp.cumsum(onehot, axis=0)
    rank = jnp.sum(csum * onehot, axis=1) - 1
    counts = csum[-1]
    padded = (counts + mb - 1) // mb * mb
    pend = jnp.cumsum(padded)
    pstart = pend - padded
    dest = jnp.sum(pstart[None, :] * onehot, axis=1) + rank
    assert n_assign % mb == 0
    n_blocks = n_assign // mb + n_experts
    block_first = jnp.arange(n_blocks) * mb
    block_expert = jnp.minimum(jnp.sum(block_first[:, None] >= pend[None, :], axis=1), n_experts - 1)
    n_used = (pend[-1] // mb).reshape(1)
    pads = padded - counts
    cpad = jnp.cumsum(pads)
    i = jnp.arange(n_experts * mb)
    grp = jnp.sum(i[:, None] >= cpad[None, :], axis=1)
    grp_hot = (grp[:, None] == jnp.arange(n_experts + 1)[None, :]).astype(jnp.int32)
    first_free = jnp.concatenate([pstart + counts, pend[-1:]])
    before = jnp.concatenate([jnp.zeros((1,), cpad.dtype), cpad])
    free = jnp.sum(grp_hot * (first_free - before)[None, :], axis=1) + i
    slot = jnp.concatenate([dest, free]).astype(jnp.int32)
    return dest.astype(jnp.int32), slot, block_expert.astype(jnp.int32), n_used.astype(jnp.int32)


def _tiles(n, t):
    return dict(tm_in=math.gcd(1024, t), tm_proj=math.gcd(512, t), tm_ffn=math.gcd(1024, n), tf=512,
                tq_dil=1024, tq_mem=math.gcd(512, t), moe_block=math.gcd(1024, n))


@jax.jit
def _forward(x, mem, g_mix_norm, w_in, rpb, g_mem_norm, w_mem_kv, g_mix_out, w_out,
             g_ffn_norm, w_dense_gate, w_dense_up, w_dense_down, w_router,
             w_moe_gate, w_moe_up, w_moe_down, g_final):
    b, t, d = x.shape
    n = b * t
    depth = w_in.shape[0]
    n_experts = w_router.shape[-1]
    ts = _tiles(n, t)
    tf = math.gcd(ts["tf"], w_dense_gate.shape[-1])
    tables = _rope_tables(t)
    row1 = lambda a: a.reshape(1, -1)
    x2 = x.reshape(n, d)
    out = None
    sub = d // LANES
    for layer in range(depth):
        qa, ka, va, qm, qbs, kbs, vbs = _inproj(x2, row1(g_mix_norm[layer]), w_in[layer].astype(BF16),
                                                tables, t, ts["tm_in"])
        shp = lambda a: a.reshape(b, t, a.shape[-1])
        oa = _na_attention(shp(qa), shp(ka), shp(va), _na_bias_table(rpb[layer])).reshape(n, NA_WIDTH)
        obs, lses = [], []
        for (window, dil), qb, kb, vb in zip(DIL_BRANCHES, qbs, kbs, vbs):
            o, lse = _dilated_branch(qb, kb, vb, window // (2 * dil), ts["tq_dil"])
            obs.append(o)
            lses.append(lse)
        km, vm = _mem_kv(mem, row1(g_mem_norm[layer]), w_mem_kv[layer].astype(BF16))
        om = _mem_attention(shp(qm), km, vm, ts["tq_mem"]).reshape(n, MEM_WIDTH)
        gm, wo = row1(g_mix_out[layer]), w_out[layer].astype(BF16)
        i = layer // 2
        if layer % 2 == 0:
            x2 = _outproj(oa, obs, lses, om, x2, gm, wo, t, ts["tm_proj"])
            x2 = _dense_ffn(x2, row1(g_ffn_norm[layer]), w_dense_gate[i].astype(BF16),
                            w_dense_up[i].astype(BF16), w_dense_down[i].astype(BF16), ts["tm_ffn"], tf)
            if layer == depth - 1:
                out = _final_norm(x2, row1(g_final), ts["tm_proj"])
        else:
            wr = jnp.pad(w_router[i], ((0, 0), (0, LANES - n_experts))).astype(BF16)
            x2, h, route = _outproj(oa, obs, lses, om, x2, gm, wo, t, ts["tm_proj"],
                                    router=(row1(g_ffn_norm[layer]), wr, n_experts))
            mb = ts["moe_block"]
            dest, slot, block_expert, n_used = _routing_plan(route, n_experts, mb)
            buf = _dispatch_rows(h, slot, ts["tm_proj"], sub)
            y = _moe_ffn(buf, block_expert, n_used, w_moe_gate[i].astype(BF16), w_moe_up[i].astype(BF16),
                         w_moe_down[i].astype(BF16), mb, tf)
            if layer == depth - 1:
                out = _combine(x2, y, dest, route, ts["tm_proj"], g_final=row1(g_final))
            else:
                x2 = _combine(x2, y, dest, route, ts["tm_proj"])
    return out.reshape(b, t, d)


def kernel(x, mem, g_mix_norm, w_in, rpb, g_mem_norm, w_mem_kv, g_mix_out, w_out, g_ffn_norm,
           w_dense_gate, w_dense_up, w_dense_down, w_router, w_moe_gate, w_moe_up, w_moe_down, g_final):
    return _forward(x, mem, g_mix_norm, w_in, rpb, g_mem_norm, w_mem_kv, g_mix_out, w_out, g_ffn_norm,
                    w_dense_gate, w_dense_up, w_dense_down, w_router, w_moe_gate, w_moe_up, w_moe_down,
                    g_final)
```

```python
import functools
import math

import jax
import jax.numpy as jnp
from jax import lax
from jax.experimental import pallas as pl
from jax.experimental.pallas import tpu as pltpu

HEAD_DIM = 64
NA_HEADS = 6
DIL_HEADS = 6
MEM_HEADS = 4
NA_WIDTH = NA_HEADS * HEAD_DIM
DIL_WIDTH = DIL_HEADS * HEAD_DIM
MEM_WIDTH = MEM_HEADS * HEAD_DIM
GRID_W = 64
NA_WIN_ROWS = 8
NA_WIN_COLS = 16
DIL_BRANCHES = ((128, 1), (512, 4), (2048, 16))
ROPE_THETA = 500000.0
ROT_DIM = HEAD_DIM // 4
TOP_K = 2
RMS_EPS = 1e-6
ATTN_SCALE = HEAD_DIM ** -0.5
LOG2E = math.log2(math.e)
Q_SCALE = ATTN_SCALE * LOG2E

LANES = 128
HEADS_PER_LANE_GROUP = LANES // HEAD_DIM
NEG = -1e30
ROWS_PER_TRIP = 4
VMEM_LIMIT = 56 * 1024 * 1024

F32 = jnp.float32
BF16 = jnp.bfloat16


def _cparams(sem):
    return pltpu.CompilerParams(dimension_semantics=sem, vmem_limit_bytes=VMEM_LIMIT)


def _rms(x, g):
    ms = jnp.mean(x * x, axis=-1, keepdims=True)
    return x * lax.rsqrt(ms + RMS_EPS) * g


def _pair_heads_attention(q, k, v, bias_fn, n_heads):
    results = []
    _attention_pipelined([(lambda: (q, k, v, bias_fn), lambda o, lses: results.append((o, lses)))], n_heads)
    return results[0]


def _lane_group(a, p):
    return a[:, p * LANES:(p + 1) * LANES]


def _attn_scores(q, k, bias_fn, n_heads):
    m = q.shape[0]
    assert HEADS_PER_LANE_GROUP == 2
    lo = lax.broadcasted_iota(jnp.int32, (m, LANES), 1) < HEAD_DIM
    s_parts = []
    for p in range(n_heads // HEADS_PER_LANE_GROUP):
        qp = _lane_group(q, p)
        zero = jnp.zeros_like(qp)
        q2 = jnp.concatenate([jnp.where(lo, qp, zero), jnp.where(lo, zero, qp)], axis=0)
        s_parts.append(lax.dot_general(q2, _lane_group(k, p), (((1,), (1,)), ((), ())),
                                       preferred_element_type=F32) + bias_fn(p))
    return jnp.concatenate(s_parts, axis=0)


def _attn_softmax(s):
    mx = jnp.max(s, axis=-1, keepdims=True)
    e = jnp.exp2(s - mx)
    den = jnp.sum(e, axis=-1, keepdims=True)
    return e.astype(BF16), den, mx


def _attn_values(eb, den, mx, v, n_heads):
    m = eb.shape[0] // n_heads
    lo = lax.broadcasted_iota(jnp.int32, (m, LANES), 1) < HEAD_DIM
    outs = []
    for p in range(n_heads // HEADS_PER_LANE_GROUP):
        rows = slice(p * 2 * m, (p + 1) * 2 * m)
        o2 = jnp.dot(eb[rows], _lane_group(v, p), preferred_element_type=F32) / den[rows]
        outs.append(jnp.where(lo, o2[:m], o2[m:]))
    lse = mx + jnp.log2(den)
    return jnp.concatenate(outs, axis=-1), [lse[h * m:(h + 1) * m] for h in range(n_heads)]


def _attention_pipelined(items, n_heads, skew=1):
    n = len(items)
    loaded, scores, soft = {}, {}, {}
    for step in range(n + 2 * skew):
        if step < n:
            loaded[step] = items[step][0]()
            q, k, _, bias_fn = loaded[step]
            scores[step] = _attn_scores(q, k, bias_fn, n_heads)
        if 0 <= step - skew < n:
            soft[step - skew] = _attn_softmax(scores.pop(step - skew))
        if 0 <= step - 2 * skew < n:
            eb, den, mx = soft.pop(step - 2 * skew)
            items[step - 2 * skew][1](*_attn_values(eb, den, mx, loaded.pop(step - 2 * skew)[2], n_heads))


def _inproj_kernel(x_ref, g_ref, w_ref, cos_ref, sa_ref, sb_ref,
                   qa_ref, ka_ref, va_ref, qm_ref, *rest, dils):
    dil_refs, scr = rest[:-1], rest[-1]
    tm = x_ref.shape[0]
    h = _rms(x_ref[...], g_ref[...]).astype(BF16)

    def emit_dilated(val, refs):
        groups = val.shape[1] // LANES
        for g in range(groups):
            scr[g] = val[:, g * LANES:(g + 1) * LANES]
        for dil, ref in zip(dils, refs):
            for r in range(dil):
                for g in range(groups):
                    ref[r, :, g * LANES:(g + 1) * LANES] = (
                        scr[g, pl.ds(r, tm // dil, stride=dil), :].astype(BF16))

    def proj_pair(c0, width_a, width_b):
        both = jnp.dot(h, w_ref[:, c0:c0 + width_a + width_b], preferred_element_type=F32)
        return both[:, :width_a], both[:, width_a:]

    def rope(a):
        cos, sa, sb = cos_ref[...], sa_ref[...], sb_ref[...]
        half = ROT_DIM // 2
        parts = []
        for gidx in range(a.shape[1] // LANES):
            xg = a[:, gidx * LANES:(gidx + 1) * LANES]
            parts.append(xg * cos + pltpu.roll(xg, LANES - half, 1) * sa + pltpu.roll(xg, half, 1) * sb)
        return jnp.concatenate(parts, axis=-1)

    nd = len(dils)
    assert (2 * NA_WIDTH) % 256 == 0 and (NA_WIDTH + DIL_WIDTH) % 256 == 0 and (2 * DIL_WIDTH) % 256 == 0
    qa, ka = proj_pair(0, NA_WIDTH, NA_WIDTH)
    qa_ref[...] = (qa * Q_SCALE).astype(BF16)
    ka_ref[...] = ka.astype(BF16)
    va, qb = proj_pair(2 * NA_WIDTH, NA_WIDTH, DIL_WIDTH)
    va_ref[...] = va.astype(BF16)
    emit_dilated(rope(qb * Q_SCALE), dil_refs[0:nd])
    kb, vb = proj_pair(3 * NA_WIDTH + DIL_WIDTH, DIL_WIDTH, DIL_WIDTH)
    emit_dilated(rope(kb), dil_refs[nd:2 * nd])
    emit_dilated(vb, dil_refs[2 * nd:3 * nd])
    c = 3 * NA_WIDTH + 3 * DIL_WIDTH
    qm_ref[...] = (jnp.dot(h, w_ref[:, c:c + MEM_WIDTH], preferred_element_type=F32) * Q_SCALE).astype(BF16)


def _rope_tables(t):
    half = ROT_DIM // 2
    inv_freq = ROPE_THETA ** (-jnp.arange(0, ROT_DIM, 2, dtype=F32) / ROT_DIM)
    ang = jnp.arange(t, dtype=F32)[:, None] * inv_freq[None, :]
    cos, sin = jnp.cos(ang), jnp.sin(ang)
    ones = jnp.ones((t, HEAD_DIM - ROT_DIM), F32)
    zeros = jnp.zeros((t, HEAD_DIM - ROT_DIM), F32)
    zh = jnp.zeros((t, half), F32)
    cos_h = jnp.concatenate([cos, cos, ones], axis=1)
    sa_h = jnp.concatenate([-sin, zh, zeros], axis=1)
    sb_h = jnp.concatenate([zh, sin, zeros], axis=1)
    tile = lambda a: jnp.tile(a, (1, HEADS_PER_LANE_GROUP))
    return tile(cos_h), tile(sa_h), tile(sb_h)


def _inproj(x2, g, w, tables, t, tm):
    n, d = x2.shape
    b = n // t
    tpb = t // tm
    dils = tuple(dil for _, dil in DIL_BRANCHES)
    assert all(tm % (dil * 16) == 0 for dil in dils)
    row = lambda width: pl.BlockSpec((tm, width), lambda i: (i, 0))
    tab = pl.BlockSpec((tm, LANES), lambda i: (i % tpb, 0))
    plain = [NA_WIDTH] * 3 + [MEM_WIDTH]
    dil_specs = [pl.BlockSpec((None, dil, tm // dil, DIL_WIDTH), lambda i: (i // tpb, 0, i % tpb, 0))
                 for dil in dils] * 3
    dil_shapes = [jax.ShapeDtypeStruct((b, dil, t // dil, DIL_WIDTH), BF16) for dil in dils] * 3
    outs = pl.pallas_call(
        functools.partial(_inproj_kernel, dils=dils),
        grid=(n // tm,),
        in_specs=[row(d), pl.BlockSpec((1, d), lambda i: (0, 0)),
                  pl.BlockSpec(w.shape, lambda i: (0, 0)), tab, tab, tab],
        out_specs=[row(wd) for wd in plain] + dil_specs,
        out_shape=[jax.ShapeDtypeStruct((n, wd), BF16) for wd in plain] + dil_shapes,
        scratch_shapes=[pltpu.VMEM((DIL_WIDTH // LANES, tm, LANES), F32)],
        compiler_params=_cparams(("parallel",)),
        name="inproj",
    )(x2, g, w, *tables)
    nd = len(dils)
    qa, ka, va, qm = outs[:4]
    return qa, ka, va, qm, outs[4:4 + nd], outs[4 + nd:4 + 2 * nd], outs[4 + 2 * nd:]


def _na_bias_table(rpb):
    c = jnp.arange(GRID_W)
    c0 = jnp.clip(c - NA_WIN_COLS // 2, 0, GRID_W - NA_WIN_COLS)
    kc = jnp.arange(GRID_W)
    valid = (kc[None, :] >= c0[:, None]) & (kc[None, :] < c0[:, None] + NA_WIN_COLS)
    coff = kc[None, :] - c[:, None] + (NA_WIN_COLS - 1)
    onehot = (coff[None] == jnp.arange(2 * NA_WIN_COLS - 1)[:, None, None]).astype(F32)
    by_col = jnp.einsum("hrd,dck->hrck", rpb.astype(F32), onehot, precision=lax.Precision.HIGHEST)
    by_col = jnp.where(valid[None, None], by_col * LOG2E, NEG)
    tab = jnp.stack([by_col[:, d:d + NA_WIN_ROWS] for d in range(NA_WIN_ROWS)], axis=1)
    tab = tab.transpose(1, 0, 3, 2, 4)
    return tab.reshape(NA_WIN_ROWS * NA_HEADS // HEADS_PER_LANE_GROUP, HEADS_PER_LANE_GROUP * GRID_W,
                       NA_WIN_ROWS * GRID_W)


def _na_kernel(q_ref, kp_ref, kc_ref, kn_ref, vp_ref, vc_ref, vn_ref, bias_ref, o_ref,
               kwin, vwin, *, rows):
    j = pl.program_id(1)
    blk = NA_WIN_ROWS * GRID_W
    for idx, (kr, vr) in enumerate(((kp_ref, vp_ref), (kc_ref, vc_ref), (kn_ref, vn_ref))):
        kwin[idx * blk:(idx + 1) * blk, :] = kr[...]
        vwin[idx * blk:(idx + 1) * blk, :] = vr[...]

    n_pairs = NA_HEADS // HEADS_PER_LANE_GROUP

    def one_row(i):
        q_rows = pl.ds(pl.multiple_of(i * GRID_W, GRID_W), GRID_W)

        def load():
            r = j * NA_WIN_ROWS + i
            r0 = jnp.clip(r - NA_WIN_ROWS // 2, 0, rows - NA_WIN_ROWS)
            dlt = r0 - r + (NA_WIN_ROWS - 1)
            start = pl.multiple_of((r0 - (j - 1) * NA_WIN_ROWS) * GRID_W, GRID_W)
            return (q_ref[q_rows, :], kwin[pl.ds(start, blk), :], vwin[pl.ds(start, blk), :],
                    lambda p: bias_ref[dlt * n_pairs + p])

        def store(o, lses):
            o_ref[q_rows, :] = o.astype(BF16)

        return load, store

    def row_group(ig, carry):
        _attention_pipelined([one_row(ig * ROWS_PER_TRIP + u) for u in range(ROWS_PER_TRIP)], NA_HEADS)
        return carry

    lax.fori_loop(0, NA_WIN_ROWS // ROWS_PER_TRIP, row_group, 0)


def _na_attention(q, k, v, bias):
    b, t, w = q.shape
    rows = t // GRID_W
    assert rows % NA_WIN_ROWS == 0 and rows >= NA_WIN_ROWS
    nj = rows // NA_WIN_ROWS
    blk = NA_WIN_ROWS * GRID_W
    cur = pl.BlockSpec((None, blk, w), lambda bi, j: (bi, j, 0))
    prev = pl.BlockSpec((None, blk, w), lambda bi, j: (bi, jnp.maximum(j - 1, 0), 0))
    nxt = pl.BlockSpec((None, blk, w), lambda bi, j: (bi, jnp.minimum(j + 1, nj - 1), 0))
    return pl.pallas_call(
        functools.partial(_na_kernel, rows=rows),
        grid=(b, nj),
        in_specs=[cur, prev, cur, nxt, prev, cur, nxt,
                  pl.BlockSpec(bias.shape, lambda bi, j: (0, 0, 0))],
        out_specs=cur,
        out_shape=jax.ShapeDtypeStruct((b, t, w), BF16),
        scratch_shapes=[pltpu.VMEM((3 * blk, w), BF16), pltpu.VMEM((3 * blk, w), BF16)],
        compiler_params=_cparams(("parallel", "parallel")),
        name="na_attn",
    )(q, k, k, k, v, v, v, bias)


def _dil_kernel(q_ref, kp_ref, kc_ref, kn_ref, vp_ref, vc_ref, vn_ref, o_ref, lse_ref,
                kwin, vwin, *, seg, tq, halo):
    i = pl.program_id(2)
    width = tq + 2 * halo
    kwin[0:halo, :] = kp_ref[...]
    kwin[halo:halo + tq, :] = kc_ref[...]
    kwin[halo + tq:width, :] = kn_ref[...]
    vwin[0:halo, :] = vp_ref[...]
    vwin[halo:halo + tq, :] = vc_ref[...]
    vwin[halo + tq:width, :] = vn_ref[...]

    sq = math.gcd(tq, 2 * halo)
    sw = sq + 2 * halo
    a = lax.broadcasted_iota(jnp.int32, (sq, sw), 0)
    c = lax.broadcasted_iota(jnp.int32, (sq, sw), 1)
    band = (c >= a) & (c <= a + 2 * halo)
    lo = lax.broadcasted_iota(jnp.int32, (sq, LANES), 1) < HEAD_DIM

    def sub_block(s):
        row0 = pl.multiple_of(s * sq, sq)

        def load():
            first_key = i * tq + s * sq - halo
            valid = band & (c >= -first_key) & (c < seg - first_key)
            negb = jnp.where(valid, 0.0, NEG).astype(F32)
            negb2 = jnp.concatenate([negb] * HEADS_PER_LANE_GROUP, axis=0)
            return (q_ref[pl.ds(row0, sq), :], kwin[pl.ds(row0, sw), :], vwin[pl.ds(row0, sw), :],
                    lambda p: negb2)

        def store(o, lses):
            o_ref[pl.ds(row0, sq), :] = o.astype(BF16)
            lse_ref[pl.ds(row0, sq), :] = jnp.concatenate(
                [jnp.where(lo, lses[2 * p], lses[2 * p + 1]) for p in range(DIL_HEADS // 2)], axis=-1)

        return load, store

    n_sub = tq // sq
    per_trip = math.gcd(n_sub, ROWS_PER_TRIP)

    def sub_block_group(sg, carry):
        _attention_pipelined([sub_block(sg * per_trip + u) for u in range(per_trip)], DIL_HEADS, skew=0)
        return carry

    lax.fori_loop(0, n_sub // per_trip, sub_block_group, 0)


def _dilated_branch(q, k, v, n_side, tq_max):
    b, dil, seg, w = q.shape
    halo = n_side
    assert seg % halo == 0 and halo % 16 == 0
    tq = math.gcd(tq_max, seg)
    assert tq % halo == 0
    hb = tq // halo
    nhalo = seg // halo
    cur = pl.BlockSpec((None, None, tq, w), lambda bi, r, i: (bi, r, i, 0))
    prev = pl.BlockSpec((None, None, halo, w), lambda bi, r, i: (bi, r, jnp.maximum(i * hb - 1, 0), 0))
    nxt = pl.BlockSpec((None, None, halo, w),
                       lambda bi, r, i: (bi, r, jnp.minimum((i + 1) * hb, nhalo - 1), 0))
    return pl.pallas_call(
        functools.partial(_dil_kernel, seg=seg, tq=tq, halo=halo),
        grid=(b, dil, seg // tq),
        in_specs=[cur, prev, cur, nxt, prev, cur, nxt],
        out_specs=[cur, cur],
        out_shape=[jax.ShapeDtypeStruct((b, dil, seg, w), BF16),
                   jax.ShapeDtypeStruct((b, dil, seg, w), F32)],
        scratch_shapes=[pltpu.VMEM((tq + 2 * halo, w), BF16), pltpu.VMEM((tq + 2 * halo, w), BF16)],
        compiler_params=_cparams(("parallel", "parallel", "parallel")),
        name=f"dilated_d{dil}",
    )(q, k, k, k, v, v, v)


def _memkv_kernel(mem_ref, g_ref, w_ref, k_ref, v_ref):
    h = _rms(mem_ref[...], g_ref[...]).astype(BF16)
    kv = jnp.dot(h, w_ref[...], preferred_element_type=F32)
    k_ref[...] = kv[:, :MEM_WIDTH].astype(BF16)
    v_ref[...] = kv[:, MEM_WIDTH:].astype(BF16)


def _mem_kv(mem, g, w):
    b, m, d = mem.shape
    blk = lambda width: pl.BlockSpec((None, m, width), lambda bi: (bi, 0, 0))
    return pl.pallas_call(
        _memkv_kernel,
        grid=(b,),
        in_specs=[blk(d), pl.BlockSpec((1, d), lambda bi: (0, 0)), pl.BlockSpec(w.shape, lambda bi: (0, 0))],
        out_specs=[blk(MEM_WIDTH), blk(MEM_WIDTH)],
        out_shape=[jax.ShapeDtypeStruct((b, m, MEM_WIDTH), BF16)] * 2,
        compiler_params=_cparams(("parallel",)),
        name="mem_kv",
    )(mem, g, w)


def _memattn_kernel(q_ref, k_ref, v_ref, o_ref):
    o, _ = _pair_heads_attention(q_ref[...], k_ref[...], v_ref[...], lambda p: 0.0, MEM_HEADS)
    o_ref[...] = o.astype(BF16)


def _mem_attention(q, k, v, tq):
    b, t, w = q.shape
    m = k.shape[1]
    qs = pl.BlockSpec((None, tq, w), lambda bi, i: (bi, i, 0))
    ks = pl.BlockSpec((None, m, w), lambda bi, i: (bi, 0, 0))
    return pl.pallas_call(
        _memattn_kernel,
        grid=(b, t // tq),
        in_specs=[qs, ks, ks],
        out_specs=qs,
        out_shape=jax.ShapeDtypeStruct((b, t, w), BF16),
        compiler_params=_cparams(("parallel", "parallel")),
        name="mem_attn",
    )(q, k, v)


def _token_order(ref, scr):
    dil = ref.shape[0]
    if dil == 1:
        return ref[0].astype(F32)
    groups, tm, _ = scr.shape
    for r in range(dil):
        for g in range(groups):
            scr[g, pl.ds(r, tm // dil, stride=dil), :] = ref[r, :, g * LANES:(g + 1) * LANES].astype(F32)
    return jnp.concatenate([scr[g] for g in range(groups)], axis=-1)


def _branch_mix(ob_refs, lse_refs, o_scrs, l_scrs):
    lses = [_token_order(r, s) for r, s in zip(lse_refs, l_scrs)]
    mx = functools.reduce(jnp.maximum, lses)
    es = [jnp.exp2(l - mx) for l in lses]
    den = functools.reduce(lambda p, q: p + q, es)
    acc = None
    for e, ob, scr in zip(es, ob_refs, o_scrs):
        term = (e / den) * _token_order(ob, scr)
        acc = term if acc is None else acc + term
    return acc


def _outproj_core(oa_ref, ob_refs, lse_refs, om_ref, x_ref, gm_ref, w_ref, o_scrs, l_scrs):
    gm = gm_ref[...]
    e0, e1 = NA_WIDTH, NA_WIDTH + DIL_WIDTH
    ya = _rms(oa_ref[...].astype(F32), gm[:, :e0]).astype(BF16)
    yb = _rms(_branch_mix(ob_refs, lse_refs, o_scrs, l_scrs), gm[:, e0:e1]).astype(BF16)
    ym = _rms(om_ref[...].astype(F32), gm[:, e1:]).astype(BF16)
    y = jnp.concatenate([ya, yb, ym], axis=-1)
    return x_ref[...] + jnp.dot(y, w_ref[...], preferred_element_type=F32)


def _split_scratch(scrs, dils):
    it = iter(scrs)
    o_scrs = [next(it) if dil > 1 else None for dil in dils]
    l_scrs = [next(it) if dil > 1 else None for dil in dils]
    return o_scrs, l_scrs


def _outproj_kernel(oa_ref, ob1, ob2, ob3, l1, l2, l3, om_ref, x_ref, gm_ref, w_ref, xo_ref, *scrs, dils):
    xo_ref[...] = _outproj_core(oa_ref, (ob1, ob2, ob3), (l1, l2, l3), om_ref, x_ref, gm_ref, w_ref,
                                *_split_scratch(scrs, dils))


def _store_row_tiles(ref, val):
    m, d = val.shape
    for j in range(d // LANES):
        ref[pl.ds(j, m, stride=d // LANES), :] = val[:, j * LANES:(j + 1) * LANES]


def _outproj_router_kernel(oa_ref, ob1, ob2, ob3, l1, l2, l3, om_ref, x_ref, gm_ref, w_ref,
                           gf_ref, wr_ref, xo_ref, h_ref, route_ref, *scrs, n_experts, dils):
    xn = _outproj_core(oa_ref, (ob1, ob2, ob3), (l1, l2, l3), om_ref, x_ref, gm_ref, w_ref,
                       *_split_scratch(scrs, dils))
    xo_ref[...] = xn
    h = _rms(xn, gf_ref[...])
    _store_row_tiles(h_ref, h)
    logits = jnp.dot(h.astype(BF16), wr_ref[...], preferred_element_type=F32)
    lane = lax.broadcasted_iota(jnp.int32, logits.shape, 1)
    lg = jnp.where(lane < n_experts, logits, NEG)
    m1 = jnp.max(lg, axis=-1, keepdims=True)
    i1 = jnp.min(jnp.where(lg == m1, lane, LANES), axis=-1, keepdims=True)
    lg2 = jnp.where(lane == i1, NEG, lg)
    m2 = jnp.max(lg2, axis=-1, keepdims=True)
    i2 = jnp.min(jnp.where(lg2 == m2, lane, LANES), axis=-1, keepdims=True)
    e2 = jnp.exp(m2 - m1)
    g1 = 1.0 / (1.0 + e2)
    g2 = e2 / (1.0 + e2)
    route = jnp.where(lane == 0, i1.astype(F32), 0.0)
    route = jnp.where(lane == 1, i2.astype(F32), route)
    route = jnp.where(lane == 2, g1, route)
    route = jnp.where(lane == 3, g2, route)
    route_ref[...] = route


def _outproj(oa, obs, lses, om, x2, gm, w, t, tm, router=None):
    n, d = x2.shape
    tpb = t // tm
    row = lambda width: pl.BlockSpec((tm, width), lambda i: (i, 0))
    full = lambda a: pl.BlockSpec(a.shape, lambda i: (0,) * a.ndim)
    grouped = lambda a: pl.BlockSpec((None, a.shape[1], tm // a.shape[1], a.shape[3]),
                                     lambda i: (i // tpb, 0, i % tpb, 0))
    dils = tuple(a.shape[1] for a in obs)
    args = [oa, *obs, *lses, om, x2, gm, w]
    in_specs = ([row(NA_WIDTH)] + [grouped(a) for a in obs] + [grouped(a) for a in lses]
                + [row(MEM_WIDTH), row(d), full(gm), full(w)])
    scratch = [pltpu.VMEM((DIL_WIDTH // LANES, tm, LANES), F32)] * (2 * sum(dil > 1 for dil in dils))
    if router is None:
        return pl.pallas_call(
            functools.partial(_outproj_kernel, dils=dils),
            grid=(n // tm,), in_specs=in_specs, out_specs=row(d),
            out_shape=jax.ShapeDtypeStruct((n, d), F32), scratch_shapes=scratch,
            compiler_params=_cparams(("parallel",)), name="outproj",
        )(*args)
    gf, wr, n_experts = router
    sub = d // LANES
    return pl.pallas_call(
        functools.partial(_outproj_router_kernel, n_experts=n_experts, dils=dils),
        grid=(n // tm,), in_specs=in_specs + [full(gf), full(wr)],
        out_specs=[row(d), pl.BlockSpec((tm * sub, LANES), lambda i: (i, 0)), row(LANES)],
        out_shape=[jax.ShapeDtypeStruct((n, d), F32), jax.ShapeDtypeStruct((n * sub, LANES), F32),
                   jax.ShapeDtypeStruct((n, LANES), F32)],
        scratch_shapes=scratch,
        compiler_params=_cparams(("parallel",)), name="outproj_router",
    )(*args, gf, wr)


MXU_COLS = 256


def _swiglu_partial(h, wg_ref, wu_ref, wd_ref, act_scr):
    tf = wg_ref.shape[-1]
    chunk = math.gcd(MXU_COLS, tf)
    for c in range(tf // chunk):
        sl = slice(c * chunk, (c + 1) * chunk)
        gate = jnp.dot(h, wg_ref[:, sl], preferred_element_type=F32)
        up = jnp.dot(h, wu_ref[:, sl], preferred_element_type=F32)
        act_scr[:, sl] = ((gate / (1.0 + jnp.exp(-gate))) * up).astype(BF16)
    return jnp.dot(act_scr[...], wd_ref[...], preferred_element_type=F32)


def _dense_ffn_kernel(x_ref, g_ref, wg_ref, wu_ref, wd_ref, o_ref, h_scr, act_scr):
    @pl.when(pl.program_id(1) == 0)
    def _():
        x = x_ref[...]
        h_scr[...] = _rms(x, g_ref[...]).astype(BF16)
        o_ref[...] = x

    o_ref[...] += _swiglu_partial(h_scr[...], wg_ref, wu_ref, wd_ref, act_scr)


def _dense_ffn(x2, g, wg, wu, wd, tm, tf):
    n, d = x2.shape
    f = wg.shape[1]
    row = pl.BlockSpec((tm, d), lambda i, k: (i, 0))
    return pl.pallas_call(
        _dense_ffn_kernel,
        grid=(n // tm, f // tf),
        in_specs=[row, pl.BlockSpec((1, d), lambda i, k: (0, 0)),
                  pl.BlockSpec((d, tf), lambda i, k: (0, k)),
                  pl.BlockSpec((d, tf), lambda i, k: (0, k)),
                  pl.BlockSpec((tf, d), lambda i, k: (k, 0))],
        out_specs=row,
        out_shape=jax.ShapeDtypeStruct((n, d), F32),
        scratch_shapes=[pltpu.VMEM((tm, d), BF16), pltpu.VMEM((tm, tf), BF16)],
        compiler_params=_cparams(("parallel", "arbitrary")),
        name="dense_ffn",
    )(x2, g, wg, wu, wd)


def _moe_ffn_kernel(be_ref, nused_ref, xb_ref, wg_ref, wu_ref, wd_ref, y_ref, h_scr, act_scr):
    j = pl.program_id(0)
    k = pl.program_id(1)
    used = j < nused_ref[0]
    mb, d = h_scr.shape
    sub = d // LANES

    @pl.when(k == 0)
    def _():
        y_ref[...] = jnp.zeros_like(y_ref)

    @pl.when(used & (k == 0))
    def _():
        for c in range(sub):
            h_scr[:, c * LANES:(c + 1) * LANES] = xb_ref[pl.ds(c, mb, stride=sub), :].astype(BF16)

    @pl.when(used)
    def _():
        part = _swiglu_partial(h_scr[...], wg_ref, wu_ref, wd_ref, act_scr)
        for c in range(sub):
            y_ref[pl.ds(c, mb, stride=sub), :] += part[:, c * LANES:(c + 1) * LANES]


def _moe_ffn(buf, block_expert, n_used, wg, wu, wd, mb, tf):
    d, f = wg.shape[1], wg.shape[2]
    sub = d // LANES
    p = buf.shape[0] // sub
    nk = f // tf
    kk = lambda j, k, be, nu: jnp.where(j < nu[0], k, nk - 1)
    row = pl.BlockSpec((mb * sub, LANES), lambda j, k, be, nu: (j, 0))
    grid_spec = pltpu.PrefetchScalarGridSpec(
        num_scalar_prefetch=2,
        grid=(p // mb, nk),
        in_specs=[row,
                  pl.BlockSpec((None, d, tf), lambda j, k, be, nu: (be[j], 0, kk(j, k, be, nu))),
                  pl.BlockSpec((None, d, tf), lambda j, k, be, nu: (be[j], 0, kk(j, k, be, nu))),
                  pl.BlockSpec((None, tf, d), lambda j, k, be, nu: (be[j], kk(j, k, be, nu), 0))],
        out_specs=row,
        scratch_shapes=[pltpu.VMEM((mb, d), BF16), pltpu.VMEM((mb, tf), BF16)],
    )
    return pl.pallas_call(
        _moe_ffn_kernel,
        grid_spec=grid_spec,
        out_shape=jax.ShapeDtypeStruct((p * sub, LANES), F32),
        compiler_params=_cparams(("parallel", "arbitrary")),
        name="moe_ffn",
    )(block_expert, n_used, buf, wg, wu, wd)


ISSUE_UNROLL = 8


def _issue_tile_copies(chunk, copy_fn):
    def body(trip, carry):
        for u in range(ISSUE_UNROLL):
            copy_fn(trip, u).start(priority=u % 2)
        return carry

    lax.fori_loop(0, chunk // ISSUE_UNROLL, body, 0)


def _tile(ref, idx, sub):
    return ref.at[pl.ds(pl.multiple_of(idx * sub, sub), sub)]


def _dispatch_kernel(slot_ref, h_ref, out_ref, zero_scr, sem, *, n_tok_steps, sub):
    i = pl.program_id(0)
    n_copies = slot_ref.shape[-1]
    dst = lambda trip, u: _tile(out_ref, slot_ref[0, 0, trip * ISSUE_UNROLL + u], sub)
    tok = lambda trip, u: trip * (ISSUE_UNROLL // TOP_K) + u // TOP_K

    @pl.when(i == 0)
    def _():
        zero_scr[...] = jnp.zeros_like(zero_scr)

    @pl.when(i < n_tok_steps)
    def _():
        _issue_tile_copies(n_copies, lambda trip, u: pltpu.make_async_copy(
            _tile(h_ref, tok(trip, u), sub), dst(trip, u), sem))

    @pl.when(i >= n_tok_steps)
    def _():
        _issue_tile_copies(n_copies, lambda trip, u: pltpu.make_async_copy(zero_scr, dst(trip, u), sem))

    for _ in range(TOP_K):
        pltpu.make_async_copy(h_ref, out_ref.at[pl.ds(0, h_ref.shape[0])], sem).wait()


def _dispatch_rows(h, slot, tm, sub):
    n = h.shape[0] // sub
    p = slot.shape[0]
    n_copies = tm * TOP_K
    assert p % n_copies == 0 and n % tm == 0 and n_copies % ISSUE_UNROLL == 0
    n_tok_steps = n // tm
    return pl.pallas_call(
        functools.partial(_dispatch_kernel, n_tok_steps=n_tok_steps, sub=sub),
        grid=(p // n_copies,),
        in_specs=[pl.BlockSpec((1, 1, n_copies), lambda i: (i, 0, 0), memory_space=pltpu.SMEM),
                  pl.BlockSpec((tm * sub, LANES), lambda i: (jnp.minimum(i, n_tok_steps - 1), 0))],
        out_specs=pl.BlockSpec(memory_space=pl.ANY),
        out_shape=jax.ShapeDtypeStruct((p * sub, LANES), h.dtype),
        scratch_shapes=[pltpu.VMEM((sub, LANES), h.dtype), pltpu.SemaphoreType.DMA(())],
        compiler_params=_cparams(("arbitrary",)),
        name="dispatch_rows",
    )(slot.reshape(p // n_copies, 1, n_copies), h)


def _combine_kernel(cur_ref, nxt_ref, x_ref, route_ref, y_ref, *rest, final_norm, sub):
    o_ref, ybuf, sem = rest[-3:]
    i = pl.program_id(0)
    tm = x_ref.shape[0]
    n_copies = tm * TOP_K
    par = i % 2

    def fetch(idx_ref, buf):
        def copy(trip, u):
            tok = trip * (ISSUE_UNROLL // TOP_K) + u // TOP_K
            return pltpu.make_async_copy(_tile(y_ref, idx_ref[0, 0, trip * ISSUE_UNROLL + u], sub),
                                         _tile(ybuf.at[buf, u % TOP_K], tok, sub), sem.at[buf])
        _issue_tile_copies(n_copies, copy)

    @pl.when(i == 0)
    def _():
        fetch(cur_ref, 0)

    @pl.when(i + 1 < pl.num_programs(0))
    def _():
        fetch(nxt_ref, 1 - par)

    pltpu.make_async_copy(ybuf.at[par], ybuf.at[par], sem.at[par]).wait()

    route = route_ref[...]
    gates = [route[:, TOP_K + k:TOP_K + k + 1] for k in range(TOP_K)]
    for j in range(sub):
        sl = slice(j * LANES, (j + 1) * LANES)
        mix = None
        for k in range(TOP_K):
            term = gates[k] * ybuf[par, k, pl.ds(j, tm, stride=sub), :]
            mix = term if mix is None else mix + term
        o_ref[:, sl] = x_ref[:, sl] + mix
    if final_norm:
        o_ref[...] = _rms(o_ref[...], rest[0][...])


def _combine(x2, y, dest, route, tm, g_final=None):
    n, d = x2.shape
    sub = d // LANES
    n_copies = tm * TOP_K
    steps = n // tm
    assert n_copies % ISSUE_UNROLL == 0 and ISSUE_UNROLL % TOP_K == 0
    dest3 = dest.reshape(steps, 1, n_copies)
    row = pl.BlockSpec((tm, d), lambda i: (i, 0))
    in_specs = [pl.BlockSpec((1, 1, n_copies), lambda i: (i, 0, 0), memory_space=pltpu.SMEM),
                pl.BlockSpec((1, 1, n_copies), lambda i: (jnp.minimum(i + 1, steps - 1), 0, 0),
                             memory_space=pltpu.SMEM),
                row, pl.BlockSpec((tm, LANES), lambda i: (i, 0)), pl.BlockSpec(memory_space=pl.ANY)]
    args = [dest3, dest3, x2, route, y]
    if g_final is not None:
        in_specs.append(pl.BlockSpec((1, d), lambda i: (0, 0)))
        args.append(g_final)
    return pl.pallas_call(
        functools.partial(_combine_kernel, final_norm=g_final is not None, sub=sub),
        grid=(steps,), in_specs=in_specs, out_specs=row,
        out_shape=jax.ShapeDtypeStruct((n, d), F32),
        scratch_shapes=[pltpu.VMEM((2, TOP_K, tm * sub, LANES), F32), pltpu.SemaphoreType.DMA((2,))],
        compiler_params=_cparams(("arbitrary",)), name="moe_combine",
    )(*args)


def _final_norm_kernel(x_ref, g_ref, o_ref):
    o_ref[...] = _rms(x_ref[...], g_ref[...])


def _final_norm(x2, g, tm):
    n, d = x2.shape
    row = pl.BlockSpec((tm, d), lambda i: (i, 0))
    return pl.pallas_call(
        _final_norm_kernel, grid=(n // tm,), in_specs=[row, pl.BlockSpec((1, d), lambda i: (0, 0))],
        out_specs=row, out_shape=jax.ShapeDtypeStruct((n, d), F32),
        compiler_params=_cparams(("parallel",)), name="final_norm",
    )(x2, g)


def _routing_plan(route, n_experts, mb):
    n = route.shape[0]
    n_assign = n * TOP_K
    flat_e = route[:, :TOP_K].astype(jnp.int32).reshape(-1)
    onehot = (flat_e[:, None] == jnp.arange(n_experts)[None, :]).astype(jnp.int32)
    csum = jnp.cumsum(onehot, axis=0)
    rank = jnp.sum(csum * onehot, axis=1) - 1
    counts = csum[-1]
    padded = (counts + mb - 1) // mb * mb
    pend = jnp.cumsum(padded)
    pstart = pend - padded
    dest = jnp.sum(pstart[None, :] * onehot, axis=1) + rank
    assert n_assign % mb == 0
    n_blocks = n_assign // mb + n_experts
    block_first = jnp.arange(n_blocks) * mb
    block_expert = jnp.minimum(jnp.sum(block_first[:, None] >= pend[None, :], axis=1), n_experts - 1)
    n_used = (pend[-1] // mb).reshape(1)
    pads = padded - counts
    cpad = jnp.cumsum(pads)
    i = jnp.arange(n_experts * mb)
    grp = jnp.sum(i[:, None] >= cpad[None, :], axis=1)
    grp_hot = (grp[:, None] == jnp.arange(n_experts + 1)[None, :]).astype(jnp.int32)
    first_free = jnp.concatenate([pstart + counts, pend[-1:]])
    before = jnp.concatenate([jnp.zeros((1,), cpad.dtype), cpad])
    free = jnp.sum(grp_hot * (first_free - before)[None, :], axis=1) + i
    slot = jnp.concatenate([dest, free]).astype(jnp.int32)
    return dest.astype(jnp.int32), slot, block_expert.astype(jnp.int32), n_used.astype(jnp.int32)


def _tiles(n, t):
    return dict(tm_in=math.gcd(1024, t), tm_proj=math.gcd(512, t), tm_ffn=math.gcd(1024, n), tf=1792,
                tq_dil=1024, tq_mem=math.gcd(512, t), moe_block=math.gcd(1024, n))


@jax.jit
def _forward(x, mem, g_mix_norm, w_in, rpb, g_mem_norm, w_mem_kv, g_mix_out, w_out,
             g_ffn_norm, w_dense_gate, w_dense_up, w_dense_down, w_router,
             w_moe_gate, w_moe_up, w_moe_down, g_final):
    b, t, d = x.shape
    n = b * t
    depth = w_in.shape[0]
    n_experts = w_router.shape[-1]
    ts = _tiles(n, t)
    tf = math.gcd(ts["tf"], w_dense_gate.shape[-1])
    tables = _rope_tables(t)
    row1 = lambda a: a.reshape(1, -1)
    x2 = x.reshape(n, d)
    out = None
    sub = d // LANES
    for layer in range(depth):
        qa, ka, va, qm, qbs, kbs, vbs = _inproj(x2, row1(g_mix_norm[layer]), w_in[layer].astype(BF16),
                                                tables, t, ts["tm_in"])
        shp = lambda a: a.reshape(b, t, a.shape[-1])
        oa = _na_attention(shp(qa), shp(ka), shp(va), _na_bias_table(rpb[layer])).reshape(n, NA_WIDTH)
        obs, lses = [], []
        for (window, dil), qb, kb, vb in zip(DIL_BRANCHES, qbs, kbs, vbs):
            o, lse = _dilated_branch(qb, kb, vb, window // (2 * dil), ts["tq_dil"])
            obs.append(o)
            lses.append(lse)
        km, vm = _mem_kv(mem, row1(g_mem_norm[layer]), w_mem_kv[layer].astype(BF16))
        om = _mem_attention(shp(qm), km, vm, ts["tq_mem"]).reshape(n, MEM_WIDTH)
        gm, wo = row1(g_mix_out[layer]), w_out[layer].astype(BF16)
        i = layer // 2
        if layer % 2 == 0:
            x2 = _outproj(oa, obs, lses, om, x2, gm, wo, t, ts["tm_proj"])
            x2 = _dense_ffn(x2, row1(g_ffn_norm[layer]), w_dense_gate[i].astype(BF16),
                            w_dense_up[i].astype(BF16), w_dense_down[i].astype(BF16), ts["tm_ffn"], tf)
            if layer == depth - 1:
                out = _final_norm(x2, row1(g_final), ts["tm_proj"])
        else:
            wr = jnp.pad(w_router[i], ((0, 0), (0, LANES - n_experts))).astype(BF16)
            x2, h, route = _outproj(oa, obs, lses, om, x2, gm, wo, t, ts["tm_proj"],
                                    router=(row1(g_ffn_norm[layer]), wr, n_experts))
            mb = ts["moe_block"]
            dest, slot, block_expert, n_used = _routing_plan(route, n_experts, mb)
            buf = _dispatch_rows(h, slot, ts["tm_proj"], sub)
            y = _moe_ffn(buf, block_expert, n_used, w_moe_gate[i].astype(BF16), w_moe_up[i].astype(BF16),
                         w_moe_down[i].astype(BF16), mb, tf)
            if layer == depth - 1:
                out = _combine(x2, y, dest, route, ts["tm_proj"], g_final=row1(g_final))
            else:
                x2 = _combine(x2, y, dest, route, ts["tm_proj"])
    return out.reshape(b, t, d)


def kernel(x, mem, g_mix_norm, w_in, rpb, g_mem_norm, w_mem_kv, g_mix_out, w_out, g_ffn_norm,
           w_dense_gate, w_dense_up, w_dense_down, w_router, w_moe_gate, w_moe_up, w_moe_down, g_final):
    return _forward(x, mem, g_mix_norm, w_in, rpb, g_mem_norm, w_mem_kv, g_mix_out, w_out, g_ffn_norm,
                    w_dense_gate, w_dense_up, w_dense_down, w_router, w_moe_gate, w_moe_up, w_moe_down,
                    g_final)
```

```python
import functools
import math

import jax
import jax.numpy as jnp
from jax import lax
from jax.experimental import pallas as pl
from jax.experimental.pallas import tpu as pltpu

HEAD_DIM = 64
NA_HEADS = 6
DIL_HEADS = 6
MEM_HEADS = 4
NA_WIDTH = NA_HEADS * HEAD_DIM
DIL_WIDTH = DIL_HEADS * HEAD_DIM
MEM_WIDTH = MEM_HEADS * HEAD_DIM
GRID_W = 64
NA_WIN_ROWS = 8
NA_WIN_COLS = 16
DIL_BRANCHES = ((128, 1), (512, 4), (2048, 16))
ROPE_THETA = 500000.0
ROT_DIM = HEAD_DIM // 4
TOP_K = 2
RMS_EPS = 1e-6
ATTN_SCALE = HEAD_DIM ** -0.5
LOG2E = math.log2(math.e)
Q_SCALE = ATTN_SCALE * LOG2E

LANES = 128
HEADS_PER_LANE_GROUP = LANES // HEAD_DIM
NEG = -1e30
ROWS_PER_TRIP = 4
VMEM_LIMIT = 56 * 1024 * 1024

F32 = jnp.float32
BF16 = jnp.bfloat16


def _cparams(sem):
    return pltpu.CompilerParams(dimension_semantics=sem, vmem_limit_bytes=VMEM_LIMIT)


def _rms(x, g):
    ms = jnp.mean(x * x, axis=-1, keepdims=True)
    return x * lax.rsqrt(ms + RMS_EPS) * g


def _pair_heads_attention(q, k, v, bias_fn, n_heads):
    results = []
    _attention_pipelined([(lambda: (q, k, v, bias_fn), lambda o, lses: results.append((o, lses)))], n_heads)
    return results[0]


def _lane_group(a, p):
    return a[:, p * LANES:(p + 1) * LANES]


def _attn_scores(q, k, bias_fn, n_heads):
    m = q.shape[0]
    assert HEADS_PER_LANE_GROUP == 2
    lo = lax.broadcasted_iota(jnp.int32, (m, LANES), 1) < HEAD_DIM
    s_parts = []
    for p in range(n_heads // HEADS_PER_LANE_GROUP):
        qp = _lane_group(q, p)
        zero = jnp.zeros_like(qp)
        q2 = jnp.concatenate([jnp.where(lo, qp, zero), jnp.where(lo, zero, qp)], axis=0)
        s_parts.append(lax.dot_general(q2, _lane_group(k, p), (((1,), (1,)), ((), ())),
                                       preferred_element_type=F32) + bias_fn(p))
    return jnp.concatenate(s_parts, axis=0)


def _attn_softmax(s):
    mx = jnp.max(s, axis=-1, keepdims=True)
    e = jnp.exp2(s - mx)
    den = jnp.sum(e, axis=-1, keepdims=True)
    return e.astype(BF16), den, mx


def _attn_values(eb, den, mx, v, n_heads):
    m = eb.shape[0] // n_heads
    lo = lax.broadcasted_iota(jnp.int32, (m, LANES), 1) < HEAD_DIM
    outs = []
    for p in range(n_heads // HEADS_PER_LANE_GROUP):
        rows = slice(p * 2 * m, (p + 1) * 2 * m)
        o2 = jnp.dot(eb[rows], _lane_group(v, p), preferred_element_type=F32) / den[rows]
        outs.append(jnp.where(lo, o2[:m], o2[m:]))
    lse = mx + jnp.log2(den)
    return jnp.concatenate(outs, axis=-1), [lse[h * m:(h + 1) * m] for h in range(n_heads)]


def _attention_pipelined(items, n_heads, skew=1):
    n = len(items)
    loaded, scores, soft = {}, {}, {}
    for step in range(n + 2 * skew):
        if step < n:
            loaded[step] = items[step][0]()
            q, k, _, bias_fn = loaded[step]
            scores[step] = _attn_scores(q, k, bias_fn, n_heads)
        if 0 <= step - skew < n:
            soft[step - skew] = _attn_softmax(scores.pop(step - skew))
        if 0 <= step - 2 * skew < n:
            eb, den, mx = soft.pop(step - 2 * skew)
            items[step - 2 * skew][1](*_attn_values(eb, den, mx, loaded.pop(step - 2 * skew)[2], n_heads))


def _inproj_kernel(x_ref, g_ref, w_ref, cos_ref, sa_ref, sb_ref,
                   qa_ref, ka_ref, va_ref, qm_ref, *rest, dils):
    dil_refs, scr = rest[:-1], rest[-1]
    tm = x_ref.shape[0]
    h = _rms(x_ref[...], g_ref[...]).astype(BF16)

    def emit_dilated(val, refs):
        groups = val.shape[1] // LANES
        for g in range(groups):
            scr[g] = val[:, g * LANES:(g + 1) * LANES]
        for dil, ref in zip(dils, refs):
            for r in range(dil):
                for g in range(groups):
                    ref[r, :, g * LANES:(g + 1) * LANES] = (
                        scr[g, pl.ds(r, tm // dil, stride=dil), :].astype(BF16))

    def proj_pair(c0, width_a, width_b):
        both = jnp.dot(h, w_ref[:, c0:c0 + width_a + width_b], preferred_element_type=F32)
        return both[:, :width_a], both[:, width_a:]

    def rope(a):
        cos, sa, sb = cos_ref[...], sa_ref[...], sb_ref[...]
        half = ROT_DIM // 2
        parts = []
        for gidx in range(a.shape[1] // LANES):
            xg = a[:, gidx * LANES:(gidx + 1) * LANES]
            parts.append(xg * cos + pltpu.roll(xg, LANES - half, 1) * sa + pltpu.roll(xg, half, 1) * sb)
        return jnp.concatenate(parts, axis=-1)

    nd = len(dils)
    assert (2 * NA_WIDTH) % 256 == 0 and (NA_WIDTH + DIL_WIDTH) % 256 == 0 and (2 * DIL_WIDTH) % 256 == 0
    qa, ka = proj_pair(0, NA_WIDTH, NA_WIDTH)
    qa_ref[...] = (qa * Q_SCALE).astype(BF16)
    ka_ref[...] = ka.astype(BF16)
    va, qb = proj_pair(2 * NA_WIDTH, NA_WIDTH, DIL_WIDTH)
    va_ref[...] = va.astype(BF16)
    emit_dilated(rope(qb * Q_SCALE), dil_refs[0:nd])
    kb, vb = proj_pair(3 * NA_WIDTH + DIL_WIDTH, DIL_WIDTH, DIL_WIDTH)
    emit_dilated(rope(kb), dil_refs[nd:2 * nd])
    emit_dilated(vb, dil_refs[2 * nd:3 * nd])
    c = 3 * NA_WIDTH + 3 * DIL_WIDTH
    qm_ref[...] = (jnp.dot(h, w_ref[:, c:c + MEM_WIDTH], preferred_element_type=F32) * Q_SCALE).astype(BF16)


def _rope_tables(t):
    half = ROT_DIM // 2
    inv_freq = ROPE_THETA ** (-jnp.arange(0, ROT_DIM, 2, dtype=F32) / ROT_DIM)
    ang = jnp.arange(t, dtype=F32)[:, None] * inv_freq[None, :]
    cos, sin = jnp.cos(ang), jnp.sin(ang)
    ones = jnp.ones((t, HEAD_DIM - ROT_DIM), F32)
    zeros = jnp.zeros((t, HEAD_DIM - ROT_DIM), F32)
    zh = jnp.zeros((t, half), F32)
    cos_h = jnp.concatenate([cos, cos, ones], axis=1)
    sa_h = jnp.concatenate([-sin, zh, zeros], axis=1)
    sb_h = jnp.concatenate([zh, sin, zeros], axis=1)
    tile = lambda a: jnp.tile(a, (1, HEADS_PER_LANE_GROUP))
    return tile(cos_h), tile(sa_h), tile(sb_h)


def _inproj(x2, g, w, tables, t, tm):
    n, d = x2.shape
    b = n // t
    tpb = t // tm
    dils = tuple(dil for _, dil in DIL_BRANCHES)
    assert all(tm % (dil * 16) == 0 for dil in dils)
    row = lambda width: pl.BlockSpec((tm, width), lambda i: (i, 0))
    tab = pl.BlockSpec((tm, LANES), lambda i: (i % tpb, 0))
    plain = [NA_WIDTH] * 3 + [MEM_WIDTH]
    dil_specs = [pl.BlockSpec((None, dil, tm // dil, DIL_WIDTH), lambda i: (i // tpb, 0, i % tpb, 0))
                 for dil in dils] * 3
    dil_shapes = [jax.ShapeDtypeStruct((b, dil, t // dil, DIL_WIDTH), BF16) for dil in dils] * 3
    outs = pl.pallas_call(
        functools.partial(_inproj_kernel, dils=dils),
        grid=(n // tm,),
        in_specs=[row(d), pl.BlockSpec((1, d), lambda i: (0, 0)),
                  pl.BlockSpec(w.shape, lambda i: (0, 0)), tab, tab, tab],
        out_specs=[row(wd) for wd in plain] + dil_specs,
        out_shape=[jax.ShapeDtypeStruct((n, wd), BF16) for wd in plain] + dil_shapes,
        scratch_shapes=[pltpu.VMEM((DIL_WIDTH // LANES, tm, LANES), F32)],
        compiler_params=_cparams(("parallel",)),
        name="inproj",
    )(x2, g, w, *tables)
    nd = len(dils)
    qa, ka, va, qm = outs[:4]
    return qa, ka, va, qm, outs[4:4 + nd], outs[4 + nd:4 + 2 * nd], outs[4 + 2 * nd:]


def _na_bias_table(rpb):
    c = jnp.arange(GRID_W)
    c0 = jnp.clip(c - NA_WIN_COLS // 2, 0, GRID_W - NA_WIN_COLS)
    kc = jnp.arange(GRID_W)
    valid = (kc[None, :] >= c0[:, None]) & (kc[None, :] < c0[:, None] + NA_WIN_COLS)
    coff = kc[None, :] - c[:, None] + (NA_WIN_COLS - 1)
    onehot = (coff[None] == jnp.arange(2 * NA_WIN_COLS - 1)[:, None, None]).astype(F32)
    by_col = jnp.einsum("hrd,dck->hrck", rpb.astype(F32), onehot, precision=lax.Precision.HIGHEST)
    by_col = jnp.where(valid[None, None], by_col * LOG2E, NEG)
    tab = jnp.stack([by_col[:, d:d + NA_WIN_ROWS] for d in range(NA_WIN_ROWS)], axis=1)
    tab = tab.transpose(1, 0, 3, 2, 4)
    return tab.reshape(NA_WIN_ROWS * NA_HEADS // HEADS_PER_LANE_GROUP, HEADS_PER_LANE_GROUP * GRID_W,
                       NA_WIN_ROWS * GRID_W)


def _na_kernel(q_ref, kp_ref, kc_ref, kn_ref, vp_ref, vc_ref, vn_ref, bias_ref, o_ref,
               kwin, vwin, *, rows):
    j = pl.program_id(1)
    blk = NA_WIN_ROWS * GRID_W
    for idx, (kr, vr) in enumerate(((kp_ref, vp_ref), (kc_ref, vc_ref), (kn_ref, vn_ref))):
        kwin[idx * blk:(idx + 1) * blk, :] = kr[...]
        vwin[idx * blk:(idx + 1) * blk, :] = vr[...]

    n_pairs = NA_HEADS // HEADS_PER_LANE_GROUP

    def one_row(i):
        q_rows = pl.ds(pl.multiple_of(i * GRID_W, GRID_W), GRID_W)

        def load():
            r = j * NA_WIN_ROWS + i
            r0 = jnp.clip(r - NA_WIN_ROWS // 2, 0, rows - NA_WIN_ROWS)
            dlt = r0 - r + (NA_WIN_ROWS - 1)
            start = pl.multiple_of((r0 - (j - 1) * NA_WIN_ROWS) * GRID_W, GRID_W)
            return (q_ref[q_rows, :], kwin[pl.ds(start, blk), :], vwin[pl.ds(start, blk), :],
                    lambda p: bias_ref[dlt * n_pairs + p])

        def store(o, lses):
            o_ref[q_rows, :] = o.astype(BF16)

        return load, store

    def row_group(ig, carry):
        _attention_pipelined([one_row(ig * ROWS_PER_TRIP + u) for u in range(ROWS_PER_TRIP)], NA_HEADS)
        return carry

    lax.fori_loop(0, NA_WIN_ROWS // ROWS_PER_TRIP, row_group, 0)


def _na_attention(q, k, v, bias):
    b, t, w = q.shape
    rows = t // GRID_W
    assert rows % NA_WIN_ROWS == 0 and rows >= NA_WIN_ROWS
    nj = rows // NA_WIN_ROWS
    blk = NA_WIN_ROWS * GRID_W
    cur = pl.BlockSpec((None, blk, w), lambda bi, j: (bi, j, 0))
    prev = pl.BlockSpec((None, blk, w), lambda bi, j: (bi, jnp.maximum(j - 1, 0), 0))
    nxt = pl.BlockSpec((None, blk, w), lambda bi, j: (bi, jnp.minimum(j + 1, nj - 1), 0))
    return pl.pallas_call(
        functools.partial(_na_kernel, rows=rows),
        grid=(b, nj),
        in_specs=[cur, prev, cur, nxt, prev, cur, nxt,
                  pl.BlockSpec(bias.shape, lambda bi, j: (0, 0, 0))],
        out_specs=cur,
        out_shape=jax.ShapeDtypeStruct((b, t, w), BF16),
        scratch_shapes=[pltpu.VMEM((3 * blk, w), BF16), pltpu.VMEM((3 * blk, w), BF16)],
        compiler_params=_cparams(("parallel", "parallel")),
        name="na_attn",
    )(q, k, k, k, v, v, v, bias)


def _dil_kernel(q_ref, kp_ref, kc_ref, kn_ref, vp_ref, vc_ref, vn_ref, o_ref, lse_ref,
                kwin, vwin, *, seg, tq, halo):
    i = pl.program_id(2)
    width = tq + 2 * halo
    kwin[0:halo, :] = kp_ref[...]
    kwin[halo:halo + tq, :] = kc_ref[...]
    kwin[halo + tq:width, :] = kn_ref[...]
    vwin[0:halo, :] = vp_ref[...]
    vwin[halo:halo + tq, :] = vc_ref[...]
    vwin[halo + tq:width, :] = vn_ref[...]

    sq = math.gcd(tq, 2 * halo)
    sw = sq + 2 * halo
    a = lax.broadcasted_iota(jnp.int32, (sq, sw), 0)
    c = lax.broadcasted_iota(jnp.int32, (sq, sw), 1)
    band = (c >= a) & (c <= a + 2 * halo)
    lo = lax.broadcasted_iota(jnp.int32, (sq, LANES), 1) < HEAD_DIM

    def sub_block(s):
        row0 = pl.multiple_of(s * sq, sq)

        def load():
            first_key = i * tq + s * sq - halo
            valid = band & (c >= -first_key) & (c < seg - first_key)
            negb = jnp.where(valid, 0.0, NEG).astype(F32)
            negb2 = jnp.concatenate([negb] * HEADS_PER_LANE_GROUP, axis=0)
            return (q_ref[pl.ds(row0, sq), :], kwin[pl.ds(row0, sw), :], vwin[pl.ds(row0, sw), :],
                    lambda p: negb2)

        def store(o, lses):
            o_ref[pl.ds(row0, sq), :] = o.astype(BF16)
            lse_ref[pl.ds(row0, sq), :] = jnp.concatenate(
                [jnp.where(lo, lses[2 * p], lses[2 * p + 1]) for p in range(DIL_HEADS // 2)], axis=-1)

        return load, store

    n_sub = tq // sq
    per_trip = math.gcd(n_sub, ROWS_PER_TRIP)

    def sub_block_group(sg, carry):
        _attention_pipelined([sub_block(sg * per_trip + u) for u in range(per_trip)], DIL_HEADS, skew=0)
        return carry

    lax.fori_loop(0, n_sub // per_trip, sub_block_group, 0)


def _dilated_branch(q, k, v, n_side, tq_max):
    b, dil, seg, w = q.shape
    halo = n_side
    assert seg % halo == 0 and halo % 16 == 0
    tq = math.gcd(tq_max, seg)
    assert tq % halo == 0
    hb = tq // halo
    nhalo = seg // halo
    cur = pl.BlockSpec((None, None, tq, w), lambda bi, r, i: (bi, r, i, 0))
    prev = pl.BlockSpec((None, None, halo, w), lambda bi, r, i: (bi, r, jnp.maximum(i * hb - 1, 0), 0))
    nxt = pl.BlockSpec((None, None, halo, w),
                       lambda bi, r, i: (bi, r, jnp.minimum((i + 1) * hb, nhalo - 1), 0))
    return pl.pallas_call(
        functools.partial(_dil_kernel, seg=seg, tq=tq, halo=halo),
        grid=(b, dil, seg // tq),
        in_specs=[cur, prev, cur, nxt, prev, cur, nxt],
        out_specs=[cur, cur],
        out_shape=[jax.ShapeDtypeStruct((b, dil, seg, w), BF16),
                   jax.ShapeDtypeStruct((b, dil, seg, w), F32)],
        scratch_shapes=[pltpu.VMEM((tq + 2 * halo, w), BF16), pltpu.VMEM((tq + 2 * halo, w), BF16)],
        compiler_params=_cparams(("parallel", "parallel", "parallel")),
        name=f"dilated_d{dil}",
    )(q, k, k, k, v, v, v)


def _memkv_kernel(mem_ref, g_ref, w_ref, k_ref, v_ref):
    h = _rms(mem_ref[...], g_ref[...]).astype(BF16)
    kv = jnp.dot(h, w_ref[...], preferred_element_type=F32)
    k_ref[...] = kv[:, :MEM_WIDTH].astype(BF16)
    v_ref[...] = kv[:, MEM_WIDTH:].astype(BF16)


def _mem_kv(mem, g, w):
    b, m, d = mem.shape
    blk = lambda width: pl.BlockSpec((None, m, width), lambda bi: (bi, 0, 0))
    return pl.pallas_call(
        _memkv_kernel,
        grid=(b,),
        in_specs=[blk(d), pl.BlockSpec((1, d), lambda bi: (0, 0)), pl.BlockSpec(w.shape, lambda bi: (0, 0))],
        out_specs=[blk(MEM_WIDTH), blk(MEM_WIDTH)],
        out_shape=[jax.ShapeDtypeStruct((b, m, MEM_WIDTH), BF16)] * 2,
        compiler_params=_cparams(("parallel",)),
        name="mem_kv",
    )(mem, g, w)


def _memattn_kernel(q_ref, k_ref, v_ref, o_ref):
    o, _ = _pair_heads_attention(q_ref[...], k_ref[...], v_ref[...], lambda p: 0.0, MEM_HEADS)
    o_ref[...] = o.astype(BF16)


def _mem_attention(q, k, v, tq):
    b, t, w = q.shape
    m = k.shape[1]
    qs = pl.BlockSpec((None, tq, w), lambda bi, i: (bi, i, 0))
    ks = pl.BlockSpec((None, m, w), lambda bi, i: (bi, 0, 0))
    return pl.pallas_call(
        _memattn_kernel,
        grid=(b, t // tq),
        in_specs=[qs, ks, ks],
        out_specs=qs,
        out_shape=jax.ShapeDtypeStruct((b, t, w), BF16),
        compiler_params=_cparams(("parallel", "parallel")),
        name="mem_attn",
    )(q, k, v)


def _token_order(ref, scr):
    dil = ref.shape[0]
    if dil == 1:
        return ref[0].astype(F32)
    groups, tm, _ = scr.shape
    for r in range(dil):
        for g in range(groups):
            scr[g, pl.ds(r, tm // dil, stride=dil), :] = ref[r, :, g * LANES:(g + 1) * LANES].astype(F32)
    return jnp.concatenate([scr[g] for g in range(groups)], axis=-1)


def _branch_mix(ob_refs, lse_refs, o_scrs, l_scrs):
    lses = [_token_order(r, s) for r, s in zip(lse_refs, l_scrs)]
    mx = functools.reduce(jnp.maximum, lses)
    es = [jnp.exp2(l - mx) for l in lses]
    den = functools.reduce(lambda p, q: p + q, es)
    acc = None
    for e, ob, scr in zip(es, ob_refs, o_scrs):
        term = (e / den) * _token_order(ob, scr)
        acc = term if acc is None else acc + term
    return acc


def _outproj_core(oa_ref, ob_refs, lse_refs, om_ref, x_ref, gm_ref, w_ref, o_scrs, l_scrs):
    gm = gm_ref[...]
    e0, e1 = NA_WIDTH, NA_WIDTH + DIL_WIDTH
    ya = _rms(oa_ref[...].astype(F32), gm[:, :e0]).astype(BF16)
    yb = _rms(_branch_mix(ob_refs, lse_refs, o_scrs, l_scrs), gm[:, e0:e1]).astype(BF16)
    ym = _rms(om_ref[...].astype(F32), gm[:, e1:]).astype(BF16)
    y = jnp.concatenate([ya, yb, ym], axis=-1)
    return x_ref[...] + jnp.dot(y, w_ref[...], preferred_element_type=F32)


def _split_scratch(scrs, dils):
    it = iter(scrs)
    o_scrs = [next(it) if dil > 1 else None for dil in dils]
    l_scrs = [next(it) if dil > 1 else None for dil in dils]
    return o_scrs, l_scrs


def _outproj_kernel(oa_ref, ob1, ob2, ob3, l1, l2, l3, om_ref, x_ref, gm_ref, w_ref, xo_ref, *scrs, dils):
    xo_ref[...] = _outproj_core(oa_ref, (ob1, ob2, ob3), (l1, l2, l3), om_ref, x_ref, gm_ref, w_ref,
                                *_split_scratch(scrs, dils))


def _store_row_tiles(ref, val):
    m, d = val.shape
    for j in range(d // LANES):
        ref[pl.ds(j, m, stride=d // LANES), :] = val[:, j * LANES:(j + 1) * LANES]


def _outproj_router_kernel(oa_ref, ob1, ob2, ob3, l1, l2, l3, om_ref, x_ref, gm_ref, w_ref,
                           gf_ref, wr_ref, xo_ref, h_ref, route_ref, *scrs, n_experts, dils):
    xn = _outproj_core(oa_ref, (ob1, ob2, ob3), (l1, l2, l3), om_ref, x_ref, gm_ref, w_ref,
                       *_split_scratch(scrs, dils))
    xo_ref[...] = xn
    h = _rms(xn, gf_ref[...])
    _store_row_tiles(h_ref, h)
    logits = jnp.dot(h.astype(BF16), wr_ref[...], preferred_element_type=F32)
    lane = lax.broadcasted_iota(jnp.int32, logits.shape, 1)
    lg = jnp.where(lane < n_experts, logits, NEG)
    m1 = jnp.max(lg, axis=-1, keepdims=True)
    i1 = jnp.min(jnp.where(lg == m1, lane, LANES), axis=-1, keepdims=True)
    lg2 = jnp.where(lane == i1, NEG, lg)
    m2 = jnp.max(lg2, axis=-1, keepdims=True)
    i2 = jnp.min(jnp.where(lg2 == m2, lane, LANES), axis=-1, keepdims=True)
    e2 = jnp.exp(m2 - m1)
    g1 = 1.0 / (1.0 + e2)
    g2 = e2 / (1.0 + e2)
    route = jnp.where(lane == 0, i1.astype(F32), 0.0)
    route = jnp.where(lane == 1, i2.astype(F32), route)
    route = jnp.where(lane == 2, g1, route)
    route = jnp.where(lane == 3, g2, route)
    route_ref[...] = route


def _outproj(oa, obs, lses, om, x2, gm, w, t, tm, router=None):
    n, d = x2.shape
    tpb = t // tm
    row = lambda width: pl.BlockSpec((tm, width), lambda i: (i, 0))
    full = lambda a: pl.BlockSpec(a.shape, lambda i: (0,) * a.ndim)
    grouped = lambda a: pl.BlockSpec((None, a.shape[1], tm // a.shape[1], a.shape[3]),
                                     lambda i: (i // tpb, 0, i % tpb, 0))
    dils = tuple(a.shape[1] for a in obs)
    args = [oa, *obs, *lses, om, x2, gm, w]
    in_specs = ([row(NA_WIDTH)] + [grouped(a) for a in obs] + [grouped(a) for a in lses]
                + [row(MEM_WIDTH), row(d), full(gm), full(w)])
    scratch = [pltpu.VMEM((DIL_WIDTH // LANES, tm, LANES), F32)] * (2 * sum(dil > 1 for dil in dils))
    if router is None:
        return pl.pallas_call(
            functools.partial(_outproj_kernel, dils=dils),
            grid=(n // tm,), in_specs=in_specs, out_specs=row(d),
            out_shape=jax.ShapeDtypeStruct((n, d), F32), scratch_shapes=scratch,
            compiler_params=_cparams(("parallel",)), name="outproj",
        )(*args)
    gf, wr, n_experts = router
    sub = d // LANES
    return pl.pallas_call(
        functools.partial(_outproj_router_kernel, n_experts=n_experts, dils=dils),
        grid=(n // tm,), in_specs=in_specs + [full(gf), full(wr)],
        out_specs=[row(d), pl.BlockSpec((tm * sub, LANES), lambda i: (i, 0)), row(LANES)],
        out_shape=[jax.ShapeDtypeStruct((n, d), F32), jax.ShapeDtypeStruct((n * sub, LANES), F32),
                   jax.ShapeDtypeStruct((n, LANES), F32)],
        scratch_shapes=scratch,
        compiler_params=_cparams(("parallel",)), name="outproj_router",
    )(*args, gf, wr)


MXU_COLS = 256


def _swiglu_partial(h, wg_ref, wu_ref, wd_ref, act_scr, emit):
    tf = wg_ref.shape[-1]
    chunk = math.gcd(MXU_COLS, tf)
    for c in range(tf // chunk):
        sl = slice(c * chunk, (c + 1) * chunk)
        gate = jnp.dot(h, wg_ref[:, sl], preferred_element_type=F32)
        up = jnp.dot(h, wu_ref[:, sl], preferred_element_type=F32)
        act_scr[:, sl] = ((gate / (1.0 + jnp.exp(-gate))) * up).astype(BF16)
    act = act_scr[...]
    for c in range(wd_ref.shape[-1] // MXU_COLS):
        emit(c, jnp.dot(act, wd_ref[:, c * MXU_COLS:(c + 1) * MXU_COLS], preferred_element_type=F32))


def _dense_ffn_kernel(x_ref, g_ref, wg_ref, wu_ref, wd_ref, o_ref, h_scr, act_scr):
    @pl.when(pl.program_id(1) == 0)
    def _():
        x = x_ref[...]
        h_scr[...] = _rms(x, g_ref[...]).astype(BF16)
        o_ref[...] = x

    def accumulate(c, part):
        o_ref[:, c * MXU_COLS:(c + 1) * MXU_COLS] += part

    _swiglu_partial(h_scr[...], wg_ref, wu_ref, wd_ref, act_scr, accumulate)


def _dense_ffn(x2, g, wg, wu, wd, tm, tf):
    n, d = x2.shape
    f = wg.shape[1]
    row = pl.BlockSpec((tm, d), lambda i, k: (i, 0))
    return pl.pallas_call(
        _dense_ffn_kernel,
        grid=(n // tm, f // tf),
        in_specs=[row, pl.BlockSpec((1, d), lambda i, k: (0, 0)),
                  pl.BlockSpec((d, tf), lambda i, k: (0, k)),
                  pl.BlockSpec((d, tf), lambda i, k: (0, k)),
                  pl.BlockSpec((tf, d), lambda i, k: (k, 0))],
        out_specs=row,
        out_shape=jax.ShapeDtypeStruct((n, d), F32),
        scratch_shapes=[pltpu.VMEM((tm, d), BF16), pltpu.VMEM((tm, tf), BF16)],
        compiler_params=_cparams(("parallel", "arbitrary")),
        name="dense_ffn",
    )(x2, g, wg, wu, wd)


def _moe_ffn_kernel(be_ref, nused_ref, xb_ref, wg_ref, wu_ref, wd_ref, y_ref, act_scr):
    j = pl.program_id(0)
    k = pl.program_id(1)
    used = j < nused_ref[0]
    mb = act_scr.shape[0]
    sub = xb_ref.shape[0] // mb
    per_tile = MXU_COLS // LANES

    @pl.when(k == 0)
    def _():
        y_ref[...] = jnp.zeros_like(y_ref)

    @pl.when(used)
    def _():
        h = jnp.concatenate([xb_ref[pl.ds(c, mb, stride=sub), :].astype(BF16) for c in range(sub)], axis=-1)

        def accumulate(c, part):
            for u in range(per_tile):
                y_ref[pl.ds(c * per_tile + u, mb, stride=sub), :] += part[:, u * LANES:(u + 1) * LANES]

        _swiglu_partial(h, wg_ref, wu_ref, wd_ref, act_scr, accumulate)


def _moe_ffn(buf, block_expert, n_used, wg, wu, wd, mb, tf):
    d, f = wg.shape[1], wg.shape[2]
    sub = d // LANES
    p = buf.shape[0] // sub
    nk = f // tf
    kk = lambda j, k, be, nu: jnp.where(j < nu[0], k, nk - 1)
    row = pl.BlockSpec((mb * sub, LANES), lambda j, k, be, nu: (j, 0))
    grid_spec = pltpu.PrefetchScalarGridSpec(
        num_scalar_prefetch=2,
        grid=(p // mb, nk),
        in_specs=[row,
                  pl.BlockSpec((None, d, tf), lambda j, k, be, nu: (be[j], 0, kk(j, k, be, nu))),
                  pl.BlockSpec((None, d, tf), lambda j, k, be, nu: (be[j], 0, kk(j, k, be, nu))),
                  pl.BlockSpec((None, tf, d), lambda j, k, be, nu: (be[j], kk(j, k, be, nu), 0))],
        out_specs=row,
        scratch_shapes=[pltpu.VMEM((mb, tf), BF16)],
    )
    return pl.pallas_call(
        _moe_ffn_kernel,
        grid_spec=grid_spec,
        out_shape=jax.ShapeDtypeStruct((p * sub, LANES), F32),
        compiler_params=_cparams(("parallel", "arbitrary")),
        name="moe_ffn",
    )(block_expert, n_used, buf, wg, wu, wd)


ISSUE_UNROLL = 8


def _issue_tile_copies(chunk, copy_fn):
    def body(trip, carry):
        for u in range(ISSUE_UNROLL):
            copy_fn(trip, u).start(priority=u % 2)
        return carry

    lax.fori_loop(0, chunk // ISSUE_UNROLL, body, 0)


def _tile(ref, idx, sub):
    return ref.at[pl.ds(pl.multiple_of(idx * sub, sub), sub)]


def _dispatch_kernel(slot_ref, h_ref, out_ref, zero_scr, sem, *, n_tok_steps, sub):
    i = pl.program_id(0)
    n_copies = slot_ref.shape[-1]
    dst = lambda trip, u: _tile(out_ref, slot_ref[0, 0, trip * ISSUE_UNROLL + u], sub)
    tok = lambda trip, u: trip * (ISSUE_UNROLL // TOP_K) + u // TOP_K

    @pl.when(i == 0)
    def _():
        zero_scr[...] = jnp.zeros_like(zero_scr)

    @pl.when(i < n_tok_steps)
    def _():
        _issue_tile_copies(n_copies, lambda trip, u: pltpu.make_async_copy(
            _tile(h_ref, tok(trip, u), sub), dst(trip, u), sem))

    @pl.when(i >= n_tok_steps)
    def _():
        _issue_tile_copies(n_copies, lambda trip, u: pltpu.make_async_copy(zero_scr, dst(trip, u), sem))

    for _ in range(TOP_K):
        pltpu.make_async_copy(h_ref, out_ref.at[pl.ds(0, h_ref.shape[0])], sem).wait()


def _dispatch_rows(h, slot, tm, sub):
    n = h.shape[0] // sub
    p = slot.shape[0]
    n_copies = tm * TOP_K
    assert p % n_copies == 0 and n % tm == 0 and n_copies % ISSUE_UNROLL == 0
    n_tok_steps = n // tm
    return pl.pallas_call(
        functools.partial(_dispatch_kernel, n_tok_steps=n_tok_steps, sub=sub),
        grid=(p // n_copies,),
        in_specs=[pl.BlockSpec((1, 1, n_copies), lambda i: (i, 0, 0), memory_space=pltpu.SMEM),
                  pl.BlockSpec((tm * sub, LANES), lambda i: (jnp.minimum(i, n_tok_steps - 1), 0))],
        out_specs=pl.BlockSpec(memory_space=pl.ANY),
        out_shape=jax.ShapeDtypeStruct((p * sub, LANES), h.dtype),
        scratch_shapes=[pltpu.VMEM((sub, LANES), h.dtype), pltpu.SemaphoreType.DMA(())],
        compiler_params=_cparams(("arbitrary",)),
        name="dispatch_rows",
    )(slot.reshape(p // n_copies, 1, n_copies), h)


def _combine_kernel(cur_ref, nxt_ref, x_ref, route_ref, y_ref, *rest, final_norm, sub):
    o_ref, ybuf, sem = rest[-3:]
    i = pl.program_id(0)
    tm = x_ref.shape[0]
    n_copies = tm * TOP_K
    par = i % 2

    def fetch(idx_ref, buf):
        def copy(trip, u):
            tok = trip * (ISSUE_UNROLL // TOP_K) + u // TOP_K
            return pltpu.make_async_copy(_tile(y_ref, idx_ref[0, 0, trip * ISSUE_UNROLL + u], sub),
                                         _tile(ybuf.at[buf, u % TOP_K], tok, sub), sem.at[buf])
        _issue_tile_copies(n_copies, copy)

    @pl.when(i == 0)
    def _():
        fetch(cur_ref, 0)

    @pl.when(i + 1 < pl.num_programs(0))
    def _():
        fetch(nxt_ref, 1 - par)

    pltpu.make_async_copy(ybuf.at[par], ybuf.at[par], sem.at[par]).wait()

    route = route_ref[...]
    gates = [route[:, TOP_K + k:TOP_K + k + 1] for k in range(TOP_K)]
    for j in range(sub):
        sl = slice(j * LANES, (j + 1) * LANES)
        mix = None
        for k in range(TOP_K):
            term = gates[k] * ybuf[par, k, pl.ds(j, tm, stride=sub), :]
            mix = term if mix is None else mix + term
        o_ref[:, sl] = x_ref[:, sl] + mix
    if final_norm:
        o_ref[...] = _rms(o_ref[...], rest[0][...])


def _combine(x2, y, dest, route, tm, g_final=None):
    n, d = x2.shape
    sub = d // LANES
    n_copies = tm * TOP_K
    steps = n // tm
    assert n_copies % ISSUE_UNROLL == 0 and ISSUE_UNROLL % TOP_K == 0
    dest3 = dest.reshape(steps, 1, n_copies)
    row = pl.BlockSpec((tm, d), lambda i: (i, 0))
    in_specs = [pl.BlockSpec((1, 1, n_copies), lambda i: (i, 0, 0), memory_space=pltpu.SMEM),
                pl.BlockSpec((1, 1, n_copies), lambda i: (jnp.minimum(i + 1, steps - 1), 0, 0),
                             memory_space=pltpu.SMEM),
                row, pl.BlockSpec((tm, LANES), lambda i: (i, 0)), pl.BlockSpec(memory_space=pl.ANY)]
    args = [dest3, dest3, x2, route, y]
    if g_final is not None:
        in_specs.append(pl.BlockSpec((1, d), lambda i: (0, 0)))
        args.append(g_final)
    return pl.pallas_call(
        functools.partial(_combine_kernel, final_norm=g_final is not None, sub=sub),
        grid=(steps,), in_specs=in_specs, out_specs=row,
        out_shape=jax.ShapeDtypeStruct((n, d), F32),
        scratch_shapes=[pltpu.VMEM((2, TOP_K, tm * sub, LANES), F32), pltpu.SemaphoreType.DMA((2,))],
        compiler_params=_cparams(("arbitrary",)), name="moe_combine",
    )(*args)


def _final_norm_kernel(x_ref, g_ref, o_ref):
    o_ref[...] = _rms(x_ref[...], g_ref[...])


def _final_norm(x2, g, tm):
    n, d = x2.shape
    row = pl.BlockSpec((tm, d), lambda i: (i, 0))
    return pl.pallas_call(
        _final_norm_kernel, grid=(n // tm,), in_specs=[row, pl.BlockSpec((1, d), lambda i: (0, 0))],
        out_specs=row, out_shape=jax.ShapeDtypeStruct((n, d), F32),
        compiler_params=_cparams(("parallel",)), name="final_norm",
    )(x2, g)


def _routing_plan(route, n_experts, mb):
    n = route.shape[0]
    n_assign = n * TOP_K
    flat_e = route[:, :TOP_K].astype(jnp.int32).reshape(-1)
    onehot = (flat_e[:, None] == jnp.arange(n_experts)[None, :]).astype(jnp.int32)
    csum = jnp.cumsum(onehot, axis=0)
    rank = jnp.sum(csum * onehot, axis=1) - 1
    counts = csum[-1]
    padded = (counts + mb - 1) // mb * mb
    pend = jnp.cumsum(padded)
    pstart = pend - padded
    dest = jnp.sum(pstart[None, :] * onehot, axis=1) + rank
    assert n_assign % mb == 0
    n_blocks = n_assign // mb + n_experts
    block_first = jnp.arange(n_blocks) * mb
    block_expert = jnp.minimum(jnp.sum(block_first[:, None] >= pend[None, :], axis=1), n_experts - 1)
    n_used = (pend[-1] // mb).reshape(1)
    pads = padded - counts
    cpad = jnp.cumsum(pads)
    i = jnp.arange(n_experts * mb)
    grp = jnp.sum(i[:, None] >= cpad[None, :], axis=1)
    grp_hot = (grp[:, None] == jnp.arange(n_experts + 1)[None, :]).astype(jnp.int32)
    first_free = jnp.concatenate([pstart + counts, pend[-1:]])
    before = jnp.concatenate([jnp.zeros((1,), cpad.dtype), cpad])
    free = jnp.sum(grp_hot * (first_free - before)[None, :], axis=1) + i
    slot = jnp.concatenate([dest, free]).astype(jnp.int32)
    return dest.astype(jnp.int32), slot, block_expert.astype(jnp.int32), n_used.astype(jnp.int32)


def _tiles(n, t):
    return dict(tm_in=math.gcd(1024, t), tm_proj=math.gcd(512, t), tm_ffn=math.gcd(1024, n), tf=1792,
                tq_dil=1024, tq_mem=math.gcd(512, t), moe_block=math.gcd(1024, n))


@jax.jit
def _forward(x, mem, g_mix_norm, w_in, rpb, g_mem_norm, w_mem_kv, g_mix_out, w_out,
             g_ffn_norm, w_dense_gate, w_dense_up, w_dense_down, w_router,
             w_moe_gate, w_moe_up, w_moe_down, g_final):
    b, t, d = x.shape
    n = b * t
    depth = w_in.shape[0]
    n_experts = w_router.shape[-1]
    ts = _tiles(n, t)
    tf = math.gcd(ts["tf"], w_dense_gate.shape[-1])
    tables = _rope_tables(t)
    row1 = lambda a: a.reshape(1, -1)
    x2 = x.reshape(n, d)
    out = None
    sub = d // LANES
    for layer in range(depth):
        qa, ka, va, qm, qbs, kbs, vbs = _inproj(x2, row1(g_mix_norm[layer]), w_in[layer].astype(BF16),
                                                tables, t, ts["tm_in"])
        shp = lambda a: a.reshape(b, t, a.shape[-1])
        oa = _na_attention(shp(qa), shp(ka), shp(va), _na_bias_table(rpb[layer])).reshape(n, NA_WIDTH)
        obs, lses = [], []
        for (window, dil), qb, kb, vb in zip(DIL_BRANCHES, qbs, kbs, vbs):
            o, lse = _dilated_branch(qb, kb, vb, window // (2 * dil), ts["tq_dil"])
            obs.append(o)
            lses.append(lse)
        km, vm = _mem_kv(mem, row1(g_mem_norm[layer]), w_mem_kv[layer].astype(BF16))
        om = _mem_attention(shp(qm), km, vm, ts["tq_mem"]).reshape(n, MEM_WIDTH)
        gm, wo = row1(g_mix_out[layer]), w_out[layer].astype(BF16)
        i = layer // 2
        if layer % 2 == 0:
            x2 = _outproj(oa, obs, lses, om, x2, gm, wo, t, ts["tm_proj"])
            x2 = _dense_ffn(x2, row1(g_ffn_norm[layer]), w_dense_gate[i].astype(BF16),
                            w_dense_up[i].astype(BF16), w_dense_down[i].astype(BF16), ts["tm_ffn"], tf)
            if layer == depth - 1:
                out = _final_norm(x2, row1(g_final), ts["tm_proj"])
        else:
            wr = jnp.pad(w_router[i], ((0, 0), (0, LANES - n_experts))).astype(BF16)
            x2, h, route = _outproj(oa, obs, lses, om, x2, gm, wo, t, ts["tm_proj"],
                                    router=(row1(g_ffn_norm[layer]), wr, n_experts))
            mb = ts["moe_block"]
            dest, slot, block_expert, n_used = _routing_plan(route, n_experts, mb)
            buf = _dispatch_rows(h, slot, ts["tm_proj"], sub)
            y = _moe_ffn(buf, block_expert, n_used, w_moe_gate[i].astype(BF16), w_moe_up[i].astype(BF16),
                         w_moe_down[i].astype(BF16), mb, tf)
            if layer == depth - 1:
                out = _combine(x2, y, dest, route, ts["tm_proj"], g_final=row1(g_final))
            else:
                x2 = _combine(x2, y, dest, route, ts["tm_proj"])
    return out.reshape(b, t, d)


def kernel(x, mem, g_mix_norm, w_in, rpb, g_mem_norm, w_mem_kv, g_mix_out, w_out, g_ffn_norm,
           w_dense_gate, w_dense_up, w_dense_down, w_router, w_moe_gate, w_moe_up, w_moe_down, g_final):
    return _forward(x, mem, g_mix_norm, w_in, rpb, g_mem_norm, w_mem_kv, g_mix_out, w_out, g_ffn_norm,
                    w_dense_gate, w_dense_up, w_dense_down, w_router, w_moe_gate, w_moe_up, w_moe_down,
                    g_final)
```

```python
import functools
import math

import jax
import jax.numpy as jnp
from jax import lax
from jax.experimental import pallas as pl
from jax.experimental.pallas import tpu as pltpu

HEAD_DIM = 64
NA_HEADS = 6
DIL_HEADS = 6
MEM_HEADS = 4
NA_WIDTH = NA_HEADS * HEAD_DIM
DIL_WIDTH = DIL_HEADS * HEAD_DIM
MEM_WIDTH = MEM_HEADS * HEAD_DIM
GRID_W = 64
NA_WIN_ROWS = 8
NA_WIN_COLS = 16
DIL_BRANCHES = ((128, 1), (512, 4), (2048, 16))
ROPE_THETA = 500000.0
ROT_DIM = HEAD_DIM // 4
TOP_K = 2
RMS_EPS = 1e-6
ATTN_SCALE = HEAD_DIM ** -0.5
LOG2E = math.log2(math.e)
Q_SCALE = ATTN_SCALE * LOG2E

LANES = 128
HEADS_PER_LANE_GROUP = LANES // HEAD_DIM
NEG = -1e30
ROWS_PER_TRIP = 8
ROW_TILE_SUBLANES = 8
VMEM_LIMIT = 56 * 1024 * 1024

F32 = jnp.float32
BF16 = jnp.bfloat16


def _cparams(sem):
    return pltpu.CompilerParams(dimension_semantics=sem, vmem_limit_bytes=VMEM_LIMIT)


def _rms(x, g):
    ms = jnp.mean(x * x, axis=-1, keepdims=True)
    return x * lax.rsqrt(ms + RMS_EPS) * g


def _pair_heads_attention(q, k, v, bias_fn, n_heads):
    results = []
    _attention_pipelined([(lambda: (q, k, v, bias_fn), lambda o, lses: results.append((o, lses)))], n_heads)
    return results[0]


def _lane_group(a, p):
    return a[:, p * LANES:(p + 1) * LANES]


def _attn_scores(q, k, bias_fn, n_heads):
    m = q.shape[0]
    assert HEADS_PER_LANE_GROUP == 2
    lo = lax.broadcasted_iota(jnp.int32, (m, LANES), 1) < HEAD_DIM
    s_parts = []
    for p in range(n_heads // HEADS_PER_LANE_GROUP):
        qp = _lane_group(q, p)
        zero = jnp.zeros_like(qp)
        q2 = jnp.concatenate([jnp.where(lo, qp, zero), jnp.where(lo, zero, qp)], axis=0)
        s_parts.append(lax.dot_general(q2, _lane_group(k, p), (((1,), (1,)), ((), ())),
                                       preferred_element_type=F32) + bias_fn(p))
    return jnp.concatenate(s_parts, axis=0)


def _attn_softmax(s):
    mx = jnp.max(s, axis=-1, keepdims=True)
    e = jnp.exp2(s - mx)
    den = jnp.sum(e, axis=-1, keepdims=True)
    return e.astype(BF16), den, mx


def _attn_values(eb, den, mx, v, n_heads):
    m = eb.shape[0] // n_heads
    lo = lax.broadcasted_iota(jnp.int32, (m, LANES), 1) < HEAD_DIM
    outs = []
    for p in range(n_heads // HEADS_PER_LANE_GROUP):
        rows = slice(p * 2 * m, (p + 1) * 2 * m)
        o2 = jnp.dot(eb[rows], _lane_group(v, p), preferred_element_type=F32) / den[rows]
        outs.append(jnp.where(lo, o2[:m], o2[m:]))
    lse = mx + jnp.log2(den)
    return jnp.concatenate(outs, axis=-1), [lse[h * m:(h + 1) * m] for h in range(n_heads)]


def _attention_pipelined(items, n_heads, skew=1):
    n = len(items)
    loaded, scores, soft = {}, {}, {}
    for step in range(n + 2 * skew):
        if step < n:
            loaded[step] = items[step][0]()
            q, k, _, bias_fn = loaded[step]
            scores[step] = _attn_scores(q, k, bias_fn, n_heads)
        if 0 <= step - skew < n:
            soft[step - skew] = _attn_softmax(scores.pop(step - skew))
        if 0 <= step - 2 * skew < n:
            eb, den, mx = soft.pop(step - 2 * skew)
            items[step - 2 * skew][1](*_attn_values(eb, den, mx, loaded.pop(step - 2 * skew)[2], n_heads))


def _inproj_kernel(x_ref, g_ref, w_ref, cos_ref, sa_ref, sb_ref,
                   qa_ref, ka_ref, va_ref, qm_ref, *rest, dils):
    dil_refs, scr = rest[:-1], rest[-1]
    tm = x_ref.shape[0]
    h = _rms(x_ref[...], g_ref[...]).astype(BF16)

    def emit_dilated(val, refs):
        groups = val.shape[1] // LANES
        for g in range(groups):
            scr[g] = val[:, g * LANES:(g + 1) * LANES]
        for dil, ref in zip(dils, refs):
            if dil == 1:
                ref[0] = val.astype(BF16)
                continue
            for r in range(dil):
                for g in range(groups):
                    ref[r, :, g * LANES:(g + 1) * LANES] = (
                        scr[g, pl.ds(r, tm // dil, stride=dil), :].astype(BF16))

    def proj_pair(c0, width_a, width_b):
        both = jnp.dot(h, w_ref[:, c0:c0 + width_a + width_b], preferred_element_type=F32)
        return both[:, :width_a], both[:, width_a:]

    def rope(a):
        cos, sa, sb = cos_ref[...], sa_ref[...], sb_ref[...]
        half = ROT_DIM // 2
        parts = []
        for gidx in range(a.shape[1] // LANES):
            xg = a[:, gidx * LANES:(gidx + 1) * LANES]
            parts.append(xg * cos + pltpu.roll(xg, LANES - half, 1) * sa + pltpu.roll(xg, half, 1) * sb)
        return jnp.concatenate(parts, axis=-1)

    nd = len(dils)
    assert (2 * NA_WIDTH) % 256 == 0 and (NA_WIDTH + DIL_WIDTH) % 256 == 0 and (2 * DIL_WIDTH) % 256 == 0
    kb, vb = proj_pair(3 * NA_WIDTH + DIL_WIDTH, DIL_WIDTH, DIL_WIDTH)
    emit_dilated(rope(kb), dil_refs[nd:2 * nd])
    emit_dilated(vb, dil_refs[2 * nd:3 * nd])
    va, qb = proj_pair(2 * NA_WIDTH, NA_WIDTH, DIL_WIDTH)
    va_ref[...] = va.astype(BF16)
    emit_dilated(rope(qb * Q_SCALE), dil_refs[0:nd])
    qa, ka = proj_pair(0, NA_WIDTH, NA_WIDTH)
    qa_ref[...] = (qa * Q_SCALE).astype(BF16)
    ka_ref[...] = ka.astype(BF16)
    c = 3 * NA_WIDTH + 3 * DIL_WIDTH
    qm_ref[...] = (jnp.dot(h, w_ref[:, c:c + MEM_WIDTH], preferred_element_type=F32) * Q_SCALE).astype(BF16)


def _rope_tables(t):
    half = ROT_DIM // 2
    inv_freq = ROPE_THETA ** (-jnp.arange(0, ROT_DIM, 2, dtype=F32) / ROT_DIM)
    ang = jnp.arange(t, dtype=F32)[:, None] * inv_freq[None, :]
    cos, sin = jnp.cos(ang), jnp.sin(ang)
    ones = jnp.ones((t, HEAD_DIM - ROT_DIM), F32)
    zeros = jnp.zeros((t, HEAD_DIM - ROT_DIM), F32)
    zh = jnp.zeros((t, half), F32)
    cos_h = jnp.concatenate([cos, cos, ones], axis=1)
    sa_h = jnp.concatenate([-sin, zh, zeros], axis=1)
    sb_h = jnp.concatenate([zh, sin, zeros], axis=1)
    tile = lambda a: jnp.tile(a, (1, HEADS_PER_LANE_GROUP))
    return tile(cos_h), tile(sa_h), tile(sb_h)


def _inproj(x2, g, w, tables, t, tm):
    n, d = x2.shape
    b = n // t
    tpb = t // tm
    dils = tuple(dil for _, dil in DIL_BRANCHES)
    assert all(tm % (dil * 16) == 0 for dil in dils)
    row = lambda width: pl.BlockSpec((tm, width), lambda i: (i, 0))
    tab = pl.BlockSpec((tm, LANES), lambda i: (i % tpb, 0))
    plain = [NA_WIDTH] * 3 + [MEM_WIDTH]
    dil_specs = [pl.BlockSpec((None, dil, tm // dil, DIL_WIDTH), lambda i: (i // tpb, 0, i % tpb, 0))
                 for dil in dils] * 3
    dil_shapes = [jax.ShapeDtypeStruct((b, dil, t // dil, DIL_WIDTH), BF16) for dil in dils] * 3
    outs = pl.pallas_call(
        functools.partial(_inproj_kernel, dils=dils),
        grid=(n // tm,),
        in_specs=[row(d), pl.BlockSpec((1, d), lambda i: (0, 0)),
                  pl.BlockSpec(w.shape, lambda i: (0, 0)), tab, tab, tab],
        out_specs=[row(wd) for wd in plain] + dil_specs,
        out_shape=[jax.ShapeDtypeStruct((n, wd), BF16) for wd in plain] + dil_shapes,
        scratch_shapes=[pltpu.VMEM((DIL_WIDTH // LANES, tm, LANES), F32)],
        compiler_params=_cparams(("parallel",)),
        name="inproj",
    )(x2, g, w, *tables)
    nd = len(dils)
    qa, ka, va, qm = outs[:4]
    return qa, ka, va, qm, outs[4:4 + nd], outs[4 + nd:4 + 2 * nd], outs[4 + 2 * nd:]


def _na_bias_table(rpb):
    c = jnp.arange(GRID_W)
    c0 = jnp.clip(c - NA_WIN_COLS // 2, 0, GRID_W - NA_WIN_COLS)
    kc = jnp.arange(GRID_W)
    valid = (kc[None, :] >= c0[:, None]) & (kc[None, :] < c0[:, None] + NA_WIN_COLS)
    coff = kc[None, :] - c[:, None] + (NA_WIN_COLS - 1)
    onehot = (coff[None] == jnp.arange(2 * NA_WIN_COLS - 1)[:, None, None]).astype(F32)
    by_col = jnp.einsum("hrd,dck->hrck", rpb.astype(F32), onehot, precision=lax.Precision.HIGHEST)
    by_col = jnp.where(valid[None, None], by_col * LOG2E, NEG)
    tab = jnp.stack([by_col[:, d:d + NA_WIN_ROWS] for d in range(NA_WIN_ROWS)], axis=1)
    tab = tab.transpose(1, 0, 3, 2, 4)
    return tab.reshape(NA_WIN_ROWS * NA_HEADS // HEADS_PER_LANE_GROUP, HEADS_PER_LANE_GROUP * GRID_W,
                       NA_WIN_ROWS * GRID_W)


def _na_kernel(q_ref, kp_ref, kc_ref, kn_ref, vp_ref, vc_ref, vn_ref, bias_ref, o_ref,
               kwin, vwin, *, rows):
    j = pl.program_id(1)
    blk = NA_WIN_ROWS * GRID_W
    for idx, (kr, vr) in enumerate(((kp_ref, vp_ref), (kc_ref, vc_ref), (kn_ref, vn_ref))):
        kwin[idx * blk:(idx + 1) * blk, :] = kr[...]
        vwin[idx * blk:(idx + 1) * blk, :] = vr[...]

    n_pairs = NA_HEADS // HEADS_PER_LANE_GROUP

    def one_row(i):
        q_rows = pl.ds(pl.multiple_of(i * GRID_W, GRID_W), GRID_W)

        def load():
            r = j * NA_WIN_ROWS + i
            r0 = jnp.clip(r - NA_WIN_ROWS // 2, 0, rows - NA_WIN_ROWS)
            dlt = r0 - r + (NA_WIN_ROWS - 1)
            start = pl.multiple_of((r0 - (j - 1) * NA_WIN_ROWS) * GRID_W, GRID_W)
            return (q_ref[q_rows, :], kwin[pl.ds(start, blk), :], vwin[pl.ds(start, blk), :],
                    lambda p: bias_ref[dlt * n_pairs + p])

        def store(o, lses):
            o_ref[q_rows, :] = o.astype(BF16)

        return load, store

    def row_group(ig, carry):
        _attention_pipelined([one_row(ig * ROWS_PER_TRIP + u) for u in range(ROWS_PER_TRIP)], NA_HEADS)
        return carry

    lax.fori_loop(0, NA_WIN_ROWS // ROWS_PER_TRIP, row_group, 0)


def _na_attention(q, k, v, bias):
    b, t, w = q.shape
    rows = t // GRID_W
    assert rows % NA_WIN_ROWS == 0 and rows >= NA_WIN_ROWS
    nj = rows // NA_WIN_ROWS
    blk = NA_WIN_ROWS * GRID_W
    cur = pl.BlockSpec((None, blk, w), lambda bi, j: (bi, j, 0))
    prev = pl.BlockSpec((None, blk, w), lambda bi, j: (bi, jnp.maximum(j - 1, 0), 0))
    nxt = pl.BlockSpec((None, blk, w), lambda bi, j: (bi, jnp.minimum(j + 1, nj - 1), 0))
    return pl.pallas_call(
        functools.partial(_na_kernel, rows=rows),
        grid=(b, nj),
        in_specs=[cur, prev, cur, nxt, prev, cur, nxt,
                  pl.BlockSpec(bias.shape, lambda bi, j: (0, 0, 0))],
        out_specs=cur,
        out_shape=jax.ShapeDtypeStruct((b, t, w), BF16),
        scratch_shapes=[pltpu.VMEM((3 * blk, w), BF16), pltpu.VMEM((3 * blk, w), BF16)],
        compiler_params=_cparams(("parallel", "parallel")),
        name="na_attn",
    )(q, k, k, k, v, v, v, bias)


def _dil_kernel(q_ref, kp_ref, kc_ref, kn_ref, vp_ref, vc_ref, vn_ref, o_ref, lse_ref,
                kwin, vwin, *, seg, tq, halo):
    i = pl.program_id(2)
    width = tq + 2 * halo
    kwin[0:halo, :] = kp_ref[...]
    kwin[halo:halo + tq, :] = kc_ref[...]
    kwin[halo + tq:width, :] = kn_ref[...]
    vwin[0:halo, :] = vp_ref[...]
    vwin[halo:halo + tq, :] = vc_ref[...]
    vwin[halo + tq:width, :] = vn_ref[...]

    sq = math.gcd(tq, 2 * halo)
    sw = sq + 2 * halo
    a = lax.broadcasted_iota(jnp.int32, (sq, sw), 0)
    c = lax.broadcasted_iota(jnp.int32, (sq, sw), 1)
    band = (c >= a) & (c <= a + 2 * halo)
    lo = lax.broadcasted_iota(jnp.int32, (sq, LANES), 1) < HEAD_DIM

    def sub_block(s):
        row0 = pl.multiple_of(s * sq, sq)

        def load():
            first_key = i * tq + s * sq - halo
            valid = band & (c >= -first_key) & (c < seg - first_key)
            negb = jnp.where(valid, 0.0, NEG).astype(F32)
            negb2 = jnp.concatenate([negb] * HEADS_PER_LANE_GROUP, axis=0)
            return (q_ref[pl.ds(row0, sq), :], kwin[pl.ds(row0, sw), :], vwin[pl.ds(row0, sw), :],
                    lambda p: negb2)

        def store(o, lses):
            o_ref[pl.ds(row0, sq), :] = o.astype(BF16)
            lse_ref[pl.ds(row0, sq), :] = jnp.concatenate(
                [jnp.where(lo, lses[2 * p], lses[2 * p + 1]) for p in range(DIL_HEADS // 2)], axis=-1)

        return load, store

    n_sub = tq // sq
    per_trip = math.gcd(n_sub, ROWS_PER_TRIP)

    def sub_block_group(sg, carry):
        _attention_pipelined([sub_block(sg * per_trip + u) for u in range(per_trip)], DIL_HEADS, skew=0)
        return carry

    lax.fori_loop(0, n_sub // per_trip, sub_block_group, 0)


def _dilated_branch(q, k, v, n_side, tq_max):
    b, dil, seg, w = q.shape
    halo = n_side
    assert seg % halo == 0 and halo % 16 == 0
    tq = math.gcd(tq_max, seg)
    assert tq % halo == 0
    hb = tq // halo
    nhalo = seg // halo
    cur = pl.BlockSpec((None, None, tq, w), lambda bi, r, i: (bi, r, i, 0))
    prev = pl.BlockSpec((None, None, halo, w), lambda bi, r, i: (bi, r, jnp.maximum(i * hb - 1, 0), 0))
    nxt = pl.BlockSpec((None, None, halo, w),
                       lambda bi, r, i: (bi, r, jnp.minimum((i + 1) * hb, nhalo - 1), 0))
    return pl.pallas_call(
        functools.partial(_dil_kernel, seg=seg, tq=tq, halo=halo),
        grid=(b, dil, seg // tq),
        in_specs=[cur, prev, cur, nxt, prev, cur, nxt],
        out_specs=[cur, cur],
        out_shape=[jax.ShapeDtypeStruct((b, dil, seg, w), BF16),
                   jax.ShapeDtypeStruct((b, dil, seg, w), F32)],
        scratch_shapes=[pltpu.VMEM((tq + 2 * halo, w), BF16), pltpu.VMEM((tq + 2 * halo, w), BF16)],
        compiler_params=_cparams(("parallel", "parallel", "parallel")),
        name=f"dilated_d{dil}",
    )(q, k, k, k, v, v, v)


def _memkv_kernel(mem_ref, g_ref, w_ref, k_ref, v_ref):
    h = _rms(mem_ref[...], g_ref[...]).astype(BF16)
    kv = jnp.dot(h, w_ref[...], preferred_element_type=F32)
    k_ref[...] = kv[:, :MEM_WIDTH].astype(BF16)
    v_ref[...] = kv[:, MEM_WIDTH:].astype(BF16)


def _mem_kv(mem, g, w):
    b, m, d = mem.shape
    blk = lambda width: pl.BlockSpec((None, m, width), lambda bi: (bi, 0, 0))
    return pl.pallas_call(
        _memkv_kernel,
        grid=(b,),
        in_specs=[blk(d), pl.BlockSpec((1, d), lambda bi: (0, 0)), pl.BlockSpec(w.shape, lambda bi: (0, 0))],
        out_specs=[blk(MEM_WIDTH), blk(MEM_WIDTH)],
        out_shape=[jax.ShapeDtypeStruct((b, m, MEM_WIDTH), BF16)] * 2,
        compiler_params=_cparams(("parallel",)),
        name="mem_kv",
    )(mem, g, w)


def _memattn_kernel(q_ref, k_ref, v_ref, o_ref):
    tq = q_ref.shape[0]
    sq = math.gcd(tq, LANES)

    def sub_block(s):
        rows = slice(s * sq, (s + 1) * sq)

        def store(o, lses):
            o_ref[rows, :] = o.astype(BF16)

        return (lambda: (q_ref[rows, :], k_ref[...], v_ref[...], lambda p: 0.0)), store

    _attention_pipelined([sub_block(s) for s in range(tq // sq)], MEM_HEADS)


def _mem_attention(q, k, v, tq):
    b, t, w = q.shape
    m = k.shape[1]
    qs = pl.BlockSpec((None, tq, w), lambda bi, i: (bi, i, 0))
    ks = pl.BlockSpec((None, m, w), lambda bi, i: (bi, 0, 0))
    return pl.pallas_call(
        _memattn_kernel,
        grid=(b, t // tq),
        in_specs=[qs, ks, ks],
        out_specs=qs,
        out_shape=jax.ShapeDtypeStruct((b, t, w), BF16),
        compiler_params=_cparams(("parallel", "parallel")),
        name="mem_attn",
    )(q, k, v)


def _token_order(ref, scr):
    dil = ref.shape[0]
    if dil == 1:
        return ref[0].astype(F32)
    groups, tm, _ = scr.shape
    for r in range(dil):
        for g in range(groups):
            scr[g, pl.ds(r, tm // dil, stride=dil), :] = ref[r, :, g * LANES:(g + 1) * LANES].astype(F32)
    return jnp.concatenate([scr[g] for g in range(groups)], axis=-1)


def _branch_mix(ob_refs, lse_refs, o_scrs, l_scrs):
    lses = [_token_order(r, s) for r, s in zip(lse_refs, l_scrs)]
    mx = functools.reduce(jnp.maximum, lses)
    es = [jnp.exp2(l - mx) for l in lses]
    den = functools.reduce(lambda p, q: p + q, es)
    acc = None
    for e, ob, scr in zip(es, ob_refs, o_scrs):
        term = (e / den) * _token_order(ob, scr)
        acc = term if acc is None else acc + term
    return acc


def _outproj_core(oa_ref, ob_refs, lse_refs, om_ref, x_ref, gm_ref, w_ref, o_scrs, l_scrs):
    gm = gm_ref[...]
    e0, e1 = NA_WIDTH, NA_WIDTH + DIL_WIDTH
    ya = _rms(oa_ref[...].astype(F32), gm[:, :e0]).astype(BF16)
    yb = _rms(_branch_mix(ob_refs, lse_refs, o_scrs, l_scrs), gm[:, e0:e1]).astype(BF16)
    ym = _rms(om_ref[...].astype(F32), gm[:, e1:]).astype(BF16)
    y = jnp.concatenate([ya, yb, ym], axis=-1)
    return x_ref[...] + jnp.dot(y, w_ref[...], preferred_element_type=F32)


def _split_scratch(scrs, dils):
    it = iter(scrs)
    o_scrs = [next(it) if dil > 1 else None for dil in dils]
    l_scrs = [next(it) if dil > 1 else None for dil in dils]
    return o_scrs, l_scrs


def _outproj_kernel(oa_ref, ob1, ob2, ob3, l1, l2, l3, om_ref, x_ref, gm_ref, w_ref, xo_ref, *scrs, dils):
    xo_ref[...] = _outproj_core(oa_ref, (ob1, ob2, ob3), (l1, l2, l3), om_ref, x_ref, gm_ref, w_ref,
                                *_split_scratch(scrs, dils))


def _store_row_tiles(ref, val):
    m, d = val.shape
    sub = ref.shape[0] // m
    tw = d // sub
    for j in range(sub):
        ref[pl.ds(j, m, stride=sub), :] = val[:, j * tw:(j + 1) * tw]


def _outproj_router_kernel(oa_ref, ob1, ob2, ob3, l1, l2, l3, om_ref, x_ref, gm_ref, w_ref,
                           gf_ref, wr_ref, xo_ref, h_ref, route_ref, *scrs, n_experts, dils):
    xn = _outproj_core(oa_ref, (ob1, ob2, ob3), (l1, l2, l3), om_ref, x_ref, gm_ref, w_ref,
                       *_split_scratch(scrs, dils))
    xo_ref[...] = xn
    h = _rms(xn, gf_ref[...])
    _store_row_tiles(h_ref, h)
    logits = jnp.dot(h.astype(BF16), wr_ref[...], preferred_element_type=F32)
    lane = lax.broadcasted_iota(jnp.int32, logits.shape, 1)
    lg = jnp.where(lane < n_experts, logits, NEG)
    m1 = jnp.max(lg, axis=-1, keepdims=True)
    i1 = jnp.min(jnp.where(lg == m1, lane, LANES), axis=-1, keepdims=True)
    lg2 = jnp.where(lane == i1, NEG, lg)
    m2 = jnp.max(lg2, axis=-1, keepdims=True)
    i2 = jnp.min(jnp.where(lg2 == m2, lane, LANES), axis=-1, keepdims=True)
    e2 = jnp.exp(m2 - m1)
    g1 = 1.0 / (1.0 + e2)
    g2 = e2 / (1.0 + e2)
    route = jnp.where(lane == 0, i1.astype(F32), 0.0)
    route = jnp.where(lane == 1, i2.astype(F32), route)
    route = jnp.where(lane == 2, g1, route)
    route = jnp.where(lane == 3, g2, route)
    route_ref[...] = route


def _outproj(oa, obs, lses, om, x2, gm, w, t, tm, router=None):
    n, d = x2.shape
    tpb = t // tm
    row = lambda width: pl.BlockSpec((tm, width), lambda i: (i, 0))
    full = lambda a: pl.BlockSpec(a.shape, lambda i: (0,) * a.ndim)
    grouped = lambda a: pl.BlockSpec((None, a.shape[1], tm // a.shape[1], a.shape[3]),
                                     lambda i: (i // tpb, 0, i % tpb, 0))
    dils = tuple(a.shape[1] for a in obs)
    args = [oa, *obs, *lses, om, x2, gm, w]
    in_specs = ([row(NA_WIDTH)] + [grouped(a) for a in obs] + [grouped(a) for a in lses]
                + [row(MEM_WIDTH), row(d), full(gm), full(w)])
    scratch = [pltpu.VMEM((DIL_WIDTH // LANES, tm, LANES), F32)] * (2 * sum(dil > 1 for dil in dils))
    if router is None:
        return pl.pallas_call(
            functools.partial(_outproj_kernel, dils=dils),
            grid=(n // tm,), in_specs=in_specs, out_specs=row(d),
            out_shape=jax.ShapeDtypeStruct((n, d), F32), scratch_shapes=scratch,
            compiler_params=_cparams(("parallel",)), name="outproj",
        )(*args)
    gf, wr, n_experts = router
    sub = ROW_TILE_SUBLANES
    return pl.pallas_call(
        functools.partial(_outproj_router_kernel, n_experts=n_experts, dils=dils),
        grid=(n // tm,), in_specs=in_specs + [full(gf), full(wr)],
        out_specs=[row(d), pl.BlockSpec((tm * sub, d // sub), lambda i: (i, 0)), row(LANES)],
        out_shape=[jax.ShapeDtypeStruct((n, d), F32), jax.ShapeDtypeStruct((n * sub, d // sub), F32),
                   jax.ShapeDtypeStruct((n, LANES), F32)],
        scratch_shapes=scratch,
        compiler_params=_cparams(("parallel",)), name="outproj_router",
    )(*args, gf, wr)


MXU_COLS = 256


def _swiglu_partial(h, wg_ref, wu_ref, wd_ref, act_scr, emit):
    tf = wg_ref.shape[-1]
    chunk = math.gcd(MXU_COLS, tf)
    for c in range(tf // chunk):
        sl = slice(c * chunk, (c + 1) * chunk)
        gate = jnp.dot(h, wg_ref[:, sl], preferred_element_type=F32)
        up = jnp.dot(h, wu_ref[:, sl], preferred_element_type=F32)
        act_scr[:, sl] = ((gate / (1.0 + jnp.exp(-gate))) * up).astype(BF16)
    act = act_scr[...]
    for c in range(wd_ref.shape[-1] // MXU_COLS):
        emit(c, jnp.dot(act, wd_ref[:, c * MXU_COLS:(c + 1) * MXU_COLS], preferred_element_type=F32))


def _dense_ffn_kernel(x_ref, g_ref, wg_ref, wu_ref, wd_ref, o_ref, h_scr, act_scr):
    @pl.when(pl.program_id(1) == 0)
    def _():
        x = x_ref[...]
        h_scr[...] = _rms(x, g_ref[...]).astype(BF16)
        o_ref[...] = x

    def accumulate(c, part):
        o_ref[:, c * MXU_COLS:(c + 1) * MXU_COLS] += part

    _swiglu_partial(h_scr[...], wg_ref, wu_ref, wd_ref, act_scr, accumulate)


def _dense_ffn(x2, g, wg, wu, wd, tm, tf):
    n, d = x2.shape
    f = wg.shape[1]
    row = pl.BlockSpec((tm, d), lambda i, k: (i, 0))
    return pl.pallas_call(
        _dense_ffn_kernel,
        grid=(n // tm, f // tf),
        in_specs=[row, pl.BlockSpec((1, d), lambda i, k: (0, 0)),
                  pl.BlockSpec((d, tf), lambda i, k: (0, k)),
                  pl.BlockSpec((d, tf), lambda i, k: (0, k)),
                  pl.BlockSpec((tf, d), lambda i, k: (k, 0))],
        out_specs=row,
        out_shape=jax.ShapeDtypeStruct((n, d), F32),
        scratch_shapes=[pltpu.VMEM((tm, d), BF16), pltpu.VMEM((tm, tf), BF16)],
        compiler_params=_cparams(("parallel", "arbitrary")),
        name="dense_ffn",
    )(x2, g, wg, wu, wd)


def _moe_ffn_kernel(be_ref, nused_ref, xb_ref, wg_ref, wu_ref, wd_ref, y_ref, act_scr):
    j = pl.program_id(0)
    k = pl.program_id(1)
    used = j < nused_ref[0]
    mb = act_scr.shape[0]
    sub = xb_ref.shape[0] // mb
    tw = xb_ref.shape[1]

    @pl.when(k == 0)
    def _():
        y_ref[...] = jnp.zeros_like(y_ref)

    @pl.when(used)
    def _():
        h = jnp.concatenate([xb_ref[pl.ds(c, mb, stride=sub), :].astype(BF16) for c in range(sub)], axis=-1)

        def accumulate(c, part):
            step = min(tw, MXU_COLS)
            for u in range(MXU_COLS // step):
                col = c * MXU_COLS + u * step
                y_ref[pl.ds(col // tw, mb, stride=sub), col % tw:col % tw + step] += (
                    part[:, u * step:(u + 1) * step])

        _swiglu_partial(h, wg_ref, wu_ref, wd_ref, act_scr, accumulate)


def _moe_ffn(buf, block_expert, n_used, wg, wu, wd, mb, tf):
    d, f = wg.shape[1], wg.shape[2]
    sub = ROW_TILE_SUBLANES
    tw = d // sub
    p = buf.shape[0] // sub
    nk = f // tf
    kk = lambda j, k, be, nu: jnp.where(j < nu[0], k, nk - 1)
    row = pl.BlockSpec((mb * sub, tw), lambda j, k, be, nu: (j, 0))
    grid_spec = pltpu.PrefetchScalarGridSpec(
        num_scalar_prefetch=2,
        grid=(p // mb, nk),
        in_specs=[row,
                  pl.BlockSpec((None, d, tf), lambda j, k, be, nu: (be[j], 0, kk(j, k, be, nu))),
                  pl.BlockSpec((None, d, tf), lambda j, k, be, nu: (be[j], 0, kk(j, k, be, nu))),
                  pl.BlockSpec((None, tf, d), lambda j, k, be, nu: (be[j], kk(j, k, be, nu), 0))],
        out_specs=row,
        scratch_shapes=[pltpu.VMEM((mb, tf), BF16)],
    )
    return pl.pallas_call(
        _moe_ffn_kernel,
        grid_spec=grid_spec,
        out_shape=jax.ShapeDtypeStruct((p * sub, tw), F32),
        compiler_params=_cparams(("parallel", "arbitrary")),
        name="moe_ffn",
    )(block_expert, n_used, buf, wg, wu, wd)


ISSUE_UNROLL = 8


def _issue_tile_copies(chunk, copy_fn):
    def body(trip, carry):
        for u in range(ISSUE_UNROLL):
            copy_fn(trip, u).start(priority=u % 2)
        return carry

    lax.fori_loop(0, chunk // ISSUE_UNROLL, body, 0)


def _tile(ref, idx, sub):
    return ref.at[pl.ds(pl.multiple_of(idx * sub, sub), sub)]


def _dispatch_kernel(slot_ref, h_ref, out_ref, zero_scr, sem, *, n_tok_steps, sub):
    i = pl.program_id(0)
    n_copies = slot_ref.shape[-1]
    dst = lambda trip, u: _tile(out_ref, slot_ref[0, 0, trip * ISSUE_UNROLL + u], sub)
    tok = lambda trip, u: trip * (ISSUE_UNROLL // TOP_K) + u // TOP_K

    @pl.when(i == 0)
    def _():
        zero_scr[...] = jnp.zeros_like(zero_scr)

    @pl.when(i < n_tok_steps)
    def _():
        _issue_tile_copies(n_copies, lambda trip, u: pltpu.make_async_copy(
            _tile(h_ref, tok(trip, u), sub), dst(trip, u), sem))

    @pl.when(i >= n_tok_steps)
    def _():
        _issue_tile_copies(n_copies, lambda trip, u: pltpu.make_async_copy(zero_scr, dst(trip, u), sem))

    for _ in range(TOP_K):
        pltpu.make_async_copy(h_ref, out_ref.at[pl.ds(0, h_ref.shape[0])], sem).wait()


def _dispatch_rows(h, slot, tm, sub):
    n = h.shape[0] // sub
    tw = h.shape[1]
    p = slot.shape[0]
    n_copies = tm * TOP_K
    assert p % n_copies == 0 and n % tm == 0 and n_copies % ISSUE_UNROLL == 0
    n_tok_steps = n // tm
    return pl.pallas_call(
        functools.partial(_dispatch_kernel, n_tok_steps=n_tok_steps, sub=sub),
        grid=(p // n_copies,),
        in_specs=[pl.BlockSpec((1, 1, n_copies), lambda i: (i, 0, 0), memory_space=pltpu.SMEM),
                  pl.BlockSpec((tm * sub, tw), lambda i: (jnp.minimum(i, n_tok_steps - 1), 0))],
        out_specs=pl.BlockSpec(memory_space=pl.ANY),
        out_shape=jax.ShapeDtypeStruct((p * sub, tw), h.dtype),
        scratch_shapes=[pltpu.VMEM((sub, tw), h.dtype), pltpu.SemaphoreType.DMA(())],
        compiler_params=_cparams(("arbitrary",)),
        name="dispatch_rows",
    )(slot.reshape(p // n_copies, 1, n_copies), h)


def _combine_kernel(cur_ref, nxt_ref, x_ref, route_ref, y_ref, *rest, final_norm, sub):
    o_ref, ybuf, sem = rest[-3:]
    i = pl.program_id(0)
    tm = x_ref.shape[0]
    n_copies = tm * TOP_K
    par = i % 2

    def fetch(idx_ref, buf):
        def copy(trip, u):
            tok = trip * (ISSUE_UNROLL // TOP_K) + u // TOP_K
            return pltpu.make_async_copy(_tile(y_ref, idx_ref[0, 0, trip * ISSUE_UNROLL + u], sub),
                                         _tile(ybuf.at[buf, u % TOP_K], tok, sub), sem.at[buf])
        _issue_tile_copies(n_copies, copy)

    @pl.when(i == 0)
    def _():
        fetch(cur_ref, 0)

    @pl.when(i + 1 < pl.num_programs(0))
    def _():
        fetch(nxt_ref, 1 - par)

    pltpu.make_async_copy(ybuf.at[par], ybuf.at[par], sem.at[par]).wait()

    route = route_ref[...]
    gates = [route[:, TOP_K + k:TOP_K + k + 1] for k in range(TOP_K)]
    tw = x_ref.shape[1] // sub
    for j in range(sub):
        sl = slice(j * tw, (j + 1) * tw)
        mix = None
        for k in range(TOP_K):
            term = gates[k] * ybuf[par, k, pl.ds(j, tm, stride=sub), :]
            mix = term if mix is None else mix + term
        o_ref[:, sl] = x_ref[:, sl] + mix
    if final_norm:
        o_ref[...] = _rms(o_ref[...], rest[0][...])


def _combine(x2, y, dest, route, tm, g_final=None):
    n, d = x2.shape
    sub = ROW_TILE_SUBLANES
    n_copies = tm * TOP_K
    steps = n // tm
    assert n_copies % ISSUE_UNROLL == 0 and ISSUE_UNROLL % TOP_K == 0
    dest3 = dest.reshape(steps, 1, n_copies)
    row = pl.BlockSpec((tm, d), lambda i: (i, 0))
    in_specs = [pl.BlockSpec((1, 1, n_copies), lambda i: (i, 0, 0), memory_space=pltpu.SMEM),
                pl.BlockSpec((1, 1, n_copies), lambda i: (jnp.minimum(i + 1, steps - 1), 0, 0),
                             memory_space=pltpu.SMEM),
                row, pl.BlockSpec((tm, LANES), lambda i: (i, 0)), pl.BlockSpec(memory_space=pl.ANY)]
    args = [dest3, dest3, x2, route, y]
    if g_final is not None:
        in_specs.append(pl.BlockSpec((1, d), lambda i: (0, 0)))
        args.append(g_final)
    return pl.pallas_call(
        functools.partial(_combine_kernel, final_norm=g_final is not None, sub=sub),
        grid=(steps,), in_specs=in_specs, out_specs=row,
        out_shape=jax.ShapeDtypeStruct((n, d), F32),
        scratch_shapes=[pltpu.VMEM((2, TOP_K, tm * sub, d // sub), F32), pltpu.SemaphoreType.DMA((2,))],
        compiler_params=_cparams(("arbitrary",)), name="moe_combine",
    )(*args)


def _final_norm_kernel(x_ref, g_ref, o_ref):
    o_ref[...] = _rms(x_ref[...], g_ref[...])


def _final_norm(x2, g, tm):
    n, d = x2.shape
    row = pl.BlockSpec((tm, d), lambda i: (i, 0))
    return pl.pallas_call(
        _final_norm_kernel, grid=(n // tm,), in_specs=[row, pl.BlockSpec((1, d), lambda i: (0, 0))],
        out_specs=row, out_shape=jax.ShapeDtypeStruct((n, d), F32),
        compiler_params=_cparams(("parallel",)), name="final_norm",
    )(x2, g)


def _routing_plan(route, n_experts, mb):
    n = route.shape[0]
    n_assign = n * TOP_K
    flat_e = route[:, :TOP_K].astype(jnp.int32).reshape(-1)
    onehot = (flat_e[:, None] == jnp.arange(n_experts)[None, :]).astype(jnp.int32)
    csum = jnp.cumsum(onehot, axis=0)
    rank = jnp.sum(csum * onehot, axis=1) - 1
    counts = csum[-1]
    padded = (counts + mb - 1) // mb * mb
    pend = jnp.cumsum(padded)
    pstart = pend - padded
    dest = jnp.sum(pstart[None, :] * onehot, axis=1) + rank
    assert n_assign % mb == 0
    n_blocks = n_assign // mb + n_experts
    block_first = jnp.arange(n_blocks) * mb
    block_expert = jnp.minimum(jnp.sum(block_first[:, None] >= pend[None, :], axis=1), n_experts - 1)
    n_used = (pend[-1] // mb).reshape(1)
    pads = padded - counts
    cpad = jnp.cumsum(pads)
    i = jnp.arange(n_experts * mb)
    grp = jnp.sum(i[:, None] >= cpad[None, :], axis=1)
    grp_hot = (grp[:, None] == jnp.arange(n_experts + 1)[None, :]).astype(jnp.int32)
    first_free = jnp.concatenate([pstart + counts, pend[-1:]])
    before = jnp.concatenate([jnp.zeros((1,), cpad.dtype), cpad])
    free = jnp.sum(grp_hot * (first_free - before)[None, :], axis=1) + i
    slot = jnp.concatenate([dest, free]).astype(jnp.int32)
    return dest.astype(jnp.int32), slot, block_expert.astype(jnp.int32), n_used.astype(jnp.int32)


def _tiles(n, t):
    return dict(tm_in=math.gcd(1024, t), tm_proj=math.gcd(512, t), tm_ffn=math.gcd(1024, n), tf=1792,
                tq_dil=1024, tq_mem=math.gcd(512, t), moe_block=math.gcd(1024, n))


@jax.jit
def _forward(x, mem, g_mix_norm, w_in, rpb, g_mem_norm, w_mem_kv, g_mix_out, w_out,
             g_ffn_norm, w_dense_gate, w_dense_up, w_dense_down, w_router,
             w_moe_gate, w_moe_up, w_moe_down, g_final):
    b, t, d = x.shape
    n = b * t
    depth = w_in.shape[0]
    n_experts = w_router.shape[-1]
    ts = _tiles(n, t)
    tf = math.gcd(ts["tf"], w_dense_gate.shape[-1])
    tables = _rope_tables(t)
    row1 = lambda a: a.reshape(1, -1)
    x2 = x.reshape(n, d)
    out = None
    sub = ROW_TILE_SUBLANES
    for layer in range(depth):
        qa, ka, va, qm, qbs, kbs, vbs = _inproj(x2, row1(g_mix_norm[layer]), w_in[layer].astype(BF16),
                                                tables, t, ts["tm_in"])
        shp = lambda a: a.reshape(b, t, a.shape[-1])
        oa = _na_attention(shp(qa), shp(ka), shp(va), _na_bias_table(rpb[layer])).reshape(n, NA_WIDTH)
        obs, lses = [], []
        for (window, dil), qb, kb, vb in zip(DIL_BRANCHES, qbs, kbs, vbs):
            o, lse = _dilated_branch(qb, kb, vb, window // (2 * dil), ts["tq_dil"])
            obs.append(o)
            lses.append(lse)
        km, vm = _mem_kv(mem, row1(g_mem_norm[layer]), w_mem_kv[layer].astype(BF16))
        om = _mem_attention(shp(qm), km, vm, ts["tq_mem"]).reshape(n, MEM_WIDTH)
        gm, wo = row1(g_mix_out[layer]), w_out[layer].astype(BF16)
        i = layer // 2
        if layer % 2 == 0:
            x2 = _outproj(oa, obs, lses, om, x2, gm, wo, t, ts["tm_proj"])
            x2 = _dense_ffn(x2, row1(g_ffn_norm[layer]), w_dense_gate[i].astype(BF16),
                            w_dense_up[i].astype(BF16), w_dense_down[i].astype(BF16), ts["tm_ffn"], tf)
            if layer == depth - 1:
                out = _final_norm(x2, row1(g_final), ts["tm_proj"])
        else:
            wr = jnp.pad(w_router[i], ((0, 0), (0, LANES - n_experts))).astype(BF16)
            x2, h, route = _outproj(oa, obs, lses, om, x2, gm, wo, t, ts["tm_proj"],
                                    router=(row1(g_ffn_norm[layer]), wr, n_experts))
            mb = ts["moe_block"]
            dest, slot, block_expert, n_used = _routing_plan(route, n_experts, mb)
            buf = _dispatch_rows(h, slot, ts["tm_proj"], sub)
            y = _moe_ffn(buf, block_expert, n_used, w_moe_gate[i].astype(BF16), w_moe_up[i].astype(BF16),
                         w_moe_down[i].astype(BF16), mb, tf)
            if layer == depth - 1:
                out = _combine(x2, y, dest, route, ts["tm_proj"], g_final=row1(g_final))
            else:
                x2 = _combine(x2, y, dest, route, ts["tm_proj"])
    return out.reshape(b, t, d)


def kernel(x, mem, g_mix_norm, w_in, rpb, g_mem_norm, w_mem_kv, g_mix_out, w_out, g_ffn_norm,
           w_dense_gate, w_dense_up, w_dense_down, w_router, w_moe_gate, w_moe_up, w_moe_down, g_final):
    return _forward(x, mem, g_mix_norm, w_in, rpb, g_mem_norm, w_mem_kv, g_mix_out, w_out, g_ffn_norm,
                    w_dense_gate, w_dense_up, w_dense_down, w_router, w_moe_gate, w_moe_up, w_moe_down,
                    g_final)
```

```python
import functools
import math

import jax
import jax.numpy as jnp
from jax import lax
from jax.experimental import pallas as pl
from jax.experimental.pallas import tpu as pltpu

HEAD_DIM = 64
NA_HEADS = 6
DIL_HEADS = 6
MEM_HEADS = 4
NA_WIDTH = NA_HEADS * HEAD_DIM
DIL_WIDTH = DIL_HEADS * HEAD_DIM
MEM_WIDTH = MEM_HEADS * HEAD_DIM
GRID_W = 64
NA_WIN_ROWS = 8
NA_WIN_COLS = 16
DIL_BRANCHES = ((128, 1), (512, 4), (2048, 16))
ROPE_THETA = 500000.0
ROT_DIM = HEAD_DIM // 4
TOP_K = 2
RMS_EPS = 1e-6
ATTN_SCALE = HEAD_DIM ** -0.5
LOG2E = math.log2(math.e)
Q_SCALE = ATTN_SCALE * LOG2E

LANES = 128
HEADS_PER_LANE_GROUP = LANES // HEAD_DIM
NEG = -1e30
ROWS_PER_TRIP = 8
ROW_TILE_SUBLANES = 8
VMEM_LIMIT = 56 * 1024 * 1024

F32 = jnp.float32
BF16 = jnp.bfloat16


def _cparams(sem):
    return pltpu.CompilerParams(dimension_semantics=sem, vmem_limit_bytes=VMEM_LIMIT)


def _rms(x, g):
    ms = jnp.mean(x * x, axis=-1, keepdims=True)
    return x * lax.rsqrt(ms + RMS_EPS) * g


def _lane_group(a, p):
    return a[:, p * LANES:(p + 1) * LANES]


def _attn_scores(q, k, bias_fn, n_heads):
    m = q.shape[0]
    assert HEADS_PER_LANE_GROUP == 2
    lo = lax.broadcasted_iota(jnp.int32, (m, LANES), 1) < HEAD_DIM
    s_parts = []
    for p in range(n_heads // HEADS_PER_LANE_GROUP):
        qp = _lane_group(q, p)
        zero = jnp.zeros_like(qp)
        q2 = jnp.concatenate([jnp.where(lo, qp, zero), jnp.where(lo, zero, qp)], axis=0)
        s_parts.append(lax.dot_general(q2, _lane_group(k, p), (((1,), (1,)), ((), ())),
                                       preferred_element_type=F32) + bias_fn(p))
    return jnp.concatenate(s_parts, axis=0)


def _attn_softmax(s):
    mx = jnp.max(s, axis=-1, keepdims=True)
    return jnp.exp2((s - mx).astype(BF16)), mx


def _values_with_ones(v):
    ones = jnp.ones((v.shape[0], LANES), v.dtype)
    return jnp.concatenate([blk for p in range(v.shape[1] // LANES) for blk in (_lane_group(v, p), ones)],
                           axis=-1)


def _attn_values(eb, mx, v, n_heads):
    m = eb.shape[0] // n_heads
    n_pairs = n_heads // HEADS_PER_LANE_GROUP
    with_ones = v.shape[1] == n_pairs * 2 * LANES
    lo = lax.broadcasted_iota(jnp.int32, (m, LANES), 1) < HEAD_DIM
    if not with_ones:
        den_all = jnp.sum(eb.astype(F32), axis=-1, keepdims=True)
    outs, lses = [], []
    for p in range(n_pairs):
        rows = slice(p * 2 * m, (p + 1) * 2 * m)
        if with_ones:
            both = jnp.dot(eb[rows], v[:, p * 2 * LANES:(p + 1) * 2 * LANES], preferred_element_type=F32)
            num, den = both[:, :LANES], both[:, LANES:]
        else:
            num, den = jnp.dot(eb[rows], _lane_group(v, p), preferred_element_type=F32), den_all[rows]
        o2 = num / den
        outs.append(jnp.where(lo, o2[:m], o2[m:]))
        lse = mx[rows] + jnp.log2(den)
        lses += [lse[:m], lse[m:]]
    return jnp.concatenate(outs, axis=-1), lses


def _attention_pipelined(items, n_heads, skew=1):
    n = len(items)
    loaded, scores, soft = {}, {}, {}
    for step in range(n + 2 * skew):
        if step < n:
            loaded[step] = items[step][0]()
            q, k, _, bias_fn = loaded[step]
            scores[step] = _attn_scores(q, k, bias_fn, n_heads)
        if 0 <= step - skew < n:
            soft[step - skew] = _attn_softmax(scores.pop(step - skew))
        if 0 <= step - 2 * skew < n:
            eb, mx = soft.pop(step - 2 * skew)
            items[step - 2 * skew][1](*_attn_values(eb, mx, loaded.pop(step - 2 * skew)[2], n_heads))


def _inproj_kernel(x_ref, g_ref, w_ref, cos_ref, sa_ref, sb_ref,
                   qa_ref, ka_ref, va_ref, qm_ref, *rest, dils):
    dil_refs, scr = rest[:-1], rest[-1]
    tm = x_ref.shape[0]
    h = _rms(x_ref[...], g_ref[...]).astype(BF16)

    def emit_dilated(val, refs):
        groups = val.shape[1] // LANES
        for g in range(groups):
            scr[g] = val[:, g * LANES:(g + 1) * LANES]
        for dil, ref in zip(dils, refs):
            if dil == 1:
                ref[0] = val.astype(BF16)
                continue
            for r in range(dil):
                for g in range(groups):
                    ref[r, :, g * LANES:(g + 1) * LANES] = (
                        scr[g, pl.ds(r, tm // dil, stride=dil), :].astype(BF16))

    def proj_pair(c0, width_a, width_b):
        both = jnp.dot(h, w_ref[:, c0:c0 + width_a + width_b], preferred_element_type=F32)
        return both[:, :width_a], both[:, width_a:]

    def rope(a):
        cos, sa, sb = cos_ref[...], sa_ref[...], sb_ref[...]
        half = ROT_DIM // 2
        parts = []
        for gidx in range(a.shape[1] // LANES):
            xg = a[:, gidx * LANES:(gidx + 1) * LANES]
            parts.append(xg * cos + pltpu.roll(xg, LANES - half, 1) * sa + pltpu.roll(xg, half, 1) * sb)
        return jnp.concatenate(parts, axis=-1)

    nd = len(dils)
    assert (2 * NA_WIDTH) % 256 == 0 and (NA_WIDTH + DIL_WIDTH) % 256 == 0 and (2 * DIL_WIDTH) % 256 == 0
    kb, vb = proj_pair(3 * NA_WIDTH + DIL_WIDTH, DIL_WIDTH, DIL_WIDTH)
    emit_dilated(rope(kb), dil_refs[nd:2 * nd])
    emit_dilated(vb, dil_refs[2 * nd:3 * nd])
    va, qb = proj_pair(2 * NA_WIDTH, NA_WIDTH, DIL_WIDTH)
    va_ref[...] = va.astype(BF16)
    emit_dilated(rope(qb * Q_SCALE), dil_refs[0:nd])
    qa, ka = proj_pair(0, NA_WIDTH, NA_WIDTH)
    qa_ref[...] = (qa * Q_SCALE).astype(BF16)
    ka_ref[...] = ka.astype(BF16)
    c = 3 * NA_WIDTH + 3 * DIL_WIDTH
    qm_ref[...] = (jnp.dot(h, w_ref[:, c:c + MEM_WIDTH], preferred_element_type=F32) * Q_SCALE).astype(BF16)


def _rope_tables(t):
    half = ROT_DIM // 2
    inv_freq = ROPE_THETA ** (-jnp.arange(0, ROT_DIM, 2, dtype=F32) / ROT_DIM)
    ang = jnp.arange(t, dtype=F32)[:, None] * inv_freq[None, :]
    cos, sin = jnp.cos(ang), jnp.sin(ang)
    ones = jnp.ones((t, HEAD_DIM - ROT_DIM), F32)
    zeros = jnp.zeros((t, HEAD_DIM - ROT_DIM), F32)
    zh = jnp.zeros((t, half), F32)
    cos_h = jnp.concatenate([cos, cos, ones], axis=1)
    sa_h = jnp.concatenate([-sin, zh, zeros], axis=1)
    sb_h = jnp.concatenate([zh, sin, zeros], axis=1)
    tile = lambda a: jnp.tile(a, (1, HEADS_PER_LANE_GROUP))
    return tile(cos_h), tile(sa_h), tile(sb_h)


def _inproj(x2, g, w, tables, t, tm):
    n, d = x2.shape
    b = n // t
    tpb = t // tm
    dils = tuple(dil for _, dil in DIL_BRANCHES)
    assert all(tm % (dil * 16) == 0 for dil in dils)
    row = lambda width: pl.BlockSpec((tm, width), lambda i: (i, 0))
    tab = pl.BlockSpec((tm, LANES), lambda i: (i % tpb, 0))
    plain = [NA_WIDTH] * 3 + [MEM_WIDTH]
    dil_specs = [pl.BlockSpec((None, dil, tm // dil, DIL_WIDTH), lambda i: (i // tpb, 0, i % tpb, 0))
                 for dil in dils] * 3
    dil_shapes = [jax.ShapeDtypeStruct((b, dil, t // dil, DIL_WIDTH), BF16) for dil in dils] * 3
    outs = pl.pallas_call(
        functools.partial(_inproj_kernel, dils=dils),
        grid=(n // tm,),
        in_specs=[row(d), pl.BlockSpec((1, d), lambda i: (0, 0)),
                  pl.BlockSpec(w.shape, lambda i: (0, 0)), tab, tab, tab],
        out_specs=[row(wd) for wd in plain] + dil_specs,
        out_shape=[jax.ShapeDtypeStruct((n, wd), BF16) for wd in plain] + dil_shapes,
        scratch_shapes=[pltpu.VMEM((DIL_WIDTH // LANES, tm, LANES), F32)],
        compiler_params=_cparams(("parallel",)),
        name="inproj",
    )(x2, g, w, *tables)
    nd = len(dils)
    qa, ka, va, qm = outs[:4]
    return qa, ka, va, qm, outs[4:4 + nd], outs[4 + nd:4 + 2 * nd], outs[4 + 2 * nd:]


def _na_bias_table(rpb):
    c = jnp.arange(GRID_W)
    c0 = jnp.clip(c - NA_WIN_COLS // 2, 0, GRID_W - NA_WIN_COLS)
    kc = jnp.arange(GRID_W)
    valid = (kc[None, :] >= c0[:, None]) & (kc[None, :] < c0[:, None] + NA_WIN_COLS)
    coff = kc[None, :] - c[:, None] + (NA_WIN_COLS - 1)
    onehot = (coff[None] == jnp.arange(2 * NA_WIN_COLS - 1)[:, None, None]).astype(F32)
    by_col = jnp.einsum("hrd,dck->hrck", rpb.astype(F32), onehot, precision=lax.Precision.HIGHEST)
    by_col = jnp.where(valid[None, None], by_col * LOG2E, NEG)
    tab = jnp.stack([by_col[:, d:d + NA_WIN_ROWS] for d in range(NA_WIN_ROWS)], axis=1)
    tab = tab.transpose(1, 0, 3, 2, 4)
    return tab.reshape(NA_WIN_ROWS * NA_HEADS // HEADS_PER_LANE_GROUP, HEADS_PER_LANE_GROUP * GRID_W,
                       NA_WIN_ROWS * GRID_W)


def _na_kernel(q_ref, kp_ref, kc_ref, kn_ref, vp_ref, vc_ref, vn_ref, bias_ref, o_ref,
               kwin, vwin, *, rows):
    j = pl.program_id(1)
    blk = NA_WIN_ROWS * GRID_W
    for idx, (kr, vr) in enumerate(((kp_ref, vp_ref), (kc_ref, vc_ref), (kn_ref, vn_ref))):
        kwin[idx * blk:(idx + 1) * blk, :] = kr[...]
        vwin[idx * blk:(idx + 1) * blk, :] = vr[...]

    n_pairs = NA_HEADS // HEADS_PER_LANE_GROUP

    def one_row(i):
        q_rows = pl.ds(pl.multiple_of(i * GRID_W, GRID_W), GRID_W)

        def load():
            r = j * NA_WIN_ROWS + i
            r0 = jnp.clip(r - NA_WIN_ROWS // 2, 0, rows - NA_WIN_ROWS)
            dlt = r0 - r + (NA_WIN_ROWS - 1)
            start = pl.multiple_of((r0 - (j - 1) * NA_WIN_ROWS) * GRID_W, GRID_W)
            return (q_ref[q_rows, :], kwin[pl.ds(start, blk), :], vwin[pl.ds(start, blk), :],
                    lambda p: bias_ref[dlt * n_pairs + p])

        def store(o, lses):
            o_ref[q_rows, :] = o.astype(BF16)

        return load, store

    def row_group(ig, carry):
        _attention_pipelined([one_row(ig * ROWS_PER_TRIP + u) for u in range(ROWS_PER_TRIP)], NA_HEADS)
        return carry

    lax.fori_loop(0, NA_WIN_ROWS // ROWS_PER_TRIP, row_group, 0)


def _na_attention(q, k, v, bias):
    b, t, w = q.shape
    rows = t // GRID_W
    assert rows % NA_WIN_ROWS == 0 and rows >= NA_WIN_ROWS
    nj = rows // NA_WIN_ROWS
    blk = NA_WIN_ROWS * GRID_W
    cur = pl.BlockSpec((None, blk, w), lambda bi, j: (bi, j, 0))
    prev = pl.BlockSpec((None, blk, w), lambda bi, j: (bi, jnp.maximum(j - 1, 0), 0))
    nxt = pl.BlockSpec((None, blk, w), lambda bi, j: (bi, jnp.minimum(j + 1, nj - 1), 0))
    return pl.pallas_call(
        functools.partial(_na_kernel, rows=rows),
        grid=(b, nj),
        in_specs=[cur, prev, cur, nxt, prev, cur, nxt,
                  pl.BlockSpec(bias.shape, lambda bi, j: (0, 0, 0))],
        out_specs=cur,
        out_shape=jax.ShapeDtypeStruct((b, t, w), BF16),
        scratch_shapes=[pltpu.VMEM((3 * blk, w), BF16), pltpu.VMEM((3 * blk, w), BF16)],
        compiler_params=_cparams(("parallel", "parallel")),
        name="na_attn",
    )(q, k, k, k, v, v, v, bias)


def _dil_kernel(q_ref, kp_ref, kc_ref, kn_ref, vp_ref, vc_ref, vn_ref, o_ref, lse_ref,
                kwin, vwin, *, seg, tq, halo):
    i = pl.program_id(2)
    width = tq + 2 * halo
    kwin[0:halo, :] = kp_ref[...]
    kwin[halo:halo + tq, :] = kc_ref[...]
    kwin[halo + tq:width, :] = kn_ref[...]
    vwin[0:halo, :] = _values_with_ones(vp_ref[...])
    vwin[halo:halo + tq, :] = _values_with_ones(vc_ref[...])
    vwin[halo + tq:width, :] = _values_with_ones(vn_ref[...])

    sq = math.gcd(tq, 2 * halo)
    sw = sq + 2 * halo
    a = lax.broadcasted_iota(jnp.int32, (sq, sw), 0)
    c = lax.broadcasted_iota(jnp.int32, (sq, sw), 1)
    band = (c >= a) & (c <= a + 2 * halo)
    lo = lax.broadcasted_iota(jnp.int32, (sq, LANES), 1) < HEAD_DIM

    def sub_block(s):
        row0 = pl.multiple_of(s * sq, sq)

        def load():
            first_key = i * tq + s * sq - halo
            valid = band & (c >= -first_key) & (c < seg - first_key)
            negb = jnp.where(valid, 0.0, NEG).astype(F32)
            negb2 = jnp.concatenate([negb] * HEADS_PER_LANE_GROUP, axis=0)
            return (q_ref[pl.ds(row0, sq), :], kwin[pl.ds(row0, sw), :], vwin[pl.ds(row0, sw), :],
                    lambda p: negb2)

        def store(o, lses):
            o_ref[pl.ds(row0, sq), :] = o.astype(BF16)
            lse_ref[pl.ds(row0, sq), :] = jnp.concatenate(
                [jnp.where(lo, lses[2 * p], lses[2 * p + 1]) for p in range(DIL_HEADS // 2)], axis=-1)

        return load, store

    n_sub = tq // sq
    per_trip = math.gcd(n_sub, ROWS_PER_TRIP)

    def sub_block_group(sg, carry):
        _attention_pipelined([sub_block(sg * per_trip + u) for u in range(per_trip)], DIL_HEADS, skew=0)
        return carry

    lax.fori_loop(0, n_sub // per_trip, sub_block_group, 0)


def _dilated_branch(q, k, v, n_side, tq_max):
    b, dil, seg, w = q.shape
    halo = n_side
    assert seg % halo == 0 and halo % 16 == 0
    tq = math.gcd(tq_max, seg)
    assert tq % halo == 0
    hb = tq // halo
    nhalo = seg // halo
    cur = pl.BlockSpec((None, None, tq, w), lambda bi, r, i: (bi, r, i, 0))
    prev = pl.BlockSpec((None, None, halo, w), lambda bi, r, i: (bi, r, jnp.maximum(i * hb - 1, 0), 0))
    nxt = pl.BlockSpec((None, None, halo, w),
                       lambda bi, r, i: (bi, r, jnp.minimum((i + 1) * hb, nhalo - 1), 0))
    return pl.pallas_call(
        functools.partial(_dil_kernel, seg=seg, tq=tq, halo=halo),
        grid=(b, dil, seg // tq),
        in_specs=[cur, prev, cur, nxt, prev, cur, nxt],
        out_specs=[cur, cur],
        out_shape=[jax.ShapeDtypeStruct((b, dil, seg, w), BF16),
                   jax.ShapeDtypeStruct((b, dil, seg, w), F32)],
        scratch_shapes=[pltpu.VMEM((tq + 2 * halo, w), BF16), pltpu.VMEM((tq + 2 * halo, 2 * w), BF16)],
        compiler_params=_cparams(("parallel", "parallel", "parallel")),
        name=f"dilated_d{dil}",
    )(q, k, k, k, v, v, v)


def _memkv_kernel(mem_ref, g_ref, w_ref, k_ref, v_ref):
    h = _rms(mem_ref[...], g_ref[...]).astype(BF16)
    kv = jnp.dot(h, w_ref[...], preferred_element_type=F32)
    k_ref[...] = kv[:, :MEM_WIDTH].astype(BF16)
    v_ref[...] = kv[:, MEM_WIDTH:].astype(BF16)


def _mem_kv(mem, g, w):
    b, m, d = mem.shape
    blk = lambda width: pl.BlockSpec((None, m, width), lambda bi: (bi, 0, 0))
    return pl.pallas_call(
        _memkv_kernel,
        grid=(b,),
        in_specs=[blk(d), pl.BlockSpec((1, d), lambda bi: (0, 0)), pl.BlockSpec(w.shape, lambda bi: (0, 0))],
        out_specs=[blk(MEM_WIDTH), blk(MEM_WIDTH)],
        out_shape=[jax.ShapeDtypeStruct((b, m, MEM_WIDTH), BF16)] * 2,
        compiler_params=_cparams(("parallel",)),
        name="mem_kv",
    )(mem, g, w)


def _memattn_kernel(q_ref, k_ref, v_ref, o_ref):
    tq = q_ref.shape[0]
    sq = math.gcd(tq, LANES)
    v_aug = _values_with_ones(v_ref[...])

    def sub_block(s):
        rows = slice(s * sq, (s + 1) * sq)

        def store(o, lses):
            o_ref[rows, :] = o.astype(BF16)

        return (lambda: (q_ref[rows, :], k_ref[...], v_aug, lambda p: 0.0)), store

    _attention_pipelined([sub_block(s) for s in range(tq // sq)], MEM_HEADS)


def _mem_attention(q, k, v, tq):
    b, t, w = q.shape
    m = k.shape[1]
    qs = pl.BlockSpec((None, tq, w), lambda bi, i: (bi, i, 0))
    ks = pl.BlockSpec((None, m, w), lambda bi, i: (bi, 0, 0))
    return pl.pallas_call(
        _memattn_kernel,
        grid=(b, t // tq),
        in_specs=[qs, ks, ks],
        out_specs=qs,
        out_shape=jax.ShapeDtypeStruct((b, t, w), BF16),
        compiler_params=_cparams(("parallel", "parallel")),
        name="mem_attn",
    )(q, k, v)


def _token_order(ref, scr):
    dil = ref.shape[0]
    if dil == 1:
        return ref[0].astype(F32)
    groups, tm, _ = scr.shape
    for r in range(dil):
        for g in range(groups):
            scr[g, pl.ds(r, tm // dil, stride=dil), :] = ref[r, :, g * LANES:(g + 1) * LANES].astype(F32)
    return jnp.concatenate([scr[g] for g in range(groups)], axis=-1)


def _branch_mix(ob_refs, lse_refs, o_scrs, l_scrs):
    lses = [_token_order(r, s) for r, s in zip(lse_refs, l_scrs)]
    mx = functools.reduce(jnp.maximum, lses)
    es = [jnp.exp2(l - mx) for l in lses]
    den = functools.reduce(lambda p, q: p + q, es)
    acc = None
    for e, ob, scr in zip(es, ob_refs, o_scrs):
        term = (e / den) * _token_order(ob, scr)
        acc = term if acc is None else acc + term
    return acc


def _outproj_core(oa_ref, ob_refs, lse_refs, om_ref, x_ref, gm_ref, w_ref, o_scrs, l_scrs):
    gm = gm_ref[...]
    e0, e1 = NA_WIDTH, NA_WIDTH + DIL_WIDTH
    ya = _rms(oa_ref[...].astype(F32), gm[:, :e0]).astype(BF16)
    yb = _rms(_branch_mix(ob_refs, lse_refs, o_scrs, l_scrs), gm[:, e0:e1]).astype(BF16)
    ym = _rms(om_ref[...].astype(F32), gm[:, e1:]).astype(BF16)
    y = jnp.concatenate([ya, yb, ym], axis=-1)
    return x_ref[...] + jnp.dot(y, w_ref[...], preferred_element_type=F32)


def _split_scratch(scrs, dils):
    it = iter(scrs)
    o_scrs = [next(it) if dil > 1 else None for dil in dils]
    l_scrs = [next(it) if dil > 1 else None for dil in dils]
    return o_scrs, l_scrs


def _outproj_kernel(oa_ref, ob1, ob2, ob3, l1, l2, l3, om_ref, x_ref, gm_ref, w_ref, xo_ref, *scrs, dils):
    xo_ref[...] = _outproj_core(oa_ref, (ob1, ob2, ob3), (l1, l2, l3), om_ref, x_ref, gm_ref, w_ref,
                                *_split_scratch(scrs, dils))


def _store_row_tiles(ref, val):
    m, d = val.shape
    sub = ref.shape[0] // m
    tw = d // sub
    for j in range(sub):
        ref[pl.ds(j, m, stride=sub), :] = val[:, j * tw:(j + 1) * tw]


def _outproj_router_kernel(oa_ref, ob1, ob2, ob3, l1, l2, l3, om_ref, x_ref, gm_ref, w_ref,
                           gf_ref, wr_ref, xo_ref, h_ref, route_ref, *scrs, n_experts, dils):
    xn = _outproj_core(oa_ref, (ob1, ob2, ob3), (l1, l2, l3), om_ref, x_ref, gm_ref, w_ref,
                       *_split_scratch(scrs, dils))
    xo_ref[...] = xn
    h = _rms(xn, gf_ref[...])
    _store_row_tiles(h_ref, h)
    logits = jnp.dot(h.astype(BF16), wr_ref[...], preferred_element_type=F32)
    lane = lax.broadcasted_iota(jnp.int32, logits.shape, 1)
    lg = jnp.where(lane < n_experts, logits, NEG)
    m1 = jnp.max(lg, axis=-1, keepdims=True)
    i1 = jnp.min(jnp.where(lg == m1, lane, LANES), axis=-1, keepdims=True)
    lg2 = jnp.where(lane == i1, NEG, lg)
    m2 = jnp.max(lg2, axis=-1, keepdims=True)
    i2 = jnp.min(jnp.where(lg2 == m2, lane, LANES), axis=-1, keepdims=True)
    e2 = jnp.exp(m2 - m1)
    g1 = 1.0 / (1.0 + e2)
    g2 = e2 / (1.0 + e2)
    route = jnp.where(lane == 0, i1.astype(F32), 0.0)
    route = jnp.where(lane == 1, i2.astype(F32), route)
    route = jnp.where(lane == 2, g1, route)
    route = jnp.where(lane == 3, g2, route)
    route_ref[...] = route


def _outproj(oa, obs, lses, om, x2, gm, w, t, tm, router=None):
    n, d = x2.shape
    tpb = t // tm
    row = lambda width: pl.BlockSpec((tm, width), lambda i: (i, 0))
    full = lambda a: pl.BlockSpec(a.shape, lambda i: (0,) * a.ndim)
    grouped = lambda a: pl.BlockSpec((None, a.shape[1], tm // a.shape[1], a.shape[3]),
                                     lambda i: (i // tpb, 0, i % tpb, 0))
    dils = tuple(a.shape[1] for a in obs)
    args = [oa, *obs, *lses, om, x2, gm, w]
    in_specs = ([row(NA_WIDTH)] + [grouped(a) for a in obs] + [grouped(a) for a in lses]
                + [row(MEM_WIDTH), row(d), full(gm), full(w)])
    scratch = [pltpu.VMEM((DIL_WIDTH // LANES, tm, LANES), F32)] * (2 * sum(dil > 1 for dil in dils))
    if router is None:
        return pl.pallas_call(
            functools.partial(_outproj_kernel, dils=dils),
            grid=(n // tm,), in_specs=in_specs, out_specs=row(d),
            out_shape=jax.ShapeDtypeStruct((n, d), F32), scratch_shapes=scratch,
            compiler_params=_cparams(("parallel",)), name="outproj",
        )(*args)
    gf, wr, n_experts = router
    sub = ROW_TILE_SUBLANES
    return pl.pallas_call(
        functools.partial(_outproj_router_kernel, n_experts=n_experts, dils=dils),
        grid=(n // tm,), in_specs=in_specs + [full(gf), full(wr)],
        out_specs=[row(d), pl.BlockSpec((tm * sub, d // sub), lambda i: (i, 0)), row(LANES)],
        out_shape=[jax.ShapeDtypeStruct((n, d), F32), jax.ShapeDtypeStruct((n * sub, d // sub), F32),
                   jax.ShapeDtypeStruct((n, LANES), F32)],
        scratch_shapes=scratch,
        compiler_params=_cparams(("parallel",)), name="outproj_router",
    )(*args, gf, wr)


MXU_COLS = 256


def _swiglu_partial(h, wg_ref, wu_ref, wd_ref, act_scr, emit):
    tf = wg_ref.shape[-1]
    chunk = math.gcd(MXU_COLS, tf)
    for c in range(tf // chunk):
        sl = slice(c * chunk, (c + 1) * chunk)
        gate = jnp.dot(h, wg_ref[:, sl], preferred_element_type=F32)
        up = jnp.dot(h, wu_ref[:, sl], preferred_element_type=F32)
        act_scr[:, sl] = ((gate / (1.0 + jnp.exp(-gate))) * up).astype(BF16)
    act = act_scr[...]
    for c in range(wd_ref.shape[-1] // MXU_COLS):
        emit(c, jnp.dot(act, wd_ref[:, c * MXU_COLS:(c + 1) * MXU_COLS], preferred_element_type=F32))


def _dense_ffn_kernel(x_ref, g_ref, wg_ref, wu_ref, wd_ref, o_ref, h_scr, act_scr):
    @pl.when(pl.program_id(1) == 0)
    def _():
        x = x_ref[...]
        h_scr[...] = _rms(x, g_ref[...]).astype(BF16)
        o_ref[...] = x

    def accumulate(c, part):
        o_ref[:, c * MXU_COLS:(c + 1) * MXU_COLS] += part

    _swiglu_partial(h_scr[...], wg_ref, wu_ref, wd_ref, act_scr, accumulate)


def _dense_ffn(x2, g, wg, wu, wd, tm, tf):
    n, d = x2.shape
    f = wg.shape[1]
    row = pl.BlockSpec((tm, d), lambda i, k: (i, 0))
    return pl.pallas_call(
        _dense_ffn_kernel,
        grid=(n // tm, f // tf),
        in_specs=[row, pl.BlockSpec((1, d), lambda i, k: (0, 0)),
                  pl.BlockSpec((d, tf), lambda i, k: (0, k)),
                  pl.BlockSpec((d, tf), lambda i, k: (0, k)),
                  pl.BlockSpec((tf, d), lambda i, k: (k, 0))],
        out_specs=row,
        out_shape=jax.ShapeDtypeStruct((n, d), F32),
        scratch_shapes=[pltpu.VMEM((tm, d), BF16), pltpu.VMEM((tm, tf), BF16)],
        compiler_params=_cparams(("parallel", "arbitrary")),
        name="dense_ffn",
    )(x2, g, wg, wu, wd)


def _moe_ffn_kernel(be_ref, nused_ref, xb_ref, wg_ref, wu_ref, wd_ref, y_ref, act_scr):
    j = pl.program_id(0)
    k = pl.program_id(1)
    used = j < nused_ref[0]
    mb = act_scr.shape[0]
    sub = xb_ref.shape[0] // mb
    tw = xb_ref.shape[1]

    @pl.when(k == 0)
    def _():
        y_ref[...] = jnp.zeros_like(y_ref)

    @pl.when(used)
    def _():
        h = jnp.concatenate([xb_ref[pl.ds(c, mb, stride=sub), :].astype(BF16) for c in range(sub)], axis=-1)

        def accumulate(c, part):
            step = min(tw, MXU_COLS)
            for u in range(MXU_COLS // step):
                col = c * MXU_COLS + u * step
                y_ref[pl.ds(col // tw, mb, stride=sub), col % tw:col % tw + step] += (
                    part[:, u * step:(u + 1) * step])

        _swiglu_partial(h, wg_ref, wu_ref, wd_ref, act_scr, accumulate)


def _moe_ffn(buf, block_expert, n_used, wg, wu, wd, mb, tf):
    d, f = wg.shape[1], wg.shape[2]
    sub = ROW_TILE_SUBLANES
    tw = d // sub
    p = buf.shape[0] // sub
    nk = f // tf
    kk = lambda j, k, be, nu: jnp.where(j < nu[0], k, nk - 1)
    row = pl.BlockSpec((mb * sub, tw), lambda j, k, be, nu: (j, 0))
    grid_spec = pltpu.PrefetchScalarGridSpec(
        num_scalar_prefetch=2,
        grid=(p // mb, nk),
        in_specs=[row,
                  pl.BlockSpec((None, d, tf), lambda j, k, be, nu: (be[j], 0, kk(j, k, be, nu))),
                  pl.BlockSpec((None, d, tf), lambda j, k, be, nu: (be[j], 0, kk(j, k, be, nu))),
                  pl.BlockSpec((None, tf, d), lambda j, k, be, nu: (be[j], kk(j, k, be, nu), 0))],
        out_specs=row,
        scratch_shapes=[pltpu.VMEM((mb, tf), BF16)],
    )
    return pl.pallas_call(
        _moe_ffn_kernel,
        grid_spec=grid_spec,
        out_shape=jax.ShapeDtypeStruct((p * sub, tw), F32),
        compiler_params=_cparams(("parallel", "arbitrary")),
        name="moe_ffn",
    )(block_expert, n_used, buf, wg, wu, wd)


ISSUE_UNROLL = 8


def _issue_tile_copies(chunk, copy_fn):
    def body(trip, carry):
        for u in range(ISSUE_UNROLL):
            copy_fn(trip, u).start(priority=u % 2)
        return carry

    lax.fori_loop(0, chunk // ISSUE_UNROLL, body, 0)


def _tile(ref, idx, sub):
    return ref.at[pl.ds(pl.multiple_of(idx * sub, sub), sub)]


def _dispatch_kernel(slot_ref, h_ref, out_ref, zero_scr, sem, *, n_tok_steps, sub):
    i = pl.program_id(0)
    n_copies = slot_ref.shape[-1]
    dst = lambda trip, u: _tile(out_ref, slot_ref[0, 0, trip * ISSUE_UNROLL + u], sub)
    tok = lambda trip, u: trip * (ISSUE_UNROLL // TOP_K) + u // TOP_K

    @pl.when(i == 0)
    def _():
        zero_scr[...] = jnp.zeros_like(zero_scr)

    @pl.when(i < n_tok_steps)
    def _():
        _issue_tile_copies(n_copies, lambda trip, u: pltpu.make_async_copy(
            _tile(h_ref, tok(trip, u), sub), dst(trip, u), sem))

    @pl.when(i >= n_tok_steps)
    def _():
        _issue_tile_copies(n_copies, lambda trip, u: pltpu.make_async_copy(zero_scr, dst(trip, u), sem))

    for _ in range(TOP_K):
        pltpu.make_async_copy(h_ref, out_ref.at[pl.ds(0, h_ref.shape[0])], sem).wait()


def _dispatch_rows(h, slot, tm, sub):
    n = h.shape[0] // sub
    tw = h.shape[1]
    p = slot.shape[0]
    n_copies = tm * TOP_K
    assert p % n_copies == 0 and n % tm == 0 and n_copies % ISSUE_UNROLL == 0
    n_tok_steps = n // tm
    return pl.pallas_call(
        functools.partial(_dispatch_kernel, n_tok_steps=n_tok_steps, sub=sub),
        grid=(p // n_copies,),
        in_specs=[pl.BlockSpec((1, 1, n_copies), lambda i: (i, 0, 0), memory_space=pltpu.SMEM),
                  pl.BlockSpec((tm * sub, tw), lambda i: (jnp.minimum(i, n_tok_steps - 1), 0))],
        out_specs=pl.BlockSpec(memory_space=pl.ANY),
        out_shape=jax.ShapeDtypeStruct((p * sub, tw), h.dtype),
        scratch_shapes=[pltpu.VMEM((sub, tw), h.dtype), pltpu.SemaphoreType.DMA(())],
        compiler_params=_cparams(("arbitrary",)),
        name="dispatch_rows",
    )(slot.reshape(p // n_copies, 1, n_copies), h)


def _combine_kernel(cur_ref, nxt_ref, x_ref, route_ref, y_ref, *rest, final_norm, sub):
    o_ref, ybuf, sem = rest[-3:]
    i = pl.program_id(0)
    tm = x_ref.shape[0]
    n_copies = tm * TOP_K
    par = i % 2

    def fetch(idx_ref, buf):
        def copy(trip, u):
            tok = trip * (ISSUE_UNROLL // TOP_K) + u // TOP_K
            return pltpu.make_async_copy(_tile(y_ref, idx_ref[0, 0, trip * ISSUE_UNROLL + u], sub),
                                         _tile(ybuf.at[buf, u % TOP_K], tok, sub), sem.at[buf])
        _issue_tile_copies(n_copies, copy)

    @pl.when(i == 0)
    def _():
        fetch(cur_ref, 0)

    @pl.when(i + 1 < pl.num_programs(0))
    def _():
        fetch(nxt_ref, 1 - par)

    pltpu.make_async_copy(ybuf.at[par], ybuf.at[par], sem.at[par]).wait()

    route = route_ref[...]
    gates = [route[:, TOP_K + k:TOP_K + k + 1] for k in range(TOP_K)]
    tw = x_ref.shape[1] // sub
    for j in range(sub):
        sl = slice(j * tw, (j + 1) * tw)
        mix = None
        for k in range(TOP_K):
            term = gates[k] * ybuf[par, k, pl.ds(j, tm, stride=sub), :]
            mix = term if mix is None else mix + term
        o_ref[:, sl] = x_ref[:, sl] + mix
    if final_norm:
        o_ref[...] = _rms(o_ref[...], rest[0][...])


def _combine(x2, y, dest, route, tm, g_final=None):
    n, d = x2.shape
    sub = ROW_TILE_SUBLANES
    n_copies = tm * TOP_K
    steps = n // tm
    assert n_copies % ISSUE_UNROLL == 0 and ISSUE_UNROLL % TOP_K == 0
    dest3 = dest.reshape(steps, 1, n_copies)
    row = pl.BlockSpec((tm, d), lambda i: (i, 0))
    in_specs = [pl.BlockSpec((1, 1, n_copies), lambda i: (i, 0, 0), memory_space=pltpu.SMEM),
                pl.BlockSpec((1, 1, n_copies), lambda i: (jnp.minimum(i + 1, steps - 1), 0, 0),
                             memory_space=pltpu.SMEM),
                row, pl.BlockSpec((tm, LANES), lambda i: (i, 0)), pl.BlockSpec(memory_space=pl.ANY)]
    args = [dest3, dest3, x2, route, y]
    if g_final is not None:
        in_specs.append(pl.BlockSpec((1, d), lambda i: (0, 0)))
        args.append(g_final)
    return pl.pallas_call(
        functools.partial(_combine_kernel, final_norm=g_final is not None, sub=sub),
        grid=(steps,), in_specs=in_specs, out_specs=row,
        out_shape=jax.ShapeDtypeStruct((n, d), F32),
        scratch_shapes=[pltpu.VMEM((2, TOP_K, tm * sub, d // sub), F32), pltpu.SemaphoreType.DMA((2,))],
        compiler_params=_cparams(("arbitrary",)), name="moe_combine",
    )(*args)


def _final_norm_kernel(x_ref, g_ref, o_ref):
    o_ref[...] = _rms(x_ref[...], g_ref[...])


def _final_norm(x2, g, tm):
    n, d = x2.shape
    row = pl.BlockSpec((tm, d), lambda i: (i, 0))
    return pl.pallas_call(
        _final_norm_kernel, grid=(n // tm,), in_specs=[row, pl.BlockSpec((1, d), lambda i: (0, 0))],
        out_specs=row, out_shape=jax.ShapeDtypeStruct((n, d), F32),
        compiler_params=_cparams(("parallel",)), name="final_norm",
    )(x2, g)


def _routing_plan(route, n_experts, mb):
    n = route.shape[0]
    n_assign = n * TOP_K
    flat_e = route[:, :TOP_K].astype(jnp.int32).reshape(-1)
    onehot = (flat_e[:, None] == jnp.arange(n_experts)[None, :]).astype(jnp.int32)
    csum = jnp.cumsum(onehot, axis=0)
    rank = jnp.sum(csum * onehot, axis=1) - 1
    counts = csum[-1]
    padded = (counts + mb - 1) // mb * mb
    pend = jnp.cumsum(padded)
    pstart = pend - padded
    dest = jnp.sum(pstart[None, :] * onehot, axis=1) + rank
    assert n_assign % mb == 0
    n_blocks = n_assign // mb + n_experts
    block_first = jnp.arange(n_blocks) * mb
    block_expert = jnp.minimum(jnp.sum(block_first[:, None] >= pend[None, :], axis=1), n_experts - 1)
    n_used = (pend[-1] // mb).reshape(1)
    pads = padded - counts
    cpad = jnp.cumsum(pads)
    i = jnp.arange(n_experts * mb)
    grp = jnp.sum(i[:, None] >= cpad[None, :], axis=1)
    grp_hot = (grp[:, None] == jnp.arange(n_experts + 1)[None, :]).astype(jnp.int32)
    first_free = jnp.concatenate([pstart + counts, pend[-1:]])
    before = jnp.concatenate([jnp.zeros((1,), cpad.dtype), cpad])
    free = jnp.sum(grp_hot * (first_free - before)[None, :], axis=1) + i
    slot = jnp.concatenate([dest, free]).astype(jnp.int32)
    return dest.astype(jnp.int32), slot, block_expert.astype(jnp.int32), n_used.astype(jnp.int32)


def _tiles(n, t):
    return dict(tm_in=math.gcd(1024, t), tm_proj=math.gcd(512, t), tm_ffn=math.gcd(1024, n), tf=1792,
                tq_dil=1024, tq_mem=math.gcd(512, t), moe_block=math.gcd(1024, n))


@jax.jit
def _forward(x, mem, g_mix_norm, w_in, rpb, g_mem_norm, w_mem_kv, g_mix_out, w_out,
             g_ffn_norm, w_dense_gate, w_dense_up, w_dense_down, w_router,
             w_moe_gate, w_moe_up, w_moe_down, g_final):
    b, t, d = x.shape
    n = b * t
    depth = w_in.shape[0]
    n_experts = w_router.shape[-1]
    ts = _tiles(n, t)
    tf = math.gcd(ts["tf"], w_dense_gate.shape[-1])
    tables = _rope_tables(t)
    row1 = lambda a: a.reshape(1, -1)
    x2 = x.reshape(n, d)
    out = None
    sub = ROW_TILE_SUBLANES
    for layer in range(depth):
        qa, ka, va, qm, qbs, kbs, vbs = _inproj(x2, row1(g_mix_norm[layer]), w_in[layer].astype(BF16),
                                                tables, t, ts["tm_in"])
        shp = lambda a: a.reshape(b, t, a.shape[-1])
        oa = _na_attention(shp(qa), shp(ka), shp(va), _na_bias_table(rpb[layer])).reshape(n, NA_WIDTH)
        obs, lses = [], []
        for (window, dil), qb, kb, vb in zip(DIL_BRANCHES, qbs, kbs, vbs):
            o, lse = _dilated_branch(qb, kb, vb, window // (2 * dil), ts["tq_dil"])
            obs.append(o)
            lses.append(lse)
        km, vm = _mem_kv(mem, row1(g_mem_norm[layer]), w_mem_kv[layer].astype(BF16))
        om = _mem_attention(shp(qm), km, vm, ts["tq_mem"]).reshape(n, MEM_WIDTH)
        gm, wo = row1(g_mix_out[layer]), w_out[layer].astype(BF16)
        i = layer // 2
        if layer % 2 == 0:
            x2 = _outproj(oa, obs, lses, om, x2, gm, wo, t, ts["tm_proj"])
            x2 = _dense_ffn(x2, row1(g_ffn_norm[layer]), w_dense_gate[i].astype(BF16),
                            w_dense_up[i].astype(BF16), w_dense_down[i].astype(BF16), ts["tm_ffn"], tf)
            if layer == depth - 1:
                out = _final_norm(x2, row1(g_final), ts["tm_proj"])
        else:
            wr = jnp.pad(w_router[i], ((0, 0), (0, LANES - n_experts))).astype(BF16)
            x2, h, route = _outproj(oa, obs, lses, om, x2, gm, wo, t, ts["tm_proj"],
                                    router=(row1(g_ffn_norm[layer]), wr, n_experts))
            mb = ts["moe_block"]
            dest, slot, block_expert, n_used = _routing_plan(route, n_experts, mb)
            buf = _dispatch_rows(h, slot, ts["tm_proj"], sub)
            y = _moe_ffn(buf, block_expert, n_used, w_moe_gate[i].astype(BF16), w_moe_up[i].astype(BF16),
                         w_moe_down[i].astype(BF16), mb, tf)
            if layer == depth - 1:
                out = _combine(x2, y, dest, route, ts["tm_proj"], g_final=row1(g_final))
            else:
                x2 = _combine(x2, y, dest, route, ts["tm_proj"])
    return out.reshape(b, t, d)


def kernel(x, mem, g_mix_norm, w_in, rpb, g_mem_norm, w_mem_kv, g_mix_out, w_out, g_ffn_norm,
           w_dense_gate, w_dense_up, w_dense_down, w_router, w_moe_gate, w_moe_up, w_moe_down, g_final):
    return _forward(x, mem, g_mix_norm, w_in, rpb, g_mem_norm, w_mem_kv, g_mix_out, w_out, g_ffn_norm,
                    w_dense_gate, w_dense_up, w_dense_down, w_router, w_moe_gate, w_moe_up, w_moe_down,
                    g_final)
```

```python
import functools
import math

import jax
import jax.numpy as jnp
from jax import lax
from jax.experimental import pallas as pl
from jax.experimental.pallas import tpu as pltpu

HEAD_DIM = 64
NA_HEADS = 6
DIL_HEADS = 6
MEM_HEADS = 4
NA_WIDTH = NA_HEADS * HEAD_DIM
DIL_WIDTH = DIL_HEADS * HEAD_DIM
MEM_WIDTH = MEM_HEADS * HEAD_DIM
GRID_W = 64
NA_WIN_ROWS = 8
NA_WIN_COLS = 16
DIL_BRANCHES = ((128, 1), (512, 4), (2048, 16))
ROPE_THETA = 500000.0
ROT_DIM = HEAD_DIM // 4
TOP_K = 2
RMS_EPS = 1e-6
ATTN_SCALE = HEAD_DIM ** -0.5
LOG2E = math.log2(math.e)
Q_SCALE = ATTN_SCALE * LOG2E

LANES = 128
MXU_COLS = 256
HEADS_PER_LANE_GROUP = LANES // HEAD_DIM
NEG = -1e30
ROWS_PER_TRIP = 8
ROW_TILE_SUBLANES = 8
VMEM_LIMIT = 56 * 1024 * 1024

F32 = jnp.float32
BF16 = jnp.bfloat16


def _cparams(sem):
    return pltpu.CompilerParams(dimension_semantics=sem, vmem_limit_bytes=VMEM_LIMIT)


def _rms(x, g):
    ms = jnp.mean(x * x, axis=-1, keepdims=True)
    return x * lax.rsqrt(ms + RMS_EPS) * g


def _lane_group(a, p):
    return a[:, p * LANES:(p + 1) * LANES]


def _attn_scores(q, k, bias_fn, n_heads):
    m = q.shape[0]
    assert HEADS_PER_LANE_GROUP == 2
    lo = lax.broadcasted_iota(jnp.int32, (m, LANES), 1) < HEAD_DIM
    s_parts = []
    for p in range(n_heads // HEADS_PER_LANE_GROUP):
        qp = _lane_group(q, p)
        zero = jnp.zeros_like(qp)
        q2 = jnp.concatenate([jnp.where(lo, qp, zero), jnp.where(lo, zero, qp)], axis=0)
        s_parts.append(lax.dot_general(q2, _lane_group(k, p), (((1,), (1,)), ((), ())),
                                       preferred_element_type=F32) + bias_fn(p))
    return jnp.concatenate(s_parts, axis=0)


def _attn_softmax(s):
    mx = jnp.max(s, axis=-1, keepdims=True)
    return jnp.exp2((s - mx).astype(BF16)), mx


def _values_with_ones(v):
    ones = jnp.ones((v.shape[0], LANES), v.dtype)
    return jnp.concatenate([blk for p in range(v.shape[1] // LANES) for blk in (_lane_group(v, p), ones)],
                           axis=-1)


def _attn_values(eb, mx, v, n_heads):
    m = eb.shape[0] // n_heads
    n_pairs = n_heads // HEADS_PER_LANE_GROUP
    with_ones = v.shape[1] == n_pairs * 2 * LANES
    lo = lax.broadcasted_iota(jnp.int32, (m, LANES), 1) < HEAD_DIM
    if not with_ones:
        den_all = jnp.sum(eb.astype(F32), axis=-1, keepdims=True)
    outs, lses = [], []
    for p in range(n_pairs):
        rows = slice(p * 2 * m, (p + 1) * 2 * m)
        if with_ones:
            both = jnp.dot(eb[rows], v[:, p * 2 * LANES:(p + 1) * 2 * LANES], preferred_element_type=F32)
            num, den = both[:, :LANES], both[:, LANES:]
        else:
            num, den = jnp.dot(eb[rows], _lane_group(v, p), preferred_element_type=F32), den_all[rows]
        o2 = num / den
        outs.append(jnp.where(lo, o2[:m], o2[m:]))
        lse = mx[rows] + jnp.log2(den)
        lses += [lse[:m], lse[m:]]
    return jnp.concatenate(outs, axis=-1), lses


def _attention_pipelined(items, n_heads, skew=1):
    n = len(items)
    loaded, scores, soft = {}, {}, {}
    for step in range(n + 2 * skew):
        if step < n:
            loaded[step] = items[step][0]()
            q, k, _, bias_fn = loaded[step]
            scores[step] = _attn_scores(q, k, bias_fn, n_heads)
        if 0 <= step - skew < n:
            soft[step - skew] = _attn_softmax(scores.pop(step - skew))
        if 0 <= step - 2 * skew < n:
            eb, mx = soft.pop(step - 2 * skew)
            items[step - 2 * skew][1](*_attn_values(eb, mx, loaded.pop(step - 2 * skew)[2], n_heads))


def _inproj_kernel(x_ref, g_ref, w_ref, cos_ref, sa_ref, sb_ref,
                   qa_ref, ka_ref, va_ref, qm_ref, *rest, dils):
    dil_refs, scr = rest[:-1], rest[-1]
    tm = x_ref.shape[0]
    h = _rms(x_ref[...], g_ref[...]).astype(BF16)

    def emit_dilated(val, refs):
        groups = val.shape[1] // LANES
        for g in range(groups):
            scr[g] = val[:, g * LANES:(g + 1) * LANES]
        for dil, ref in zip(dils, refs):
            if dil == 1:
                ref[0] = val.astype(BF16)
                continue
            for r in range(dil):
                for g in range(groups):
                    ref[r, :, g * LANES:(g + 1) * LANES] = (
                        scr[g, pl.ds(r, tm // dil, stride=dil), :].astype(BF16))

    def proj_pair(c0, width_a, width_b):
        both = jnp.dot(h, w_ref[:, c0:c0 + width_a + width_b], preferred_element_type=F32)
        return both[:, :width_a], both[:, width_a:]

    def rope(a):
        cos, sa, sb = cos_ref[...], sa_ref[...], sb_ref[...]
        half = ROT_DIM // 2
        parts = []
        for gidx in range(a.shape[1] // LANES):
            xg = a[:, gidx * LANES:(gidx + 1) * LANES]
            parts.append(xg * cos + pltpu.roll(xg, LANES - half, 1) * sa + pltpu.roll(xg, half, 1) * sb)
        return jnp.concatenate(parts, axis=-1)

    nd = len(dils)
    assert all(wd % MXU_COLS == 0 for wd in (2 * NA_WIDTH, NA_WIDTH + DIL_WIDTH, 2 * DIL_WIDTH))
    kb, vb = proj_pair(3 * NA_WIDTH + DIL_WIDTH, DIL_WIDTH, DIL_WIDTH)
    emit_dilated(rope(kb), dil_refs[nd:2 * nd])
    emit_dilated(vb, dil_refs[2 * nd:3 * nd])
    va, qb = proj_pair(2 * NA_WIDTH, NA_WIDTH, DIL_WIDTH)
    va_ref[...] = va.astype(BF16)
    emit_dilated(rope(qb * Q_SCALE), dil_refs[0:nd])
    qa, ka = proj_pair(0, NA_WIDTH, NA_WIDTH)
    qa_ref[...] = (qa * Q_SCALE).astype(BF16)
    ka_ref[...] = ka.astype(BF16)
    c = 3 * NA_WIDTH + 3 * DIL_WIDTH
    qm_ref[...] = (jnp.dot(h, w_ref[:, c:c + MEM_WIDTH], preferred_element_type=F32) * Q_SCALE).astype(BF16)


def _rope_tables(t):
    half = ROT_DIM // 2
    inv_freq = ROPE_THETA ** (-jnp.arange(0, ROT_DIM, 2, dtype=F32) / ROT_DIM)
    ang = jnp.arange(t, dtype=F32)[:, None] * inv_freq[None, :]
    cos, sin = jnp.cos(ang), jnp.sin(ang)
    ones = jnp.ones((t, HEAD_DIM - ROT_DIM), F32)
    zeros = jnp.zeros((t, HEAD_DIM - ROT_DIM), F32)
    zh = jnp.zeros((t, half), F32)
    cos_h = jnp.concatenate([cos, cos, ones], axis=1)
    sa_h = jnp.concatenate([-sin, zh, zeros], axis=1)
    sb_h = jnp.concatenate([zh, sin, zeros], axis=1)
    tile = lambda a: jnp.tile(a, (1, HEADS_PER_LANE_GROUP))
    return tile(cos_h), tile(sa_h), tile(sb_h)


def _inproj(x2, g, w, tables, t, tm):
    n, d = x2.shape
    b = n // t
    tpb = t // tm
    dils = tuple(dil for _, dil in DIL_BRANCHES)
    assert all(tm % (dil * 16) == 0 for dil in dils)
    row = lambda width: pl.BlockSpec((tm, width), lambda i: (i, 0))
    tab = pl.BlockSpec((tm, LANES), lambda i: (i % tpb, 0))
    plain = [NA_WIDTH] * 3 + [MEM_WIDTH]
    dil_specs = [pl.BlockSpec((None, dil, tm // dil, DIL_WIDTH), lambda i: (i // tpb, 0, i % tpb, 0))
                 for dil in dils] * 3
    dil_shapes = [jax.ShapeDtypeStruct((b, dil, t // dil, DIL_WIDTH), BF16) for dil in dils] * 3
    outs = pl.pallas_call(
        functools.partial(_inproj_kernel, dils=dils),
        grid=(n // tm,),
        in_specs=[row(d), pl.BlockSpec((1, d), lambda i: (0, 0)),
                  pl.BlockSpec(w.shape, lambda i: (0, 0)), tab, tab, tab],
        out_specs=[row(wd) for wd in plain] + dil_specs,
        out_shape=[jax.ShapeDtypeStruct((n, wd), BF16) for wd in plain] + dil_shapes,
        scratch_shapes=[pltpu.VMEM((DIL_WIDTH // LANES, tm, LANES), F32)],
        compiler_params=_cparams(("parallel",)),
        name="inproj",
    )(x2, g, w, *tables)
    nd = len(dils)
    qa, ka, va, qm = outs[:4]
    return qa, ka, va, qm, outs[4:4 + nd], outs[4 + nd:4 + 2 * nd], outs[4 + 2 * nd:]


def _na_bias_table(rpb):
    c = jnp.arange(GRID_W)
    c0 = jnp.clip(c - NA_WIN_COLS // 2, 0, GRID_W - NA_WIN_COLS)
    kc = jnp.arange(GRID_W)
    valid = (kc[None, :] >= c0[:, None]) & (kc[None, :] < c0[:, None] + NA_WIN_COLS)
    coff = kc[None, :] - c[:, None] + (NA_WIN_COLS - 1)
    onehot = (coff[None] == jnp.arange(2 * NA_WIN_COLS - 1)[:, None, None]).astype(F32)
    by_col = jnp.einsum("hrd,dck->hrck", rpb.astype(F32), onehot, precision=lax.Precision.HIGHEST)
    by_col = jnp.where(valid[None, None], by_col * LOG2E, NEG)
    tab = jnp.stack([by_col[:, d:d + NA_WIN_ROWS] for d in range(NA_WIN_ROWS)], axis=1)
    tab = tab.transpose(1, 0, 3, 2, 4)
    return tab.reshape(NA_WIN_ROWS * NA_HEADS // HEADS_PER_LANE_GROUP, HEADS_PER_LANE_GROUP * GRID_W,
                       NA_WIN_ROWS * GRID_W)


def _na_kernel(q_ref, kp_ref, kc_ref, kn_ref, vp_ref, vc_ref, vn_ref, bias_ref, o_ref,
               kwin, vwin, *, rows):
    j = pl.program_id(1)
    blk = NA_WIN_ROWS * GRID_W
    for idx, (kr, vr) in enumerate(((kp_ref, vp_ref), (kc_ref, vc_ref), (kn_ref, vn_ref))):
        kwin[idx * blk:(idx + 1) * blk, :] = kr[...]
        vwin[idx * blk:(idx + 1) * blk, :] = vr[...]

    n_pairs = NA_HEADS // HEADS_PER_LANE_GROUP

    def one_row(i):
        q_rows = pl.ds(pl.multiple_of(i * GRID_W, GRID_W), GRID_W)

        def load():
            r = j * NA_WIN_ROWS + i
            r0 = jnp.clip(r - NA_WIN_ROWS // 2, 0, rows - NA_WIN_ROWS)
            dlt = r0 - r + (NA_WIN_ROWS - 1)
            start = pl.multiple_of((r0 - (j - 1) * NA_WIN_ROWS) * GRID_W, GRID_W)
            return (q_ref[q_rows, :], kwin[pl.ds(start, blk), :], vwin[pl.ds(start, blk), :],
                    lambda p: bias_ref[dlt * n_pairs + p])

        def store(o, lses):
            o_ref[q_rows, :] = o.astype(BF16)

        return load, store

    def row_group(ig, carry):
        _attention_pipelined([one_row(ig * ROWS_PER_TRIP + u) for u in range(ROWS_PER_TRIP)], NA_HEADS)
        return carry

    lax.fori_loop(0, NA_WIN_ROWS // ROWS_PER_TRIP, row_group, 0)


def _na_attention(q, k, v, bias):
    b, t, w = q.shape
    rows = t // GRID_W
    assert rows % NA_WIN_ROWS == 0 and rows >= NA_WIN_ROWS
    nj = rows // NA_WIN_ROWS
    blk = NA_WIN_ROWS * GRID_W
    cur = pl.BlockSpec((None, blk, w), lambda bi, j: (bi, j, 0))
    prev = pl.BlockSpec((None, blk, w), lambda bi, j: (bi, jnp.maximum(j - 1, 0), 0))
    nxt = pl.BlockSpec((None, blk, w), lambda bi, j: (bi, jnp.minimum(j + 1, nj - 1), 0))
    return pl.pallas_call(
        functools.partial(_na_kernel, rows=rows),
        grid=(b, nj),
        in_specs=[cur, prev, cur, nxt, prev, cur, nxt,
                  pl.BlockSpec(bias.shape, lambda bi, j: (0, 0, 0))],
        out_specs=cur,
        out_shape=jax.ShapeDtypeStruct((b, t, w), BF16),
        scratch_shapes=[pltpu.VMEM((3 * blk, w), BF16), pltpu.VMEM((3 * blk, w), BF16)],
        compiler_params=_cparams(("parallel", "parallel")),
        name="na_attn",
    )(q, k, k, k, v, v, v, bias)


def _dil_kernel(q_ref, kp_ref, kc_ref, kn_ref, vp_ref, vc_ref, vn_ref, o_ref, lse_ref,
                kwin, vwin, *, seg, tq, halo):
    i = pl.program_id(2)
    width = tq + 2 * halo
    kwin[0:halo, :] = kp_ref[...]
    kwin[halo:halo + tq, :] = kc_ref[...]
    kwin[halo + tq:width, :] = kn_ref[...]
    vwin[0:halo, :] = _values_with_ones(vp_ref[...])
    vwin[halo:halo + tq, :] = _values_with_ones(vc_ref[...])
    vwin[halo + tq:width, :] = _values_with_ones(vn_ref[...])

    sq = math.gcd(tq, 2 * halo)
    sw = sq + 2 * halo
    a = lax.broadcasted_iota(jnp.int32, (sq, sw), 0)
    c = lax.broadcasted_iota(jnp.int32, (sq, sw), 1)
    band = (c >= a) & (c <= a + 2 * halo)
    lo = lax.broadcasted_iota(jnp.int32, (sq, LANES), 1) < HEAD_DIM

    def sub_block(s):
        row0 = pl.multiple_of(s * sq, sq)

        def load():
            first_key = i * tq + s * sq - halo
            valid = band & (c >= -first_key) & (c < seg - first_key)
            negb = jnp.where(valid, 0.0, NEG).astype(F32)
            negb2 = jnp.concatenate([negb] * HEADS_PER_LANE_GROUP, axis=0)
            return (q_ref[pl.ds(row0, sq), :], kwin[pl.ds(row0, sw), :], vwin[pl.ds(row0, sw), :],
                    lambda p: negb2)

        def store(o, lses):
            o_ref[pl.ds(row0, sq), :] = o.astype(BF16)
            lse_ref[pl.ds(row0, sq), :] = jnp.concatenate(
                [jnp.where(lo, lses[2 * p], lses[2 * p + 1]) for p in range(DIL_HEADS // 2)], axis=-1)

        return load, store

    n_sub = tq // sq
    per_trip = math.gcd(n_sub, ROWS_PER_TRIP)

    def sub_block_group(sg, carry):
        _attention_pipelined([sub_block(sg * per_trip + u) for u in range(per_trip)], DIL_HEADS, skew=0)
        return carry

    lax.fori_loop(0, n_sub // per_trip, sub_block_group, 0)


def _dilated_branch(q, k, v, n_side, tq_max):
    b, dil, seg, w = q.shape
    halo = n_side
    assert seg % halo == 0 and halo % 16 == 0
    tq = math.gcd(tq_max, seg)
    assert tq % halo == 0
    hb = tq // halo
    nhalo = seg // halo
    cur = pl.BlockSpec((None, None, tq, w), lambda bi, r, i: (bi, r, i, 0))
    prev = pl.BlockSpec((None, None, halo, w), lambda bi, r, i: (bi, r, jnp.maximum(i * hb - 1, 0), 0))
    nxt = pl.BlockSpec((None, None, halo, w),
                       lambda bi, r, i: (bi, r, jnp.minimum((i + 1) * hb, nhalo - 1), 0))
    return pl.pallas_call(
        functools.partial(_dil_kernel, seg=seg, tq=tq, halo=halo),
        grid=(b, dil, seg // tq),
        in_specs=[cur, prev, cur, nxt, prev, cur, nxt],
        out_specs=[cur, cur],
        out_shape=[jax.ShapeDtypeStruct((b, dil, seg, w), BF16),
                   jax.ShapeDtypeStruct((b, dil, seg, w), F32)],
        scratch_shapes=[pltpu.VMEM((tq + 2 * halo, w), BF16), pltpu.VMEM((tq + 2 * halo, 2 * w), BF16)],
        compiler_params=_cparams(("parallel", "parallel", "parallel")),
        name=f"dilated_d{dil}",
    )(q, k, k, k, v, v, v)


def _memkv_kernel(mem_ref, g_ref, w_ref, k_ref, v_ref):
    h = _rms(mem_ref[...], g_ref[...]).astype(BF16)
    kv = jnp.dot(h, w_ref[...], preferred_element_type=F32)
    k_ref[...] = kv[:, :MEM_WIDTH].astype(BF16)
    v_ref[...] = kv[:, MEM_WIDTH:].astype(BF16)


def _mem_kv(mem, g, w):
    b, m, d = mem.shape
    blk = lambda width: pl.BlockSpec((None, m, width), lambda bi: (bi, 0, 0))
    return pl.pallas_call(
        _memkv_kernel,
        grid=(b,),
        in_specs=[blk(d), pl.BlockSpec((1, d), lambda bi: (0, 0)), pl.BlockSpec(w.shape, lambda bi: (0, 0))],
        out_specs=[blk(MEM_WIDTH), blk(MEM_WIDTH)],
        out_shape=[jax.ShapeDtypeStruct((b, m, MEM_WIDTH), BF16)] * 2,
        compiler_params=_cparams(("parallel",)),
        name="mem_kv",
    )(mem, g, w)


def _memattn_kernel(q_ref, k_ref, v_ref, o_ref):
    tq = q_ref.shape[0]
    sq = math.gcd(tq, LANES)
    v_aug = _values_with_ones(v_ref[...])

    def sub_block(s):
        rows = slice(s * sq, (s + 1) * sq)

        def store(o, lses):
            o_ref[rows, :] = o.astype(BF16)

        return (lambda: (q_ref[rows, :], k_ref[...], v_aug, lambda p: 0.0)), store

    _attention_pipelined([sub_block(s) for s in range(tq // sq)], MEM_HEADS)


def _mem_attention(q, k, v, tq):
    b, t, w = q.shape
    m = k.shape[1]
    qs = pl.BlockSpec((None, tq, w), lambda bi, i: (bi, i, 0))
    ks = pl.BlockSpec((None, m, w), lambda bi, i: (bi, 0, 0))
    return pl.pallas_call(
        _memattn_kernel,
        grid=(b, t // tq),
        in_specs=[qs, ks, ks],
        out_specs=qs,
        out_shape=jax.ShapeDtypeStruct((b, t, w), BF16),
        compiler_params=_cparams(("parallel", "parallel")),
        name="mem_attn",
    )(q, k, v)


def _token_order(ref, scr):
    dil = ref.shape[0]
    if dil == 1:
        return ref[0].astype(F32)
    groups, tm, _ = scr.shape
    for r in range(dil):
        for g in range(groups):
            scr[g, pl.ds(r, tm // dil, stride=dil), :] = ref[r, :, g * LANES:(g + 1) * LANES].astype(F32)
    return jnp.concatenate([scr[g] for g in range(groups)], axis=-1)


def _branch_mix(ob_refs, lse_refs, o_scrs, l_scrs):
    lses = [_token_order(r, s) for r, s in zip(lse_refs, l_scrs)]
    mx = functools.reduce(jnp.maximum, lses)
    es = [jnp.exp2(l - mx) for l in lses]
    den = functools.reduce(lambda p, q: p + q, es)
    acc = None
    for e, ob, scr in zip(es, ob_refs, o_scrs):
        term = (e / den) * _token_order(ob, scr)
        acc = term if acc is None else acc + term
    return acc


def _outproj_core(oa_ref, ob_refs, lse_refs, om_ref, x_ref, gm_ref, w_ref, o_scrs, l_scrs):
    gm = gm_ref[...]
    e0, e1 = NA_WIDTH, NA_WIDTH + DIL_WIDTH
    ya = _rms(oa_ref[...].astype(F32), gm[:, :e0]).astype(BF16)
    yb = _rms(_branch_mix(ob_refs, lse_refs, o_scrs, l_scrs), gm[:, e0:e1]).astype(BF16)
    ym = _rms(om_ref[...].astype(F32), gm[:, e1:]).astype(BF16)
    y = jnp.concatenate([ya, yb, ym], axis=-1)
    return x_ref[...] + jnp.dot(y, w_ref[...], preferred_element_type=F32)


def _split_scratch(scrs, dils):
    it = iter(scrs)
    o_scrs = [next(it) if dil > 1 else None for dil in dils]
    l_scrs = [next(it) if dil > 1 else None for dil in dils]
    return o_scrs, l_scrs


def _outproj_kernel(oa_ref, ob1, ob2, ob3, l1, l2, l3, om_ref, x_ref, gm_ref, w_ref, xo_ref, *scrs, dils):
    xo_ref[...] = _outproj_core(oa_ref, (ob1, ob2, ob3), (l1, l2, l3), om_ref, x_ref, gm_ref, w_ref,
                                *_split_scratch(scrs, dils))


def _store_row_tiles(ref, val):
    m, d = val.shape
    sub = ref.shape[0] // m
    tw = d // sub
    for j in range(sub):
        ref[pl.ds(j, m, stride=sub), :] = val[:, j * tw:(j + 1) * tw]


def _outproj_router_kernel(oa_ref, ob1, ob2, ob3, l1, l2, l3, om_ref, x_ref, gm_ref, w_ref,
                           gf_ref, wr_ref, xo_ref, h_ref, route_ref, *scrs, n_experts, dils):
    xn = _outproj_core(oa_ref, (ob1, ob2, ob3), (l1, l2, l3), om_ref, x_ref, gm_ref, w_ref,
                       *_split_scratch(scrs, dils))
    xo_ref[...] = xn
    h = _rms(xn, gf_ref[...])
    _store_row_tiles(h_ref, h)
    logits = jnp.dot(h.astype(BF16), wr_ref[...], preferred_element_type=F32)
    lane = lax.broadcasted_iota(jnp.int32, logits.shape, 1)
    lg = jnp.where(lane < n_experts, logits, NEG)
    m1 = jnp.max(lg, axis=-1, keepdims=True)
    i1 = jnp.min(jnp.where(lg == m1, lane, LANES), axis=-1, keepdims=True)
    lg2 = jnp.where(lane == i1, NEG, lg)
    m2 = jnp.max(lg2, axis=-1, keepdims=True)
    i2 = jnp.min(jnp.where(lg2 == m2, lane, LANES), axis=-1, keepdims=True)
    e2 = jnp.exp(m2 - m1)
    g1 = 1.0 / (1.0 + e2)
    g2 = e2 / (1.0 + e2)
    route = jnp.where(lane == 0, i1.astype(F32), 0.0)
    route = jnp.where(lane == 1, i2.astype(F32), route)
    route = jnp.where(lane == 2, g1, route)
    route = jnp.where(lane == 3, g2, route)
    route_ref[...] = route


def _outproj(oa, obs, lses, om, x2, gm, w, t, tm, router=None):
    n, d = x2.shape
    tpb = t // tm
    row = lambda width: pl.BlockSpec((tm, width), lambda i: (i, 0))
    full = lambda a: pl.BlockSpec(a.shape, lambda i: (0,) * a.ndim)
    grouped = lambda a: pl.BlockSpec((None, a.shape[1], tm // a.shape[1], a.shape[3]),
                                     lambda i: (i // tpb, 0, i % tpb, 0))
    dils = tuple(a.shape[1] for a in obs)
    args = [oa, *obs, *lses, om, x2, gm, w]
    in_specs = ([row(NA_WIDTH)] + [grouped(a) for a in obs] + [grouped(a) for a in lses]
                + [row(MEM_WIDTH), row(d), full(gm), full(w)])
    scratch = [pltpu.VMEM((DIL_WIDTH // LANES, tm, LANES), F32)] * (2 * sum(dil > 1 for dil in dils))
    if router is None:
        return pl.pallas_call(
            functools.partial(_outproj_kernel, dils=dils),
            grid=(n // tm,), in_specs=in_specs, out_specs=row(d),
            out_shape=jax.ShapeDtypeStruct((n, d), F32), scratch_shapes=scratch,
            compiler_params=_cparams(("parallel",)), name="outproj",
        )(*args)
    gf, wr, n_experts = router
    sub = ROW_TILE_SUBLANES
    return pl.pallas_call(
        functools.partial(_outproj_router_kernel, n_experts=n_experts, dils=dils),
        grid=(n // tm,), in_specs=in_specs + [full(gf), full(wr)],
        out_specs=[row(d), pl.BlockSpec((tm * sub, d // sub), lambda i: (i, 0)), row(LANES)],
        out_shape=[jax.ShapeDtypeStruct((n, d), F32), jax.ShapeDtypeStruct((n * sub, d // sub), F32),
                   jax.ShapeDtypeStruct((n, LANES), F32)],
        scratch_shapes=scratch,
        compiler_params=_cparams(("parallel",)), name="outproj_router",
    )(*args, gf, wr)


def _swiglu_partial(h, wg_ref, wu_ref, wd_ref, act_scr, emit):
    tf = wg_ref.shape[-1]
    chunk = math.gcd(MXU_COLS, tf)
    for c in range(tf // chunk):
        sl = slice(c * chunk, (c + 1) * chunk)
        gate = jnp.dot(h, wg_ref[:, sl], preferred_element_type=F32)
        up = jnp.dot(h, wu_ref[:, sl], preferred_element_type=F32)
        act_scr[:, sl] = ((gate / (1.0 + jnp.exp(-gate))) * up).astype(BF16)
    act = act_scr[...]
    for c in range(wd_ref.shape[-1] // MXU_COLS):
        emit(c, jnp.dot(act, wd_ref[:, c * MXU_COLS:(c + 1) * MXU_COLS], preferred_element_type=F32))


def _dense_ffn_kernel(x_ref, g_ref, wg_ref, wu_ref, wd_ref, o_ref, h_scr, act_scr):
    @pl.when(pl.program_id(1) == 0)
    def _():
        x = x_ref[...]
        h_scr[...] = _rms(x, g_ref[...]).astype(BF16)
        o_ref[...] = x

    def accumulate(c, part):
        o_ref[:, c * MXU_COLS:(c + 1) * MXU_COLS] += part

    _swiglu_partial(h_scr[...], wg_ref, wu_ref, wd_ref, act_scr, accumulate)


def _dense_ffn(x2, g, wg, wu, wd, li, tm, tf):
    n, d = x2.shape
    f = wg.shape[2]
    row = pl.BlockSpec((tm, d), lambda i, k: (i, 0))
    return pl.pallas_call(
        _dense_ffn_kernel,
        grid=(n // tm, f // tf),
        in_specs=[row, pl.BlockSpec((1, d), lambda i, k: (0, 0)),
                  pl.BlockSpec((None, d, tf), lambda i, k: (li, 0, k)),
                  pl.BlockSpec((None, d, tf), lambda i, k: (li, 0, k)),
                  pl.BlockSpec((None, tf, d), lambda i, k: (li, k, 0))],
        out_specs=row,
        out_shape=jax.ShapeDtypeStruct((n, d), F32),
        scratch_shapes=[pltpu.VMEM((tm, d), BF16), pltpu.VMEM((tm, tf), BF16)],
        compiler_params=_cparams(("parallel", "arbitrary")),
        name="dense_ffn",
    )(x2, g, wg, wu, wd)


def _moe_ffn_kernel(be_ref, nused_ref, xb_ref, wg_ref, wu_ref, wd_ref, y_ref, act_scr):
    j = pl.program_id(0)
    k = pl.program_id(1)
    used = j < nused_ref[0]
    mb = act_scr.shape[0]
    sub = xb_ref.shape[0] // mb
    tw = xb_ref.shape[1]

    @pl.when(k == 0)
    def _():
        y_ref[...] = jnp.zeros_like(y_ref)

    @pl.when(used)
    def _():
        h = jnp.concatenate([xb_ref[pl.ds(c, mb, stride=sub), :].astype(BF16) for c in range(sub)], axis=-1)

        def accumulate(c, part):
            step = min(tw, MXU_COLS)
            for u in range(MXU_COLS // step):
                col = c * MXU_COLS + u * step
                y_ref[pl.ds(col // tw, mb, stride=sub), col % tw:col % tw + step] += (
                    part[:, u * step:(u + 1) * step])

        _swiglu_partial(h, wg_ref, wu_ref, wd_ref, act_scr, accumulate)


def _moe_ffn(buf, block_expert, n_used, wg, wu, wd, li, mb, tf):
    d, f = wg.shape[2], wg.shape[3]
    sub = ROW_TILE_SUBLANES
    tw = d // sub
    p = buf.shape[0] // sub
    nk = f // tf
    kk = lambda j, k, be, nu: jnp.where(j < nu[0], k, nk - 1)
    row = pl.BlockSpec((mb * sub, tw), lambda j, k, be, nu: (j, 0))
    grid_spec = pltpu.PrefetchScalarGridSpec(
        num_scalar_prefetch=2,
        grid=(p // mb, nk),
        in_specs=[row,
                  pl.BlockSpec((None, None, d, tf), lambda j, k, be, nu: (li, be[j], 0, kk(j, k, be, nu))),
                  pl.BlockSpec((None, None, d, tf), lambda j, k, be, nu: (li, be[j], 0, kk(j, k, be, nu))),
                  pl.BlockSpec((None, None, tf, d), lambda j, k, be, nu: (li, be[j], kk(j, k, be, nu), 0))],
        out_specs=row,
        scratch_shapes=[pltpu.VMEM((mb, tf), BF16)],
    )
    return pl.pallas_call(
        _moe_ffn_kernel,
        grid_spec=grid_spec,
        out_shape=jax.ShapeDtypeStruct((p * sub, tw), F32),
        compiler_params=_cparams(("parallel", "arbitrary")),
        name="moe_ffn",
    )(block_expert, n_used, buf, wg, wu, wd)


ISSUE_UNROLL = 8


def _issue_tile_copies(chunk, copy_fn):
    def body(trip, carry):
        for u in range(ISSUE_UNROLL):
            copy_fn(trip, u).start(priority=u % 2)
        return carry

    lax.fori_loop(0, chunk // ISSUE_UNROLL, body, 0)


def _tile(ref, idx, sub):
    return ref.at[pl.ds(pl.multiple_of(idx * sub, sub), sub)]


def _dispatch_kernel(slot_ref, h_ref, out_ref, zero_scr, sem, *, n_tok_steps, sub):
    i = pl.program_id(0)
    n_copies = slot_ref.shape[-1]
    dst = lambda trip, u: _tile(out_ref, slot_ref[0, 0, trip * ISSUE_UNROLL + u], sub)
    tok = lambda trip, u: trip * (ISSUE_UNROLL // TOP_K) + u // TOP_K

    @pl.when(i == 0)
    def _():
        zero_scr[...] = jnp.zeros_like(zero_scr)

    @pl.when(i < n_tok_steps)
    def _():
        _issue_tile_copies(n_copies, lambda trip, u: pltpu.make_async_copy(
            _tile(h_ref, tok(trip, u), sub), dst(trip, u), sem))

    @pl.when(i >= n_tok_steps)
    def _():
        _issue_tile_copies(n_copies, lambda trip, u: pltpu.make_async_copy(zero_scr, dst(trip, u), sem))

    for _ in range(TOP_K):
        pltpu.make_async_copy(h_ref, out_ref.at[pl.ds(0, h_ref.shape[0])], sem).wait()


def _dispatch_rows(h, slot, tm, sub):
    n = h.shape[0] // sub
    tw = h.shape[1]
    p = slot.shape[0]
    n_copies = tm * TOP_K
    assert p % n_copies == 0 and n % tm == 0 and n_copies % ISSUE_UNROLL == 0
    n_tok_steps = n // tm
    return pl.pallas_call(
        functools.partial(_dispatch_kernel, n_tok_steps=n_tok_steps, sub=sub),
        grid=(p // n_copies,),
        in_specs=[pl.BlockSpec((1, 1, n_copies), lambda i: (i, 0, 0), memory_space=pltpu.SMEM),
                  pl.BlockSpec((tm * sub, tw), lambda i: (jnp.minimum(i, n_tok_steps - 1), 0))],
        out_specs=pl.BlockSpec(memory_space=pl.ANY),
        out_shape=jax.ShapeDtypeStruct((p * sub, tw), h.dtype),
        scratch_shapes=[pltpu.VMEM((sub, tw), h.dtype), pltpu.SemaphoreType.DMA(())],
        compiler_params=_cparams(("arbitrary",)),
        name="dispatch_rows",
    )(slot.reshape(p // n_copies, 1, n_copies), h)


def _combine_kernel(cur_ref, nxt_ref, x_ref, route_ref, y_ref, *rest, final_norm, sub):
    o_ref, ybuf, sem = rest[-3:]
    i = pl.program_id(0)
    tm = x_ref.shape[0]
    n_copies = tm * TOP_K
    par = i % 2

    def fetch(idx_ref, buf):
        def copy(trip, u):
            tok = trip * (ISSUE_UNROLL // TOP_K) + u // TOP_K
            return pltpu.make_async_copy(_tile(y_ref, idx_ref[0, 0, trip * ISSUE_UNROLL + u], sub),
                                         _tile(ybuf.at[buf, u % TOP_K], tok, sub), sem.at[buf])
        _issue_tile_copies(n_copies, copy)

    @pl.when(i == 0)
    def _():
        fetch(cur_ref, 0)

    @pl.when(i + 1 < pl.num_programs(0))
    def _():
        fetch(nxt_ref, 1 - par)

    pltpu.make_async_copy(ybuf.at[par], ybuf.at[par], sem.at[par]).wait()

    route = route_ref[...]
    gates = [route[:, TOP_K + k:TOP_K + k + 1] for k in range(TOP_K)]
    tw = x_ref.shape[1] // sub
    for j in range(sub):
        sl = slice(j * tw, (j + 1) * tw)
        mix = None
        for k in range(TOP_K):
            term = gates[k] * ybuf[par, k, pl.ds(j, tm, stride=sub), :]
            mix = term if mix is None else mix + term
        o_ref[:, sl] = x_ref[:, sl] + mix
    if final_norm:
        o_ref[...] = _rms(o_ref[...], rest[0][...])


def _combine(x2, y, dest, route, tm, g_final=None):
    n, d = x2.shape
    sub = ROW_TILE_SUBLANES
    n_copies = tm * TOP_K
    steps = n // tm
    assert n_copies % ISSUE_UNROLL == 0 and ISSUE_UNROLL % TOP_K == 0
    dest3 = dest.reshape(steps, 1, n_copies)
    row = pl.BlockSpec((tm, d), lambda i: (i, 0))
    in_specs = [pl.BlockSpec((1, 1, n_copies), lambda i: (i, 0, 0), memory_space=pltpu.SMEM),
                pl.BlockSpec((1, 1, n_copies), lambda i: (jnp.minimum(i + 1, steps - 1), 0, 0),
                             memory_space=pltpu.SMEM),
                row, pl.BlockSpec((tm, LANES), lambda i: (i, 0)), pl.BlockSpec(memory_space=pl.ANY)]
    args = [dest3, dest3, x2, route, y]
    if g_final is not None:
        in_specs.append(pl.BlockSpec((1, d), lambda i: (0, 0)))
        args.append(g_final)
    return pl.pallas_call(
        functools.partial(_combine_kernel, final_norm=g_final is not None, sub=sub),
        grid=(steps,), in_specs=in_specs, out_specs=row,
        out_shape=jax.ShapeDtypeStruct((n, d), F32),
        scratch_shapes=[pltpu.VMEM((2, TOP_K, tm * sub, d // sub), F32), pltpu.SemaphoreType.DMA((2,))],
        compiler_params=_cparams(("arbitrary",)), name="moe_combine",
    )(*args)


def _final_norm_kernel(x_ref, g_ref, o_ref):
    o_ref[...] = _rms(x_ref[...], g_ref[...])


def _final_norm(x2, g, tm):
    n, d = x2.shape
    row = pl.BlockSpec((tm, d), lambda i: (i, 0))
    return pl.pallas_call(
        _final_norm_kernel, grid=(n // tm,), in_specs=[row, pl.BlockSpec((1, d), lambda i: (0, 0))],
        out_specs=row, out_shape=jax.ShapeDtypeStruct((n, d), F32),
        compiler_params=_cparams(("parallel",)), name="final_norm",
    )(x2, g)


def _routing_plan(route, n_experts, mb):
    n = route.shape[0]
    n_assign = n * TOP_K
    flat_e = route[:, :TOP_K].astype(jnp.int32).reshape(-1)
    onehot = (flat_e[:, None] == jnp.arange(n_experts)[None, :]).astype(jnp.int32)
    csum = jnp.cumsum(onehot, axis=0)
    rank = jnp.sum(csum * onehot, axis=1) - 1
    counts = csum[-1]
    padded = (counts + mb - 1) // mb * mb
    pend = jnp.cumsum(padded)
    pstart = pend - padded
    dest = jnp.sum(pstart[None, :] * onehot, axis=1) + rank
    assert n_assign % mb == 0
    n_blocks = n_assign // mb + n_experts
    block_first = jnp.arange(n_blocks) * mb
    block_expert = jnp.minimum(jnp.sum(block_first[:, None] >= pend[None, :], axis=1), n_experts - 1)
    n_used = (pend[-1] // mb).reshape(1)
    pads = padded - counts
    cpad = jnp.cumsum(pads)
    i = jnp.arange(n_experts * mb)
    grp = jnp.sum(i[:, None] >= cpad[None, :], axis=1)
    grp_hot = (grp[:, None] == jnp.arange(n_experts + 1)[None, :]).astype(jnp.int32)
    first_free = jnp.concatenate([pstart + counts, pend[-1:]])
    before = jnp.concatenate([jnp.zeros((1,), cpad.dtype), cpad])
    free = jnp.sum(grp_hot * (first_free - before)[None, :], axis=1) + i
    slot = jnp.concatenate([dest, free]).astype(jnp.int32)
    return dest.astype(jnp.int32), slot, block_expert.astype(jnp.int32), n_used.astype(jnp.int32)


def _tiles(n, t):
    return dict(tm_in=math.gcd(1024, t), tm_proj=math.gcd(512, t), tm_ffn=math.gcd(1024, n), tf=1792,
                tq_dil=1024, tq_mem=math.gcd(512, t), moe_block=math.gcd(1024, n))


@jax.jit
def _forward(x, mem, g_mix_norm, w_in, rpb, g_mem_norm, w_mem_kv, g_mix_out, w_out,
             g_ffn_norm, w_dense_gate, w_dense_up, w_dense_down, w_router,
             w_moe_gate, w_moe_up, w_moe_down, g_final):
    b, t, d = x.shape
    n = b * t
    depth = w_in.shape[0]
    n_experts = w_router.shape[-1]
    ts = _tiles(n, t)
    tf = math.gcd(ts["tf"], w_dense_gate.shape[-1])
    tables = _rope_tables(t)
    dense_w = [w.astype(BF16) for w in (w_dense_gate, w_dense_up, w_dense_down)]
    moe_w = [w.astype(BF16) for w in (w_moe_gate, w_moe_up, w_moe_down)]
    row1 = lambda a: a.reshape(1, -1)
    x2 = x.reshape(n, d)
    out = None
    sub = ROW_TILE_SUBLANES
    for layer in range(depth):
        qa, ka, va, qm, qbs, kbs, vbs = _inproj(x2, row1(g_mix_norm[layer]), w_in[layer].astype(BF16),
                                                tables, t, ts["tm_in"])
        shp = lambda a: a.reshape(b, t, a.shape[-1])
        oa = _na_attention(shp(qa), shp(ka), shp(va), _na_bias_table(rpb[layer])).reshape(n, NA_WIDTH)
        obs, lses = [], []
        for (window, dil), qb, kb, vb in zip(DIL_BRANCHES, qbs, kbs, vbs):
            o, lse = _dilated_branch(qb, kb, vb, window // (2 * dil), ts["tq_dil"])
            obs.append(o)
            lses.append(lse)
        km, vm = _mem_kv(mem, row1(g_mem_norm[layer]), w_mem_kv[layer].astype(BF16))
        om = _mem_attention(shp(qm), km, vm, ts["tq_mem"]).reshape(n, MEM_WIDTH)
        gm, wo = row1(g_mix_out[layer]), w_out[layer].astype(BF16)
        i = layer // 2
        if layer % 2 == 0:
            x2 = _outproj(oa, obs, lses, om, x2, gm, wo, t, ts["tm_proj"])
            x2 = _dense_ffn(x2, row1(g_ffn_norm[layer]), *dense_w, i, ts["tm_ffn"], tf)
            if layer == depth - 1:
                out = _final_norm(x2, row1(g_final), ts["tm_proj"])
        else:
            wr = jnp.pad(w_router[i], ((0, 0), (0, LANES - n_experts))).astype(BF16)
            x2, h, route = _outproj(oa, obs, lses, om, x2, gm, wo, t, ts["tm_proj"],
                                    router=(row1(g_ffn_norm[layer]), wr, n_experts))
            mb = ts["moe_block"]
            dest, slot, block_expert, n_used = _routing_plan(route, n_experts, mb)
            buf = _dispatch_rows(h, slot, ts["tm_proj"], sub)
            y = _moe_ffn(buf, block_expert, n_used, *moe_w, i, mb, tf)
            if layer == depth - 1:
                out = _combine(x2, y, dest, route, ts["tm_proj"], g_final=row1(g_final))
            else:
                x2 = _combine(x2, y, dest, route, ts["tm_proj"])
    return out.reshape(b, t, d)


def kernel(x, mem, g_mix_norm, w_in, rpb, g_mem_norm, w_mem_kv, g_mix_out, w_out, g_ffn_norm,
           w_dense_gate, w_dense_up, w_dense_down, w_router, w_moe_gate, w_moe_up, w_moe_down, g_final):
    return _forward(x, mem, g_mix_norm, w_in, rpb, g_mem_norm, w_mem_kv, g_mix_out, w_out, g_ffn_norm,
                    w_dense_gate, w_dense_up, w_dense_down, w_router, w_moe_gate, w_moe_up, w_moe_down,
                    g_final)
```

```python
import functools
import math

import jax
import jax.numpy as jnp
from jax import lax
from jax.experimental import pallas as pl
from jax.experimental.pallas import tpu as pltpu

HEAD_DIM = 64
NA_HEADS = 6
DIL_HEADS = 6
MEM_HEADS = 4
NA_WIDTH = NA_HEADS * HEAD_DIM
DIL_WIDTH = DIL_HEADS * HEAD_DIM
MEM_WIDTH = MEM_HEADS * HEAD_DIM
GRID_W = 64
NA_WIN_ROWS = 8
NA_WIN_COLS = 16
DIL_BRANCHES = ((128, 1), (512, 4), (2048, 16))
ROPE_THETA = 500000.0
ROT_DIM = HEAD_DIM // 4
TOP_K = 2
RMS_EPS = 1e-6
ATTN_SCALE = HEAD_DIM ** -0.5
LOG2E = math.log2(math.e)
Q_SCALE = ATTN_SCALE * LOG2E

LANES = 128
MXU_COLS = 256
HEADS_PER_LANE_GROUP = LANES // HEAD_DIM
NEG = -1e30
ROWS_PER_TRIP = 8
ROW_TILE_SUBLANES = 8
VMEM_LIMIT = 56 * 1024 * 1024

F32 = jnp.float32
BF16 = jnp.bfloat16


def _cparams(sem):
    return pltpu.CompilerParams(dimension_semantics=sem, vmem_limit_bytes=VMEM_LIMIT)


def _rms(x, g):
    ms = jnp.mean(x * x, axis=-1, keepdims=True)
    return x * lax.rsqrt(ms + RMS_EPS) * g


def _lane_group(a, p):
    return a[:, p * LANES:(p + 1) * LANES]


def _attn_scores(q, k, bias_fn, n_heads):
    m = q.shape[0]
    assert HEADS_PER_LANE_GROUP == 2
    lo = lax.broadcasted_iota(jnp.int32, (m, LANES), 1) < HEAD_DIM
    s_parts = []
    for p in range(n_heads // HEADS_PER_LANE_GROUP):
        qp = _lane_group(q, p)
        zero = jnp.zeros_like(qp)
        q2 = jnp.concatenate([jnp.where(lo, qp, zero), jnp.where(lo, zero, qp)], axis=0)
        s_parts.append(lax.dot_general(q2, _lane_group(k, p), (((1,), (1,)), ((), ())),
                                       preferred_element_type=F32) + bias_fn(p))
    return jnp.concatenate(s_parts, axis=0)


def _attn_softmax(s):
    mx = jnp.max(s, axis=-1, keepdims=True)
    return jnp.exp2((s - mx).astype(BF16)), mx


def _values_with_ones(v):
    ones = jnp.ones((v.shape[0], LANES), v.dtype)
    return jnp.concatenate([blk for p in range(v.shape[1] // LANES) for blk in (_lane_group(v, p), ones)],
                           axis=-1)


def _attn_values(eb, mx, v, n_heads):
    m = eb.shape[0] // n_heads
    n_pairs = n_heads // HEADS_PER_LANE_GROUP
    with_ones = v.shape[1] == n_pairs * 2 * LANES
    lo = lax.broadcasted_iota(jnp.int32, (m, LANES), 1) < HEAD_DIM
    if not with_ones:
        den_all = jnp.sum(eb.astype(F32), axis=-1, keepdims=True)
    outs, lses = [], []
    for p in range(n_pairs):
        rows = slice(p * 2 * m, (p + 1) * 2 * m)
        if with_ones:
            both = jnp.dot(eb[rows], v[:, p * 2 * LANES:(p + 1) * 2 * LANES], preferred_element_type=F32)
            num, den = both[:, :LANES], both[:, LANES:]
        else:
            num, den = jnp.dot(eb[rows], _lane_group(v, p), preferred_element_type=F32), den_all[rows]
        o2 = num / den
        outs.append(jnp.where(lo, o2[:m], o2[m:]))
        lse = mx[rows] + jnp.log2(den)
        lses += [lse[:m], lse[m:]]
    return jnp.concatenate(outs, axis=-1), lses


def _attention_pipelined(items, n_heads, skew=1):
    n = len(items)
    loaded, scores, soft = {}, {}, {}
    for step in range(n + 2 * skew):
        if step < n:
            loaded[step] = items[step][0]()
            q, k, _, bias_fn = loaded[step]
            scores[step] = _attn_scores(q, k, bias_fn, n_heads)
        if 0 <= step - skew < n:
            soft[step - skew] = _attn_softmax(scores.pop(step - skew))
        if 0 <= step - 2 * skew < n:
            eb, mx = soft.pop(step - 2 * skew)
            items[step - 2 * skew][1](*_attn_values(eb, mx, loaded.pop(step - 2 * skew)[2], n_heads))


def _inproj_kernel(x_ref, g_ref, w_ref, cos_ref, sa_ref, sb_ref,
                   qa_ref, ka_ref, va_ref, qm_ref, *rest, dils):
    dil_refs, scr = rest[:-1], rest[-1]
    tm = x_ref.shape[0]
    h = _rms(x_ref[...], g_ref[...]).astype(BF16)

    def emit_dilated(val, refs):
        groups = val.shape[1] // LANES
        for g in range(groups):
            scr[g] = val[:, g * LANES:(g + 1) * LANES]
        for dil, ref in zip(dils, refs):
            if dil == 1:
                ref[0] = val.astype(BF16)
                continue
            for r in range(dil):
                for g in range(groups):
                    ref[r, :, g * LANES:(g + 1) * LANES] = (
                        scr[g, pl.ds(r, tm // dil, stride=dil), :].astype(BF16))

    def proj_pair(c0, width_a, width_b):
        both = jnp.dot(h, w_ref[:, c0:c0 + width_a + width_b], preferred_element_type=F32)
        return both[:, :width_a], both[:, width_a:]

    def rope(a):
        cos, sa, sb = cos_ref[...], sa_ref[...], sb_ref[...]
        half = ROT_DIM // 2
        parts = []
        for gidx in range(a.shape[1] // LANES):
            xg = a[:, gidx * LANES:(gidx + 1) * LANES]
            parts.append(xg * cos + pltpu.roll(xg, LANES - half, 1) * sa + pltpu.roll(xg, half, 1) * sb)
        return jnp.concatenate(parts, axis=-1)

    nd = len(dils)
    assert all(wd % MXU_COLS == 0 for wd in (2 * NA_WIDTH, NA_WIDTH + DIL_WIDTH, 2 * DIL_WIDTH))
    kb, vb = proj_pair(3 * NA_WIDTH + DIL_WIDTH, DIL_WIDTH, DIL_WIDTH)
    emit_dilated(rope(kb), dil_refs[nd:2 * nd])
    emit_dilated(vb, dil_refs[2 * nd:3 * nd])
    va, qb = proj_pair(2 * NA_WIDTH, NA_WIDTH, DIL_WIDTH)
    va_ref[...] = va.astype(BF16)
    emit_dilated(rope(qb * Q_SCALE), dil_refs[0:nd])
    qa, ka = proj_pair(0, NA_WIDTH, NA_WIDTH)
    qa_ref[...] = (qa * Q_SCALE).astype(BF16)
    ka_ref[...] = ka.astype(BF16)
    c = 3 * NA_WIDTH + 3 * DIL_WIDTH
    qm_ref[...] = (jnp.dot(h, w_ref[:, c:c + MEM_WIDTH], preferred_element_type=F32) * Q_SCALE).astype(BF16)


def _rope_tables(t):
    half = ROT_DIM // 2
    inv_freq = ROPE_THETA ** (-jnp.arange(0, ROT_DIM, 2, dtype=F32) / ROT_DIM)
    ang = jnp.arange(t, dtype=F32)[:, None] * inv_freq[None, :]
    cos, sin = jnp.cos(ang), jnp.sin(ang)
    ones = jnp.ones((t, HEAD_DIM - ROT_DIM), F32)
    zeros = jnp.zeros((t, HEAD_DIM - ROT_DIM), F32)
    zh = jnp.zeros((t, half), F32)
    cos_h = jnp.concatenate([cos, cos, ones], axis=1)
    sa_h = jnp.concatenate([-sin, zh, zeros], axis=1)
    sb_h = jnp.concatenate([zh, sin, zeros], axis=1)
    tile = lambda a: jnp.tile(a, (1, HEADS_PER_LANE_GROUP))
    return tile(cos_h), tile(sa_h), tile(sb_h)


def _inproj(x2, g, w, tables, t, tm):
    n, d = x2.shape
    b = n // t
    tpb = t // tm
    dils = tuple(dil for _, dil in DIL_BRANCHES)
    assert all(tm % (dil * 16) == 0 for dil in dils)
    row = lambda width: pl.BlockSpec((tm, width), lambda i: (i, 0))
    tab = pl.BlockSpec((tm, LANES), lambda i: (i % tpb, 0))
    plain = [NA_WIDTH] * 3 + [MEM_WIDTH]
    dil_specs = [pl.BlockSpec((None, dil, tm // dil, DIL_WIDTH), lambda i: (i // tpb, 0, i % tpb, 0))
                 for dil in dils] * 3
    dil_shapes = [jax.ShapeDtypeStruct((b, dil, t // dil, DIL_WIDTH), BF16) for dil in dils] * 3
    outs = pl.pallas_call(
        functools.partial(_inproj_kernel, dils=dils),
        grid=(n // tm,),
        in_specs=[row(d), pl.BlockSpec((1, d), lambda i: (0, 0)),
                  pl.BlockSpec(w.shape, lambda i: (0, 0)), tab, tab, tab],
        out_specs=[row(wd) for wd in plain] + dil_specs,
        out_shape=[jax.ShapeDtypeStruct((n, wd), BF16) for wd in plain] + dil_shapes,
        scratch_shapes=[pltpu.VMEM((DIL_WIDTH // LANES, tm, LANES), F32)],
        compiler_params=_cparams(("parallel",)),
        name="inproj",
    )(x2, g, w, *tables)
    nd = len(dils)
    qa, ka, va, qm = outs[:4]
    return qa, ka, va, qm, outs[4:4 + nd], outs[4 + nd:4 + 2 * nd], outs[4 + 2 * nd:]


def _na_bias_table(rpb):
    c = jnp.arange(GRID_W)
    c0 = jnp.clip(c - NA_WIN_COLS // 2, 0, GRID_W - NA_WIN_COLS)
    kc = jnp.arange(GRID_W)
    valid = (kc[None, :] >= c0[:, None]) & (kc[None, :] < c0[:, None] + NA_WIN_COLS)
    coff = kc[None, :] - c[:, None] + (NA_WIN_COLS - 1)
    onehot = (coff[None] == jnp.arange(2 * NA_WIN_COLS - 1)[:, None, None]).astype(F32)
    by_col = jnp.einsum("hrd,dck->hrck", rpb.astype(F32), onehot, precision=lax.Precision.HIGHEST)
    by_col = jnp.where(valid[None, None], by_col * LOG2E, NEG)
    tab = jnp.stack([by_col[:, d:d + NA_WIN_ROWS] for d in range(NA_WIN_ROWS)], axis=1)
    tab = tab.transpose(1, 0, 3, 2, 4)
    return tab.reshape(NA_WIN_ROWS * NA_HEADS // HEADS_PER_LANE_GROUP, HEADS_PER_LANE_GROUP * GRID_W,
                       NA_WIN_ROWS * GRID_W)


def _na_kernel(q_ref, kp_ref, kc_ref, kn_ref, vp_ref, vc_ref, vn_ref, bias_ref, o_ref,
               kwin, vwin, *, rows):
    j = pl.program_id(1)
    blk = NA_WIN_ROWS * GRID_W
    for idx, (kr, vr) in enumerate(((kp_ref, vp_ref), (kc_ref, vc_ref), (kn_ref, vn_ref))):
        kwin[idx * blk:(idx + 1) * blk, :] = kr[...]
        vwin[idx * blk:(idx + 1) * blk, :] = vr[...]

    n_pairs = NA_HEADS // HEADS_PER_LANE_GROUP

    def one_row(i):
        q_rows = pl.ds(pl.multiple_of(i * GRID_W, GRID_W), GRID_W)

        def load():
            r = j * NA_WIN_ROWS + i
            r0 = jnp.clip(r - NA_WIN_ROWS // 2, 0, rows - NA_WIN_ROWS)
            dlt = r0 - r + (NA_WIN_ROWS - 1)
            start = pl.multiple_of((r0 - (j - 1) * NA_WIN_ROWS) * GRID_W, GRID_W)
            return (q_ref[q_rows, :], kwin[pl.ds(start, blk), :], vwin[pl.ds(start, blk), :],
                    lambda p: bias_ref[dlt * n_pairs + p])

        def store(o, lses):
            o_ref[q_rows, :] = o.astype(BF16)

        return load, store

    def row_group(ig, carry):
        _attention_pipelined([one_row(ig * ROWS_PER_TRIP + u) for u in range(ROWS_PER_TRIP)], NA_HEADS)
        return carry

    lax.fori_loop(0, NA_WIN_ROWS // ROWS_PER_TRIP, row_group, 0)


def _na_attention(q, k, v, bias):
    b, t, w = q.shape
    rows = t // GRID_W
    assert rows % NA_WIN_ROWS == 0 and rows >= NA_WIN_ROWS
    nj = rows // NA_WIN_ROWS
    blk = NA_WIN_ROWS * GRID_W
    cur = pl.BlockSpec((None, blk, w), lambda bi, j: (bi, j, 0))
    prev = pl.BlockSpec((None, blk, w), lambda bi, j: (bi, jnp.maximum(j - 1, 0), 0))
    nxt = pl.BlockSpec((None, blk, w), lambda bi, j: (bi, jnp.minimum(j + 1, nj - 1), 0))
    return pl.pallas_call(
        functools.partial(_na_kernel, rows=rows),
        grid=(b, nj),
        in_specs=[cur, prev, cur, nxt, prev, cur, nxt,
                  pl.BlockSpec(bias.shape, lambda bi, j: (0, 0, 0))],
        out_specs=cur,
        out_shape=jax.ShapeDtypeStruct((b, t, w), BF16),
        scratch_shapes=[pltpu.VMEM((3 * blk, w), BF16), pltpu.VMEM((3 * blk, w), BF16)],
        compiler_params=_cparams(("parallel", "parallel")),
        name="na_attn",
    )(q, k, k, k, v, v, v, bias)


def _dil_kernel(q_ref, kp_ref, kc_ref, kn_ref, vp_ref, vc_ref, vn_ref, o_ref, lse_ref,
                kwin, vwin, *, seg, tq, halo):
    i = pl.program_id(2)
    width = tq + 2 * halo
    kwin[0:halo, :] = kp_ref[...]
    kwin[halo:halo + tq, :] = kc_ref[...]
    kwin[halo + tq:width, :] = kn_ref[...]
    vwin[0:halo, :] = _values_with_ones(vp_ref[...])
    vwin[halo:halo + tq, :] = _values_with_ones(vc_ref[...])
    vwin[halo + tq:width, :] = _values_with_ones(vn_ref[...])

    sq = math.gcd(tq, 2 * halo)
    sw = sq + 2 * halo
    a = lax.broadcasted_iota(jnp.int32, (sq, sw), 0)
    c = lax.broadcasted_iota(jnp.int32, (sq, sw), 1)
    band = (c >= a) & (c <= a + 2 * halo)
    lo = lax.broadcasted_iota(jnp.int32, (sq, LANES), 1) < HEAD_DIM

    def sub_block(s):
        row0 = pl.multiple_of(s * sq, sq)

        def load():
            first_key = i * tq + s * sq - halo
            valid = band & (c >= -first_key) & (c < seg - first_key)
            negb = jnp.where(valid, 0.0, NEG).astype(F32)
            negb2 = jnp.concatenate([negb] * HEADS_PER_LANE_GROUP, axis=0)
            return (q_ref[pl.ds(row0, sq), :], kwin[pl.ds(row0, sw), :], vwin[pl.ds(row0, sw), :],
                    lambda p: negb2)

        def store(o, lses):
            o_ref[pl.ds(row0, sq), :] = o.astype(BF16)
            lse_ref[pl.ds(row0, sq), :] = jnp.concatenate(
                [jnp.where(lo, lses[2 * p], lses[2 * p + 1]) for p in range(DIL_HEADS // 2)], axis=-1)

        return load, store

    n_sub = tq // sq
    per_trip = math.gcd(n_sub, ROWS_PER_TRIP)

    def sub_block_group(sg, carry):
        _attention_pipelined([sub_block(sg * per_trip + u) for u in range(per_trip)], DIL_HEADS, skew=0)
        return carry

    lax.fori_loop(0, n_sub // per_trip, sub_block_group, 0)


def _dilated_branch(q, k, v, n_side, tq_max):
    b, dil, seg, w = q.shape
    halo = n_side
    assert seg % halo == 0 and halo % 16 == 0
    tq = math.gcd(tq_max, seg)
    assert tq % halo == 0
    hb = tq // halo
    nhalo = seg // halo
    cur = pl.BlockSpec((None, None, tq, w), lambda bi, r, i: (bi, r, i, 0))
    prev = pl.BlockSpec((None, None, halo, w), lambda bi, r, i: (bi, r, jnp.maximum(i * hb - 1, 0), 0))
    nxt = pl.BlockSpec((None, None, halo, w),
                       lambda bi, r, i: (bi, r, jnp.minimum((i + 1) * hb, nhalo - 1), 0))
    return pl.pallas_call(
        functools.partial(_dil_kernel, seg=seg, tq=tq, halo=halo),
        grid=(b, dil, seg // tq),
        in_specs=[cur, prev, cur, nxt, prev, cur, nxt],
        out_specs=[cur, cur],
        out_shape=[jax.ShapeDtypeStruct((b, dil, seg, w), BF16),
                   jax.ShapeDtypeStruct((b, dil, seg, w), F32)],
        scratch_shapes=[pltpu.VMEM((tq + 2 * halo, w), BF16), pltpu.VMEM((tq + 2 * halo, 2 * w), BF16)],
        compiler_params=_cparams(("parallel", "parallel", "parallel")),
        name=f"dilated_d{dil}",
    )(q, k, k, k, v, v, v)


def _memkv_kernel(mem_ref, g_ref, w_ref, k_ref, v_ref):
    h = _rms(mem_ref[...], g_ref[...]).astype(BF16)
    kv = jnp.dot(h, w_ref[...], preferred_element_type=F32)
    k_ref[...] = kv[:, :MEM_WIDTH].astype(BF16)
    v_ref[...] = kv[:, MEM_WIDTH:].astype(BF16)


def _mem_kv(mem, g, w):
    b, m, d = mem.shape
    blk = lambda width: pl.BlockSpec((None, m, width), lambda bi: (bi, 0, 0))
    return pl.pallas_call(
        _memkv_kernel,
        grid=(b,),
        in_specs=[blk(d), pl.BlockSpec((1, d), lambda bi: (0, 0)), pl.BlockSpec(w.shape, lambda bi: (0, 0))],
        out_specs=[blk(MEM_WIDTH), blk(MEM_WIDTH)],
        out_shape=[jax.ShapeDtypeStruct((b, m, MEM_WIDTH), BF16)] * 2,
        compiler_params=_cparams(("parallel",)),
        name="mem_kv",
    )(mem, g, w)


def _memattn_kernel(q_ref, k_ref, v_ref, o_ref):
    tq = q_ref.shape[0]
    sq = math.gcd(tq, LANES)
    v_aug = _values_with_ones(v_ref[...])

    def sub_block(s):
        rows = slice(s * sq, (s + 1) * sq)

        def store(o, lses):
            o_ref[rows, :] = o.astype(BF16)

        return (lambda: (q_ref[rows, :], k_ref[...], v_aug, lambda p: 0.0)), store

    _attention_pipelined([sub_block(s) for s in range(tq // sq)], MEM_HEADS)


def _mem_attention(q, k, v, tq):
    b, t, w = q.shape
    m = k.shape[1]
    qs = pl.BlockSpec((None, tq, w), lambda bi, i: (bi, i, 0))
    ks = pl.BlockSpec((None, m, w), lambda bi, i: (bi, 0, 0))
    return pl.pallas_call(
        _memattn_kernel,
        grid=(b, t // tq),
        in_specs=[qs, ks, ks],
        out_specs=qs,
        out_shape=jax.ShapeDtypeStruct((b, t, w), BF16),
        compiler_params=_cparams(("parallel", "parallel")),
        name="mem_attn",
    )(q, k, v)


def _token_order(ref, scr):
    dil = ref.shape[0]
    if dil == 1:
        return ref[0].astype(F32)
    groups, tm, _ = scr.shape
    for r in range(dil):
        for g in range(groups):
            scr[g, pl.ds(r, tm // dil, stride=dil), :] = ref[r, :, g * LANES:(g + 1) * LANES].astype(F32)
    return jnp.concatenate([scr[g] for g in range(groups)], axis=-1)


def _branch_mix(ob_refs, lse_refs, o_scrs, l_scrs):
    lses = [_token_order(r, s) for r, s in zip(lse_refs, l_scrs)]
    mx = functools.reduce(jnp.maximum, lses)
    es = [jnp.exp2(l - mx) for l in lses]
    den = functools.reduce(lambda p, q: p + q, es)
    acc = None
    for e, ob, scr in zip(es, ob_refs, o_scrs):
        term = (e / den) * _token_order(ob, scr)
        acc = term if acc is None else acc + term
    return acc


def _outproj_core(oa_ref, ob_refs, lse_refs, om_ref, x_ref, gm_ref, w_ref, o_scrs, l_scrs):
    gm = gm_ref[...]
    e0, e1 = NA_WIDTH, NA_WIDTH + DIL_WIDTH
    ya = _rms(oa_ref[...].astype(F32), gm[:, :e0]).astype(BF16)
    yb = _rms(_branch_mix(ob_refs, lse_refs, o_scrs, l_scrs), gm[:, e0:e1]).astype(BF16)
    ym = _rms(om_ref[...].astype(F32), gm[:, e1:]).astype(BF16)
    y = jnp.concatenate([ya, yb, ym], axis=-1)
    return x_ref[...] + jnp.dot(y, w_ref[...], preferred_element_type=F32)


def _split_scratch(scrs, dils):
    it = iter(scrs)
    o_scrs = [next(it) if dil > 1 else None for dil in dils]
    l_scrs = [next(it) if dil > 1 else None for dil in dils]
    return o_scrs, l_scrs


def _outproj_kernel(oa_ref, ob1, ob2, ob3, l1, l2, l3, om_ref, x_ref, gm_ref, w_ref, xo_ref, *scrs, dils):
    xo_ref[...] = _outproj_core(oa_ref, (ob1, ob2, ob3), (l1, l2, l3), om_ref, x_ref, gm_ref, w_ref,
                                *_split_scratch(scrs, dils))


def _store_row_tiles(ref, val):
    m, d = val.shape
    sub = ref.shape[0] // m
    tw = d // sub
    for j in range(sub):
        ref[pl.ds(j, m, stride=sub), :] = val[:, j * tw:(j + 1) * tw]


def _outproj_router_kernel(oa_ref, ob1, ob2, ob3, l1, l2, l3, om_ref, x_ref, gm_ref, w_ref,
                           gf_ref, wr_ref, xo_ref, h_ref, route_ref, *scrs, n_experts, dils):
    xn = _outproj_core(oa_ref, (ob1, ob2, ob3), (l1, l2, l3), om_ref, x_ref, gm_ref, w_ref,
                       *_split_scratch(scrs, dils))
    xo_ref[...] = xn
    h = _rms(xn, gf_ref[...])
    _store_row_tiles(h_ref, h)
    logits = jnp.dot(h.astype(BF16), wr_ref[...], preferred_element_type=F32)
    lane = lax.broadcasted_iota(jnp.int32, logits.shape, 1)
    lg = jnp.where(lane < n_experts, logits, NEG)
    m1 = jnp.max(lg, axis=-1, keepdims=True)
    i1 = jnp.min(jnp.where(lg == m1, lane, LANES), axis=-1, keepdims=True)
    lg2 = jnp.where(lane == i1, NEG, lg)
    m2 = jnp.max(lg2, axis=-1, keepdims=True)
    i2 = jnp.min(jnp.where(lg2 == m2, lane, LANES), axis=-1, keepdims=True)
    e2 = jnp.exp(m2 - m1)
    g1 = 1.0 / (1.0 + e2)
    g2 = e2 / (1.0 + e2)
    route = jnp.where(lane == 0, i1.astype(F32), 0.0)
    route = jnp.where(lane == 1, i2.astype(F32), route)
    route = jnp.where(lane == 2, g1, route)
    route = jnp.where(lane == 3, g2, route)
    route_ref[...] = route


def _outproj(oa, obs, lses, om, x2, gm, w, t, tm, router=None):
    n, d = x2.shape
    tpb = t // tm
    row = lambda width: pl.BlockSpec((tm, width), lambda i: (i, 0))
    full = lambda a: pl.BlockSpec(a.shape, lambda i: (0,) * a.ndim)
    grouped = lambda a: pl.BlockSpec((None, a.shape[1], tm // a.shape[1], a.shape[3]),
                                     lambda i: (i // tpb, 0, i % tpb, 0))
    dils = tuple(a.shape[1] for a in obs)
    args = [oa, *obs, *lses, om, x2, gm, w]
    in_specs = ([row(NA_WIDTH)] + [grouped(a) for a in obs] + [grouped(a) for a in lses]
                + [row(MEM_WIDTH), row(d), full(gm), full(w)])
    scratch = [pltpu.VMEM((DIL_WIDTH // LANES, tm, LANES), F32)] * (2 * sum(dil > 1 for dil in dils))
    if router is None:
        return pl.pallas_call(
            functools.partial(_outproj_kernel, dils=dils),
            grid=(n // tm,), in_specs=in_specs, out_specs=row(d),
            out_shape=jax.ShapeDtypeStruct((n, d), F32), scratch_shapes=scratch,
            compiler_params=_cparams(("parallel",)), name="outproj",
        )(*args)
    gf, wr, n_experts = router
    sub = ROW_TILE_SUBLANES
    return pl.pallas_call(
        functools.partial(_outproj_router_kernel, n_experts=n_experts, dils=dils),
        grid=(n // tm,), in_specs=in_specs + [full(gf), full(wr)],
        out_specs=[row(d), pl.BlockSpec((tm * sub, d // sub), lambda i: (i, 0)), row(LANES)],
        out_shape=[jax.ShapeDtypeStruct((n, d), F32), jax.ShapeDtypeStruct((n * sub, d // sub), F32),
                   jax.ShapeDtypeStruct((n, LANES), F32)],
        scratch_shapes=scratch,
        compiler_params=_cparams(("parallel",)), name="outproj_router",
    )(*args, gf, wr)


def _swiglu_partial(h, wg_ref, wu_ref, wd_ref, act_scr, emit):
    tf = wg_ref.shape[-1]
    chunk = math.gcd(MXU_COLS, tf)
    for c in range(tf // chunk):
        sl = slice(c * chunk, (c + 1) * chunk)
        gate = jnp.dot(h, wg_ref[:, sl], preferred_element_type=F32)
        up = jnp.dot(h, wu_ref[:, sl], preferred_element_type=F32)
        act_scr[:, sl] = ((gate / (1.0 + jnp.exp(-gate))) * up).astype(BF16)
    act = act_scr[...]
    for c in range(wd_ref.shape[-1] // MXU_COLS):
        emit(c, jnp.dot(act, wd_ref[:, c * MXU_COLS:(c + 1) * MXU_COLS], preferred_element_type=F32))


def _dense_ffn_kernel(x_ref, g_ref, wg_ref, wu_ref, wd_ref, o_ref, h_scr, act_scr):
    @pl.when(pl.program_id(1) == 0)
    def _():
        x = x_ref[...]
        h_scr[...] = _rms(x, g_ref[...]).astype(BF16)
        o_ref[...] = x

    def accumulate(c, part):
        o_ref[:, c * MXU_COLS:(c + 1) * MXU_COLS] += part

    _swiglu_partial(h_scr[...], wg_ref, wu_ref, wd_ref, act_scr, accumulate)


def _dense_ffn(x2, g, wg, wu, wd, li, tm, tf):
    n, d = x2.shape
    f = wg.shape[2]
    row = pl.BlockSpec((tm, d), lambda i, k: (i, 0))
    return pl.pallas_call(
        _dense_ffn_kernel,
        grid=(n // tm, f // tf),
        in_specs=[row, pl.BlockSpec((1, d), lambda i, k: (0, 0)),
                  pl.BlockSpec((None, d, tf), lambda i, k: (li, 0, k)),
                  pl.BlockSpec((None, d, tf), lambda i, k: (li, 0, k)),
                  pl.BlockSpec((None, tf, d), lambda i, k: (li, k, 0))],
        out_specs=row,
        out_shape=jax.ShapeDtypeStruct((n, d), F32),
        scratch_shapes=[pltpu.VMEM((tm, d), BF16), pltpu.VMEM((tm, tf), BF16)],
        compiler_params=_cparams(("parallel", "arbitrary")),
        name="dense_ffn",
    )(x2, g, wg, wu, wd)


def _moe_ffn_kernel(be_ref, nused_ref, xb_ref, wg_ref, wu_ref, wd_ref, y_ref, act_scr):
    j = pl.program_id(0)
    k = pl.program_id(1)
    used = j < nused_ref[0]
    mb = act_scr.shape[0]
    sub = xb_ref.shape[0] // mb
    tw = xb_ref.shape[1]

    @pl.when(k == 0)
    def _():
        y_ref[...] = jnp.zeros_like(y_ref)

    @pl.when(used)
    def _():
        h = jnp.concatenate([xb_ref[pl.ds(c, mb, stride=sub), :].astype(BF16) for c in range(sub)], axis=-1)

        def accumulate(c, part):
            step = min(tw, MXU_COLS)
            for u in range(MXU_COLS // step):
                col = c * MXU_COLS + u * step
                y_ref[pl.ds(col // tw, mb, stride=sub), col % tw:col % tw + step] += (
                    part[:, u * step:(u + 1) * step])

        _swiglu_partial(h, wg_ref, wu_ref, wd_ref, act_scr, accumulate)


def _moe_ffn(buf, block_expert, n_used, wg, wu, wd, li, mb, tf):
    d, f = wg.shape[2], wg.shape[3]
    sub = ROW_TILE_SUBLANES
    tw = d // sub
    p = buf.shape[0] // sub
    nk = f // tf
    kk = lambda j, k, be, nu: jnp.where(j < nu[0], k, nk - 1)
    row = pl.BlockSpec((mb * sub, tw), lambda j, k, be, nu: (j, 0))
    grid_spec = pltpu.PrefetchScalarGridSpec(
        num_scalar_prefetch=2,
        grid=(p // mb, nk),
        in_specs=[row,
                  pl.BlockSpec((None, None, d, tf), lambda j, k, be, nu: (li, be[j], 0, kk(j, k, be, nu))),
                  pl.BlockSpec((None, None, d, tf), lambda j, k, be, nu: (li, be[j], 0, kk(j, k, be, nu))),
                  pl.BlockSpec((None, None, tf, d), lambda j, k, be, nu: (li, be[j], kk(j, k, be, nu), 0))],
        out_specs=row,
        scratch_shapes=[pltpu.VMEM((mb, tf), BF16)],
    )
    return pl.pallas_call(
        _moe_ffn_kernel,
        grid_spec=grid_spec,
        out_shape=jax.ShapeDtypeStruct((p * sub, tw), F32),
        compiler_params=_cparams(("parallel", "arbitrary")),
        name="moe_ffn",
    )(block_expert, n_used, buf, wg, wu, wd)


ISSUE_UNROLL = 8


def _issue_tile_copies(chunk, copy_fn):
    def body(trip, carry):
        for u in range(ISSUE_UNROLL):
            copy_fn(trip, u).start(priority=u % 2)
        return carry

    lax.fori_loop(0, chunk // ISSUE_UNROLL, body, 0)


def _tile(ref, idx, sub):
    return ref.at[pl.ds(pl.multiple_of(idx * sub, sub), sub)]


def _dispatch_kernel(slot_ref, h_ref, out_ref, zero_scr, sem, *, n_tok_steps, sub):
    i = pl.program_id(0)
    n_copies = slot_ref.shape[-1]
    dst = lambda trip, u: _tile(out_ref, slot_ref[0, 0, trip * ISSUE_UNROLL + u], sub)
    tok = lambda trip, u: trip * (ISSUE_UNROLL // TOP_K) + u // TOP_K

    @pl.when(i == 0)
    def _():
        zero_scr[...] = jnp.zeros_like(zero_scr)

    @pl.when(i < n_tok_steps)
    def _():
        _issue_tile_copies(n_copies, lambda trip, u: pltpu.make_async_copy(
            _tile(h_ref, tok(trip, u), sub), dst(trip, u), sem))

    @pl.when(i >= n_tok_steps)
    def _():
        _issue_tile_copies(n_copies, lambda trip, u: pltpu.make_async_copy(zero_scr, dst(trip, u), sem))

    for _ in range(TOP_K):
        pltpu.make_async_copy(h_ref, out_ref.at[pl.ds(0, h_ref.shape[0])], sem).wait()


def _dispatch_rows(h, slot, tm, sub):
    n = h.shape[0] // sub
    tw = h.shape[1]
    p = slot.shape[0]
    n_copies = tm * TOP_K
    assert p % n_copies == 0 and n % tm == 0 and n_copies % ISSUE_UNROLL == 0
    n_tok_steps = n // tm
    return pl.pallas_call(
        functools.partial(_dispatch_kernel, n_tok_steps=n_tok_steps, sub=sub),
        grid=(p // n_copies,),
        in_specs=[pl.BlockSpec((1, 1, n_copies), lambda i: (i, 0, 0), memory_space=pltpu.SMEM),
                  pl.BlockSpec((tm * sub, tw), lambda i: (jnp.minimum(i, n_tok_steps - 1), 0))],
        out_specs=pl.BlockSpec(memory_space=pl.ANY),
        out_shape=jax.ShapeDtypeStruct((p * sub, tw), h.dtype),
        scratch_shapes=[pltpu.VMEM((sub, tw), h.dtype), pltpu.SemaphoreType.DMA(())],
        compiler_params=_cparams(("arbitrary",)),
        name="dispatch_rows",
    )(slot.reshape(p // n_copies, 1, n_copies), h)


def _combine_kernel(cur_ref, nxt_ref, x_ref, route_ref, y_ref, *rest, final_norm, sub):
    o_ref, ybuf, sem = rest[-3:]
    i = pl.program_id(0)
    tm = x_ref.shape[0]
    n_copies = tm * TOP_K
    par = i % 2

    def fetch(idx_ref, buf):
        def copy(trip, u):
            tok = trip * (ISSUE_UNROLL // TOP_K) + u // TOP_K
            return pltpu.make_async_copy(_tile(y_ref, idx_ref[0, 0, trip * ISSUE_UNROLL + u], sub),
                                         _tile(ybuf.at[buf, u % TOP_K], tok, sub), sem.at[buf])
        _issue_tile_copies(n_copies, copy)

    @pl.when(i == 0)
    def _():
        fetch(cur_ref, 0)

    @pl.when(i + 1 < pl.num_programs(0))
    def _():
        fetch(nxt_ref, 1 - par)

    pltpu.make_async_copy(ybuf.at[par], ybuf.at[par], sem.at[par]).wait()

    route = route_ref[...]
    gates = [route[:, TOP_K + k:TOP_K + k + 1] for k in range(TOP_K)]
    tw = x_ref.shape[1] // sub
    for j in range(sub):
        sl = slice(j * tw, (j + 1) * tw)
        mix = None
        for k in range(TOP_K):
            term = gates[k] * ybuf[par, k, pl.ds(j, tm, stride=sub), :]
            mix = term if mix is None else mix + term
        o_ref[:, sl] = x_ref[:, sl] + mix
    if final_norm:
        o_ref[...] = _rms(o_ref[...], rest[0][...])


def _combine(x2, y, dest, route, tm, g_final=None):
    n, d = x2.shape
    sub = ROW_TILE_SUBLANES
    n_copies = tm * TOP_K
    steps = n // tm
    assert n_copies % ISSUE_UNROLL == 0 and ISSUE_UNROLL % TOP_K == 0
    dest3 = dest.reshape(steps, 1, n_copies)
    row = pl.BlockSpec((tm, d), lambda i: (i, 0))
    in_specs = [pl.BlockSpec((1, 1, n_copies), lambda i: (i, 0, 0), memory_space=pltpu.SMEM),
                pl.BlockSpec((1, 1, n_copies), lambda i: (jnp.minimum(i + 1, steps - 1), 0, 0),
                             memory_space=pltpu.SMEM),
                row, pl.BlockSpec((tm, LANES), lambda i: (i, 0)), pl.BlockSpec(memory_space=pl.ANY)]
    args = [dest3, dest3, x2, route, y]
    if g_final is not None:
        in_specs.append(pl.BlockSpec((1, d), lambda i: (0, 0)))
        args.append(g_final)
    return pl.pallas_call(
        functools.partial(_combine_kernel, final_norm=g_final is not None, sub=sub),
        grid=(steps,), in_specs=in_specs, out_specs=row,
        out_shape=jax.ShapeDtypeStruct((n, d), F32),
        scratch_shapes=[pltpu.VMEM((2, TOP_K, tm * sub, d // sub), F32), pltpu.SemaphoreType.DMA((2,))],
        compiler_params=_cparams(("arbitrary",)), name="moe_combine",
    )(*args)


def _final_norm_kernel(x_ref, g_ref, o_ref):
    o_ref[...] = _rms(x_ref[...], g_ref[...])


def _final_norm(x2, g, tm):
    n, d = x2.shape
    row = pl.BlockSpec((tm, d), lambda i: (i, 0))
    return pl.pallas_call(
        _final_norm_kernel, grid=(n // tm,), in_specs=[row, pl.BlockSpec((1, d), lambda i: (0, 0))],
        out_specs=row, out_shape=jax.ShapeDtypeStruct((n, d), F32),
        compiler_params=_cparams(("parallel",)), name="final_norm",
    )(x2, g)


def _routing_plan(route, n_experts, mb):
    n = route.shape[0]
    n_assign = n * TOP_K
    flat_e = route[:, :TOP_K].astype(jnp.int32).reshape(-1)
    onehot = (flat_e[:, None] == jnp.arange(n_experts)[None, :]).astype(jnp.int32)
    csum = jnp.cumsum(onehot, axis=0)
    rank = jnp.sum(csum * onehot, axis=1) - 1
    counts = csum[-1]
    padded = (counts + mb - 1) // mb * mb
    pend = jnp.cumsum(padded)
    pstart = pend - padded
    dest = jnp.sum(pstart[None, :] * onehot, axis=1) + rank
    assert n_assign % mb == 0
    n_blocks = n_assign // mb + n_experts
    block_first = jnp.arange(n_blocks) * mb
    block_expert = jnp.minimum(jnp.sum(block_first[:, None] >= pend[None, :], axis=1), n_experts - 1)
    n_used = (pend[-1] // mb).reshape(1)
    pads = padded - counts
    cpad = jnp.cumsum(pads)
    i = jnp.arange(n_experts * mb)
    grp = jnp.sum(i[:, None] >= cpad[None, :], axis=1)
    grp_hot = (grp[:, None] == jnp.arange(n_experts + 1)[None, :]).astype(jnp.int32)
    first_free = jnp.concatenate([pstart + counts, pend[-1:]])
    before = jnp.concatenate([jnp.zeros((1,), cpad.dtype), cpad])
    free = jnp.sum(grp_hot * (first_free - before)[None, :], axis=1) + i
    slot = jnp.concatenate([dest, free]).astype(jnp.int32)
    return dest.astype(jnp.int32), slot, block_expert.astype(jnp.int32), n_used.astype(jnp.int32)


def _tiles(n, t):
    return dict(tm_in=math.gcd(1024, t), tm_proj=math.gcd(512, t), tm_ffn=math.gcd(1024, n), tf=1792,
                tq_dil=1024, tq_mem=math.gcd(512, t), moe_block=math.gcd(1024, n))


@jax.jit
def _forward(x, mem, g_mix_norm, w_in, rpb, g_mem_norm, w_mem_kv, g_mix_out, w_out,
             g_ffn_norm, w_dense_gate, w_dense_up, w_dense_down, w_router,
             w_moe_gate, w_moe_up, w_moe_down, g_final):
    b, t, d = x.shape
    n = b * t
    depth = w_in.shape[0]
    n_experts = w_router.shape[-1]
    ts = _tiles(n, t)
    tf = math.gcd(ts["tf"], w_dense_gate.shape[-1])
    tables = _rope_tables(t)
    dense_w = [w.astype(BF16) for w in (w_dense_gate, w_dense_up, w_dense_down)]
    moe_w = [w.astype(BF16) for w in (w_moe_gate, w_moe_up, w_moe_down)]
    row1 = lambda a: a.reshape(1, -1)
    x2 = x.reshape(n, d)
    out = None
    sub = ROW_TILE_SUBLANES
    for layer in range(depth):
        qa, ka, va, qm, qbs, kbs, vbs = _inproj(x2, row1(g_mix_norm[layer]), w_in[layer].astype(BF16),
                                                tables, t, ts["tm_in"])
        shp = lambda a: a.reshape(b, t, a.shape[-1])
        oa = _na_attention(shp(qa), shp(ka), shp(va), _na_bias_table(rpb[layer])).reshape(n, NA_WIDTH)
        obs, lses = [], []
        for (window, dil), qb, kb, vb in zip(DIL_BRANCHES, qbs, kbs, vbs):
            o, lse = _dilated_branch(qb, kb, vb, window // (2 * dil), ts["tq_dil"])
            obs.append(o)
            lses.append(lse)
        km, vm = _mem_kv(mem, row1(g_mem_norm[layer]), w_mem_kv[layer].astype(BF16))
        om = _mem_attention(shp(qm), km, vm, ts["tq_mem"]).reshape(n, MEM_WIDTH)
        gm, wo = row1(g_mix_out[layer]), w_out[layer].astype(BF16)
        i = layer // 2
        if layer % 2 == 0:
            x2 = _outproj(oa, obs, lses, om, x2, gm, wo, t, ts["tm_in"])
            x2 = _dense_ffn(x2, row1(g_ffn_norm[layer]), *dense_w, i, ts["tm_ffn"], tf)
            if layer == depth - 1:
                out = _final_norm(x2, row1(g_final), ts["tm_proj"])
        else:
            wr = jnp.pad(w_router[i], ((0, 0), (0, LANES - n_experts))).astype(BF16)
            x2, h, route = _outproj(oa, obs, lses, om, x2, gm, wo, t, ts["tm_proj"],
                                    router=(row1(g_ffn_norm[layer]), wr, n_experts))
            mb = ts["moe_block"]
            dest, slot, block_expert, n_used = _routing_plan(route, n_experts, mb)
            buf = _dispatch_rows(h, slot, ts["tm_proj"], sub)
            y = _moe_ffn(buf, block_expert, n_used, *moe_w, i, mb, tf)
            if layer == depth - 1:
                out = _combine(x2, y, dest, route, ts["tm_proj"], g_final=row1(g_final))
            else:
                x2 = _combine(x2, y, dest, route, ts["tm_proj"])
    return out.reshape(b, t, d)


def kernel(x, mem, g_mix_norm, w_in, rpb, g_mem_norm, w_mem_kv, g_mix_out, w_out, g_ffn_norm,
           w_dense_gate, w_dense_up, w_dense_down, w_router, w_moe_gate, w_moe_up, w_moe_down, g_final):
    return _forward(x, mem, g_mix_norm, w_in, rpb, g_mem_norm, w_mem_kv, g_mix_out, w_out, g_ffn_norm,
                    w_dense_gate, w_dense_up, w_dense_down, w_router, w_moe_gate, w_moe_up, w_moe_down,
                    g_final)
```

```python
import functools
import math

import jax
import jax.numpy as jnp
from jax import lax
from jax.experimental import pallas as pl
from jax.experimental.pallas import tpu as pltpu

HEAD_DIM = 64
NA_HEADS = 6
DIL_HEADS = 6
MEM_HEADS = 4
NA_WIDTH = NA_HEADS * HEAD_DIM
DIL_WIDTH = DIL_HEADS * HEAD_DIM
MEM_WIDTH = MEM_HEADS * HEAD_DIM
GRID_W = 64
NA_WIN_ROWS = 8
NA_WIN_COLS = 16
DIL_BRANCHES = ((128, 1), (512, 4), (2048, 16))
ROPE_THETA = 500000.0
ROT_DIM = HEAD_DIM // 4
TOP_K = 2
RMS_EPS = 1e-6
ATTN_SCALE = HEAD_DIM ** -0.5
LOG2E = math.log2(math.e)
Q_SCALE = ATTN_SCALE * LOG2E

LANES = 128
MXU_COLS = 256
HEADS_PER_LANE_GROUP = LANES // HEAD_DIM
NEG = -1e30
ROWS_PER_TRIP = 8
ROW_TILE_SUBLANES = 8
VMEM_LIMIT = 56 * 1024 * 1024

F32 = jnp.float32
BF16 = jnp.bfloat16


def _cparams(sem):
    return pltpu.CompilerParams(dimension_semantics=sem, vmem_limit_bytes=VMEM_LIMIT)


def _rms(x, g):
    ms = jnp.mean(x * x, axis=-1, keepdims=True)
    return x * lax.rsqrt(ms + RMS_EPS) * g


def _lane_group(a, p):
    return a[:, p * LANES:(p + 1) * LANES]


def _attn_scores(q, k, bias_fn, n_heads):
    m = q.shape[0]
    assert HEADS_PER_LANE_GROUP == 2
    lo = lax.broadcasted_iota(jnp.int32, (m, LANES), 1) < HEAD_DIM
    s_parts = []
    for p in range(n_heads // HEADS_PER_LANE_GROUP):
        qp = _lane_group(q, p)
        zero = jnp.zeros_like(qp)
        q2 = jnp.concatenate([jnp.where(lo, qp, zero), jnp.where(lo, zero, qp)], axis=0)
        s_parts.append(lax.dot_general(q2, _lane_group(k, p), (((1,), (1,)), ((), ())),
                                       preferred_element_type=F32) + bias_fn(p))
    return jnp.concatenate(s_parts, axis=0)


def _attn_softmax(s):
    mx = jnp.max(s, axis=-1, keepdims=True)
    return jnp.exp2((s - mx).astype(BF16)), mx


def _values_with_ones(v):
    ones = jnp.ones((v.shape[0], LANES), v.dtype)
    return jnp.concatenate([blk for p in range(v.shape[1] // LANES) for blk in (_lane_group(v, p), ones)],
                           axis=-1)


def _attn_values(eb, mx, v, n_heads):
    m = eb.shape[0] // n_heads
    n_pairs = n_heads // HEADS_PER_LANE_GROUP
    with_ones = v.shape[1] == n_pairs * 2 * LANES
    lo = lax.broadcasted_iota(jnp.int32, (m, LANES), 1) < HEAD_DIM
    if not with_ones:
        den_all = jnp.sum(eb.astype(F32), axis=-1, keepdims=True)
    outs, lses = [], []
    for p in range(n_pairs):
        rows = slice(p * 2 * m, (p + 1) * 2 * m)
        if with_ones:
            both = jnp.dot(eb[rows], v[:, p * 2 * LANES:(p + 1) * 2 * LANES], preferred_element_type=F32)
            num, den = both[:, :LANES], both[:, LANES:]
        else:
            num, den = jnp.dot(eb[rows], _lane_group(v, p), preferred_element_type=F32), den_all[rows]
        o2 = num / den
        outs.append(jnp.where(lo, o2[:m], o2[m:]))
        lse = mx[rows] + jnp.log2(den)
        lses += [lse[:m], lse[m:]]
    return jnp.concatenate(outs, axis=-1), lses


def _attention_pipelined(items, n_heads, skew=1):
    n = len(items)
    loaded, scores, soft = {}, {}, {}
    for step in range(n + 2 * skew):
        if step < n:
            loaded[step] = items[step][0]()
            q, k, _, bias_fn = loaded[step]
            scores[step] = _attn_scores(q, k, bias_fn, n_heads)
        if 0 <= step - skew < n:
            soft[step - skew] = _attn_softmax(scores.pop(step - skew))
        if 0 <= step - 2 * skew < n:
            eb, mx = soft.pop(step - 2 * skew)
            items[step - 2 * skew][1](*_attn_values(eb, mx, loaded.pop(step - 2 * skew)[2], n_heads))


def _inproj_kernel(x_ref, g_ref, w_ref, cos_ref, sa_ref, sb_ref,
                   qa_ref, ka_ref, va_ref, qm_ref, *rest, dils):
    dil_refs, scr = rest[:-1], rest[-1]
    tm = x_ref.shape[0]
    h = _rms(x_ref[...], g_ref[...]).astype(BF16)

    def emit_dilated(val, refs):
        groups = val.shape[1] // LANES
        for g in range(groups):
            scr[g] = val[:, g * LANES:(g + 1) * LANES]
        for dil, ref in zip(dils, refs):
            if dil == 1:
                ref[0] = val.astype(BF16)
                continue
            for r in range(dil):
                for g in range(groups):
                    ref[r, :, g * LANES:(g + 1) * LANES] = (
                        scr[g, pl.ds(r, tm // dil, stride=dil), :].astype(BF16))

    def proj_pair(c0, width_a, width_b):
        both = jnp.dot(h, w_ref[:, c0:c0 + width_a + width_b], preferred_element_type=F32)
        return both[:, :width_a], both[:, width_a:]

    def rope(a):
        cos, sa, sb = cos_ref[...], sa_ref[...], sb_ref[...]
        half = ROT_DIM // 2
        parts = []
        for gidx in range(a.shape[1] // LANES):
            xg = a[:, gidx * LANES:(gidx + 1) * LANES]
            parts.append(xg * cos + pltpu.roll(xg, LANES - half, 1) * sa + pltpu.roll(xg, half, 1) * sb)
        return jnp.concatenate(parts, axis=-1)

    nd = len(dils)
    assert all(wd % MXU_COLS == 0 for wd in (2 * NA_WIDTH, NA_WIDTH + DIL_WIDTH, 2 * DIL_WIDTH))
    kb, vb = proj_pair(3 * NA_WIDTH + DIL_WIDTH, DIL_WIDTH, DIL_WIDTH)
    emit_dilated(rope(kb), dil_refs[nd:2 * nd])
    emit_dilated(vb, dil_refs[2 * nd:3 * nd])
    va, qb = proj_pair(2 * NA_WIDTH, NA_WIDTH, DIL_WIDTH)
    va_ref[...] = va.astype(BF16)
    emit_dilated(rope(qb * Q_SCALE), dil_refs[0:nd])
    qa, ka = proj_pair(0, NA_WIDTH, NA_WIDTH)
    qa_ref[...] = (qa * Q_SCALE).astype(BF16)
    ka_ref[...] = ka.astype(BF16)
    c = 3 * NA_WIDTH + 3 * DIL_WIDTH
    qm_ref[...] = (jnp.dot(h, w_ref[:, c:c + MEM_WIDTH], preferred_element_type=F32) * Q_SCALE).astype(BF16)


def _rope_tables(t):
    half = ROT_DIM // 2
    inv_freq = ROPE_THETA ** (-jnp.arange(0, ROT_DIM, 2, dtype=F32) / ROT_DIM)
    ang = jnp.arange(t, dtype=F32)[:, None] * inv_freq[None, :]
    cos, sin = jnp.cos(ang), jnp.sin(ang)
    ones = jnp.ones((t, HEAD_DIM - ROT_DIM), F32)
    zeros = jnp.zeros((t, HEAD_DIM - ROT_DIM), F32)
    zh = jnp.zeros((t, half), F32)
    cos_h = jnp.concatenate([cos, cos, ones], axis=1)
    sa_h = jnp.concatenate([-sin, zh, zeros], axis=1)
    sb_h = jnp.concatenate([zh, sin, zeros], axis=1)
    tile = lambda a: jnp.tile(a, (1, HEADS_PER_LANE_GROUP))
    return tile(cos_h), tile(sa_h), tile(sb_h)


def _inproj(x2, g, w, tables, t, tm):
    n, d = x2.shape
    b = n // t
    tpb = t // tm
    dils = tuple(dil for _, dil in DIL_BRANCHES)
    assert all(tm % (dil * 16) == 0 for dil in dils)
    row = lambda width: pl.BlockSpec((tm, width), lambda i: (i, 0))
    tab = pl.BlockSpec((tm, LANES), lambda i: (i % tpb, 0))
    plain = [NA_WIDTH] * 3 + [MEM_WIDTH]
    dil_specs = [pl.BlockSpec((None, dil, tm // dil, DIL_WIDTH), lambda i: (i // tpb, 0, i % tpb, 0))
                 for dil in dils] * 3
    dil_shapes = [jax.ShapeDtypeStruct((b, dil, t // dil, DIL_WIDTH), BF16) for dil in dils] * 3
    outs = pl.pallas_call(
        functools.partial(_inproj_kernel, dils=dils),
        grid=(n // tm,),
        in_specs=[row(d), pl.BlockSpec((1, d), lambda i: (0, 0)),
                  pl.BlockSpec(w.shape, lambda i: (0, 0)), tab, tab, tab],
        out_specs=[row(wd) for wd in plain] + dil_specs,
        out_shape=[jax.ShapeDtypeStruct((n, wd), BF16) for wd in plain] + dil_shapes,
        scratch_shapes=[pltpu.VMEM((DIL_WIDTH // LANES, tm, LANES), F32)],
        compiler_params=_cparams(("parallel",)),
        name="inproj",
    )(x2, g, w, *tables)
    nd = len(dils)
    qa, ka, va, qm = outs[:4]
    return qa, ka, va, qm, outs[4:4 + nd], outs[4 + nd:4 + 2 * nd], outs[4 + 2 * nd:]


def _na_bias_table(rpb):
    c = jnp.arange(GRID_W)
    c0 = jnp.clip(c - NA_WIN_COLS // 2, 0, GRID_W - NA_WIN_COLS)
    kc = jnp.arange(GRID_W)
    valid = (kc[None, :] >= c0[:, None]) & (kc[None, :] < c0[:, None] + NA_WIN_COLS)
    coff = kc[None, :] - c[:, None] + (NA_WIN_COLS - 1)
    onehot = (coff[None] == jnp.arange(2 * NA_WIN_COLS - 1)[:, None, None]).astype(F32)
    by_col = jnp.einsum("hrd,dck->hrck", rpb.astype(F32), onehot, precision=lax.Precision.HIGHEST)
    by_col = jnp.where(valid[None, None], by_col * LOG2E, NEG)
    tab = jnp.stack([by_col[:, d:d + NA_WIN_ROWS] for d in range(NA_WIN_ROWS)], axis=1)
    tab = tab.transpose(1, 0, 3, 2, 4)
    return tab.reshape(NA_WIN_ROWS * NA_HEADS // HEADS_PER_LANE_GROUP, HEADS_PER_LANE_GROUP * GRID_W,
                       NA_WIN_ROWS * GRID_W)


def _na_kernel(q_ref, kp_ref, kc_ref, kn_ref, vp_ref, vc_ref, vn_ref, bias_ref, o_ref,
               kwin, vwin, *, rows):
    j = pl.program_id(1)
    blk = NA_WIN_ROWS * GRID_W
    for idx, (kr, vr) in enumerate(((kp_ref, vp_ref), (kc_ref, vc_ref), (kn_ref, vn_ref))):
        kwin[idx * blk:(idx + 1) * blk, :] = kr[...]
        vwin[idx * blk:(idx + 1) * blk, :] = vr[...]

    n_pairs = NA_HEADS // HEADS_PER_LANE_GROUP

    def one_row(i):
        q_rows = pl.ds(pl.multiple_of(i * GRID_W, GRID_W), GRID_W)

        def load():
            r = j * NA_WIN_ROWS + i
            r0 = jnp.clip(r - NA_WIN_ROWS // 2, 0, rows - NA_WIN_ROWS)
            dlt = r0 - r + (NA_WIN_ROWS - 1)
            start = pl.multiple_of((r0 - (j - 1) * NA_WIN_ROWS) * GRID_W, GRID_W)
            return (q_ref[q_rows, :], kwin[pl.ds(start, blk), :], vwin[pl.ds(start, blk), :],
                    lambda p: bias_ref[dlt * n_pairs + p])

        def store(o, lses):
            o_ref[q_rows, :] = o.astype(BF16)

        return load, store

    def row_group(ig, carry):
        _attention_pipelined([one_row(ig * ROWS_PER_TRIP + u) for u in range(ROWS_PER_TRIP)], NA_HEADS)
        return carry

    lax.fori_loop(0, NA_WIN_ROWS // ROWS_PER_TRIP, row_group, 0)


def _na_attention(q, k, v, bias):
    b, t, w = q.shape
    rows = t // GRID_W
    assert rows % NA_WIN_ROWS == 0 and rows >= NA_WIN_ROWS
    nj = rows // NA_WIN_ROWS
    blk = NA_WIN_ROWS * GRID_W
    cur = pl.BlockSpec((None, blk, w), lambda bi, j: (bi, j, 0))
    prev = pl.BlockSpec((None, blk, w), lambda bi, j: (bi, jnp.maximum(j - 1, 0), 0))
    nxt = pl.BlockSpec((None, blk, w), lambda bi, j: (bi, jnp.minimum(j + 1, nj - 1), 0))
    return pl.pallas_call(
        functools.partial(_na_kernel, rows=rows),
        grid=(b, nj),
        in_specs=[cur, prev, cur, nxt, prev, cur, nxt,
                  pl.BlockSpec(bias.shape, lambda bi, j: (0, 0, 0))],
        out_specs=cur,
        out_shape=jax.ShapeDtypeStruct((b, t, w), BF16),
        scratch_shapes=[pltpu.VMEM((3 * blk, w), BF16), pltpu.VMEM((3 * blk, w), BF16)],
        compiler_params=_cparams(("parallel", "parallel")),
        name="na_attn",
    )(q, k, k, k, v, v, v, bias)


def _dil_kernel(q_ref, kp_ref, kc_ref, kn_ref, vp_ref, vc_ref, vn_ref, o_ref, lse_ref,
                kwin, vwin, *, seg, tq, halo):
    i = pl.program_id(2)
    width = tq + 2 * halo
    kwin[0:halo, :] = kp_ref[...]
    kwin[halo:halo + tq, :] = kc_ref[...]
    kwin[halo + tq:width, :] = kn_ref[...]
    vwin[0:halo, :] = _values_with_ones(vp_ref[...])
    vwin[halo:halo + tq, :] = _values_with_ones(vc_ref[...])
    vwin[halo + tq:width, :] = _values_with_ones(vn_ref[...])

    sq = math.gcd(tq, 2 * halo)
    sw = sq + 2 * halo
    a = lax.broadcasted_iota(jnp.int32, (sq, sw), 0)
    c = lax.broadcasted_iota(jnp.int32, (sq, sw), 1)
    band = (c >= a) & (c <= a + 2 * halo)
    lo = lax.broadcasted_iota(jnp.int32, (sq, LANES), 1) < HEAD_DIM

    def sub_block(s):
        row0 = pl.multiple_of(s * sq, sq)

        def load():
            first_key = i * tq + s * sq - halo
            valid = band & (c >= -first_key) & (c < seg - first_key)
            negb = jnp.where(valid, 0.0, NEG).astype(F32)
            negb2 = jnp.concatenate([negb] * HEADS_PER_LANE_GROUP, axis=0)
            return (q_ref[pl.ds(row0, sq), :], kwin[pl.ds(row0, sw), :], vwin[pl.ds(row0, sw), :],
                    lambda p: negb2)

        def store(o, lses):
            o_ref[pl.ds(row0, sq), :] = o.astype(BF16)
            lse_ref[pl.ds(row0, sq), :] = jnp.concatenate(
                [jnp.where(lo, lses[2 * p], lses[2 * p + 1]) for p in range(DIL_HEADS // 2)], axis=-1)

        return load, store

    n_sub = tq // sq
    per_trip = math.gcd(n_sub, ROWS_PER_TRIP)

    def sub_block_group(sg, carry):
        _attention_pipelined([sub_block(sg * per_trip + u) for u in range(per_trip)], DIL_HEADS, skew=0)
        return carry

    lax.fori_loop(0, n_sub // per_trip, sub_block_group, 0)


def _dilated_branch(q, k, v, n_side, tq_max):
    b, dil, seg, w = q.shape
    halo = n_side
    assert seg % halo == 0 and halo % 16 == 0
    tq = math.gcd(tq_max, seg)
    assert tq % halo == 0
    hb = tq // halo
    nhalo = seg // halo
    cur = pl.BlockSpec((None, None, tq, w), lambda bi, r, i: (bi, r, i, 0))
    prev = pl.BlockSpec((None, None, halo, w), lambda bi, r, i: (bi, r, jnp.maximum(i * hb - 1, 0), 0))
    nxt = pl.BlockSpec((None, None, halo, w),
                       lambda bi, r, i: (bi, r, jnp.minimum((i + 1) * hb, nhalo - 1), 0))
    return pl.pallas_call(
        functools.partial(_dil_kernel, seg=seg, tq=tq, halo=halo),
        grid=(b, dil, seg // tq),
        in_specs=[cur, prev, cur, nxt, prev, cur, nxt],
        out_specs=[cur, cur],
        out_shape=[jax.ShapeDtypeStruct((b, dil, seg, w), BF16),
                   jax.ShapeDtypeStruct((b, dil, seg, w), F32)],
        scratch_shapes=[pltpu.VMEM((tq + 2 * halo, w), BF16), pltpu.VMEM((tq + 2 * halo, 2 * w), BF16)],
        compiler_params=_cparams(("parallel", "parallel", "parallel")),
        name=f"dilated_d{dil}",
    )(q, k, k, k, v, v, v)


def _memkv_kernel(mem_ref, g_ref, w_ref, k_ref, v_ref):
    h = _rms(mem_ref[...], g_ref[...]).astype(BF16)
    kv = jnp.dot(h, w_ref[...], preferred_element_type=F32)
    k_ref[...] = kv[:, :MEM_WIDTH].astype(BF16)
    v_ref[...] = kv[:, MEM_WIDTH:].astype(BF16)


def _mem_kv(mem, g, w):
    b, m, d = mem.shape
    blk = lambda width: pl.BlockSpec((None, m, width), lambda bi: (bi, 0, 0))
    return pl.pallas_call(
        _memkv_kernel,
        grid=(b,),
        in_specs=[blk(d), pl.BlockSpec((1, d), lambda bi: (0, 0)), pl.BlockSpec(w.shape, lambda bi: (0, 0))],
        out_specs=[blk(MEM_WIDTH), blk(MEM_WIDTH)],
        out_shape=[jax.ShapeDtypeStruct((b, m, MEM_WIDTH), BF16)] * 2,
        compiler_params=_cparams(("parallel",)),
        name="mem_kv",
    )(mem, g, w)


def _memattn_kernel(q_ref, k_ref, v_ref, o_ref):
    tq = q_ref.shape[0]
    sq = math.gcd(tq, LANES)
    v_aug = _values_with_ones(v_ref[...])

    def sub_block(s):
        rows = slice(s * sq, (s + 1) * sq)

        def store(o, lses):
            o_ref[rows, :] = o.astype(BF16)

        return (lambda: (q_ref[rows, :], k_ref[...], v_aug, lambda p: 0.0)), store

    _attention_pipelined([sub_block(s) for s in range(tq // sq)], MEM_HEADS)


def _mem_attention(q, k, v, tq):
    b, t, w = q.shape
    m = k.shape[1]
    qs = pl.BlockSpec((None, tq, w), lambda bi, i: (bi, i, 0))
    ks = pl.BlockSpec((None, m, w), lambda bi, i: (bi, 0, 0))
    return pl.pallas_call(
        _memattn_kernel,
        grid=(b, t // tq),
        in_specs=[qs, ks, ks],
        out_specs=qs,
        out_shape=jax.ShapeDtypeStruct((b, t, w), BF16),
        compiler_params=_cparams(("parallel", "parallel")),
        name="mem_attn",
    )(q, k, v)


def _token_order(ref, scr):
    dil = ref.shape[0]
    if dil == 1:
        return ref[0].astype(F32)
    groups, tm, _ = scr.shape
    for r in range(dil):
        for g in range(groups):
            scr[g, pl.ds(r, tm // dil, stride=dil), :] = ref[r, :, g * LANES:(g + 1) * LANES].astype(F32)
    return jnp.concatenate([scr[g] for g in range(groups)], axis=-1)


def _branch_mix(ob_refs, lse_refs, o_scrs, l_scrs):
    lses = [_token_order(r, s) for r, s in zip(lse_refs, l_scrs)]
    mx = functools.reduce(jnp.maximum, lses)
    es = [jnp.exp2(l - mx) for l in lses]
    den = functools.reduce(lambda p, q: p + q, es)
    acc = None
    for e, ob, scr in zip(es, ob_refs, o_scrs):
        term = (e / den) * _token_order(ob, scr)
        acc = term if acc is None else acc + term
    return acc


def _outproj_core(oa_ref, ob_refs, lse_refs, om_ref, x_ref, gm_ref, w_ref, o_scrs, l_scrs):
    gm = gm_ref[...]
    e0, e1 = NA_WIDTH, NA_WIDTH + DIL_WIDTH
    ya = _rms(oa_ref[...].astype(F32), gm[:, :e0]).astype(BF16)
    yb = _rms(_branch_mix(ob_refs, lse_refs, o_scrs, l_scrs), gm[:, e0:e1]).astype(BF16)
    ym = _rms(om_ref[...].astype(F32), gm[:, e1:]).astype(BF16)
    y = jnp.concatenate([ya, yb, ym], axis=-1)
    return x_ref[...] + jnp.dot(y, w_ref[...], preferred_element_type=F32)


def _split_scratch(scrs, dils):
    it = iter(scrs)
    o_scrs = [next(it) if dil > 1 else None for dil in dils]
    l_scrs = [next(it) if dil > 1 else None for dil in dils]
    return o_scrs, l_scrs


def _outproj_kernel(oa_ref, ob1, ob2, ob3, l1, l2, l3, om_ref, x_ref, gm_ref, w_ref, xo_ref, *scrs, dils):
    xo_ref[...] = _outproj_core(oa_ref, (ob1, ob2, ob3), (l1, l2, l3), om_ref, x_ref, gm_ref, w_ref,
                                *_split_scratch(scrs, dils))


def _store_row_tiles(ref, val):
    m, d = val.shape
    sub = ref.shape[0] // m
    tw = d // sub
    for j in range(sub):
        ref[pl.ds(j, m, stride=sub), :] = val[:, j * tw:(j + 1) * tw]


def _outproj_router_kernel(oa_ref, ob1, ob2, ob3, l1, l2, l3, om_ref, x_ref, gm_ref, w_ref,
                           gf_ref, wr_ref, xo_ref, h_ref, route_ref, *scrs, n_experts, dils):
    xn = _outproj_core(oa_ref, (ob1, ob2, ob3), (l1, l2, l3), om_ref, x_ref, gm_ref, w_ref,
                       *_split_scratch(scrs, dils))
    xo_ref[...] = xn
    h = _rms(xn, gf_ref[...])
    _store_row_tiles(h_ref, h)
    logits = jnp.dot(h.astype(BF16), wr_ref[...], preferred_element_type=F32)
    lane = lax.broadcasted_iota(jnp.int32, logits.shape, 1)
    lg = jnp.where(lane < n_experts, logits, NEG)
    m1 = jnp.max(lg, axis=-1, keepdims=True)
    i1 = jnp.min(jnp.where(lg == m1, lane, LANES), axis=-1, keepdims=True)
    lg2 = jnp.where(lane == i1, NEG, lg)
    m2 = jnp.max(lg2, axis=-1, keepdims=True)
    i2 = jnp.min(jnp.where(lg2 == m2, lane, LANES), axis=-1, keepdims=True)
    e2 = jnp.exp(m2 - m1)
    g1 = 1.0 / (1.0 + e2)
    g2 = e2 / (1.0 + e2)
    route = jnp.where(lane == 0, i1.astype(F32), 0.0)
    route = jnp.where(lane == 1, i2.astype(F32), route)
    route = jnp.where(lane == 2, g1, route)
    route = jnp.where(lane == 3, g2, route)
    route_ref[...] = route


def _outproj(oa, obs, lses, om, x2, gm, w, t, tm, router=None):
    n, d = x2.shape
    tpb = t // tm
    row = lambda width: pl.BlockSpec((tm, width), lambda i: (i, 0))
    full = lambda a: pl.BlockSpec(a.shape, lambda i: (0,) * a.ndim)
    grouped = lambda a: pl.BlockSpec((None, a.shape[1], tm // a.shape[1], a.shape[3]),
                                     lambda i: (i // tpb, 0, i % tpb, 0))
    dils = tuple(a.shape[1] for a in obs)
    args = [oa, *obs, *lses, om, x2, gm, w]
    in_specs = ([row(NA_WIDTH)] + [grouped(a) for a in obs] + [grouped(a) for a in lses]
                + [row(MEM_WIDTH), row(d), full(gm), full(w)])
    scratch = [pltpu.VMEM((DIL_WIDTH // LANES, tm, LANES), F32)] * (2 * sum(dil > 1 for dil in dils))
    if router is None:
        return pl.pallas_call(
            functools.partial(_outproj_kernel, dils=dils),
            grid=(n // tm,), in_specs=in_specs, out_specs=row(d),
            out_shape=jax.ShapeDtypeStruct((n, d), F32), scratch_shapes=scratch,
            compiler_params=_cparams(("parallel",)), name="outproj",
        )(*args)
    gf, wr, n_experts = router
    sub = ROW_TILE_SUBLANES
    return pl.pallas_call(
        functools.partial(_outproj_router_kernel, n_experts=n_experts, dils=dils),
        grid=(n // tm,), in_specs=in_specs + [full(gf), full(wr)],
        out_specs=[row(d), pl.BlockSpec((tm * sub, d // sub), lambda i: (i, 0)), row(LANES)],
        out_shape=[jax.ShapeDtypeStruct((n, d), F32), jax.ShapeDtypeStruct((n * sub, d // sub), F32),
                   jax.ShapeDtypeStruct((n, LANES), F32)],
        scratch_shapes=scratch,
        compiler_params=_cparams(("parallel",)), name="outproj_router",
    )(*args, gf, wr)


def _swiglu_partial(h, wg_ref, wu_ref, wd_ref, act_scr, emit):
    tf = wg_ref.shape[-1]
    chunk = math.gcd(MXU_COLS, tf)
    for c in range(tf // chunk):
        sl = slice(c * chunk, (c + 1) * chunk)
        gate = jnp.dot(h, wg_ref[:, sl], preferred_element_type=F32)
        up = jnp.dot(h, wu_ref[:, sl], preferred_element_type=F32)
        act_scr[:, sl] = ((gate / (1.0 + jnp.exp(-gate))) * up).astype(BF16)
    act = act_scr[...]
    for c in range(wd_ref.shape[-1] // MXU_COLS):
        emit(c, jnp.dot(act, wd_ref[:, c * MXU_COLS:(c + 1) * MXU_COLS], preferred_element_type=F32))


def _dense_ffn_kernel(x_ref, g_ref, wg_ref, wu_ref, wd_ref, o_ref, h_scr, act_scr):
    @pl.when(pl.program_id(1) == 0)
    def _():
        x = x_ref[...]
        h_scr[...] = _rms(x, g_ref[...]).astype(BF16)
        o_ref[...] = x

    def accumulate(c, part):
        o_ref[:, c * MXU_COLS:(c + 1) * MXU_COLS] += part

    _swiglu_partial(h_scr[...], wg_ref, wu_ref, wd_ref, act_scr, accumulate)


def _dense_ffn(x2, g, wg, wu, wd, li, tm, tf):
    n, d = x2.shape
    f = wg.shape[2]
    row = pl.BlockSpec((tm, d), lambda i, k: (i, 0))
    return pl.pallas_call(
        _dense_ffn_kernel,
        grid=(n // tm, f // tf),
        in_specs=[row, pl.BlockSpec((1, d), lambda i, k: (0, 0)),
                  pl.BlockSpec((None, d, tf), lambda i, k: (li, 0, k)),
                  pl.BlockSpec((None, d, tf), lambda i, k: (li, 0, k)),
                  pl.BlockSpec((None, tf, d), lambda i, k: (li, k, 0))],
        out_specs=row,
        out_shape=jax.ShapeDtypeStruct((n, d), F32),
        scratch_shapes=[pltpu.VMEM((tm, d), BF16), pltpu.VMEM((tm, tf), BF16)],
        compiler_params=_cparams(("parallel", "arbitrary")),
        name="dense_ffn",
    )(x2, g, wg, wu, wd)


def _moe_ffn_kernel(be_ref, nused_ref, xb_ref, wg_ref, wu_ref, wd_ref, y_ref, act_scr):
    j = pl.program_id(0)
    k = pl.program_id(1)
    used = j < nused_ref[0]
    mb = act_scr.shape[0]
    sub = xb_ref.shape[0] // mb
    tw = xb_ref.shape[1]

    @pl.when(k == 0)
    def _():
        y_ref[...] = jnp.zeros_like(y_ref)

    @pl.when(used)
    def _():
        h = jnp.concatenate([xb_ref[pl.ds(c, mb, stride=sub), :].astype(BF16) for c in range(sub)], axis=-1)

        def accumulate(c, part):
            step = min(tw, MXU_COLS)
            for u in range(MXU_COLS // step):
                col = c * MXU_COLS + u * step
                y_ref[pl.ds(col // tw, mb, stride=sub), col % tw:col % tw + step] += (
                    part[:, u * step:(u + 1) * step])

        _swiglu_partial(h, wg_ref, wu_ref, wd_ref, act_scr, accumulate)


def _moe_ffn(buf, block_expert, n_used, wg, wu, wd, li, mb, tf):
    d, f = wg.shape[2], wg.shape[3]
    sub = ROW_TILE_SUBLANES
    tw = d // sub
    p = buf.shape[0] // sub
    nk = f // tf
    kk = lambda j, k, be, nu: jnp.where(j < nu[0], k, nk - 1)
    row = pl.BlockSpec((mb * sub, tw), lambda j, k, be, nu: (j, 0))
    grid_spec = pltpu.PrefetchScalarGridSpec(
        num_scalar_prefetch=2,
        grid=(p // mb, nk),
        in_specs=[row,
                  pl.BlockSpec((None, None, d, tf), lambda j, k, be, nu: (li, be[j], 0, kk(j, k, be, nu))),
                  pl.BlockSpec((None, None, d, tf), lambda j, k, be, nu: (li, be[j], 0, kk(j, k, be, nu))),
                  pl.BlockSpec((None, None, tf, d), lambda j, k, be, nu: (li, be[j], kk(j, k, be, nu), 0))],
        out_specs=row,
        scratch_shapes=[pltpu.VMEM((mb, tf), BF16)],
    )
    return pl.pallas_call(
        _moe_ffn_kernel,
        grid_spec=grid_spec,
        out_shape=jax.ShapeDtypeStruct((p * sub, tw), F32),
        compiler_params=_cparams(("parallel", "arbitrary")),
        name="moe_ffn",
    )(block_expert, n_used, buf, wg, wu, wd)


ISSUE_UNROLL = 8


def _issue_tile_copies(chunk, copy_fn):
    def body(trip, carry):
        for u in range(ISSUE_UNROLL):
            copy_fn(trip, u).start(priority=u % 2)
        return carry

    lax.fori_loop(0, chunk // ISSUE_UNROLL, body, 0)


def _tile(ref, idx, sub):
    return ref.at[pl.ds(pl.multiple_of(idx * sub, sub), sub)]


def _dispatch_kernel(slot_ref, h_ref, out_ref, zero_scr, sem, *, n_tok_steps, sub):
    i = pl.program_id(0)
    n_copies = slot_ref.shape[-1]
    dst = lambda trip, u: _tile(out_ref, slot_ref[0, 0, trip * ISSUE_UNROLL + u], sub)
    tok = lambda trip, u: trip * (ISSUE_UNROLL // TOP_K) + u // TOP_K

    @pl.when(i == 0)
    def _():
        zero_scr[...] = jnp.zeros_like(zero_scr)

    @pl.when(i < n_tok_steps)
    def _():
        _issue_tile_copies(n_copies, lambda trip, u: pltpu.make_async_copy(
            _tile(h_ref, tok(trip, u), sub), dst(trip, u), sem))

    @pl.when(i >= n_tok_steps)
    def _():
        _issue_tile_copies(n_copies, lambda trip, u: pltpu.make_async_copy(zero_scr, dst(trip, u), sem))

    for _ in range(TOP_K):
        pltpu.make_async_copy(h_ref, out_ref.at[pl.ds(0, h_ref.shape[0])], sem).wait()


def _dispatch_rows(h, slot, tm, sub):
    n = h.shape[0] // sub
    tw = h.shape[1]
    p = slot.shape[0]
    n_copies = tm * TOP_K
    assert p % n_copies == 0 and n % tm == 0 and n_copies % ISSUE_UNROLL == 0
    n_tok_steps = n // tm
    return pl.pallas_call(
        functools.partial(_dispatch_kernel, n_tok_steps=n_tok_steps, sub=sub),
        grid=(p // n_copies,),
        in_specs=[pl.BlockSpec((1, 1, n_copies), lambda i: (i, 0, 0), memory_space=pltpu.SMEM),
                  pl.BlockSpec((tm * sub, tw), lambda i: (jnp.minimum(i, n_tok_steps - 1), 0))],
        out_specs=pl.BlockSpec(memory_space=pl.ANY),
        out_shape=jax.ShapeDtypeStruct((p * sub, tw), h.dtype),
        scratch_shapes=[pltpu.VMEM((sub, tw), h.dtype), pltpu.SemaphoreType.DMA(())],
        compiler_params=_cparams(("arbitrary",)),
        name="dispatch_rows",
    )(slot.reshape(p // n_copies, 1, n_copies), h)


def _combine_kernel(cur_ref, nxt_ref, x_ref, route_ref, y_ref, *rest, final_norm, sub):
    o_ref, ybuf, sem = rest[-3:]
    i = pl.program_id(0)
    tm = x_ref.shape[0]
    n_copies = tm * TOP_K
    par = i % 2

    def fetch(idx_ref, buf):
        def copy(trip, u):
            tok = trip * (ISSUE_UNROLL // TOP_K) + u // TOP_K
            return pltpu.make_async_copy(_tile(y_ref, idx_ref[0, 0, trip * ISSUE_UNROLL + u], sub),
                                         _tile(ybuf.at[buf, u % TOP_K], tok, sub), sem.at[buf])
        _issue_tile_copies(n_copies, copy)

    @pl.when(i == 0)
    def _():
        fetch(cur_ref, 0)

    @pl.when(i + 1 < pl.num_programs(0))
    def _():
        fetch(nxt_ref, 1 - par)

    pltpu.make_async_copy(ybuf.at[par], ybuf.at[par], sem.at[par]).wait()

    route = route_ref[...]
    gates = [route[:, TOP_K + k:TOP_K + k + 1] for k in range(TOP_K)]
    tw = x_ref.shape[1] // sub
    for j in range(sub):
        sl = slice(j * tw, (j + 1) * tw)
        mix = None
        for k in range(TOP_K):
            term = gates[k] * ybuf[par, k, pl.ds(j, tm, stride=sub), :]
            mix = term if mix is None else mix + term
        o_ref[:, sl] = x_ref[:, sl] + mix
    if final_norm:
        o_ref[...] = _rms(o_ref[...], rest[0][...])


def _combine(x2, y, dest, route, tm, g_final=None):
    n, d = x2.shape
    sub = ROW_TILE_SUBLANES
    n_copies = tm * TOP_K
    steps = n // tm
    assert n_copies % ISSUE_UNROLL == 0 and ISSUE_UNROLL % TOP_K == 0
    dest3 = dest.reshape(steps, 1, n_copies)
    row = pl.BlockSpec((tm, d), lambda i: (i, 0))
    in_specs = [pl.BlockSpec((1, 1, n_copies), lambda i: (i, 0, 0), memory_space=pltpu.SMEM),
                pl.BlockSpec((1, 1, n_copies), lambda i: (jnp.minimum(i + 1, steps - 1), 0, 0),
                             memory_space=pltpu.SMEM),
                row, pl.BlockSpec((tm, LANES), lambda i: (i, 0)), pl.BlockSpec(memory_space=pl.ANY)]
    args = [dest3, dest3, x2, route, y]
    if g_final is not None:
        in_specs.append(pl.BlockSpec((1, d), lambda i: (0, 0)))
        args.append(g_final)
    return pl.pallas_call(
        functools.partial(_combine_kernel, final_norm=g_final is not None, sub=sub),
        grid=(steps,), in_specs=in_specs, out_specs=row,
        out_shape=jax.ShapeDtypeStruct((n, d), F32),
        scratch_shapes=[pltpu.VMEM((2, TOP_K, tm * sub, d // sub), F32), pltpu.SemaphoreType.DMA((2,))],
        compiler_params=_cparams(("arbitrary",)), name="moe_combine",
    )(*args)


def _final_norm_kernel(x_ref, g_ref, o_ref):
    o_ref[...] = _rms(x_ref[...], g_ref[...])


def _final_norm(x2, g, tm):
    n, d = x2.shape
    row = pl.BlockSpec((tm, d), lambda i: (i, 0))
    return pl.pallas_call(
        _final_norm_kernel, grid=(n // tm,), in_specs=[row, pl.BlockSpec((1, d), lambda i: (0, 0))],
        out_specs=row, out_shape=jax.ShapeDtypeStruct((n, d), F32),
        compiler_params=_cparams(("parallel",)), name="final_norm",
    )(x2, g)


def _routing_plan(route, n_experts, mb):
    n = route.shape[0]
    n_assign = n * TOP_K
    flat_e = route[:, :TOP_K].astype(jnp.int32).reshape(-1)
    onehot = (flat_e[:, None] == jnp.arange(n_experts)[None, :]).astype(jnp.int32)
    csum = jnp.cumsum(onehot, axis=0)
    rank = jnp.sum(csum * onehot, axis=1) - 1
    counts = csum[-1]
    padded = (counts + mb - 1) // mb * mb
    pend = jnp.cumsum(padded)
    pstart = pend - padded
    dest = jnp.sum(pstart[None, :] * onehot, axis=1) + rank
    assert n_assign % mb == 0
    n_blocks = n_assign // mb + n_experts
    block_first = jnp.arange(n_blocks) * mb
    block_expert = jnp.minimum(jnp.sum(block_first[:, None] >= pend[None, :], axis=1), n_experts - 1)
    n_used = (pend[-1] // mb).reshape(1)
    pads = padded - counts
    cpad = jnp.cumsum(pads)
    i = jnp.arange(n_experts * mb)
    grp = jnp.sum(i[:, None] >= cpad[None, :], axis=1)
    grp_hot = (grp[:, None] == jnp.arange(n_experts + 1)[None, :]).astype(jnp.int32)
    first_free = jnp.concatenate([pstart + counts, pend[-1:]])
    before = jnp.concatenate([jnp.zeros((1,), cpad.dtype), cpad])
    free = jnp.sum(grp_hot * (first_free - before)[None, :], axis=1) + i
    slot = jnp.concatenate([dest, free]).astype(jnp.int32)
    return dest.astype(jnp.int32), slot, block_expert.astype(jnp.int32), n_used.astype(jnp.int32)


def _tiles(n, t):
    return dict(tm_in=math.gcd(1024, t), tm_proj=math.gcd(512, t), tm_ffn=math.gcd(1024, n), tf=1792,
                tq_dil=1024, tq_mem=math.gcd(512, t), moe_block=math.gcd(1024, n))


@jax.jit
def _forward(x, mem, g_mix_norm, w_in, rpb, g_mem_norm, w_mem_kv, g_mix_out, w_out,
             g_ffn_norm, w_dense_gate, w_dense_up, w_dense_down, w_router,
             w_moe_gate, w_moe_up, w_moe_down, g_final):
    b, t, d = x.shape
    n = b * t
    depth = w_in.shape[0]
    n_experts = w_router.shape[-1]
    ts = _tiles(n, t)
    tf = math.gcd(ts["tf"], w_dense_gate.shape[-1])
    tables = _rope_tables(t)
    dense_w = [w.astype(BF16) for w in (w_dense_gate, w_dense_up, w_dense_down)]
    moe_w = [w.astype(BF16) for w in (w_moe_gate, w_moe_up, w_moe_down)]
    row1 = lambda a: a.reshape(1, -1)
    x2 = x.reshape(n, d)
    out = None
    sub = ROW_TILE_SUBLANES
    for layer in range(depth):
        qa, ka, va, qm, qbs, kbs, vbs = _inproj(x2, row1(g_mix_norm[layer]), w_in[layer].astype(BF16),
                                                tables, t, ts["tm_in"])
        shp = lambda a: a.reshape(b, t, a.shape[-1])
        oa = _na_attention(shp(qa), shp(ka), shp(va), _na_bias_table(rpb[layer])).reshape(n, NA_WIDTH)
        obs, lses = [], []
        for (window, dil), qb, kb, vb in zip(DIL_BRANCHES, qbs, kbs, vbs):
            o, lse = _dilated_branch(qb, kb, vb, window // (2 * dil), ts["tq_dil"])
            obs.append(o)
            lses.append(lse)
        km, vm = _mem_kv(mem, row1(g_mem_norm[layer]), w_mem_kv[layer].astype(BF16))
        om = _mem_attention(shp(qm), km, vm, ts["tq_mem"]).reshape(n, MEM_WIDTH)
        gm, wo = row1(g_mix_out[layer]), w_out[layer].astype(BF16)
        i = layer // 2
        if layer % 2 == 0:
            x2 = _outproj(oa, obs, lses, om, x2, gm, wo, t, ts["tm_in"])
            x2 = _dense_ffn(x2, row1(g_ffn_norm[layer]), *dense_w, i, ts["tm_ffn"], tf)
            if layer == depth - 1:
                out = _final_norm(x2, row1(g_final), ts["tm_proj"])
        else:
            wr = jnp.pad(w_router[i], ((0, 0), (0, LANES - n_experts))).astype(BF16)
            x2, h, route = _outproj(oa, obs, lses, om, x2, gm, wo, t, ts["tm_in"],
                                    router=(row1(g_ffn_norm[layer]), wr, n_experts))
            mb = ts["moe_block"]
            dest, slot, block_expert, n_used = _routing_plan(route, n_experts, mb)
            buf = _dispatch_rows(h, slot, ts["tm_proj"], sub)
            y = _moe_ffn(buf, block_expert, n_used, *moe_w, i, mb, tf)
            if layer == depth - 1:
                out = _combine(x2, y, dest, route, ts["tm_proj"], g_final=row1(g_final))
            else:
                x2 = _combine(x2, y, dest, route, ts["tm_proj"])
    return out.reshape(b, t, d)


def kernel(x, mem, g_mix_norm, w_in, rpb, g_mem_norm, w_mem_kv, g_mix_out, w_out, g_ffn_norm,
           w_dense_gate, w_dense_up, w_dense_down, w_router, w_moe_gate, w_moe_up, w_moe_down, g_final):
    return _forward(x, mem, g_mix_norm, w_in, rpb, g_mem_norm, w_mem_kv, g_mix_out, w_out, g_ffn_norm,
                    w_dense_gate, w_dense_up, w_dense_down, w_router, w_moe_gate, w_moe_up, w_moe_down,
                    g_final)
```

```python
import functools
import math

import jax
import jax.numpy as jnp
from jax import lax
from jax.experimental import pallas as pl
from jax.experimental.pallas import tpu as pltpu

HEAD_DIM = 64
NA_HEADS = 6
DIL_HEADS = 6
MEM_HEADS = 4
NA_WIDTH = NA_HEADS * HEAD_DIM
DIL_WIDTH = DIL_HEADS * HEAD_DIM
MEM_WIDTH = MEM_HEADS * HEAD_DIM
GRID_W = 64
NA_WIN_ROWS = 8
NA_WIN_COLS = 16
DIL_BRANCHES = ((128, 1), (512, 4), (2048, 16))
ROPE_THETA = 500000.0
ROT_DIM = HEAD_DIM // 4
TOP_K = 2
RMS_EPS = 1e-6
ATTN_SCALE = HEAD_DIM ** -0.5
LOG2E = math.log2(math.e)
Q_SCALE = ATTN_SCALE * LOG2E

LANES = 128
MXU_COLS = 256
HEADS_PER_LANE_GROUP = LANES // HEAD_DIM
NEG = -1e30
ROWS_PER_TRIP = 8
ROW_TILE_SUBLANES = 8
VMEM_LIMIT = 56 * 1024 * 1024

F32 = jnp.float32
BF16 = jnp.bfloat16


def _cparams(sem):
    return pltpu.CompilerParams(dimension_semantics=sem, vmem_limit_bytes=VMEM_LIMIT)


def _rms(x, g):
    ms = jnp.mean(x * x, axis=-1, keepdims=True)
    return x * lax.rsqrt(ms + RMS_EPS) * g


def _lane_group(a, p):
    return a[:, p * LANES:(p + 1) * LANES]


def _attn_scores(q, k, bias_fn, n_heads):
    m = q.shape[0]
    assert HEADS_PER_LANE_GROUP == 2
    lo = lax.broadcasted_iota(jnp.int32, (m, LANES), 1) < HEAD_DIM
    s_parts = []
    for p in range(n_heads // HEADS_PER_LANE_GROUP):
        qp = _lane_group(q, p)
        zero = jnp.zeros_like(qp)
        q2 = jnp.concatenate([jnp.where(lo, qp, zero), jnp.where(lo, zero, qp)], axis=0)
        s_parts.append(lax.dot_general(q2, _lane_group(k, p), (((1,), (1,)), ((), ())),
                                       preferred_element_type=F32) + bias_fn(p))
    return jnp.concatenate(s_parts, axis=0)


def _attn_softmax(s):
    mx = jnp.max(s, axis=-1, keepdims=True)
    return jnp.exp2((s - mx).astype(BF16)), mx


def _values_with_ones(v):
    ones = jnp.ones((v.shape[0], LANES), v.dtype)
    return jnp.concatenate([blk for p in range(v.shape[1] // LANES) for blk in (_lane_group(v, p), ones)],
                           axis=-1)


def _attn_values(eb, mx, v, n_heads):
    m = eb.shape[0] // n_heads
    n_pairs = n_heads // HEADS_PER_LANE_GROUP
    with_ones = v.shape[1] == n_pairs * 2 * LANES
    lo = lax.broadcasted_iota(jnp.int32, (m, LANES), 1) < HEAD_DIM
    if not with_ones:
        den_all = jnp.sum(eb.astype(F32), axis=-1, keepdims=True)
    outs, lses = [], []
    for p in range(n_pairs):
        rows = slice(p * 2 * m, (p + 1) * 2 * m)
        if with_ones:
            both = jnp.dot(eb[rows], v[:, p * 2 * LANES:(p + 1) * 2 * LANES], preferred_element_type=F32)
            num, den = both[:, :LANES], both[:, LANES:]
        else:
            num, den = jnp.dot(eb[rows], _lane_group(v, p), preferred_element_type=F32), den_all[rows]
        o2 = num / den
        outs.append(jnp.where(lo, o2[:m], o2[m:]))
        lse = mx[rows] + jnp.log2(den)
        lses += [lse[:m], lse[m:]]
    return jnp.concatenate(outs, axis=-1), lses


def _attention_pipelined(items, n_heads, skew=1):
    n = len(items)
    loaded, scores, soft = {}, {}, {}
    for step in range(n + 2 * skew):
        if step < n:
            loaded[step] = items[step][0]()
            q, k, _, bias_fn = loaded[step]
            scores[step] = _attn_scores(q, k, bias_fn, n_heads)
        if 0 <= step - skew < n:
            soft[step - skew] = _attn_softmax(scores.pop(step - skew))
        if 0 <= step - 2 * skew < n:
            eb, mx = soft.pop(step - 2 * skew)
            items[step - 2 * skew][1](*_attn_values(eb, mx, loaded.pop(step - 2 * skew)[2], n_heads))


def _inproj_kernel(x_ref, g_ref, w_ref, cos_ref, sa_ref, sb_ref,
                   qa_ref, ka_ref, va_ref, qm_ref, *rest, dils):
    dil_refs, scr = rest[:-1], rest[-1]
    tm = x_ref.shape[0]
    h = _rms(x_ref[...], g_ref[...]).astype(BF16)

    def emit_dilated(val, refs):
        groups = val.shape[1] // LANES
        for g in range(groups):
            scr[g] = val[:, g * LANES:(g + 1) * LANES]
        for dil, ref in zip(dils, refs):
            if dil == 1:
                ref[0] = val.astype(BF16)
                continue
            for r in range(dil):
                for g in range(groups):
                    ref[r, :, g * LANES:(g + 1) * LANES] = (
                        scr[g, pl.ds(r, tm // dil, stride=dil), :].astype(BF16))

    def proj_pair(c0, width_a, width_b):
        both = jnp.dot(h, w_ref[:, c0:c0 + width_a + width_b], preferred_element_type=F32)
        return both[:, :width_a], both[:, width_a:]

    def rope(a):
        cos, sa, sb = cos_ref[...], sa_ref[...], sb_ref[...]
        half = ROT_DIM // 2
        parts = []
        for gidx in range(a.shape[1] // LANES):
            xg = a[:, gidx * LANES:(gidx + 1) * LANES]
            parts.append(xg * cos + pltpu.roll(xg, LANES - half, 1) * sa + pltpu.roll(xg, half, 1) * sb)
        return jnp.concatenate(parts, axis=-1)

    nd = len(dils)
    assert all(wd % MXU_COLS == 0 for wd in (2 * NA_WIDTH, NA_WIDTH + DIL_WIDTH, 2 * DIL_WIDTH))
    kb, vb = proj_pair(3 * NA_WIDTH + DIL_WIDTH, DIL_WIDTH, DIL_WIDTH)
    emit_dilated(rope(kb), dil_refs[nd:2 * nd])
    emit_dilated(vb, dil_refs[2 * nd:3 * nd])
    va, qb = proj_pair(2 * NA_WIDTH, NA_WIDTH, DIL_WIDTH)
    va_ref[...] = va.astype(BF16)
    emit_dilated(rope(qb * Q_SCALE), dil_refs[0:nd])
    qa, ka = proj_pair(0, NA_WIDTH, NA_WIDTH)
    qa_ref[...] = (qa * Q_SCALE).astype(BF16)
    ka_ref[...] = ka.astype(BF16)
    c = 3 * NA_WIDTH + 3 * DIL_WIDTH
    qm_ref[...] = (jnp.dot(h, w_ref[:, c:c + MEM_WIDTH], preferred_element_type=F32) * Q_SCALE).astype(BF16)


def _rope_tables(t):
    half = ROT_DIM // 2
    inv_freq = ROPE_THETA ** (-jnp.arange(0, ROT_DIM, 2, dtype=F32) / ROT_DIM)
    ang = jnp.arange(t, dtype=F32)[:, None] * inv_freq[None, :]
    cos, sin = jnp.cos(ang), jnp.sin(ang)
    ones = jnp.ones((t, HEAD_DIM - ROT_DIM), F32)
    zeros = jnp.zeros((t, HEAD_DIM - ROT_DIM), F32)
    zh = jnp.zeros((t, half), F32)
    cos_h = jnp.concatenate([cos, cos, ones], axis=1)
    sa_h = jnp.concatenate([-sin, zh, zeros], axis=1)
    sb_h = jnp.concatenate([zh, sin, zeros], axis=1)
    tile = lambda a: jnp.tile(a, (1, HEADS_PER_LANE_GROUP))
    return tile(cos_h), tile(sa_h), tile(sb_h)


def _inproj(x2, g, w, tables, t, tm):
    n, d = x2.shape
    b = n // t
    tpb = t // tm
    dils = tuple(dil for _, dil in DIL_BRANCHES)
    assert all(tm % (dil * 16) == 0 for dil in dils)
    row = lambda width: pl.BlockSpec((tm, width), lambda i: (i, 0))
    tab = pl.BlockSpec((tm, LANES), lambda i: (i % tpb, 0))
    plain = [NA_WIDTH] * 3 + [MEM_WIDTH]
    dil_specs = [pl.BlockSpec((None, dil, tm // dil, DIL_WIDTH), lambda i: (i // tpb, 0, i % tpb, 0))
                 for dil in dils] * 3
    dil_shapes = [jax.ShapeDtypeStruct((b, dil, t // dil, DIL_WIDTH), BF16) for dil in dils] * 3
    outs = pl.pallas_call(
        functools.partial(_inproj_kernel, dils=dils),
        grid=(n // tm,),
        in_specs=[row(d), pl.BlockSpec((1, d), lambda i: (0, 0)),
                  pl.BlockSpec(w.shape, lambda i: (0, 0)), tab, tab, tab],
        out_specs=[row(wd) for wd in plain] + dil_specs,
        out_shape=[jax.ShapeDtypeStruct((n, wd), BF16) for wd in plain] + dil_shapes,
        scratch_shapes=[pltpu.VMEM((DIL_WIDTH // LANES, tm, LANES), F32)],
        compiler_params=_cparams(("parallel",)),
        name="inproj",
    )(x2, g, w, *tables)
    nd = len(dils)
    qa, ka, va, qm = outs[:4]
    return qa, ka, va, qm, outs[4:4 + nd], outs[4 + nd:4 + 2 * nd], outs[4 + 2 * nd:]


def _na_bias_table(rpb):
    c = jnp.arange(GRID_W)
    c0 = jnp.clip(c - NA_WIN_COLS // 2, 0, GRID_W - NA_WIN_COLS)
    kc = jnp.arange(GRID_W)
    valid = (kc[None, :] >= c0[:, None]) & (kc[None, :] < c0[:, None] + NA_WIN_COLS)
    coff = kc[None, :] - c[:, None] + (NA_WIN_COLS - 1)
    onehot = (coff[None] == jnp.arange(2 * NA_WIN_COLS - 1)[:, None, None]).astype(F32)
    by_col = jnp.einsum("hrd,dck->hrck", rpb.astype(F32), onehot, precision=lax.Precision.HIGHEST)
    by_col = jnp.where(valid[None, None], by_col * LOG2E, NEG)
    tab = jnp.stack([by_col[:, d:d + NA_WIN_ROWS] for d in range(NA_WIN_ROWS)], axis=1)
    tab = tab.transpose(1, 0, 3, 2, 4)
    return tab.reshape(NA_WIN_ROWS * NA_HEADS // HEADS_PER_LANE_GROUP, HEADS_PER_LANE_GROUP * GRID_W,
                       NA_WIN_ROWS * GRID_W)


def _na_kernel(q_ref, kp_ref, kc_ref, kn_ref, vp_ref, vc_ref, vn_ref, bias_ref, o_ref,
               kwin, vwin, *, rows):
    j = pl.program_id(1)
    blk = NA_WIN_ROWS * GRID_W
    for idx, (kr, vr) in enumerate(((kp_ref, vp_ref), (kc_ref, vc_ref), (kn_ref, vn_ref))):
        kwin[idx * blk:(idx + 1) * blk, :] = kr[...]
        vwin[idx * blk:(idx + 1) * blk, :] = vr[...]

    n_pairs = NA_HEADS // HEADS_PER_LANE_GROUP

    def one_row(i):
        q_rows = pl.ds(pl.multiple_of(i * GRID_W, GRID_W), GRID_W)

        def load():
            r = j * NA_WIN_ROWS + i
            r0 = jnp.clip(r - NA_WIN_ROWS // 2, 0, rows - NA_WIN_ROWS)
            dlt = r0 - r + (NA_WIN_ROWS - 1)
            start = pl.multiple_of((r0 - (j - 1) * NA_WIN_ROWS) * GRID_W, GRID_W)
            return (q_ref[q_rows, :], kwin[pl.ds(start, blk), :], vwin[pl.ds(start, blk), :],
                    lambda p: bias_ref[dlt * n_pairs + p])

        def store(o, lses):
            o_ref[q_rows, :] = o.astype(BF16)

        return load, store

    def row_group(ig, carry):
        _attention_pipelined([one_row(ig * ROWS_PER_TRIP + u) for u in range(ROWS_PER_TRIP)], NA_HEADS)
        return carry

    lax.fori_loop(0, NA_WIN_ROWS // ROWS_PER_TRIP, row_group, 0)


def _na_attention(q, k, v, bias):
    b, t, w = q.shape
    rows = t // GRID_W
    assert rows % NA_WIN_ROWS == 0 and rows >= NA_WIN_ROWS
    nj = rows // NA_WIN_ROWS
    blk = NA_WIN_ROWS * GRID_W
    cur = pl.BlockSpec((None, blk, w), lambda bi, j: (bi, j, 0))
    prev = pl.BlockSpec((None, blk, w), lambda bi, j: (bi, jnp.maximum(j - 1, 0), 0))
    nxt = pl.BlockSpec((None, blk, w), lambda bi, j: (bi, jnp.minimum(j + 1, nj - 1), 0))
    return pl.pallas_call(
        functools.partial(_na_kernel, rows=rows),
        grid=(b, nj),
        in_specs=[cur, prev, cur, nxt, prev, cur, nxt,
                  pl.BlockSpec(bias.shape, lambda bi, j: (0, 0, 0))],
        out_specs=cur,
        out_shape=jax.ShapeDtypeStruct((b, t, w), BF16),
        scratch_shapes=[pltpu.VMEM((3 * blk, w), BF16), pltpu.VMEM((3 * blk, w), BF16)],
        compiler_params=_cparams(("parallel", "parallel")),
        name="na_attn",
    )(q, k, k, k, v, v, v, bias)


def _dil_kernel(q_ref, kp_ref, kc_ref, kn_ref, vp_ref, vc_ref, vn_ref, o_ref, lse_ref,
                kwin, vwin, *, seg, tq, halo):
    i = pl.program_id(2)
    width = tq + 2 * halo
    kwin[0:halo, :] = kp_ref[...]
    kwin[halo:halo + tq, :] = kc_ref[...]
    kwin[halo + tq:width, :] = kn_ref[...]
    vwin[0:halo, :] = _values_with_ones(vp_ref[...])
    vwin[halo:halo + tq, :] = _values_with_ones(vc_ref[...])
    vwin[halo + tq:width, :] = _values_with_ones(vn_ref[...])

    sq = math.gcd(tq, 2 * halo)
    sw = sq + 2 * halo
    a = lax.broadcasted_iota(jnp.int32, (sq, sw), 0)
    c = lax.broadcasted_iota(jnp.int32, (sq, sw), 1)
    band = (c >= a) & (c <= a + 2 * halo)
    lo = lax.broadcasted_iota(jnp.int32, (sq, LANES), 1) < HEAD_DIM

    def sub_block(s):
        row0 = pl.multiple_of(s * sq, sq)

        def load():
            first_key = i * tq + s * sq - halo
            valid = band & (c >= -first_key) & (c < seg - first_key)
            negb = jnp.where(valid, 0.0, NEG).astype(F32)
            negb2 = jnp.concatenate([negb] * HEADS_PER_LANE_GROUP, axis=0)
            return (q_ref[pl.ds(row0, sq), :], kwin[pl.ds(row0, sw), :], vwin[pl.ds(row0, sw), :],
                    lambda p: negb2)

        def store(o, lses):
            o_ref[pl.ds(row0, sq), :] = o.astype(BF16)
            lse_ref[pl.ds(row0, sq), :] = jnp.concatenate(
                [jnp.where(lo, lses[2 * p], lses[2 * p + 1]) for p in range(DIL_HEADS // 2)], axis=-1)

        return load, store

    n_sub = tq // sq
    per_trip = math.gcd(n_sub, ROWS_PER_TRIP)

    def sub_block_group(sg, carry):
        _attention_pipelined([sub_block(sg * per_trip + u) for u in range(per_trip)], DIL_HEADS, skew=0)
        return carry

    lax.fori_loop(0, n_sub // per_trip, sub_block_group, 0)


def _dilated_branch(q, k, v, n_side, tq_max):
    b, dil, seg, w = q.shape
    halo = n_side
    assert seg % halo == 0 and halo % 16 == 0
    tq = math.gcd(tq_max, seg)
    assert tq % halo == 0
    hb = tq // halo
    nhalo = seg // halo
    cur = pl.BlockSpec((None, None, tq, w), lambda bi, r, i: (bi, r, i, 0))
    prev = pl.BlockSpec((None, None, halo, w), lambda bi, r, i: (bi, r, jnp.maximum(i * hb - 1, 0), 0))
    nxt = pl.BlockSpec((None, None, halo, w),
                       lambda bi, r, i: (bi, r, jnp.minimum((i + 1) * hb, nhalo - 1), 0))
    return pl.pallas_call(
        functools.partial(_dil_kernel, seg=seg, tq=tq, halo=halo),
        grid=(b, dil, seg // tq),
        in_specs=[cur, prev, cur, nxt, prev, cur, nxt],
        out_specs=[cur, cur],
        out_shape=[jax.ShapeDtypeStruct((b, dil, seg, w), BF16),
                   jax.ShapeDtypeStruct((b, dil, seg, w), F32)],
        scratch_shapes=[pltpu.VMEM((tq + 2 * halo, w), BF16), pltpu.VMEM((tq + 2 * halo, 2 * w), BF16)],
        compiler_params=_cparams(("parallel", "parallel", "parallel")),
        name=f"dilated_d{dil}",
    )(q, k, k, k, v, v, v)


def _memkv_kernel(mem_ref, g_ref, w_ref, k_ref, v_ref):
    h = _rms(mem_ref[...], g_ref[...]).astype(BF16)
    kv = jnp.dot(h, w_ref[...], preferred_element_type=F32)
    k_ref[...] = kv[:, :MEM_WIDTH].astype(BF16)
    v_ref[...] = kv[:, MEM_WIDTH:].astype(BF16)


def _mem_kv(mem, g, w):
    b, m, d = mem.shape
    blk = lambda width: pl.BlockSpec((None, m, width), lambda bi: (bi, 0, 0))
    return pl.pallas_call(
        _memkv_kernel,
        grid=(b,),
        in_specs=[blk(d), pl.BlockSpec((1, d), lambda bi: (0, 0)), pl.BlockSpec(w.shape, lambda bi: (0, 0))],
        out_specs=[blk(MEM_WIDTH), blk(MEM_WIDTH)],
        out_shape=[jax.ShapeDtypeStruct((b, m, MEM_WIDTH), BF16)] * 2,
        compiler_params=_cparams(("parallel",)),
        name="mem_kv",
    )(mem, g, w)


def _memattn_kernel(q_ref, k_ref, v_ref, o_ref):
    tq = q_ref.shape[0]
    sq = math.gcd(tq, LANES)
    v_aug = _values_with_ones(v_ref[...])

    def sub_block(s):
        rows = slice(s * sq, (s + 1) * sq)

        def store(o, lses):
            o_ref[rows, :] = o.astype(BF16)

        return (lambda: (q_ref[rows, :], k_ref[...], v_aug, lambda p: 0.0)), store

    _attention_pipelined([sub_block(s) for s in range(tq // sq)], MEM_HEADS)


def _mem_attention(q, k, v, tq):
    b, t, w = q.shape
    m = k.shape[1]
    qs = pl.BlockSpec((None, tq, w), lambda bi, i: (bi, i, 0))
    ks = pl.BlockSpec((None, m, w), lambda bi, i: (bi, 0, 0))
    return pl.pallas_call(
        _memattn_kernel,
        grid=(b, t // tq),
        in_specs=[qs, ks, ks],
        out_specs=qs,
        out_shape=jax.ShapeDtypeStruct((b, t, w), BF16),
        compiler_params=_cparams(("parallel", "parallel")),
        name="mem_attn",
    )(q, k, v)


def _token_order(ref, scr):
    dil = ref.shape[0]
    if dil == 1:
        return ref[0].astype(F32)
    groups, tm, _ = scr.shape
    for r in range(dil):
        for g in range(groups):
            scr[g, pl.ds(r, tm // dil, stride=dil), :] = ref[r, :, g * LANES:(g + 1) * LANES].astype(F32)
    return jnp.concatenate([scr[g] for g in range(groups)], axis=-1)


def _branch_mix(ob_refs, lse_refs, o_scrs, l_scrs):
    lses = [_token_order(r, s) for r, s in zip(lse_refs, l_scrs)]
    mx = functools.reduce(jnp.maximum, lses)
    es = [jnp.exp2(l - mx) for l in lses]
    den = functools.reduce(lambda p, q: p + q, es)
    acc = None
    for e, ob, scr in zip(es, ob_refs, o_scrs):
        term = (e / den) * _token_order(ob, scr)
        acc = term if acc is None else acc + term
    return acc


def _outproj_core(oa_ref, ob_refs, lse_refs, om_ref, x_ref, gm_ref, w_ref, o_scrs, l_scrs):
    gm = gm_ref[...]
    e0, e1 = NA_WIDTH, NA_WIDTH + DIL_WIDTH
    ya = _rms(oa_ref[...].astype(F32), gm[:, :e0]).astype(BF16)
    yb = _rms(_branch_mix(ob_refs, lse_refs, o_scrs, l_scrs), gm[:, e0:e1]).astype(BF16)
    ym = _rms(om_ref[...].astype(F32), gm[:, e1:]).astype(BF16)
    y = jnp.concatenate([ya, yb, ym], axis=-1)
    return x_ref[...] + jnp.dot(y, w_ref[...], preferred_element_type=F32)


def _split_scratch(scrs, dils):
    it = iter(scrs)
    o_scrs = [next(it) if dil > 1 else None for dil in dils]
    l_scrs = [next(it) if dil > 1 else None for dil in dils]
    return o_scrs, l_scrs


def _outproj_kernel(oa_ref, ob1, ob2, ob3, l1, l2, l3, om_ref, x_ref, gm_ref, w_ref, xo_ref, *scrs, dils):
    xo_ref[...] = _outproj_core(oa_ref, (ob1, ob2, ob3), (l1, l2, l3), om_ref, x_ref, gm_ref, w_ref,
                                *_split_scratch(scrs, dils))


def _store_row_tiles(ref, val):
    m, d = val.shape
    sub = ref.shape[0] // m
    tw = d // sub
    for j in range(sub):
        ref[pl.ds(j, m, stride=sub), :] = val[:, j * tw:(j + 1) * tw]


def _outproj_router_kernel(oa_ref, ob1, ob2, ob3, l1, l2, l3, om_ref, x_ref, gm_ref, w_ref,
                           gf_ref, wr_ref, xo_ref, h_ref, route_ref, *scrs, n_experts, dils):
    xn = _outproj_core(oa_ref, (ob1, ob2, ob3), (l1, l2, l3), om_ref, x_ref, gm_ref, w_ref,
                       *_split_scratch(scrs, dils))
    xo_ref[...] = xn
    h = _rms(xn, gf_ref[...])
    _store_row_tiles(h_ref, h)
    logits = jnp.dot(h.astype(BF16), wr_ref[...], preferred_element_type=F32)
    lane = lax.broadcasted_iota(jnp.int32, logits.shape, 1)
    lg = jnp.where(lane < n_experts, logits, NEG)
    m1 = jnp.max(lg, axis=-1, keepdims=True)
    i1 = jnp.min(jnp.where(lg == m1, lane, LANES), axis=-1, keepdims=True)
    lg2 = jnp.where(lane == i1, NEG, lg)
    m2 = jnp.max(lg2, axis=-1, keepdims=True)
    i2 = jnp.min(jnp.where(lg2 == m2, lane, LANES), axis=-1, keepdims=True)
    e2 = jnp.exp(m2 - m1)
    g1 = 1.0 / (1.0 + e2)
    g2 = e2 / (1.0 + e2)
    route = jnp.where(lane == 0, i1.astype(F32), 0.0)
    route = jnp.where(lane == 1, i2.astype(F32), route)
    route = jnp.where(lane == 2, g1, route)
    route = jnp.where(lane == 3, g2, route)
    route_ref[...] = route


def _outproj(oa, obs, lses, om, x2, gm, w, t, tm, router=None):
    n, d = x2.shape
    tpb = t // tm
    row = lambda width: pl.BlockSpec((tm, width), lambda i: (i, 0))
    full = lambda a: pl.BlockSpec(a.shape, lambda i: (0,) * a.ndim)
    grouped = lambda a: pl.BlockSpec((None, a.shape[1], tm // a.shape[1], a.shape[3]),
                                     lambda i: (i // tpb, 0, i % tpb, 0))
    dils = tuple(a.shape[1] for a in obs)
    args = [oa, *obs, *lses, om, x2, gm, w]
    in_specs = ([row(NA_WIDTH)] + [grouped(a) for a in obs] + [grouped(a) for a in lses]
                + [row(MEM_WIDTH), row(d), full(gm), full(w)])
    scratch = [pltpu.VMEM((DIL_WIDTH // LANES, tm, LANES), F32)] * (2 * sum(dil > 1 for dil in dils))
    if router is None:
        return pl.pallas_call(
            functools.partial(_outproj_kernel, dils=dils),
            grid=(n // tm,), in_specs=in_specs, out_specs=row(d),
            out_shape=jax.ShapeDtypeStruct((n, d), F32), scratch_shapes=scratch,
            compiler_params=_cparams(("parallel",)), name="outproj",
        )(*args)
    gf, wr, n_experts = router
    sub = ROW_TILE_SUBLANES
    return pl.pallas_call(
        functools.partial(_outproj_router_kernel, n_experts=n_experts, dils=dils),
        grid=(n // tm,), in_specs=in_specs + [full(gf), full(wr)],
        out_specs=[row(d), pl.BlockSpec((tm * sub, d // sub), lambda i: (i, 0)), row(LANES)],
        out_shape=[jax.ShapeDtypeStruct((n, d), F32), jax.ShapeDtypeStruct((n * sub, d // sub), F32),
                   jax.ShapeDtypeStruct((n, LANES), F32)],
        scratch_shapes=scratch,
        compiler_params=_cparams(("parallel",)), name="outproj_router",
    )(*args, gf, wr)


def _swiglu_partial(h, wg_ref, wu_ref, wd_ref, act_scr, emit):
    tf = wg_ref.shape[-1]
    chunk = math.gcd(MXU_COLS, tf)
    for c in range(tf // chunk):
        sl = slice(c * chunk, (c + 1) * chunk)
        gate = jnp.dot(h, wg_ref[:, sl], preferred_element_type=F32)
        up = jnp.dot(h, wu_ref[:, sl], preferred_element_type=F32)
        act_scr[:, sl] = ((gate / (1.0 + jnp.exp(-gate))) * up).astype(BF16)
    act = act_scr[...]
    for c in range(wd_ref.shape[-1] // MXU_COLS):
        emit(c, jnp.dot(act, wd_ref[:, c * MXU_COLS:(c + 1) * MXU_COLS], preferred_element_type=F32))


def _dense_ffn_kernel(x_ref, g_ref, wg_ref, wu_ref, wd_ref, o_ref, h_scr, act_scr):
    @pl.when(pl.program_id(1) == 0)
    def _():
        x = x_ref[...]
        h_scr[...] = _rms(x, g_ref[...]).astype(BF16)
        o_ref[...] = x

    def accumulate(c, part):
        o_ref[:, c * MXU_COLS:(c + 1) * MXU_COLS] += part

    _swiglu_partial(h_scr[...], wg_ref, wu_ref, wd_ref, act_scr, accumulate)


def _dense_ffn(x2, g, wg, wu, wd, li, tm, tf):
    n, d = x2.shape
    f = wg.shape[2]
    row = pl.BlockSpec((tm, d), lambda i, k: (i, 0))
    return pl.pallas_call(
        _dense_ffn_kernel,
        grid=(n // tm, f // tf),
        in_specs=[row, pl.BlockSpec((1, d), lambda i, k: (0, 0)),
                  pl.BlockSpec((None, d, tf), lambda i, k: (li, 0, k)),
                  pl.BlockSpec((None, d, tf), lambda i, k: (li, 0, k)),
                  pl.BlockSpec((None, tf, d), lambda i, k: (li, k, 0))],
        out_specs=row,
        out_shape=jax.ShapeDtypeStruct((n, d), F32),
        scratch_shapes=[pltpu.VMEM((tm, d), BF16), pltpu.VMEM((tm, tf), BF16)],
        compiler_params=_cparams(("parallel", "arbitrary")),
        name="dense_ffn",
    )(x2, g, wg, wu, wd)


def _moe_ffn_kernel(be_ref, nused_ref, xb_ref, wg_ref, wu_ref, wd_ref, y_ref, act_scr):
    j = pl.program_id(0)
    k = pl.program_id(1)
    used = j < nused_ref[0]
    mb = act_scr.shape[0]
    sub = xb_ref.shape[0] // mb
    tw = xb_ref.shape[1]

    @pl.when(k == 0)
    def _():
        y_ref[...] = jnp.zeros_like(y_ref)

    @pl.when(used)
    def _():
        h = jnp.concatenate([xb_ref[pl.ds(c, mb, stride=sub), :].astype(BF16) for c in range(sub)], axis=-1)

        def accumulate(c, part):
            step = min(tw, MXU_COLS)
            for u in range(MXU_COLS // step):
                col = c * MXU_COLS + u * step
                y_ref[pl.ds(col // tw, mb, stride=sub), col % tw:col % tw + step] += (
                    part[:, u * step:(u + 1) * step])

        _swiglu_partial(h, wg_ref, wu_ref, wd_ref, act_scr, accumulate)


def _moe_ffn(buf, block_expert, n_used, wg, wu, wd, li, mb, tf):
    d, f = wg.shape[2], wg.shape[3]
    sub = ROW_TILE_SUBLANES
    tw = d // sub
    p = buf.shape[0] // sub
    nk = f // tf
    kk = lambda j, k, be, nu: jnp.where(j < nu[0], k, nk - 1)
    row = pl.BlockSpec((mb * sub, tw), lambda j, k, be, nu: (j, 0))
    grid_spec = pltpu.PrefetchScalarGridSpec(
        num_scalar_prefetch=2,
        grid=(p // mb, nk),
        in_specs=[row,
                  pl.BlockSpec((None, None, d, tf), lambda j, k, be, nu: (li, be[j], 0, kk(j, k, be, nu))),
                  pl.BlockSpec((None, None, d, tf), lambda j, k, be, nu: (li, be[j], 0, kk(j, k, be, nu))),
                  pl.BlockSpec((None, None, tf, d), lambda j, k, be, nu: (li, be[j], kk(j, k, be, nu), 0))],
        out_specs=row,
        scratch_shapes=[pltpu.VMEM((mb, tf), BF16)],
    )
    return pl.pallas_call(
        _moe_ffn_kernel,
        grid_spec=grid_spec,
        out_shape=jax.ShapeDtypeStruct((p * sub, tw), F32),
        compiler_params=_cparams(("parallel", "arbitrary")),
        name="moe_ffn",
    )(block_expert, n_used, buf, wg, wu, wd)


ISSUE_UNROLL = 8


def _issue_tile_copies(chunk, copy_fn):
    def body(trip, carry):
        for u in range(ISSUE_UNROLL):
            copy_fn(trip, u).start(priority=u % 2)
        return carry

    lax.fori_loop(0, chunk // ISSUE_UNROLL, body, 0)


def _tile(ref, idx, sub):
    return ref.at[pl.ds(pl.multiple_of(idx * sub, sub), sub)]


def _dispatch_kernel(slot_ref, h_ref, out_ref, zero_scr, sem, *, n_tok_steps, sub):
    i = pl.program_id(0)
    n_copies = slot_ref.shape[-1]
    dst = lambda trip, u: _tile(out_ref, slot_ref[0, 0, trip * ISSUE_UNROLL + u], sub)
    tok = lambda trip, u: trip * (ISSUE_UNROLL // TOP_K) + u // TOP_K

    @pl.when(i == 0)
    def _():
        zero_scr[...] = jnp.zeros_like(zero_scr)

    @pl.when(i < n_tok_steps)
    def _():
        _issue_tile_copies(n_copies, lambda trip, u: pltpu.make_async_copy(
            _tile(h_ref, tok(trip, u), sub), dst(trip, u), sem))

    @pl.when(i >= n_tok_steps)
    def _():
        _issue_tile_copies(n_copies, lambda trip, u: pltpu.make_async_copy(zero_scr, dst(trip, u), sem))

    for _ in range(TOP_K):
        pltpu.make_async_copy(h_ref, out_ref.at[pl.ds(0, h_ref.shape[0])], sem).wait()


def _dispatch_rows(h, slot, tm, sub):
    n = h.shape[0] // sub
    tw = h.shape[1]
    p = slot.shape[0]
    n_copies = tm * TOP_K
    assert p % n_copies == 0 and n % tm == 0 and n_copies % ISSUE_UNROLL == 0
    n_tok_steps = n // tm
    return pl.pallas_call(
        functools.partial(_dispatch_kernel, n_tok_steps=n_tok_steps, sub=sub),
        grid=(p // n_copies,),
        in_specs=[pl.BlockSpec((1, 1, n_copies), lambda i: (i, 0, 0), memory_space=pltpu.SMEM),
                  pl.BlockSpec((tm * sub, tw), lambda i: (jnp.minimum(i, n_tok_steps - 1), 0))],
        out_specs=pl.BlockSpec(memory_space=pl.ANY),
        out_shape=jax.ShapeDtypeStruct((p * sub, tw), h.dtype),
        scratch_shapes=[pltpu.VMEM((sub, tw), h.dtype), pltpu.SemaphoreType.DMA(())],
        compiler_params=_cparams(("arbitrary",)),
        name="dispatch_rows",
    )(slot.reshape(p // n_copies, 1, n_copies), h)


def _combine_kernel(cur_ref, nxt_ref, x_ref, route_ref, y_ref, *rest, final_norm, sub):
    o_ref, ybuf, sem = rest[-3:]
    i = pl.program_id(0)
    tm = x_ref.shape[0]
    n_copies = tm * TOP_K
    par = i % 2

    def fetch(idx_ref, buf):
        def copy(trip, u):
            tok = trip * (ISSUE_UNROLL // TOP_K) + u // TOP_K
            return pltpu.make_async_copy(_tile(y_ref, idx_ref[0, 0, trip * ISSUE_UNROLL + u], sub),
                                         _tile(ybuf.at[buf, u % TOP_K], tok, sub), sem.at[buf])
        _issue_tile_copies(n_copies, copy)

    @pl.when(i == 0)
    def _():
        fetch(cur_ref, 0)

    @pl.when(i + 1 < pl.num_programs(0))
    def _():
        fetch(nxt_ref, 1 - par)

    pltpu.make_async_copy(ybuf.at[par], ybuf.at[par], sem.at[par]).wait()

    route = route_ref[...]
    gates = [route[:, TOP_K + k:TOP_K + k + 1] for k in range(TOP_K)]
    tw = x_ref.shape[1] // sub
    for j in range(sub):
        sl = slice(j * tw, (j + 1) * tw)
        mix = None
        for k in range(TOP_K):
            term = gates[k] * ybuf[par, k, pl.ds(j, tm, stride=sub), :]
            mix = term if mix is None else mix + term
        o_ref[:, sl] = x_ref[:, sl] + mix
    if final_norm:
        o_ref[...] = _rms(o_ref[...], rest[0][...])


def _combine(x2, y, dest, route, tm, g_final=None):
    n, d = x2.shape
    sub = ROW_TILE_SUBLANES
    n_copies = tm * TOP_K
    steps = n // tm
    assert n_copies % ISSUE_UNROLL == 0 and ISSUE_UNROLL % TOP_K == 0
    dest3 = dest.reshape(steps, 1, n_copies)
    row = pl.BlockSpec((tm, d), lambda i: (i, 0))
    in_specs = [pl.BlockSpec((1, 1, n_copies), lambda i: (i, 0, 0), memory_space=pltpu.SMEM),
                pl.BlockSpec((1, 1, n_copies), lambda i: (jnp.minimum(i + 1, steps - 1), 0, 0),
                             memory_space=pltpu.SMEM),
                row, pl.BlockSpec((tm, LANES), lambda i: (i, 0)), pl.BlockSpec(memory_space=pl.ANY)]
    args = [dest3, dest3, x2, route, y]
    if g_final is not None:
        in_specs.append(pl.BlockSpec((1, d), lambda i: (0, 0)))
        args.append(g_final)
    return pl.pallas_call(
        functools.partial(_combine_kernel, final_norm=g_final is not None, sub=sub),
        grid=(steps,), in_specs=in_specs, out_specs=row,
        out_shape=jax.ShapeDtypeStruct((n, d), F32),
        scratch_shapes=[pltpu.VMEM((2, TOP_K, tm * sub, d // sub), F32), pltpu.SemaphoreType.DMA((2,))],
        compiler_params=_cparams(("arbitrary",)), name="moe_combine",
    )(*args)


def _final_norm_kernel(x_ref, g_ref, o_ref):
    o_ref[...] = _rms(x_ref[...], g_ref[...])


def _final_norm(x2, g, tm):
    n, d = x2.shape
    row = pl.BlockSpec((tm, d), lambda i: (i, 0))
    return pl.pallas_call(
        _final_norm_kernel, grid=(n // tm,), in_specs=[row, pl.BlockSpec((1, d), lambda i: (0, 0))],
        out_specs=row, out_shape=jax.ShapeDtypeStruct((n, d), F32),
        compiler_params=_cparams(("parallel",)), name="final_norm",
    )(x2, g)


def _routing_plan(route, n_experts, mb):
    n = route.shape[0]
    n_assign = n * TOP_K
    flat_e = route[:, :TOP_K].astype(jnp.int32).reshape(-1)
    onehot = (flat_e[:, None] == jnp.arange(n_experts)[None, :]).astype(jnp.int32)
    csum = jnp.cumsum(onehot, axis=0)
    rank = jnp.sum(csum * onehot, axis=1) - 1
    counts = csum[-1]
    padded = (counts + mb - 1) // mb * mb
    pend = jnp.cumsum(padded)
    pstart = pend - padded
    dest = jnp.sum(pstart[None, :] * onehot, axis=1) + rank
    assert n_assign % mb == 0
    n_blocks = n_assign // mb + n_experts
    block_first = jnp.arange(n_blocks) * mb
    block_expert = jnp.minimum(jnp.sum(block_first[:, None] >= pend[None, :], axis=1), n_experts - 1)
    n_used = (pend[-1] // mb).reshape(1)
    pads = padded - counts
    cpad = jnp.cumsum(pads)
    i = jnp.arange(n_experts * mb)
    grp = jnp.sum(i[:, None] >= cpad[None, :], axis=1)
    grp_hot = (grp[:, None] == jnp.arange(n_experts + 1)[None, :]).astype(jnp.int32)
    first_free = jnp.concatenate([pstart + counts, pend[-1:]])
    before = jnp.concatenate([jnp.zeros((1,), cpad.dtype), cpad])
    free = jnp.sum(grp_hot * (first_free - before)[None, :], axis=1) + i
    slot = jnp.concatenate([dest, free]).astype(jnp.int32)
    return dest.astype(jnp.int32), slot, block_expert.astype(jnp.int32), n_used.astype(jnp.int32)


def _tiles(n, t):
    return dict(tm_in=math.gcd(1024, t), tm_proj=math.gcd(1024, t), tm_ffn=math.gcd(1024, n), tf=1792,
                tq_dil=1024, tq_mem=math.gcd(1024, t), moe_block=math.gcd(1024, n))


@jax.jit
def _forward(x, mem, g_mix_norm, w_in, rpb, g_mem_norm, w_mem_kv, g_mix_out, w_out,
             g_ffn_norm, w_dense_gate, w_dense_up, w_dense_down, w_router,
             w_moe_gate, w_moe_up, w_moe_down, g_final):
    b, t, d = x.shape
    n = b * t
    depth = w_in.shape[0]
    n_experts = w_router.shape[-1]
    ts = _tiles(n, t)
    tf = math.gcd(ts["tf"], w_dense_gate.shape[-1])
    tables = _rope_tables(t)
    dense_w = [w.astype(BF16) for w in (w_dense_gate, w_dense_up, w_dense_down)]
    moe_w = [w.astype(BF16) for w in (w_moe_gate, w_moe_up, w_moe_down)]
    row1 = lambda a: a.reshape(1, -1)
    x2 = x.reshape(n, d)
    out = None
    sub = ROW_TILE_SUBLANES
    for layer in range(depth):
        qa, ka, va, qm, qbs, kbs, vbs = _inproj(x2, row1(g_mix_norm[layer]), w_in[layer].astype(BF16),
                                                tables, t, ts["tm_in"])
        shp = lambda a: a.reshape(b, t, a.shape[-1])
        oa = _na_attention(shp(qa), shp(ka), shp(va), _na_bias_table(rpb[layer])).reshape(n, NA_WIDTH)
        obs, lses = [], []
        for (window, dil), qb, kb, vb in zip(DIL_BRANCHES, qbs, kbs, vbs):
            o, lse = _dilated_branch(qb, kb, vb, window // (2 * dil), ts["tq_dil"])
            obs.append(o)
            lses.append(lse)
        km, vm = _mem_kv(mem, row1(g_mem_norm[layer]), w_mem_kv[layer].astype(BF16))
        om = _mem_attention(shp(qm), km, vm, ts["tq_mem"]).reshape(n, MEM_WIDTH)
        gm, wo = row1(g_mix_out[layer]), w_out[layer].astype(BF16)
        i = layer // 2
        if layer % 2 == 0:
            x2 = _outproj(oa, obs, lses, om, x2, gm, wo, t, ts["tm_in"])
            x2 = _dense_ffn(x2, row1(g_ffn_norm[layer]), *dense_w, i, ts["tm_ffn"], tf)
            if layer == depth - 1:
                out = _final_norm(x2, row1(g_final), ts["tm_proj"])
        else:
            wr = jnp.pad(w_router[i], ((0, 0), (0, LANES - n_experts))).astype(BF16)
            x2, h, route = _outproj(oa, obs, lses, om, x2, gm, wo, t, ts["tm_in"],
                                    router=(row1(g_ffn_norm[layer]), wr, n_experts))
            mb = ts["moe_block"]
            dest, slot, block_expert, n_used = _routing_plan(route, n_experts, mb)
            buf = _dispatch_rows(h, slot, ts["tm_proj"], sub)
            y = _moe_ffn(buf, block_expert, n_used, *moe_w, i, mb, tf)
            if layer == depth - 1:
                out = _combine(x2, y, dest, route, ts["tm_proj"], g_final=row1(g_final))
            else:
                x2 = _combine(x2, y, dest, route, ts["tm_proj"])
    return out.reshape(b, t, d)


def kernel(x, mem, g_mix_norm, w_in, rpb, g_mem_norm, w_mem_kv, g_mix_out, w_out, g_ffn_norm,
           w_dense_gate, w_dense_up, w_dense_down, w_router, w_moe_gate, w_moe_up, w_moe_down, g_final):
    return _forward(x, mem, g_mix_norm, w_in, rpb, g_mem_norm, w_mem_kv, g_mix_out, w_out, g_ffn_norm,
                    w_dense_gate, w_dense_up, w_dense_down, w_router, w_moe_gate, w_moe_up, w_moe_down,
                    g_final)
```

```python
import functools
import math

import jax
import jax.numpy as jnp
from jax import lax
from jax.experimental import pallas as pl
from jax.experimental.pallas import tpu as pltpu

HEAD_DIM = 64
NA_HEADS = 6
DIL_HEADS = 6
MEM_HEADS = 4
NA_WIDTH = NA_HEADS * HEAD_DIM
DIL_WIDTH = DIL_HEADS * HEAD_DIM
MEM_WIDTH = MEM_HEADS * HEAD_DIM
GRID_W = 64
NA_WIN_ROWS = 8
NA_WIN_COLS = 16
DIL_BRANCHES = ((128, 1), (512, 4), (2048, 16))
ROPE_THETA = 500000.0
ROT_DIM = HEAD_DIM // 4
TOP_K = 2
RMS_EPS = 1e-6
ATTN_SCALE = HEAD_DIM ** -0.5
LOG2E = math.log2(math.e)
Q_SCALE = ATTN_SCALE * LOG2E

LANES = 128
MXU_COLS = 256
HEADS_PER_LANE_GROUP = LANES // HEAD_DIM
NEG = -1e30
ROWS_PER_TRIP = 8
ROW_TILE_SUBLANES = 8
VMEM_LIMIT = 56 * 1024 * 1024

F32 = jnp.float32
BF16 = jnp.bfloat16


def _cparams(sem):
    return pltpu.CompilerParams(dimension_semantics=sem, vmem_limit_bytes=VMEM_LIMIT)


def _rms(x, g):
    ms = jnp.mean(x * x, axis=-1, keepdims=True)
    return x * lax.rsqrt(ms + RMS_EPS) * g


def _lane_group(a, p):
    return a[:, p * LANES:(p + 1) * LANES]


def _attn_scores(q, k, bias_fn, n_heads):
    m = q.shape[0]
    assert HEADS_PER_LANE_GROUP == 2
    lo = lax.broadcasted_iota(jnp.int32, (m, LANES), 1) < HEAD_DIM
    s_parts = []
    for p in range(n_heads // HEADS_PER_LANE_GROUP):
        qp = _lane_group(q, p)
        zero = jnp.zeros_like(qp)
        q2 = jnp.concatenate([jnp.where(lo, qp, zero), jnp.where(lo, zero, qp)], axis=0)
        s_parts.append(lax.dot_general(q2, _lane_group(k, p), (((1,), (1,)), ((), ())),
                                       preferred_element_type=F32) + bias_fn(p))
    return jnp.concatenate(s_parts, axis=0)


def _attn_softmax(s):
    mx = jnp.max(s, axis=-1, keepdims=True)
    return jnp.exp2((s - mx).astype(BF16)), mx


def _values_with_ones(v):
    ones = jnp.ones((v.shape[0], LANES), v.dtype)
    return jnp.concatenate([blk for p in range(v.shape[1] // LANES) for blk in (_lane_group(v, p), ones)],
                           axis=-1)


def _attn_values(eb, mx, v, n_heads):
    m = eb.shape[0] // n_heads
    n_pairs = n_heads // HEADS_PER_LANE_GROUP
    with_ones = v.shape[1] == n_pairs * 2 * LANES
    lo = lax.broadcasted_iota(jnp.int32, (m, LANES), 1) < HEAD_DIM
    if not with_ones:
        den_all = jnp.sum(eb.astype(F32), axis=-1, keepdims=True)
    outs, lses = [], []
    for p in range(n_pairs):
        rows = slice(p * 2 * m, (p + 1) * 2 * m)
        if with_ones:
            both = jnp.dot(eb[rows], v[:, p * 2 * LANES:(p + 1) * 2 * LANES], preferred_element_type=F32)
            num, den = both[:, :LANES], both[:, LANES:]
        else:
            num, den = jnp.dot(eb[rows], _lane_group(v, p), preferred_element_type=F32), den_all[rows]
        o2 = num / den
        outs.append(jnp.where(lo, o2[:m], o2[m:]))
        lse = mx[rows] + jnp.log2(den)
        lses += [lse[:m], lse[m:]]
    return jnp.concatenate(outs, axis=-1), lses


def _attention_pipelined(items, n_heads, skew=1):
    n = len(items)
    loaded, scores, soft = {}, {}, {}
    for step in range(n + 2 * skew):
        if step < n:
            loaded[step] = items[step][0]()
            q, k, _, bias_fn = loaded[step]
            scores[step] = _attn_scores(q, k, bias_fn, n_heads)
        if 0 <= step - skew < n:
            soft[step - skew] = _attn_softmax(scores.pop(step - skew))
        if 0 <= step - 2 * skew < n:
            eb, mx = soft.pop(step - 2 * skew)
            items[step - 2 * skew][1](*_attn_values(eb, mx, loaded.pop(step - 2 * skew)[2], n_heads))


def _inproj_kernel(x_ref, g_ref, w_ref, cos_ref, sa_ref, sb_ref,
                   qa_ref, ka_ref, va_ref, qm_ref, *rest, dils):
    dil_refs, scr = rest[:-1], rest[-1]
    tm = x_ref.shape[0]
    h = _rms(x_ref[...], g_ref[...]).astype(BF16)

    def emit_dilated(val, refs):
        groups = val.shape[1] // LANES
        for g in range(groups):
            scr[g] = val[:, g * LANES:(g + 1) * LANES]
        for dil, ref in zip(dils, refs):
            if dil == 1:
                ref[0] = val.astype(BF16)
                continue
            for r in range(dil):
                for g in range(groups):
                    ref[r, :, g * LANES:(g + 1) * LANES] = (
                        scr[g, pl.ds(r, tm // dil, stride=dil), :].astype(BF16))

    def proj_pair(c0, width_a, width_b):
        both = jnp.dot(h, w_ref[:, c0:c0 + width_a + width_b], preferred_element_type=F32)
        return both[:, :width_a], both[:, width_a:]

    def rope(a):
        cos, sa, sb = cos_ref[...], sa_ref[...], sb_ref[...]
        half = ROT_DIM // 2
        parts = []
        for gidx in range(a.shape[1] // LANES):
            xg = a[:, gidx * LANES:(gidx + 1) * LANES]
            parts.append(xg * cos + pltpu.roll(xg, LANES - half, 1) * sa + pltpu.roll(xg, half, 1) * sb)
        return jnp.concatenate(parts, axis=-1)

    nd = len(dils)
    assert all(wd % MXU_COLS == 0 for wd in (2 * NA_WIDTH, NA_WIDTH + DIL_WIDTH, 2 * DIL_WIDTH))
    kb, vb = proj_pair(3 * NA_WIDTH + DIL_WIDTH, DIL_WIDTH, DIL_WIDTH)
    emit_dilated(rope(kb), dil_refs[nd:2 * nd])
    emit_dilated(vb, dil_refs[2 * nd:3 * nd])
    va, qb = proj_pair(2 * NA_WIDTH, NA_WIDTH, DIL_WIDTH)
    va_ref[...] = va.astype(BF16)
    emit_dilated(rope(qb * Q_SCALE), dil_refs[0:nd])
    qa, ka = proj_pair(0, NA_WIDTH, NA_WIDTH)
    qa_ref[...] = (qa * Q_SCALE).astype(BF16)
    ka_ref[...] = ka.astype(BF16)
    c = 3 * NA_WIDTH + 3 * DIL_WIDTH
    qm_ref[...] = (jnp.dot(h, w_ref[:, c:c + MEM_WIDTH], preferred_element_type=F32) * Q_SCALE).astype(BF16)


def _rope_tables(t):
    half = ROT_DIM // 2
    inv_freq = ROPE_THETA ** (-jnp.arange(0, ROT_DIM, 2, dtype=F32) / ROT_DIM)
    ang = jnp.arange(t, dtype=F32)[:, None] * inv_freq[None, :]
    cos, sin = jnp.cos(ang), jnp.sin(ang)
    ones = jnp.ones((t, HEAD_DIM - ROT_DIM), F32)
    zeros = jnp.zeros((t, HEAD_DIM - ROT_DIM), F32)
    zh = jnp.zeros((t, half), F32)
    cos_h = jnp.concatenate([cos, cos, ones], axis=1)
    sa_h = jnp.concatenate([-sin, zh, zeros], axis=1)
    sb_h = jnp.concatenate([zh, sin, zeros], axis=1)
    tile = lambda a: jnp.tile(a, (1, HEADS_PER_LANE_GROUP))
    return tile(cos_h), tile(sa_h), tile(sb_h)


def _inproj(x2, g, w, tables, t, tm):
    n, d = x2.shape
    b = n // t
    tpb = t // tm
    dils = tuple(dil for _, dil in DIL_BRANCHES)
    assert all(tm % (dil * 16) == 0 for dil in dils)
    row = lambda width: pl.BlockSpec((tm, width), lambda i: (i, 0))
    tab = pl.BlockSpec((tm, LANES), lambda i: (i % tpb, 0))
    plain = [NA_WIDTH] * 3 + [MEM_WIDTH]
    dil_specs = [pl.BlockSpec((None, dil, tm // dil, DIL_WIDTH), lambda i: (i // tpb, 0, i % tpb, 0))
                 for dil in dils] * 3
    dil_shapes = [jax.ShapeDtypeStruct((b, dil, t // dil, DIL_WIDTH), BF16) for dil in dils] * 3
    outs = pl.pallas_call(
        functools.partial(_inproj_kernel, dils=dils),
        grid=(n // tm,),
        in_specs=[row(d), pl.BlockSpec((1, d), lambda i: (0, 0)),
                  pl.BlockSpec(w.shape, lambda i: (0, 0)), tab, tab, tab],
        out_specs=[row(wd) for wd in plain] + dil_specs,
        out_shape=[jax.ShapeDtypeStruct((n, wd), BF16) for wd in plain] + dil_shapes,
        scratch_shapes=[pltpu.VMEM((DIL_WIDTH // LANES, tm, LANES), F32)],
        compiler_params=_cparams(("parallel",)),
        name="inproj",
    )(x2, g, w, *tables)
    nd = len(dils)
    qa, ka, va, qm = outs[:4]
    return qa, ka, va, qm, outs[4:4 + nd], outs[4 + nd:4 + 2 * nd], outs[4 + 2 * nd:]


def _na_bias_table(rpb):
    c = jnp.arange(GRID_W)
    c0 = jnp.clip(c - NA_WIN_COLS // 2, 0, GRID_W - NA_WIN_COLS)
    kc = jnp.arange(GRID_W)
    valid = (kc[None, :] >= c0[:, None]) & (kc[None, :] < c0[:, None] + NA_WIN_COLS)
    coff = kc[None, :] - c[:, None] + (NA_WIN_COLS - 1)
    onehot = (coff[None] == jnp.arange(2 * NA_WIN_COLS - 1)[:, None, None]).astype(F32)
    by_col = jnp.einsum("hrd,dck->hrck", rpb.astype(F32), onehot, precision=lax.Precision.HIGHEST)
    by_col = jnp.where(valid[None, None], by_col * LOG2E, NEG)
    tab = jnp.stack([by_col[:, d:d + NA_WIN_ROWS] for d in range(NA_WIN_ROWS)], axis=1)
    tab = tab.transpose(1, 0, 3, 2, 4)
    return tab.reshape(NA_WIN_ROWS * NA_HEADS // HEADS_PER_LANE_GROUP, HEADS_PER_LANE_GROUP * GRID_W,
                       NA_WIN_ROWS * GRID_W)


def _na_kernel(q_ref, kp_ref, kc_ref, kn_ref, vp_ref, vc_ref, vn_ref, bias_ref, o_ref,
               kwin, vwin, *, rows):
    j = pl.program_id(1)
    blk = NA_WIN_ROWS * GRID_W
    for idx, (kr, vr) in enumerate(((kp_ref, vp_ref), (kc_ref, vc_ref), (kn_ref, vn_ref))):
        kwin[idx * blk:(idx + 1) * blk, :] = kr[...]
        vwin[idx * blk:(idx + 1) * blk, :] = vr[...]

    n_pairs = NA_HEADS // HEADS_PER_LANE_GROUP

    def one_row(i):
        q_rows = pl.ds(pl.multiple_of(i * GRID_W, GRID_W), GRID_W)

        def load():
            r = j * NA_WIN_ROWS + i
            r0 = jnp.clip(r - NA_WIN_ROWS // 2, 0, rows - NA_WIN_ROWS)
            dlt = r0 - r + (NA_WIN_ROWS - 1)
            start = pl.multiple_of((r0 - (j - 1) * NA_WIN_ROWS) * GRID_W, GRID_W)
            return (q_ref[q_rows, :], kwin[pl.ds(start, blk), :], vwin[pl.ds(start, blk), :],
                    lambda p: bias_ref[dlt * n_pairs + p])

        def store(o, lses):
            o_ref[q_rows, :] = o.astype(BF16)

        return load, store

    def row_group(ig, carry):
        _attention_pipelined([one_row(ig * ROWS_PER_TRIP + u) for u in range(ROWS_PER_TRIP)], NA_HEADS)
        return carry

    lax.fori_loop(0, NA_WIN_ROWS // ROWS_PER_TRIP, row_group, 0)


def _na_attention(q, k, v, bias):
    b, t, w = q.shape
    rows = t // GRID_W
    assert rows % NA_WIN_ROWS == 0 and rows >= NA_WIN_ROWS
    nj = rows // NA_WIN_ROWS
    blk = NA_WIN_ROWS * GRID_W
    cur = pl.BlockSpec((None, blk, w), lambda bi, j: (bi, j, 0))
    prev = pl.BlockSpec((None, blk, w), lambda bi, j: (bi, jnp.maximum(j - 1, 0), 0))
    nxt = pl.BlockSpec((None, blk, w), lambda bi, j: (bi, jnp.minimum(j + 1, nj - 1), 0))
    return pl.pallas_call(
        functools.partial(_na_kernel, rows=rows),
        grid=(b, nj),
        in_specs=[cur, prev, cur, nxt, prev, cur, nxt,
                  pl.BlockSpec(bias.shape, lambda bi, j: (0, 0, 0))],
        out_specs=cur,
        out_shape=jax.ShapeDtypeStruct((b, t, w), BF16),
        scratch_shapes=[pltpu.VMEM((3 * blk, w), BF16), pltpu.VMEM((3 * blk, w), BF16)],
        compiler_params=_cparams(("parallel", "parallel")),
        name="na_attn",
    )(q, k, k, k, v, v, v, bias)


def _dil_kernel(q_ref, kp_ref, kc_ref, kn_ref, vp_ref, vc_ref, vn_ref, o_ref, lse_ref,
                kwin, vwin, *, seg, tq, halo):
    i = pl.program_id(2)
    width = tq + 2 * halo
    kwin[0:halo, :] = kp_ref[...]
    kwin[halo:halo + tq, :] = kc_ref[...]
    kwin[halo + tq:width, :] = kn_ref[...]
    vwin[0:halo, :] = _values_with_ones(vp_ref[...])
    vwin[halo:halo + tq, :] = _values_with_ones(vc_ref[...])
    vwin[halo + tq:width, :] = _values_with_ones(vn_ref[...])

    sq = math.gcd(tq, 2 * halo)
    sw = sq + 2 * halo
    a = lax.broadcasted_iota(jnp.int32, (sq, sw), 0)
    c = lax.broadcasted_iota(jnp.int32, (sq, sw), 1)
    band = (c >= a) & (c <= a + 2 * halo)
    lo = lax.broadcasted_iota(jnp.int32, (sq, LANES), 1) < HEAD_DIM

    def sub_block(s):
        row0 = pl.multiple_of(s * sq, sq)

        def load():
            first_key = i * tq + s * sq - halo
            valid = band & (c >= -first_key) & (c < seg - first_key)
            negb = jnp.where(valid, 0.0, NEG).astype(F32)
            negb2 = jnp.concatenate([negb] * HEADS_PER_LANE_GROUP, axis=0)
            return (q_ref[pl.ds(row0, sq), :], kwin[pl.ds(row0, sw), :], vwin[pl.ds(row0, sw), :],
                    lambda p: negb2)

        def store(o, lses):
            o_ref[pl.ds(row0, sq), :] = o.astype(BF16)
            lse_ref[pl.ds(row0, sq), :] = jnp.concatenate(
                [jnp.where(lo, lses[2 * p], lses[2 * p + 1]) for p in range(DIL_HEADS // 2)], axis=-1)

        return load, store

    n_sub = tq // sq
    per_trip = math.gcd(n_sub, ROWS_PER_TRIP)

    def sub_block_group(sg, carry):
        _attention_pipelined([sub_block(sg * per_trip + u) for u in range(per_trip)], DIL_HEADS, skew=0)
        return carry

    lax.fori_loop(0, n_sub // per_trip, sub_block_group, 0)


def _dilated_branch(q, k, v, n_side, tq_max):
    b, dil, seg, w = q.shape
    halo = n_side
    assert seg % halo == 0 and halo % 16 == 0
    tq = math.gcd(tq_max, seg)
    assert tq % halo == 0
    hb = tq // halo
    nhalo = seg // halo
    cur = pl.BlockSpec((None, None, tq, w), lambda bi, r, i: (bi, r, i, 0))
    prev = pl.BlockSpec((None, None, halo, w), lambda bi, r, i: (bi, r, jnp.maximum(i * hb - 1, 0), 0))
    nxt = pl.BlockSpec((None, None, halo, w),
                       lambda bi, r, i: (bi, r, jnp.minimum((i + 1) * hb, nhalo - 1), 0))
    return pl.pallas_call(
        functools.partial(_dil_kernel, seg=seg, tq=tq, halo=halo),
        grid=(b, dil, seg // tq),
        in_specs=[cur, prev, cur, nxt, prev, cur, nxt],
        out_specs=[cur, cur],
        out_shape=[jax.ShapeDtypeStruct((b, dil, seg, w), BF16),
                   jax.ShapeDtypeStruct((b, dil, seg, w), F32)],
        scratch_shapes=[pltpu.VMEM((tq + 2 * halo, w), BF16), pltpu.VMEM((tq + 2 * halo, 2 * w), BF16)],
        compiler_params=_cparams(("parallel", "parallel", "parallel")),
        name=f"dilated_d{dil}",
    )(q, k, k, k, v, v, v)


def _memkv_kernel(mem_ref, g_ref, w_ref, k_ref, v_ref):
    h = _rms(mem_ref[...], g_ref[...]).astype(BF16)
    kv = jnp.dot(h, w_ref[...], preferred_element_type=F32)
    k_ref[...] = kv[:, :MEM_WIDTH].astype(BF16)
    v_ref[...] = kv[:, MEM_WIDTH:].astype(BF16)


def _mem_kv(mem, g, w):
    b, m, d = mem.shape
    blk = lambda width: pl.BlockSpec((None, m, width), lambda bi: (bi, 0, 0))
    return pl.pallas_call(
        _memkv_kernel,
        grid=(b,),
        in_specs=[blk(d), pl.BlockSpec((1, d), lambda bi: (0, 0)), pl.BlockSpec(w.shape, lambda bi: (0, 0))],
        out_specs=[blk(MEM_WIDTH), blk(MEM_WIDTH)],
        out_shape=[jax.ShapeDtypeStruct((b, m, MEM_WIDTH), BF16)] * 2,
        compiler_params=_cparams(("parallel",)),
        name="mem_kv",
    )(mem, g, w)


def _memattn_kernel(q_ref, k_ref, v_ref, o_ref):
    tq = q_ref.shape[0]
    sq = math.gcd(tq, LANES)
    v_aug = _values_with_ones(v_ref[...])

    def sub_block(s):
        rows = slice(s * sq, (s + 1) * sq)

        def store(o, lses):
            o_ref[rows, :] = o.astype(BF16)

        return (lambda: (q_ref[rows, :], k_ref[...], v_aug, lambda p: 0.0)), store

    _attention_pipelined([sub_block(s) for s in range(tq // sq)], MEM_HEADS)


def _mem_attention(q, k, v, tq):
    b, t, w = q.shape
    m = k.shape[1]
    qs = pl.BlockSpec((None, tq, w), lambda bi, i: (bi, i, 0))
    ks = pl.BlockSpec((None, m, w), lambda bi, i: (bi, 0, 0))
    return pl.pallas_call(
        _memattn_kernel,
        grid=(b, t // tq),
        in_specs=[qs, ks, ks],
        out_specs=qs,
        out_shape=jax.ShapeDtypeStruct((b, t, w), BF16),
        compiler_params=_cparams(("parallel", "parallel")),
        name="mem_attn",
    )(q, k, v)


def _token_order(ref, scr):
    dil = ref.shape[0]
    if dil == 1:
        return ref[0].astype(F32)
    groups, tm, _ = scr.shape
    for r in range(dil):
        for g in range(groups):
            scr[g, pl.ds(r, tm // dil, stride=dil), :] = ref[r, :, g * LANES:(g + 1) * LANES].astype(F32)
    return jnp.concatenate([scr[g] for g in range(groups)], axis=-1)


def _branch_mix(ob_refs, lse_refs, o_scrs, l_scrs):
    lses = [_token_order(r, s) for r, s in zip(lse_refs, l_scrs)]
    mx = functools.reduce(jnp.maximum, lses)
    es = [jnp.exp2(l - mx) for l in lses]
    den = functools.reduce(lambda p, q: p + q, es)
    acc = None
    for e, ob, scr in zip(es, ob_refs, o_scrs):
        term = (e / den) * _token_order(ob, scr)
        acc = term if acc is None else acc + term
    return acc


def _outproj_core(oa_ref, ob_refs, lse_refs, om_ref, x_ref, gm_ref, w_ref, o_scrs, l_scrs):
    gm = gm_ref[...]
    e0, e1 = NA_WIDTH, NA_WIDTH + DIL_WIDTH
    ya = _rms(oa_ref[...].astype(F32), gm[:, :e0]).astype(BF16)
    yb = _rms(_branch_mix(ob_refs, lse_refs, o_scrs, l_scrs), gm[:, e0:e1]).astype(BF16)
    ym = _rms(om_ref[...].astype(F32), gm[:, e1:]).astype(BF16)
    y = jnp.concatenate([ya, yb, ym], axis=-1)
    return x_ref[...] + jnp.dot(y, w_ref[...], preferred_element_type=F32)


def _split_scratch(scrs, dils):
    it = iter(scrs)
    o_scrs = [next(it) if dil > 1 else None for dil in dils]
    l_scrs = [next(it) if dil > 1 else None for dil in dils]
    return o_scrs, l_scrs


def _outproj_kernel(oa_ref, ob1, ob2, ob3, l1, l2, l3, om_ref, x_ref, gm_ref, w_ref, xo_ref, *scrs, dils):
    xo_ref[...] = _outproj_core(oa_ref, (ob1, ob2, ob3), (l1, l2, l3), om_ref, x_ref, gm_ref, w_ref,
                                *_split_scratch(scrs, dils))


def _store_row_tiles(ref, val):
    m, d = val.shape
    sub = ref.shape[0] // m
    tw = d // sub
    for j in range(sub):
        ref[pl.ds(j, m, stride=sub), :] = val[:, j * tw:(j + 1) * tw]


def _outproj_router_kernel(oa_ref, ob1, ob2, ob3, l1, l2, l3, om_ref, x_ref, gm_ref, w_ref,
                           gf_ref, wr_ref, xo_ref, h_ref, route_ref, *scrs, n_experts, dils):
    xn = _outproj_core(oa_ref, (ob1, ob2, ob3), (l1, l2, l3), om_ref, x_ref, gm_ref, w_ref,
                       *_split_scratch(scrs, dils))
    xo_ref[...] = xn
    h = _rms(xn, gf_ref[...])
    _store_row_tiles(h_ref, h)
    logits = jnp.dot(h.astype(BF16), wr_ref[...], preferred_element_type=F32)
    lane = lax.broadcasted_iota(jnp.int32, logits.shape, 1)
    lg = jnp.where(lane < n_experts, logits, NEG)
    m1 = jnp.max(lg, axis=-1, keepdims=True)
    i1 = jnp.min(jnp.where(lg == m1, lane, LANES), axis=-1, keepdims=True)
    lg2 = jnp.where(lane == i1, NEG, lg)
    m2 = jnp.max(lg2, axis=-1, keepdims=True)
    i2 = jnp.min(jnp.where(lg2 == m2, lane, LANES), axis=-1, keepdims=True)
    e2 = jnp.exp(m2 - m1)
    g1 = 1.0 / (1.0 + e2)
    g2 = e2 / (1.0 + e2)
    route = jnp.where(lane == 0, i1.astype(F32), 0.0)
    route = jnp.where(lane == 1, i2.astype(F32), route)
    route = jnp.where(lane == 2, g1, route)
    route = jnp.where(lane == 3, g2, route)
    route_ref[...] = route


def _outproj(oa, obs, lses, om, x2, gm, w, t, tm, router=None):
    n, d = x2.shape
    tpb = t // tm
    row = lambda width: pl.BlockSpec((tm, width), lambda i: (i, 0))
    full = lambda a: pl.BlockSpec(a.shape, lambda i: (0,) * a.ndim)
    grouped = lambda a: pl.BlockSpec((None, a.shape[1], tm // a.shape[1], a.shape[3]),
                                     lambda i: (i // tpb, 0, i % tpb, 0))
    dils = tuple(a.shape[1] for a in obs)
    args = [oa, *obs, *lses, om, x2, gm, w]
    in_specs = ([row(NA_WIDTH)] + [grouped(a) for a in obs] + [grouped(a) for a in lses]
                + [row(MEM_WIDTH), row(d), full(gm), full(w)])
    scratch = [pltpu.VMEM((DIL_WIDTH // LANES, tm, LANES), F32)] * (2 * sum(dil > 1 for dil in dils))
    if router is None:
        return pl.pallas_call(
            functools.partial(_outproj_kernel, dils=dils),
            grid=(n // tm,), in_specs=in_specs, out_specs=row(d),
            out_shape=jax.ShapeDtypeStruct((n, d), F32), scratch_shapes=scratch,
            compiler_params=_cparams(("parallel",)), name="outproj",
        )(*args)
    gf, wr, n_experts = router
    sub = ROW_TILE_SUBLANES
    return pl.pallas_call(
        functools.partial(_outproj_router_kernel, n_experts=n_experts, dils=dils),
        grid=(n // tm,), in_specs=in_specs + [full(gf), full(wr)],
        out_specs=[row(d), pl.BlockSpec((tm * sub, d // sub), lambda i: (i, 0)), row(LANES)],
        out_shape=[jax.ShapeDtypeStruct((n, d), F32), jax.ShapeDtypeStruct((n * sub, d // sub), F32),
                   jax.ShapeDtypeStruct((n, LANES), F32)],
        scratch_shapes=scratch,
        compiler_params=_cparams(("parallel",)), name="outproj_router",
    )(*args, gf, wr)


def _swiglu_partial(h, wg_ref, wu_ref, wd_ref, act_scr, emit):
    tf = wg_ref.shape[-1]
    chunk = math.gcd(MXU_COLS, tf)
    for c in range(tf // chunk):
        sl = slice(c * chunk, (c + 1) * chunk)
        gate = jnp.dot(h, wg_ref[:, sl], preferred_element_type=F32)
        up = jnp.dot(h, wu_ref[:, sl], preferred_element_type=F32)
        act_scr[:, sl] = ((gate / (1.0 + jnp.exp(-gate))) * up).astype(BF16)
    act = act_scr[...]
    for c in range(wd_ref.shape[-1] // MXU_COLS):
        emit(c, jnp.dot(act, wd_ref[:, c * MXU_COLS:(c + 1) * MXU_COLS], preferred_element_type=F32))


def _dense_ffn_kernel(x_ref, g_ref, wg_ref, wu_ref, wd_ref, o_ref, h_scr, act_scr):
    @pl.when(pl.program_id(1) == 0)
    def _():
        x = x_ref[...]
        h_scr[...] = _rms(x, g_ref[...]).astype(BF16)
        o_ref[...] = x

    def accumulate(c, part):
        o_ref[:, c * MXU_COLS:(c + 1) * MXU_COLS] += part

    _swiglu_partial(h_scr[...], wg_ref, wu_ref, wd_ref, act_scr, accumulate)


def _dense_ffn(x2, g, wg, wu, wd, li, tm, tf):
    n, d = x2.shape
    f = wg.shape[2]
    row = pl.BlockSpec((tm, d), lambda i, k: (i, 0))
    return pl.pallas_call(
        _dense_ffn_kernel,
        grid=(n // tm, f // tf),
        in_specs=[row, pl.BlockSpec((1, d), lambda i, k: (0, 0)),
                  pl.BlockSpec((None, d, tf), lambda i, k: (li, 0, k)),
                  pl.BlockSpec((None, d, tf), lambda i, k: (li, 0, k)),
                  pl.BlockSpec((None, tf, d), lambda i, k: (li, k, 0))],
        out_specs=row,
        out_shape=jax.ShapeDtypeStruct((n, d), F32),
        scratch_shapes=[pltpu.VMEM((tm, d), BF16), pltpu.VMEM((tm, tf), BF16)],
        compiler_params=_cparams(("parallel", "arbitrary")),
        name="dense_ffn",
    )(x2, g, wg, wu, wd)


def _moe_ffn_kernel(be_ref, nused_ref, xb_ref, wg_ref, wu_ref, wd_ref, y_ref, act_scr):
    j = pl.program_id(0)
    k = pl.program_id(1)
    used = j < nused_ref[0]
    mb = act_scr.shape[0]
    sub = xb_ref.shape[0] // mb
    tw = xb_ref.shape[1]

    @pl.when(k == 0)
    def _():
        y_ref[...] = jnp.zeros_like(y_ref)

    @pl.when(used)
    def _():
        h = jnp.concatenate([xb_ref[pl.ds(c, mb, stride=sub), :].astype(BF16) for c in range(sub)], axis=-1)

        def accumulate(c, part):
            step = min(tw, MXU_COLS)
            for u in range(MXU_COLS // step):
                col = c * MXU_COLS + u * step
                y_ref[pl.ds(col // tw, mb, stride=sub), col % tw:col % tw + step] += (
                    part[:, u * step:(u + 1) * step])

        _swiglu_partial(h, wg_ref, wu_ref, wd_ref, act_scr, accumulate)


def _moe_ffn(buf, block_expert, n_used, wg, wu, wd, li, mb, tf):
    d, f = wg.shape[2], wg.shape[3]
    sub = ROW_TILE_SUBLANES
    tw = d // sub
    p = buf.shape[0] // sub
    nk = f // tf
    kk = lambda j, k, be, nu: jnp.where(j < nu[0], k, nk - 1)
    row = pl.BlockSpec((mb * sub, tw), lambda j, k, be, nu: (j, 0))
    grid_spec = pltpu.PrefetchScalarGridSpec(
        num_scalar_prefetch=2,
        grid=(p // mb, nk),
        in_specs=[row,
                  pl.BlockSpec((None, None, d, tf), lambda j, k, be, nu: (li, be[j], 0, kk(j, k, be, nu))),
                  pl.BlockSpec((None, None, d, tf), lambda j, k, be, nu: (li, be[j], 0, kk(j, k, be, nu))),
                  pl.BlockSpec((None, None, tf, d), lambda j, k, be, nu: (li, be[j], kk(j, k, be, nu), 0))],
        out_specs=row,
        scratch_shapes=[pltpu.VMEM((mb, tf), BF16)],
    )
    return pl.pallas_call(
        _moe_ffn_kernel,
        grid_spec=grid_spec,
        out_shape=jax.ShapeDtypeStruct((p * sub, tw), F32),
        compiler_params=_cparams(("parallel", "arbitrary")),
        name="moe_ffn",
    )(block_expert, n_used, buf, wg, wu, wd)


ISSUE_UNROLL = 8


def _issue_tile_copies(chunk, copy_fn):
    def body(trip, carry):
        for u in range(ISSUE_UNROLL):
            copy_fn(trip, u).start(priority=u % 2)
        return carry

    lax.fori_loop(0, chunk // ISSUE_UNROLL, body, 0)


def _tile(ref, idx, sub):
    return ref.at[pl.ds(pl.multiple_of(idx * sub, sub), sub)]


def _dispatch_kernel(slot_ref, h_ref, out_ref, zero_scr, sem, *, n_tok_steps, sub):
    i = pl.program_id(0)
    n_copies = slot_ref.shape[-1]
    dst = lambda trip, u: _tile(out_ref, slot_ref[0, 0, trip * ISSUE_UNROLL + u], sub)
    tok = lambda trip, u: trip * (ISSUE_UNROLL // TOP_K) + u // TOP_K

    @pl.when(i == 0)
    def _():
        zero_scr[...] = jnp.zeros_like(zero_scr)

    @pl.when(i < n_tok_steps)
    def _():
        _issue_tile_copies(n_copies, lambda trip, u: pltpu.make_async_copy(
            _tile(h_ref, tok(trip, u), sub), dst(trip, u), sem))

    @pl.when(i >= n_tok_steps)
    def _():
        _issue_tile_copies(n_copies, lambda trip, u: pltpu.make_async_copy(zero_scr, dst(trip, u), sem))

    for _ in range(TOP_K):
        pltpu.make_async_copy(h_ref, out_ref.at[pl.ds(0, h_ref.shape[0])], sem).wait()


def _dispatch_rows(h, slot, tm, sub):
    n = h.shape[0] // sub
    tw = h.shape[1]
    p = slot.shape[0]
    n_copies = tm * TOP_K
    assert p % n_copies == 0 and n % tm == 0 and n_copies % ISSUE_UNROLL == 0
    n_tok_steps = n // tm
    return pl.pallas_call(
        functools.partial(_dispatch_kernel, n_tok_steps=n_tok_steps, sub=sub),
        grid=(p // n_copies,),
        in_specs=[pl.BlockSpec((1, 1, n_copies), lambda i: (i, 0, 0), memory_space=pltpu.SMEM),
                  pl.BlockSpec((tm * sub, tw), lambda i: (jnp.minimum(i, n_tok_steps - 1), 0))],
        out_specs=pl.BlockSpec(memory_space=pl.ANY),
        out_shape=jax.ShapeDtypeStruct((p * sub, tw), h.dtype),
        scratch_shapes=[pltpu.VMEM((sub, tw), h.dtype), pltpu.SemaphoreType.DMA(())],
        compiler_params=_cparams(("arbitrary",)),
        name="dispatch_rows",
    )(slot.reshape(p // n_copies, 1, n_copies), h)


def _combine_kernel(cur_ref, nxt_ref, x_ref, route_ref, y_ref, *rest, final_norm, sub):
    o_ref, ybuf, sem = rest[-3:]
    i = pl.program_id(0)
    tm = x_ref.shape[0]
    n_copies = tm * TOP_K
    par = i % 2

    def fetch(idx_ref, buf):
        def copy(trip, u):
            tok = trip * (ISSUE_UNROLL // TOP_K) + u // TOP_K
            return pltpu.make_async_copy(_tile(y_ref, idx_ref[0, 0, trip * ISSUE_UNROLL + u], sub),
                                         _tile(ybuf.at[buf, u % TOP_K], tok, sub), sem.at[buf])
        _issue_tile_copies(n_copies, copy)

    @pl.when(i == 0)
    def _():
        fetch(cur_ref, 0)

    @pl.when(i + 1 < pl.num_programs(0))
    def _():
        fetch(nxt_ref, 1 - par)

    pltpu.make_async_copy(ybuf.at[par], ybuf.at[par], sem.at[par]).wait()

    route = route_ref[...]
    gates = [route[:, TOP_K + k:TOP_K + k + 1] for k in range(TOP_K)]
    tw = x_ref.shape[1] // sub
    for j in range(sub):
        sl = slice(j * tw, (j + 1) * tw)
        mix = None
        for k in range(TOP_K):
            term = gates[k] * ybuf[par, k, pl.ds(j, tm, stride=sub), :]
            mix = term if mix is None else mix + term
        o_ref[:, sl] = x_ref[:, sl] + mix
    if final_norm:
        o_ref[...] = _rms(o_ref[...], rest[0][...])


def _combine(x2, y, dest, route, tm, g_final=None):
    n, d = x2.shape
    sub = ROW_TILE_SUBLANES
    n_copies = tm * TOP_K
    steps = n // tm
    assert n_copies % ISSUE_UNROLL == 0 and ISSUE_UNROLL % TOP_K == 0
    dest3 = dest.reshape(steps, 1, n_copies)
    row = pl.BlockSpec((tm, d), lambda i: (i, 0))
    in_specs = [pl.BlockSpec((1, 1, n_copies), lambda i: (i, 0, 0), memory_space=pltpu.SMEM),
                pl.BlockSpec((1, 1, n_copies), lambda i: (jnp.minimum(i + 1, steps - 1), 0, 0),
                             memory_space=pltpu.SMEM),
                row, pl.BlockSpec((tm, LANES), lambda i: (i, 0)), pl.BlockSpec(memory_space=pl.ANY)]
    args = [dest3, dest3, x2, route, y]
    if g_final is not None:
        in_specs.append(pl.BlockSpec((1, d), lambda i: (0, 0)))
        args.append(g_final)
    return pl.pallas_call(
        functools.partial(_combine_kernel, final_norm=g_final is not None, sub=sub),
        grid=(steps,), in_specs=in_specs, out_specs=row,
        out_shape=jax.ShapeDtypeStruct((n, d), F32),
        scratch_shapes=[pltpu.VMEM((2, TOP_K, tm * sub, d // sub), F32), pltpu.SemaphoreType.DMA((2,))],
        compiler_params=_cparams(("arbitrary",)), name="moe_combine",
    )(*args)


def _final_norm_kernel(x_ref, g_ref, o_ref):
    o_ref[...] = _rms(x_ref[...], g_ref[...])


def _final_norm(x2, g, tm):
    n, d = x2.shape
    row = pl.BlockSpec((tm, d), lambda i: (i, 0))
    return pl.pallas_call(
        _final_norm_kernel, grid=(n // tm,), in_specs=[row, pl.BlockSpec((1, d), lambda i: (0, 0))],
        out_specs=row, out_shape=jax.ShapeDtypeStruct((n, d), F32),
        compiler_params=_cparams(("parallel",)), name="final_norm",
    )(x2, g)


def _routing_plan(route, n_experts, mb):
    n = route.shape[0]
    n_assign = n * TOP_K
    flat_e = route[:, :TOP_K].astype(jnp.int32).reshape(-1)
    onehot = (flat_e[:, None] == jnp.arange(n_experts)[None, :]).astype(jnp.int32)
    csum = jnp.cumsum(onehot, axis=0)
    rank = jnp.sum(csum * onehot, axis=1) - 1
    counts = csum[-1]
    padded = (counts + mb - 1) // mb * mb
    pend = jnp.cumsum(padded)
    pstart = pend - padded
    dest = jnp.sum(pstart[None, :] * onehot, axis=1) + rank
    assert n_assign % mb == 0
    n_blocks = n_assign // mb + n_experts
    block_first = jnp.arange(n_blocks) * mb
    block_expert = jnp.minimum(jnp.sum(block_first[:, None] >= pend[None, :], axis=1), n_experts - 1)
    n_used = (pend[-1] // mb).reshape(1)
    pads = padded - counts
    cpad = jnp.cumsum(pads)
    i = jnp.arange(n_experts * mb)
    grp = jnp.sum(i[:, None] >= cpad[None, :], axis=1)
    grp_hot = (grp[:, None] == jnp.arange(n_experts + 1)[None, :]).astype(jnp.int32)
    first_free = jnp.concatenate([pstart + counts, pend[-1:]])
    before = jnp.concatenate([jnp.zeros((1,), cpad.dtype), cpad])
    free = jnp.sum(grp_hot * (first_free - before)[None, :], axis=1) + i
    slot = jnp.concatenate([dest, free]).astype(jnp.int32)
    return dest.astype(jnp.int32), slot, block_expert.astype(jnp.int32), n_used.astype(jnp.int32)


def _tiles(n, t):
    return dict(tm_in=math.gcd(1024, t), tm_proj=math.gcd(512, t), tm_ffn=math.gcd(1024, n), tf=1792,
                tq_dil=1024, tq_mem=math.gcd(1024, t), moe_block=math.gcd(1024, n))


@jax.jit
def _forward(x, mem, g_mix_norm, w_in, rpb, g_mem_norm, w_mem_kv, g_mix_out, w_out,
             g_ffn_norm, w_dense_gate, w_dense_up, w_dense_down, w_router,
             w_moe_gate, w_moe_up, w_moe_down, g_final):
    b, t, d = x.shape
    n = b * t
    depth = w_in.shape[0]
    n_experts = w_router.shape[-1]
    ts = _tiles(n, t)
    tf = math.gcd(ts["tf"], w_dense_gate.shape[-1])
    tables = _rope_tables(t)
    dense_w = [w.astype(BF16) for w in (w_dense_gate, w_dense_up, w_dense_down)]
    moe_w = [w.astype(BF16) for w in (w_moe_gate, w_moe_up, w_moe_down)]
    row1 = lambda a: a.reshape(1, -1)
    x2 = x.reshape(n, d)
    out = None
    sub = ROW_TILE_SUBLANES
    for layer in range(depth):
        qa, ka, va, qm, qbs, kbs, vbs = _inproj(x2, row1(g_mix_norm[layer]), w_in[layer].astype(BF16),
                                                tables, t, ts["tm_in"])
        shp = lambda a: a.reshape(b, t, a.shape[-1])
        oa = _na_attention(shp(qa), shp(ka), shp(va), _na_bias_table(rpb[layer])).reshape(n, NA_WIDTH)
        obs, lses = [], []
        for (window, dil), qb, kb, vb in zip(DIL_BRANCHES, qbs, kbs, vbs):
            o, lse = _dilated_branch(qb, kb, vb, window // (2 * dil), ts["tq_dil"])
            obs.append(o)
            lses.append(lse)
        km, vm = _mem_kv(mem, row1(g_mem_norm[layer]), w_mem_kv[layer].astype(BF16))
        om = _mem_attention(shp(qm), km, vm, ts["tq_mem"]).reshape(n, MEM_WIDTH)
        gm, wo = row1(g_mix_out[layer]), w_out[layer].astype(BF16)
        i = layer // 2
        if layer % 2 == 0:
            x2 = _outproj(oa, obs, lses, om, x2, gm, wo, t, ts["tm_in"])
            x2 = _dense_ffn(x2, row1(g_ffn_norm[layer]), *dense_w, i, ts["tm_ffn"], tf)
            if layer == depth - 1:
                out = _final_norm(x2, row1(g_final), ts["tm_proj"])
        else:
            wr = jnp.pad(w_router[i], ((0, 0), (0, LANES - n_experts))).astype(BF16)
            x2, h, route = _outproj(oa, obs, lses, om, x2, gm, wo, t, ts["tm_in"],
                                    router=(row1(g_ffn_norm[layer]), wr, n_experts))
            mb = ts["moe_block"]
            dest, slot, block_expert, n_used = _routing_plan(route, n_experts, mb)
            buf = _dispatch_rows(h, slot, ts["tm_in"], sub)
            y = _moe_ffn(buf, block_expert, n_used, *moe_w, i, mb, tf)
            if layer == depth - 1:
                out = _combine(x2, y, dest, route, ts["tm_proj"], g_final=row1(g_final))
            else:
                x2 = _combine(x2, y, dest, route, ts["tm_proj"])
    return out.reshape(b, t, d)


def kernel(x, mem, g_mix_norm, w_in, rpb, g_mem_norm, w_mem_kv, g_mix_out, w_out, g_ffn_norm,
           w_dense_gate, w_dense_up, w_dense_down, w_router, w_moe_gate, w_moe_up, w_moe_down, g_final):
    return _forward(x, mem, g_mix_norm, w_in, rpb, g_mem_norm, w_mem_kv, g_mix_out, w_out, g_ffn_norm,
                    w_dense_gate, w_dense_up, w_dense_down, w_router, w_moe_gate, w_moe_up, w_moe_down,
                    g_final)
```
